```python
import math
import jax, jax.numpy as jnp
from jax import lax
import numpy as np

D_MODEL = 2048
BATCH = 4
SEQ = 8192
DEPTH = 1

GRID_W = 64
CTX_LEN = 256
RWKV_HEADS = 16
RWKV_HEAD_DIM = 64
RWKV_DIM = RWKV_HEADS * RWKV_HEAD_DIM
DECAY_LORA = 96
AAA_LORA = 96
GATE_LORA = 256
RWKV_COLS = 3 * RWKV_DIM + 2 * DECAY_LORA + 2 * AAA_LORA + GATE_LORA
DIFF_HEADS = 8
DIFF_QK_DIM = 64
DIFF_V_DIM = 2 * DIFF_QK_DIM
DIFF_DIM = DIFF_HEADS * DIFF_V_DIM
DIFF_QK_COLS = DIFF_HEADS * 2 * DIFF_QK_DIM
DIFF_COLS = 2 * DIFF_QK_COLS + DIFF_DIM
DIFF_SCALE = DIFF_QK_DIM ** -0.5
ROPE_THETA = 10000.0
Q_BLOCK = 128
GATE_COLS = 2 * D_MODEL
N_IN = RWKV_COLS + DIFF_COLS + GATE_COLS
N_GROUPS = 4
EXPERTS_PER_GROUP = 8
N_EXPERTS = N_GROUPS * EXPERTS_PER_GROUP
TOP_K = 2
D_EXPERT = 512
MOE_BLOCK = 128
NORM_EPS = 1e-6
SUBLN_EPS = 1e-5
LNX_EPS = 64e-5

kernel_name = "hybrid_rwkv7_diffattn_hmoe_dit"


def rmsnorm(x, g, eps=NORM_EPS):
    xf = x.astype(jnp.float32)
    y = xf * lax.rsqrt(jnp.mean(xf * xf, axis=-1, keepdims=True) + eps)
    return (y * g.astype(jnp.float32)).astype(x.dtype)


def centred_shift(p, mu):
    zero = jnp.zeros_like(p[:, :1])
    prev = jnp.concatenate([zero, p[:, :-1]], axis=1)
    nxt = jnp.concatenate([p[:, 1:], zero], axis=1)
    return p + mu * (0.5 * (prev + nxt) - p)


def rwkv7_scan(s0, r, decay, k, v, kk, a, reverse):
    def step(S, inp):
        r_t, w_t, k_t, v_t, kk_t, a_t = inp
        sa = jnp.einsum('bhvk,bhk->bhv', S, kk_t)
        S = (S * w_t[:, :, None, :] - sa[..., None] * (kk_t * a_t)[:, :, None, :]
             + v_t[..., None] * k_t[:, :, None, :])
        return S, jnp.einsum('bhvk,bhk->bhv', S, r_t)
    xs = tuple(jnp.moveaxis(t, 1, 0) for t in (r, decay, k, v, kk, a))
    s_final, y = lax.scan(step, s0, xs, reverse=reverse)
    return s_final, jnp.moveaxis(y, 0, 1)


def rwkv_branch(p, s0_f, s0_b, w0, w2, a0, a2, g2, k_k, k_a, r_k, lnx_g, lnx_b):
    B_, L, _ = p.shape
    pf = p.astype(jnp.float32)
    C = RWKV_DIM
    r, k, v = pf[..., :C], pf[..., C:2 * C], pf[..., 2 * C:3 * C]
    o = 3 * C
    xw = pf[..., o:o + 2 * DECAY_LORA].reshape(B_, L, 2, DECAY_LORA)
    o += 2 * DECAY_LORA
    xa = pf[..., o:o + 2 * AAA_LORA].reshape(B_, L, 2, AAA_LORA)
    o += 2 * AAA_LORA
    xg = pf[..., o:o + GATE_LORA]
    w_log = -jax.nn.softplus(-(w0 + jnp.einsum('bldr,drc->bldc', jnp.tanh(xw), w2))) - 0.5
    decay = jnp.exp(-jnp.exp(w_log))
    a = jax.nn.sigmoid(a0 + jnp.einsum('bldr,drc->bldc', xa, a2))
    g = jax.nn.sigmoid(xg) @ g2
    heads = lambda t: t.reshape(*t.shape[:-1], RWKV_HEADS, RWKV_HEAD_DIM)
    kk = heads(k * k_k)
    kk = kk * lax.rsqrt(jnp.sum(kk * kk, axis=-1, keepdims=True) + 1e-12)
    kd = k[:, :, None, :] * (1.0 + (a - 1.0) * k_a)
    r_h, v_h, kd_h, a_h, dec_h = heads(r), heads(v), heads(kd), heads(a), heads(decay)
    s_f, y_f = rwkv7_scan(s0_f, r_h, dec_h[:, :, 0], kd_h[:, :, 0], v_h, kk, a_h[:, :, 0], reverse=False)
    s_b, y_b = rwkv7_scan(s0_b, r_h, dec_h[:, :, 1], kd_h[:, :, 1], v_h, kk, a_h[:, :, 1], reverse=True)
    y = y_f + y_b
    mean = jnp.mean(y, axis=-1, keepdims=True)
    var = jnp.mean(jnp.square(y - mean), axis=-1, keepdims=True)
    y = (y - mean) * lax.rsqrt(var + LNX_EPS) * heads(lnx_g) + heads(lnx_b)
    coef = jnp.einsum('blhn,bldhn,hn->blh', r_h, kd_h, r_k)
    y = y + coef[..., None] * v_h
    out = (y.reshape(B_, L, C) * g).astype(p.dtype)
    return out, s_f, s_b


def diff_qkv(p, qn_g, kn_g):
    B_, L, _ = p.shape
    q = p[..., :DIFF_QK_COLS].reshape(B_, L, DIFF_HEADS, 2, DIFF_QK_DIM)
    k = p[..., DIFF_QK_COLS:2 * DIFF_QK_COLS].reshape(B_, L, DIFF_HEADS, 2, DIFF_QK_DIM)
    v = p[..., 2 * DIFF_QK_COLS:].reshape(B_, L, DIFF_HEADS, DIFF_V_DIM)
    return rmsnorm(q, qn_g), rmsnorm(k, kn_g), v


def axial_rope(x, rows, cols):
    half = DIFF_QK_DIM // 2
    inv_freq = ROPE_THETA ** (-jnp.arange(0, half, 2, dtype=jnp.float32) / half)
    def rot(xs, pos):
        ang = pos.astype(jnp.float32)[:, None] * inv_freq
        cos = jnp.cos(ang)[None, :, None, None, :]
        sin = jnp.sin(ang)[None, :, None, None, :]
        x1, x2 = jnp.split(xs.astype(jnp.float32), 2, axis=-1)
        return jnp.concatenate([x1 * cos - x2 * sin, x2 * cos + x1 * sin], axis=-1)
    out = jnp.concatenate([rot(x[..., :half], rows), rot(x[..., half:], cols)], axis=-1)
    return out.astype(x.dtype)


def diff_attn_block(q, k, v, lam):
    s = jnp.einsum('bqhmd,bkhmd->bmhqk', q, k).astype(jnp.float32) * DIFF_SCALE
    p = jax.nn.softmax(s, axis=-1)
    a = p[:, 0] - lam * p[:, 1]
    return jnp.einsum('bhqk,bkhd->bqhd', a.astype(v.dtype), v)


def diff_attn_latent(q, k_all, v_all, lam):
    B_, L = q.shape[:2]
    nb = L // Q_BLOCK
    qb = jnp.moveaxis(q.reshape(B_, nb, Q_BLOCK, *q.shape[2:]), 1, 0)
    o = lax.map(lambda qi: diff_attn_block(qi, k_all, v_all, lam), qb)
    return jnp.moveaxis(o, 0, 1).reshape(B_, L, *o.shape[3:])


def diff_out(o, subln_g, lam_init):
    B_, L = o.shape[:2]
    return (rmsnorm(o, subln_g, SUBLN_EPS) * (1.0 - lam_init)).reshape(B_, L, DIFF_DIM)


def merge_branches(p_gate, y_rwkv, y_diff, w_pa, w_pb, w_out):
    gates = jax.nn.sigmoid(p_gate.reshape(*p_gate.shape[:-1], 2, D_MODEL))
    mixed = gates[..., 0, :] * (y_rwkv @ w_pa) + gates[..., 1, :] * (y_diff @ w_pb)
    return mixed @ w_out


def hierarchical_moe(h, wg, bg, we, be, w1, w3, w2):
    B_, L, D = h.shape
    hf = h.reshape(-1, D)
    N = hf.shape[0]
    g_logits = (hf @ wg + bg).astype(jnp.float32)
    p_group = jax.nn.softmax(g_logits, axis=-1)
    g_top = jnp.argmax(g_logits, axis=-1)
    p_g = jnp.take_along_axis(p_group, g_top[:, None], axis=1)[:, 0]
    e_logits = (hf @ we + be).astype(jnp.float32).reshape(N, N_GROUPS, EXPERTS_PER_GROUP)
    e_sel = jnp.take_along_axis(e_logits, g_top[:, None, None], axis=1)[:, 0]
    top_p, top_i = lax.top_k(jax.nn.softmax(e_sel, axis=-1), TOP_K)
    top_p = top_p / jnp.sum(top_p, axis=-1, keepdims=True)
    gate = p_g[:, None] * top_p
    expert = g_top[:, None] * EXPERTS_PER_GROUP + top_i
    A = N * TOP_K
    flat_e = expert.reshape(-1).astype(jnp.int32)
    flat_tok = jnp.repeat(jnp.arange(N, dtype=jnp.int32), TOP_K)
    flat_w = gate.reshape(-1)
    order = jnp.argsort(flat_e)
    se = flat_e[order]
    counts = jnp.bincount(flat_e, length=N_EXPERTS)
    padded = (counts + MOE_BLOCK - 1) // MOE_BLOCK * MOE_BLOCK
    pend = jnp.cumsum(padded)
    pstart = pend - padded
    start = jnp.cumsum(counts) - counts
    slot = pstart[se] + jnp.arange(A, dtype=jnp.int32) - start[se]
    CAP = (A + MOE_BLOCK - 1) // MOE_BLOCK * MOE_BLOCK + N_EXPERTS * MOE_BLOCK
    n_blocks = CAP // MOE_BLOCK
    slot_tok = jnp.full((CAP,), N, jnp.int32).at[slot].set(flat_tok[order])
    slot_w = jnp.zeros((CAP,), jnp.float32).at[slot].set(flat_w[order])
    blk_expert = jnp.minimum(
        jnp.searchsorted(pend, jnp.arange(n_blocks, dtype=jnp.int32) * MOE_BLOCK, side='right'),
        N_EXPERTS - 1)
    h_pad = jnp.concatenate([hf, jnp.zeros((1, D), hf.dtype)], axis=0)

    def run(args):
        tok, e = args
        xb = h_pad[tok]
        return (jax.nn.silu(xb @ w1[e]) * (xb @ w3[e])) @ w2[e]

    out = lax.map(run, (slot_tok.reshape(n_blocks, MOE_BLOCK), blk_expert)).reshape(CAP, D)
    out = out * slot_w[:, None].astype(out.dtype)
    y = jax.ops.segment_sum(out, slot_tok, num_segments=N + 1)[:N]
    return y.reshape(B_, L, D).astype(h.dtype)


def setup_inputs(seed: int = 0) -> dict:
    key = jax.random.key(seed)
    ks = iter(jax.random.split(key, 40))
    D = D_MODEL
    nrm = lambda shape, scale: jax.random.normal(next(ks), shape, jnp.float32) * scale
    uni = lambda shape, lo, hi: jax.random.uniform(next(ks), shape, jnp.float32, lo, hi)
    return {
        "x": nrm((BATCH, SEQ, D), 1.0),
        "c": nrm((BATCH, D), 1.0),
        "ctx": nrm((BATCH, CTX_LEN, D), 1.0),
        "c_ctx": nrm((D,), 1.0),
        "ada_w": nrm((DEPTH, D, 6 * D), 0.3 * D ** -0.5),
        "ada_b": nrm((DEPTH, 6 * D), 0.02),
        "norm1_g": 1.0 + nrm((DEPTH, D), 0.02),
        "norm2_g": 1.0 + nrm((DEPTH, D), 0.02),
        "w_in": nrm((DEPTH, D, N_IN), D ** -0.5),
        "shift_mu": uni((DEPTH, RWKV_COLS), 0.0, 1.0),
        "rwkv_w0": uni((DEPTH, 2, RWKV_DIM), -6.0, -1.0),
        "rwkv_w2": nrm((DEPTH, 2, DECAY_LORA, RWKV_DIM), 0.5 * DECAY_LORA ** -0.5),
        "rwkv_a0": nrm((DEPTH, 2, RWKV_DIM), 0.1),
        "rwkv_a2": nrm((DEPTH, 2, AAA_LORA, RWKV_DIM), 0.5 * AAA_LORA ** -0.5),
        "rwkv_g2": nrm((DEPTH, GATE_LORA, RWKV_DIM), GATE_LORA ** -0.5),
        "rwkv_k_k": 0.85 + nrm((DEPTH, RWKV_DIM), 0.02),
        "rwkv_k_a": 1.0 + nrm((DEPTH, RWKV_DIM), 0.02),
        "rwkv_r_k": nrm((DEPTH, RWKV_HEADS, RWKV_HEAD_DIM), 0.1),
        "rwkv_lnx_g": 1.0 + nrm((DEPTH, RWKV_DIM), 0.02),
        "rwkv_lnx_b": nrm((DEPTH, RWKV_DIM), 0.02),
        "qn_g": 1.0 + nrm((DEPTH, DIFF_QK_DIM), 0.02),
        "kn_g": 1.0 + nrm((DEPTH, DIFF_QK_DIM), 0.02),
        "diff_lambda": nrm((DEPTH, 4, DIFF_QK_DIM), 0.1),
        "subln_g": 1.0 + nrm((DEPTH, DIFF_V_DIM), 0.02),
        "w_pa": nrm((DEPTH, RWKV_DIM, D), RWKV_DIM ** -0.5),
        "w_pb": nrm((DEPTH, DIFF_DIM, D), DIFF_DIM ** -0.5),
        "w_out": nrm((DEPTH, D, D), D ** -0.5),
        "router_g_w": nrm((DEPTH, D, N_GROUPS), D ** -0.5),
        "router_g_b": nrm((DEPTH, N_GROUPS), 0.01),
        "router_e_w": nrm((DEPTH, D, N_EXPERTS), D ** -0.5),
        "router_e_b": nrm((DEPTH, N_EXPERTS), 0.01),
        "exp_w1": nrm((DEPTH, N_EXPERTS, D, D_EXPERT), D ** -0.5),
        "exp_w3": nrm((DEPTH, N_EXPERTS, D, D_EXPERT), D ** -0.5),
        "exp_w2": nrm((DEPTH, N_EXPERTS, D_EXPERT, D), D_EXPERT ** -0.5),
    }


def reference(x, c, ctx, c_ctx, ada_w, ada_b, norm1_g, norm2_g, w_in, shift_mu,
              rwkv_w0, rwkv_w2, rwkv_a0, rwkv_a2, rwkv_g2, rwkv_k_k, rwkv_k_a, rwkv_r_k,
              rwkv_lnx_g, rwkv_lnx_b, qn_g, kn_g, diff_lambda, subln_g, w_pa, w_pb, w_out,
              router_g_w, router_g_b, router_e_w, router_e_b, exp_w1, exp_w3, exp_w2):
    B_, L, _ = x.shape
    ROWS = L // GRID_W
    rows = jnp.repeat(jnp.arange(ROWS, dtype=jnp.int32), GRID_W)
    cols = jnp.tile(jnp.arange(GRID_W, dtype=jnp.int32), ROWS)
    o_diff = RWKV_COLS
    o_gate = RWKV_COLS + DIFF_COLS
    cx = ctx
    for l in range(DEPTH):
        lam_init = 0.8 - 0.6 * math.exp(-0.3 * l)
        lv = diff_lambda[l].astype(jnp.float32)
        lam = jnp.exp(jnp.sum(lv[0] * lv[1])) - jnp.exp(jnp.sum(lv[2] * lv[3])) + lam_init
        mod_x = jax.nn.silu(c) @ ada_w[l] + ada_b[l]
        mod_c = jax.nn.silu(c_ctx) @ ada_w[l] + ada_b[l]
        sh1, sc1, gt1, sh2, sc2, gt2 = jnp.split(mod_x[:, None, :], 6, axis=-1)
        csh1, csc1, cgt1, csh2, csc2, cgt2 = jnp.split(mod_c, 6, axis=-1)
        hx = rmsnorm(x, norm1_g[l]) * (1.0 + sc1) + sh1
        hc = rmsnorm(cx, norm1_g[l]) * (1.0 + csc1) + csh1
        px = hx @ w_in[l]
        pc = hc @ w_in[l]
        rwkv_p = (rwkv_w0[l], rwkv_w2[l], rwkv_a0[l], rwkv_a2[l], rwkv_g2[l], rwkv_k_k[l],
                  rwkv_k_a[l], rwkv_r_k[l], rwkv_lnx_g[l], rwkv_lnx_b[l])
        s0 = jnp.zeros((B_, RWKV_HEADS, RWKV_HEAD_DIM, RWKV_HEAD_DIM), jnp.float32)
        yc_r, s_f, s_b = rwkv_branch(centred_shift(pc[..., :RWKV_COLS], shift_mu[l]), s0, s0, *rwkv_p)
        yx_r, _, _ = rwkv_branch(centred_shift(px[..., :RWKV_COLS], shift_mu[l]), s_f, s_b, *rwkv_p)
        qx, kx, vx = diff_qkv(px[..., o_diff:o_gate], qn_g[l], kn_g[l])
        qx = axial_rope(qx, rows, cols)
        kx = axial_rope(kx, rows, cols)
        qc, kc, vc = diff_qkv(pc[..., o_diff:o_gate], qn_g[l], kn_g[l])
        k_all = jnp.concatenate([kc, kx], axis=1)
        v_all = jnp.concatenate([vc, vx], axis=1)
        yx_d = diff_out(diff_attn_latent(qx, k_all, v_all, lam), subln_g[l], lam_init)
        mix_x = merge_branches(px[..., o_gate:], yx_r, yx_d, w_pa[l], w_pb[l], w_out[l])
        x_new = x + gt1 * mix_x
        moe_p = (router_g_w[l], router_g_b[l], router_e_w[l], router_e_b[l], exp_w1[l], exp_w3[l], exp_w2[l])
        h2 = rmsnorm(x_new, norm2_g[l]) * (1.0 + sc2) + sh2
        x_new = x_new + gt2 * hierarchical_moe(h2, *moe_p)
        if l < DEPTH - 1:
            yc_d = diff_out(diff_attn_block(qc, kc, vc, lam), subln_g[l], lam_init)
            mix_c = merge_branches(pc[..., o_gate:], yc_r, yc_d, w_pa[l], w_pb[l], w_out[l])
            cx = cx + cgt1 * mix_c
            hc2 = rmsnorm(cx, norm2_g[l]) * (1.0 + csc2) + csh2
            cx = cx + cgt2 * hierarchical_moe(hc2, *moe_p)
        x = x_new
    return x
```

```python
import functools
import math

import jax
import jax.numpy as jnp
from jax import lax
from jax.experimental import pallas as pl
from jax.experimental.pallas import tpu as pltpu

F32 = jnp.float32
BF16 = jnp.bfloat16
HIGHEST = lax.Precision.HIGHEST

D_MODEL = 2048
GRID_W = 64
RWKV_HEADS = 16
RWKV_HEAD_DIM = 64
RWKV_DIM = RWKV_HEADS * RWKV_HEAD_DIM
DECAY_LORA = 96
AAA_LORA = 96
GATE_LORA = 256
RWKV_COLS = 3 * RWKV_DIM + 2 * DECAY_LORA + 2 * AAA_LORA + GATE_LORA
DIFF_HEADS = 8
DIFF_QK_DIM = 64
DIFF_V_DIM = 2 * DIFF_QK_DIM
DIFF_DIM = DIFF_HEADS * DIFF_V_DIM
DIFF_QK_COLS = DIFF_HEADS * 2 * DIFF_QK_DIM
DIFF_COLS = 2 * DIFF_QK_COLS + DIFF_DIM
DIFF_SCALE = DIFF_QK_DIM ** -0.5
ROPE_THETA = 10000.0
GATE_COLS = 2 * D_MODEL
N_GROUPS = 4
EXPERTS_PER_GROUP = 8
N_EXPERTS = N_GROUPS * EXPERTS_PER_GROUP
TOP_K = 2
D_EXPERT = 512
NORM_EPS = 1e-6
SUBLN_EPS = 1e-5
LNX_EPS = 64e-5

LANES = 128
SUBLANES = 8
VMEM_LIMIT_BYTES = 56 * 1024 * 1024

LORA_PAD = LANES
RWKV_PCOLS = 3 * RWKV_DIM + 4 * LORA_PAD + GATE_LORA
CHUNK = 64
ROUTER_PAD = LANES
MOE_TILE = 512


def _cparams(*sem):
    return pltpu.CompilerParams(dimension_semantics=sem, vmem_limit_bytes=VMEM_LIMIT_BYTES)


def _sigmoid(x):
    return 1.0 / (1.0 + jnp.exp(-x))


def _dot(a, b, dims=(((1,), (0,)), ((), ()))):
    return lax.dot_general(a.astype(BF16), b.astype(BF16), dims, preferred_element_type=F32)


def _dot_f32(a, b, dims=(((1,), (0,)), ((), ()))):
    return lax.dot_general(a, b, dims, precision=HIGHEST, preferred_element_type=F32)


_NT = (((1,), (1,)), ((), ()))
_TN = (((0,), (0,)), ((), ()))


def _group_ones(width, group):
    r = lax.broadcasted_iota(jnp.int32, (width, width), 0) // group
    c = lax.broadcasted_iota(jnp.int32, (width, width), 1) // group
    return (r == c).astype(F32)


def _group_sum(x, group):
    ones = _group_ones(LANES, group)
    parts = [_dot_f32(x[:, j * LANES:(j + 1) * LANES], ones) for j in range(x.shape[1] // LANES)]
    return parts[0] if len(parts) == 1 else jnp.concatenate(parts, axis=1)


def _mod_kernel(c_ref, w_ref, b_ref, o_ref):
    c = c_ref[...]
    o_ref[...] = _dot_f32(c * _sigmoid(c), w_ref[...]) + b_ref[...]


def _modulation(cm, ada_w, ada_b):
    rows, d = cm.shape
    n = ada_w.shape[1]
    tn = 1536
    return pl.pallas_call(
        _mod_kernel,
        grid=(n // tn,),
        in_specs=[pl.BlockSpec((rows, d), lambda j: (0, 0)),
                  pl.BlockSpec((d, tn), lambda j: (0, j)),
                  pl.BlockSpec((1, tn), lambda j: (0, j))],
        out_specs=pl.BlockSpec((rows, tn), lambda j: (0, j)),
        out_shape=jax.ShapeDtypeStruct((rows, n), F32),
        compiler_params=_cparams("parallel"),
        name="modulation",
    )(cm, ada_w, ada_b.reshape(1, n))


def _inproj_kernel(x_ref, g_ref, sc_ref, sh_ref, w_ref, o_ref, h_ref):
    @pl.when(pl.program_id(2) == 0)
    def _():
        x = x_ref[0]
        y = x * lax.rsqrt(jnp.mean(x * x, axis=-1, keepdims=True) + NORM_EPS) * g_ref[...]
        h_ref[...] = (y * (1.0 + sc_ref[0]) + sh_ref[0]).astype(BF16)

    o_ref[0] = jnp.dot(h_ref[...], w_ref[...], preferred_element_type=F32).astype(o_ref.dtype)


def _pick(n, prefs):
    for t in prefs:
        if n % t == 0:
            return t
    return n


def _norm_proj(x, g, sc, sh, w, out_dtype):
    b, l, d = x.shape
    n = w.shape[1]
    tm = _pick(l, (1024, 512, 256, 128))
    tn = _pick(n, (640, 512, 384, 256, 128))
    return pl.pallas_call(
        _inproj_kernel,
        grid=(b, l // tm, n // tn),
        in_specs=[pl.BlockSpec((1, tm, d), lambda bi, i, j: (bi, i, 0)),
                  pl.BlockSpec((1, d), lambda bi, i, j: (0, 0)),
                  pl.BlockSpec((1, 1, d), lambda bi, i, j: (bi, 0, 0)),
                  pl.BlockSpec((1, 1, d), lambda bi, i, j: (bi, 0, 0)),
                  pl.BlockSpec((d, tn), lambda bi, i, j: (0, j))],
        out_specs=pl.BlockSpec((1, tm, tn), lambda bi, i, j: (bi, i, j)),
        out_shape=jax.ShapeDtypeStruct((b, l, n), out_dtype),
        scratch_shapes=[pltpu.VMEM((tm, d), BF16)],
        compiler_params=_cparams("parallel", "parallel", "arbitrary"),
        name="norm_proj",
    )(x, g.reshape(1, d), sc, sh, w)


def _tri_inverse(a):
    n = a.shape[0]
    eye = (lax.broadcasted_iota(jnp.int32, (n, n), 0) == lax.broadcasted_iota(jnp.int32, (n, n), 1)).astype(F32)
    x = eye + a
    p = _dot_f32(a, a)
    steps = int(math.log2(n)) - 1
    for s in range(steps):
        if s < steps - 1:
            xp = _dot_f32(jnp.concatenate([x, p], axis=0), p)
            x = x + xp[:n]
            p = xp[n:]
        else:
            x = x + _dot_f32(x, p)
    return x


def _rwkv_chunk_kernel(p_ref, pp_ref, pn_ref, mu_ref, w0_ref, w2_ref, a0_ref, a2_ref, g2_ref, kk_ref, ka_ref,
                       rk_ref, q_ref, y0_ref, m_ref, n_ref, bonus_ref, gate_ref):
    i = pl.program_id(1)
    last = pl.num_programs(1) - 1
    c = RWKV_DIM
    hd = RWKV_HEAD_DIM
    p = p_ref[0]
    row = lax.broadcasted_iota(jnp.int32, (CHUNK, 1), 0)
    prev_row = jnp.where(i == 0, 0.0, pp_ref[0, SUBLANES - 1:SUBLANES, :])
    next_row = jnp.where(i == last, 0.0, pn_ref[0, 0:1, :])
    prev = jnp.where(row == 0, prev_row, pltpu.roll(p, 1, axis=0))
    nxt = jnp.where(row == CHUNK - 1, next_row, pltpu.roll(p, CHUNK - 1, axis=0))
    ps = p + mu_ref[...] * (0.5 * (prev + nxt) - p)

    r, k, v = ps[:, :c], ps[:, c:2 * c], ps[:, 2 * c:3 * c]
    o = 3 * c
    xw = (ps[:, o:o + LORA_PAD], ps[:, o + LORA_PAD:o + 2 * LORA_PAD])
    xa = (ps[:, o + 2 * LORA_PAD:o + 3 * LORA_PAD], ps[:, o + 3 * LORA_PAD:o + 4 * LORA_PAD])
    xg = ps[:, o + 4 * LORA_PAD:]

    gate_ref[0] = _dot(_sigmoid(xg), g2_ref[...])
    kk = k * kk_ref[...]
    kk = kk * lax.rsqrt(_group_sum(kk * kk, hd) + 1e-12)

    tr = lax.broadcasted_iota(jnp.int32, (CHUNK, CHUNK), 0)
    tc = lax.broadcasted_iota(jnp.int32, (CHUNK, CHUNK), 1)
    eye = tr == tc
    tr2 = lax.broadcasted_iota(jnp.int32, (CHUNK, 2 * CHUNK), 0)
    tc2 = lax.broadcasted_iota(jnp.int32, (CHUNK, 2 * CHUNK), 1) % CHUNK
    kd_sum = jnp.zeros_like(k)
    for d in range(2):
        before = (tc < tr) if d == 0 else (tc > tr)
        upto = (tc <= tr) if d == 0 else (tc >= tr)
        upto2 = (tc2 <= tr2) if d == 0 else (tc2 >= tr2)
        z = w0_ref[d:d + 1, :] + _dot(jnp.tanh(xw[d]), w2_ref[d])
        w_log = -(jnp.maximum(-z, 0.0) + jnp.log(1.0 + jnp.exp(-jnp.abs(z)))) - 0.5
        logw = -jnp.exp(w_log)
        a = _sigmoid(a0_ref[d:d + 1, :] + _dot(xa[d], a2_ref[d]))
        kd = k * (1.0 + (a - 1.0) * ka_ref[...])
        kd_sum = kd_sum + kd
        cum = _dot_f32(upto.astype(F32), logw)
        total = cum[CHUNK - 1:CHUNK, :] if d == 0 else cum[0:1, :]
        e_excl = jnp.exp(cum - logw)
        e_incl = jnp.exp(cum)
        e_neg = jnp.exp(-cum)
        e_rest = jnp.exp(total - cum)
        p_total = jnp.exp(total)
        al = -kk * e_excl
        rt = r * e_incl
        beta = kk * a
        bt = beta * e_neg
        kt = kd * e_neg
        bh = beta * e_rest
        kh = kd * e_rest
        for h in range(RWKV_HEADS):
            hs = slice(h * hd, (h + 1) * hd)
            a_h, r_h, v_h = al[:, hs], rt[:, hs], v[:, hs]
            g = _dot_f32(jnp.concatenate([a_h, r_h], axis=0), jnp.concatenate([bt[:, hs], kt[:, hs]], axis=0), _NT)
            a_ab = jnp.where(before, g[:CHUNK, :CHUNK], 0.0)
            a_ak = jnp.where(before, g[:CHUNK, CHUNK:], 0.0)
            a_rbk = jnp.where(upto2, g[CHUNK:, :], 0.0)
            t_inv = _tri_inverse(a_ab)
            wu = _dot_f32(t_inv, jnp.concatenate([a_h, _dot_f32(a_ak, v_h)], axis=1))
            zmat = jnp.concatenate([wu, jnp.concatenate([jnp.zeros_like(v_h), v_h], axis=1)], axis=0)
            qy = _dot_f32(a_rbk, zmat)
            mn = _dot_f32(zmat, jnp.concatenate([bh[:, hs], kh[:, hs]], axis=0), _TN)
            q_ref[0, d, :, hs] = r_h + qy[:, :hd]
            y0_ref[0, d, :, hs] = qy[:, hd:]
            m_ref[0, d, 0, :, hs] = jnp.where(eye, p_total[:, hs], 0.0) + mn[:hd]
            n_ref[0, d, 0, :, hs] = mn[hd:]
    bonus_ref[0] = _group_sum(r * kd_sum * rk_ref[...], hd) * v


def _rwkv_chunk_ops(p, params):
    b, l, pc = p.shape
    nc = l // CHUNK
    c = RWKV_DIM
    hb = CHUNK // SUBLANES
    nb8 = l // SUBLANES
    mu, w0, w2, a0, a2, g2, k_k, k_a, r_k = params
    const = lambda shape: pl.BlockSpec(shape, lambda bi, i: (0,) * len(shape))
    tok = lambda: pl.BlockSpec((1, CHUNK, c), lambda bi, i: (bi, i, 0))
    return pl.pallas_call(
        _rwkv_chunk_kernel,
        grid=(b, nc),
        in_specs=[pl.BlockSpec((1, CHUNK, pc), lambda bi, i: (bi, i, 0)),
                  pl.BlockSpec((1, SUBLANES, pc), lambda bi, i: (bi, jnp.maximum(i * hb - 1, 0), 0)),
                  pl.BlockSpec((1, SUBLANES, pc), lambda bi, i: (bi, jnp.minimum((i + 1) * hb, nb8 - 1), 0)),
                  const((1, pc)), const((2, c)), const((2, LORA_PAD, c)), const((2, c)), const((2, LORA_PAD, c)),
                  const((GATE_LORA, c)), const((1, c)), const((1, c)), const((1, c))],
        out_specs=[pl.BlockSpec((1, 2, CHUNK, c), lambda bi, i: (bi, 0, i, 0)),
                   pl.BlockSpec((1, 2, CHUNK, c), lambda bi, i: (bi, 0, i, 0)),
                   pl.BlockSpec((1, 2, 1, RWKV_HEAD_DIM, c), lambda bi, i: (bi, 0, i, 0, 0)),
                   pl.BlockSpec((1, 2, 1, RWKV_HEAD_DIM, c), lambda bi, i: (bi, 0, i, 0, 0)),
                   tok(), tok()],
        out_shape=[jax.ShapeDtypeStruct((b, 2, l, c), F32), jax.ShapeDtypeStruct((b, 2, l, c), F32),
                   jax.ShapeDtypeStruct((b, 2, nc, RWKV_HEAD_DIM, c), F32),
                   jax.ShapeDtypeStruct((b, 2, nc, RWKV_HEAD_DIM, c), F32),
                   jax.ShapeDtypeStruct((b, l, c), F32), jax.ShapeDtypeStruct((b, l, c), F32)],
        compiler_params=_cparams("parallel", "parallel"),
        name="rwkv_chunk_ops",
    )(p, p, p, mu, w0, w2, a0, a2, g2, k_k, k_a, r_k)


def _rwkv_sweep_kernel(qf_ref, y0f_ref, mf_ref, nf_ref, qb_ref, y0b_ref, mb_ref, nb_ref, s0_ref,
                       yf_ref, yb_ref, sfin_ref, s_ref):
    j = pl.program_id(1)
    hd = RWKV_HEAD_DIM

    @pl.when(j == 0)
    def _():
        s_ref[...] = s0_ref[0]

    dirs = ((qf_ref, y0f_ref, mf_ref, nf_ref, yf_ref), (qb_ref, y0b_ref, mb_ref, nb_ref, yb_ref))
    for d, (q_ref, y0_ref, m_ref, n_ref, y_ref) in enumerate(dirs):
        for h in range(RWKV_HEADS):
            hs = slice(h * hd, (h + 1) * hd)
            s = s_ref[d, :, hs]
            y_ref[0, 0, :, hs] = y0_ref[0, 0, :, hs] + _dot_f32(q_ref[0, 0, :, hs], s, _NT)
            s_ref[d, :, hs] = _dot_f32(s, m_ref[0, 0, 0, :, hs]) + n_ref[0, 0, 0, :, hs]

    @pl.when(j == pl.num_programs(1) - 1)
    def _():
        sfin_ref[0] = s_ref[...]


def _rwkv_sweep(q, y0, m, n, s0):
    b, _, l, c = q.shape
    nc = l // CHUNK
    hd = RWKV_HEAD_DIM
    tokf = lambda: pl.BlockSpec((1, 1, CHUNK, c), lambda bi, j: (bi, 0, j, 0))
    tokb = lambda: pl.BlockSpec((1, 1, CHUNK, c), lambda bi, j: (bi, 1, nc - 1 - j, 0))
    opf = lambda: pl.BlockSpec((1, 1, 1, hd, c), lambda bi, j: (bi, 0, j, 0, 0))
    opb = lambda: pl.BlockSpec((1, 1, 1, hd, c), lambda bi, j: (bi, 1, nc - 1 - j, 0, 0))
    return pl.pallas_call(
        _rwkv_sweep_kernel,
        grid=(b, nc),
        in_specs=[tokf(), tokf(), opf(), opf(), tokb(), tokb(), opb(), opb(),
                  pl.BlockSpec((1, 2, hd, c), lambda bi, j: (bi, 0, 0, 0))],
        out_specs=[tokf(), tokb(), pl.BlockSpec((1, 2, hd, c), lambda bi, j: (bi, 0, 0, 0))],
        out_shape=[jax.ShapeDtypeStruct((b, 2, l, c), F32), jax.ShapeDtypeStruct((b, 2, l, c), F32),
                   jax.ShapeDtypeStruct((b, 2, hd, c), F32)],
        scratch_shapes=[pltpu.VMEM((2, hd, c), F32)],
        compiler_params=_cparams("parallel", "arbitrary"),
        name="rwkv_sweep",
    )(q, y0, m, n, q, y0, m, n, s0)


def _diff_prep_kernel(p_ref, cos_ref, sin_ref, qg_ref, kg_ref, q_ref, k_ref, v_ref, *, rope):
    lane = lax.broadcasted_iota(jnp.int32, (1, LANES), 1)
    first = (lane % 32) < 16
    for hd in range(DIFF_HEADS):
        for off, g_ref, o_ref, scale in ((0, qg_ref, q_ref, DIFF_SCALE), (DIFF_QK_COLS, kg_ref, k_ref, 1.0)):
            cs = slice(hd * LANES, (hd + 1) * LANES)
            xb = p_ref[0, :, off + hd * LANES:off + (hd + 1) * LANES]
            ms = _group_sum(xb * xb, DIFF_QK_DIM) * (1.0 / DIFF_QK_DIM)
            y = xb * lax.rsqrt(ms + NORM_EPS) * g_ref[...]
            if rope:
                swapped = jnp.where(first, pltpu.roll(y, LANES - 16, axis=1), pltpu.roll(y, 16, axis=1))
                y = y * cos_ref[...] + swapped * sin_ref[...]
            o_ref[0, :, cs] = (y * scale).astype(BF16)
    v_ref[0] = p_ref[0, :, 2 * DIFF_QK_COLS:].astype(BF16)


def _diff_prep(p, cos_t, sin_t, qg, kg, rope):
    b, l, pc = p.shape
    t = _pick(l, (512, 256, 128))
    tok = lambda: pl.BlockSpec((1, t, DIFF_DIM), lambda bi, i: (bi, i, 0))
    shp = jax.ShapeDtypeStruct((b, l, DIFF_DIM), BF16)
    return pl.pallas_call(
        functools.partial(_diff_prep_kernel, rope=rope),
        grid=(b, l // t),
        in_specs=[pl.BlockSpec((1, t, pc), lambda bi, i: (bi, i, 0)),
                  pl.BlockSpec((t, LANES), lambda bi, i: (i, 0)),
                  pl.BlockSpec((t, LANES), lambda bi, i: (i, 0)),
                  pl.BlockSpec((1, LANES), lambda bi, i: (0, 0)),
                  pl.BlockSpec((1, LANES), lambda bi, i: (0, 0))],
        out_specs=[tok(), tok(), tok()],
        out_shape=[shp, shp, shp],
        compiler_params=_cparams("parallel", "parallel"),
        name="diff_prep_rope" if rope else "diff_prep",
    )(p, cos_t, sin_t, qg, kg)


def _flash_kernel(lam_ref, sg_ref, q_ref, kc_ref, vc_ref, k_ref, v_ref, o_ref, m_ref, l_ref, acc_ref):
    j = pl.program_id(3)
    qd = DIFF_QK_DIM

    @pl.when(j == 0)
    def _():
        m_ref[...] = jnp.full(m_ref.shape, -jnp.inf, F32)
        l_ref[...] = jnp.zeros(l_ref.shape, F32)
        acc_ref[...] = jnp.zeros(acc_ref.shape, F32)

    def absorb(k, v):
        for mp in range(2):
            ms = slice(mp * qd, (mp + 1) * qd)
            s = lax.dot_general(q_ref[0, :, ms], k[:, ms], _NT, preferred_element_type=F32)
            m_prev = m_ref[mp]
            m_new = jnp.maximum(m_prev, jnp.max(s, axis=-1, keepdims=True))
            alpha = jnp.exp(m_prev - m_new)
            pr = jnp.exp(s - m_new)
            l_ref[mp] = alpha * l_ref[mp] + jnp.sum(pr, axis=-1, keepdims=True)
            acc_ref[mp] = alpha * acc_ref[mp] + jnp.dot(pr.astype(BF16), v, preferred_element_type=F32)
            m_ref[mp] = m_new

    @pl.when(j == 0)
    def _():
        absorb(kc_ref[0], vc_ref[0])

    absorb(k_ref[0], v_ref[0])

    @pl.when(j == pl.num_programs(3) - 1)
    def _():
        o = acc_ref[0] / l_ref[0] - lam_ref[...] * (acc_ref[1] / l_ref[1])
        o = o * lax.rsqrt(jnp.mean(o * o, axis=-1, keepdims=True) + SUBLN_EPS)
        o_ref[0] = (o * sg_ref[...]).astype(o_ref.dtype)


def _diff_attention(q, kc, vc, k, v, lam_vec, sg_vec):
    b, l, _ = q.shape
    lc = kc.shape[1]
    tq = _pick(l, (512, 256, 128))
    tk = _pick(l, (512, 256, 128))
    return pl.pallas_call(
        _flash_kernel,
        grid=(b, DIFF_HEADS, l // tq, l // tk),
        in_specs=[pl.BlockSpec((1, LANES), lambda bi, h, i, j: (0, 0)),
                  pl.BlockSpec((1, LANES), lambda bi, h, i, j: (0, 0)),
                  pl.BlockSpec((1, tq, LANES), lambda bi, h, i, j: (bi, i, h)),
                  pl.BlockSpec((1, lc, LANES), lambda bi, h, i, j: (bi, 0, h)),
                  pl.BlockSpec((1, lc, LANES), lambda bi, h, i, j: (bi, 0, h)),
                  pl.BlockSpec((1, tk, LANES), lambda bi, h, i, j: (bi, j, h)),
                  pl.BlockSpec((1, tk, LANES), lambda bi, h, i, j: (bi, j, h))],
        out_specs=pl.BlockSpec((1, tq, LANES), lambda bi, h, i, j: (bi, i, h)),
        out_shape=jax.ShapeDtypeStruct((b, l, DIFF_DIM), BF16),
        scratch_shapes=[pltpu.VMEM((2, tq, 1), F32), pltpu.VMEM((2, tq, 1), F32),
                        pltpu.VMEM((2, tq, DIFF_V_DIM), F32)],
        compiler_params=_cparams("parallel", "parallel", "parallel", "arbitrary"),
        name="diff_flash",
    )(lam_vec, sg_vec, q, kc, vc, k, v)


def _merge_kernel(yf_ref, yb_ref, bonus_ref, gate_ref, yd_ref, pg_ref, lg_ref, lb_ref, wpa_ref, wpb_ref, o_ref):
    hd = RWKV_HEAD_DIM
    y = yf_ref[0, 0] + yb_ref[0, 0]
    dev = y - _group_sum(y, hd) * (1.0 / hd)
    var = _group_sum(dev * dev, hd) * (1.0 / hd)
    yn = dev * lax.rsqrt(var + LNX_EPS) * lg_ref[...] + lb_ref[...]
    y_rwkv = (yn + bonus_ref[0]) * gate_ref[0]
    a = _dot(y_rwkv, wpa_ref[...])
    bb = jnp.dot(yd_ref[0], wpb_ref[...], preferred_element_type=F32)
    ga = _sigmoid(pg_ref[0, :, :D_MODEL])
    gb = _sigmoid(pg_ref[0, :, D_MODEL:])
    o_ref[0] = (ga * a + gb * bb).astype(BF16)


def _merge(y_sweep, bonus, gate, y_diff, p_gate, lnx_g, lnx_b, w_pa, w_pb):
    b, l, c = bonus.shape
    d = D_MODEL
    tm = _pick(l, (256, 128))
    const = lambda shape: pl.BlockSpec(shape, lambda bi, i: (0,) * len(shape))
    tok = lambda w: pl.BlockSpec((1, tm, w), lambda bi, i: (bi, i, 0))
    return pl.pallas_call(
        _merge_kernel,
        grid=(b, l // tm),
        in_specs=[pl.BlockSpec((1, 1, tm, c), lambda bi, i: (bi, 0, i, 0)),
                  pl.BlockSpec((1, 1, tm, c), lambda bi, i: (bi, 1, i, 0)),
                  tok(c), tok(c), tok(DIFF_DIM), tok(GATE_COLS),
                  const((1, c)), const((1, c)), const((c, d)), const((DIFF_DIM, d))],
        out_specs=tok(d),
        out_shape=jax.ShapeDtypeStruct((b, l, d), BF16),
        compiler_params=_cparams("parallel", "parallel"),
        name="merge",
    )(y_sweep[0], y_sweep[1], bonus, gate, y_diff, p_gate, lnx_g, lnx_b, w_pa, w_pb)


def _outproj_kernel(mx_ref, x_ref, gt_ref, g_ref, sc_ref, sh_ref, wo_ref, wr_ref, br_ref, xn_ref, h_ref, lg_ref):
    mix = jnp.dot(mx_ref[0], wo_ref[...], preferred_element_type=F32)
    xn = x_ref[0] + gt_ref[0] * mix
    xn_ref[0] = xn
    y = xn * lax.rsqrt(jnp.mean(xn * xn, axis=-1, keepdims=True) + NORM_EPS) * g_ref[...]
    h = y * (1.0 + sc_ref[0]) + sh_ref[0]
    h_ref[0] = h.astype(BF16)
    lg_ref[0] = _dot_f32(h, wr_ref[...]) + br_ref[...]


def _outproj(mixed, x, gt1, g2, sc2, sh2, w_out, w_router, b_router):
    b, l, d = x.shape
    tm = _pick(l, (256, 128))
    const = lambda shape: pl.BlockSpec(shape, lambda bi, i: (0,) * len(shape))
    tok = lambda w: pl.BlockSpec((1, tm, w), lambda bi, i: (bi, i, 0))
    per_b = lambda: pl.BlockSpec((1, 1, d), lambda bi, i: (bi, 0, 0))
    return pl.pallas_call(
        _outproj_kernel,
        grid=(b, l // tm),
        in_specs=[tok(d), tok(d), per_b(), const((1, d)), per_b(), per_b(),
                  const((d, d)), const((d, ROUTER_PAD)), const((1, ROUTER_PAD))],
        out_specs=[tok(d), tok(d), tok(ROUTER_PAD)],
        out_shape=[jax.ShapeDtypeStruct((b, l, d), F32), jax.ShapeDtypeStruct((b, l, d), BF16),
                   jax.ShapeDtypeStruct((b, l, ROUTER_PAD), F32)],
        compiler_params=_cparams("parallel", "parallel"),
        name="outproj_router",
    )(mixed, x, gt1, g2, sc2, sh2, w_out, w_router, b_router)


def _moe_kernel(be_ref, nu_ref, x_ref, sw_ref, w1_ref, w3_ref, w2_ref, o_ref):
    i = pl.program_id(0)

    @pl.when(i < nu_ref[0])
    def _():
        xb = x_ref[...]
        u = jnp.dot(xb, w1_ref[0], preferred_element_type=F32)
        g = jnp.dot(xb, w3_ref[0], preferred_element_type=F32)
        hmid = (u * _sigmoid(u) * g).astype(BF16)
        o_ref[...] = jnp.dot(hmid, w2_ref[0], preferred_element_type=F32) * sw_ref[...]

    @pl.when(i >= nu_ref[0])
    def _():
        o_ref[...] = jnp.zeros(o_ref.shape, F32)


def _moe_ffn(xs, sw, blk_expert, n_used, w1, w3, w2):
    cap, d = xs.shape
    nb = cap // MOE_TILE
    grid_spec = pltpu.PrefetchScalarGridSpec(
        num_scalar_prefetch=2,
        grid=(nb,),
        in_specs=[pl.BlockSpec((MOE_TILE, d), lambda i, be, nu: (i, 0)),
                  pl.BlockSpec((MOE_TILE, 1), lambda i, be, nu: (i, 0)),
                  pl.BlockSpec((1, d, D_EXPERT), lambda i, be, nu: (be[i], 0, 0)),
                  pl.BlockSpec((1, d, D_EXPERT), lambda i, be, nu: (be[i], 0, 0)),
                  pl.BlockSpec((1, D_EXPERT, d), lambda i, be, nu: (be[i], 0, 0))],
        out_specs=pl.BlockSpec((MOE_TILE, d), lambda i, be, nu: (i, 0)),
    )
    return pl.pallas_call(
        _moe_kernel,
        grid_spec=grid_spec,
        out_shape=jax.ShapeDtypeStruct((cap, d), F32),
        compiler_params=_cparams("arbitrary"),
        name="moe_ffn",
    )(blk_expert, n_used, xs, sw, w1, w3, w2)


def _final_kernel(x_ref, gt_ref, y0_ref, y1_ref, o_ref):
    o_ref[0] = x_ref[0] + gt_ref[0] * (y0_ref[0] + y1_ref[0])


def _final(x_new, gt2, y0, y1):
    b, l, d = x_new.shape
    tm = _pick(l, (512, 256, 128))
    tok = lambda: pl.BlockSpec((1, tm, d), lambda bi, i: (bi, i, 0))
    return pl.pallas_call(
        _final_kernel,
        grid=(b, l // tm),
        in_specs=[tok(), pl.BlockSpec((1, 1, d), lambda bi, i: (bi, 0, 0)), tok(), tok()],
        out_specs=tok(),
        out_shape=jax.ShapeDtypeStruct((b, l, d), F32),
        compiler_params=_cparams("parallel", "parallel"),
        name="moe_residual",
    )(x_new, gt2, y0, y1)


def _pad_lora_cols(w, widths):
    parts, o = [], 0
    for wd in widths:
        blk = w[..., o:o + wd]
        parts.append(jnp.pad(blk, [(0, 0)] * (w.ndim - 1) + [(0, LORA_PAD - wd)]))
        o += wd
    return jnp.concatenate(parts, axis=-1)


def _rope_tables(l):
    half = DIFF_QK_DIM // 2
    inv_freq = ROPE_THETA ** (-jnp.arange(0, half, 2, dtype=F32) / half)
    t = jnp.arange(l, dtype=jnp.int32)
    rows = (t // GRID_W).astype(F32)[:, None] * inv_freq
    cols = (t % GRID_W).astype(F32)[:, None] * inv_freq
    cos64 = jnp.concatenate([jnp.cos(rows), jnp.cos(rows), jnp.cos(cols), jnp.cos(cols)], axis=1)
    sin64 = jnp.concatenate([-jnp.sin(rows), jnp.sin(rows), -jnp.sin(cols), jnp.sin(cols)], axis=1)
    return jnp.tile(cos64, (1, 2)), jnp.tile(sin64, (1, 2))


def _route(logits, n_tok):
    g_logits = logits[:, :N_GROUPS]
    e_logits = logits[:, N_GROUPS:N_GROUPS + N_EXPERTS].reshape(n_tok, N_GROUPS, EXPERTS_PER_GROUP)
    p_group = jax.nn.softmax(g_logits, axis=-1)
    g_top = jnp.argmax(g_logits, axis=-1)
    p_g = jnp.take_along_axis(p_group, g_top[:, None], axis=1)[:, 0]
    e_sel = jnp.take_along_axis(e_logits, g_top[:, None, None], axis=1)[:, 0]
    top_p, top_i = lax.top_k(jax.nn.softmax(e_sel, axis=-1), TOP_K)
    top_p = top_p / jnp.sum(top_p, axis=-1, keepdims=True)
    gate = p_g[:, None] * top_p
    expert = g_top[:, None] * EXPERTS_PER_GROUP + top_i

    n_assign = n_tok * TOP_K
    flat_e = expert.reshape(-1).astype(jnp.int32)
    flat_tok = jnp.repeat(jnp.arange(n_tok, dtype=jnp.int32), TOP_K)
    flat_w = gate.reshape(-1)
    order = jnp.argsort(flat_e)
    se = flat_e[order]
    counts = jnp.bincount(flat_e, length=N_EXPERTS)
    padded = (counts + MOE_TILE - 1) // MOE_TILE * MOE_TILE
    pend = jnp.cumsum(padded)
    pstart = pend - padded
    start = jnp.cumsum(counts) - counts
    slot = (pstart[se] + jnp.arange(n_assign, dtype=jnp.int32) - start[se]).astype(jnp.int32)
    cap = (n_assign + MOE_TILE - 1) // MOE_TILE * MOE_TILE + N_EXPERTS * MOE_TILE
    nb = cap // MOE_TILE
    slot_tok = jnp.zeros((cap,), jnp.int32).at[slot].set(flat_tok[order])
    slot_w = jnp.zeros((cap,), F32).at[slot].set(flat_w[order])
    slot_of = jnp.zeros((n_assign,), jnp.int32).at[order].set(slot).reshape(n_tok, TOP_K)
    blk_expert = jnp.minimum(
        jnp.searchsorted(pend, jnp.arange(nb, dtype=jnp.int32) * MOE_TILE, side='right'),
        N_EXPERTS - 1).astype(jnp.int32)
    n_used = (pend[-1] // MOE_TILE).astype(jnp.int32).reshape(1)
    return slot_tok, slot_w, slot_of, blk_expert, n_used


def kernel(x, c, ctx, c_ctx, ada_w, ada_b, norm1_g, norm2_g, w_in, shift_mu, rwkv_w0, rwkv_w2, rwkv_a0, rwkv_a2,
           rwkv_g2, rwkv_k_k, rwkv_k_a, rwkv_r_k, rwkv_lnx_g, rwkv_lnx_b, qn_g, kn_g, diff_lambda, subln_g,
           w_pa, w_pb, w_out, router_g_w, router_g_b, router_e_w, router_e_b, exp_w1, exp_w3, exp_w2):
    assert ada_w.shape[0] == 1, "single-layer block"
    b, l, d = x.shape
    lc = ctx.shape[1]
    lam_init = 0.8 - 0.6 * math.exp(-0.3 * 0)
    lv = diff_lambda[0].astype(F32)
    lam = jnp.exp(jnp.sum(lv[0] * lv[1])) - jnp.exp(jnp.sum(lv[2] * lv[3])) + lam_init

    rows = (b + 1 + SUBLANES - 1) // SUBLANES * SUBLANES
    cm = jnp.zeros((rows, d), F32).at[:b].set(c).at[b].set(c_ctx)
    mod = _modulation(cm, ada_w[0], ada_b[0])
    sh1, sc1, gt1, sh2, sc2, gt2 = [mod[:b, None, k * d:(k + 1) * d] for k in range(6)]
    csh1, csc1 = [jnp.broadcast_to(mod[b, k * d:(k + 1) * d], (b, 1, d)) for k in range(2)]

    w = w_in[0]
    lora_widths = (DECAY_LORA, DECAY_LORA, AAA_LORA, AAA_LORA)
    o_lora = 3 * RWKV_DIM
    o_glora = o_lora + sum(lora_widths)
    pad_cols = lambda m: jnp.concatenate(
        [m[..., :o_lora], _pad_lora_cols(m[..., o_lora:o_glora], lora_widths), m[..., o_glora:RWKV_COLS]], axis=-1)
    w_rwkv = pad_cols(w).astype(BF16)
    w_diff = w[:, RWKV_COLS:RWKV_COLS + DIFF_COLS].astype(BF16)
    w_gate = w[:, RWKV_COLS + DIFF_COLS:].astype(BF16)
    g1 = norm1_g[0]
    px_r = _norm_proj(x, g1, sc1, sh1, w_rwkv, F32)
    px_d = _norm_proj(x, g1, sc1, sh1, w_diff, F32)
    px_g = _norm_proj(x, g1, sc1, sh1, w_gate, F32)
    pc_r = _norm_proj(ctx, g1, csc1, csh1, w_rwkv, F32)
    pc_d = _norm_proj(ctx, g1, csc1, csh1, w_diff, F32)

    pad_rows = lambda m: jnp.pad(m, ((0, 0), (0, LORA_PAD - m.shape[1]), (0, 0)))
    rparams = (pad_cols(shift_mu[0])[None], rwkv_w0[0], pad_rows(rwkv_w2[0]), rwkv_a0[0], pad_rows(rwkv_a2[0]),
               rwkv_g2[0], rwkv_k_k[0][None], rwkv_k_a[0][None], rwkv_r_k[0].reshape(1, RWKV_DIM))
    qc, y0c, mc, nc_, _, _ = _rwkv_chunk_ops(pc_r, rparams)
    s_zero = jnp.zeros((b, 2, RWKV_HEAD_DIM, RWKV_DIM), F32)
    _, _, s_ctx = _rwkv_sweep(qc, y0c, mc, nc_, s_zero)
    qx, y0x, mx, nx, bonus, gate = _rwkv_chunk_ops(px_r, rparams)
    yf, yb, _ = _rwkv_sweep(qx, y0x, mx, nx, s_ctx)

    cos_t, sin_t = _rope_tables(l)
    qg = jnp.tile(qn_g[0], 2)[None]
    kg = jnp.tile(kn_g[0], 2)[None]
    q_x, k_x, v_x = _diff_prep(px_d, cos_t, sin_t, qg, kg, True)
    _, k_c, v_c = _diff_prep(pc_d, cos_t[:lc], sin_t[:lc], qg, kg, False)
    lam_vec = jnp.full((1, LANES), lam, F32)
    sg_vec = (subln_g[0] * (1.0 - lam_init))[None]
    y_diff = _diff_attention(q_x, k_c, v_c, k_x, v_x, lam_vec, sg_vec)

    mixed = _merge((yf, yb), bonus, gate, y_diff, px_g, rwkv_lnx_g[0][None], rwkv_lnx_b[0][None],
                   w_pa[0].astype(BF16), w_pb[0].astype(BF16))
    n_r = N_GROUPS + N_EXPERTS
    w_router = jnp.zeros((d, ROUTER_PAD), F32).at[:, :N_GROUPS].set(router_g_w[0]).at[:, N_GROUPS:n_r].set(router_e_w[0])
    b_router = jnp.zeros((1, ROUTER_PAD), F32).at[0, :N_GROUPS].set(router_g_b[0]).at[0, N_GROUPS:n_r].set(router_e_b[0])
    x_new, h2, logits = _outproj(mixed, x, gt1, norm2_g[0][None], sc2, sh2, w_out[0].astype(BF16), w_router, b_router)

    n_tok = b * l
    slot_tok, slot_w, slot_of, blk_expert, n_used = _route(logits.reshape(n_tok, ROUTER_PAD), n_tok)
    xs = h2.reshape(n_tok, d)[slot_tok]
    out = _moe_ffn(xs, slot_w[:, None], blk_expert, n_used,
                   exp_w1[0].astype(BF16), exp_w3[0].astype(BF16), exp_w2[0].astype(BF16))
    y0 = out[slot_of[:, 0]].reshape(b, l, d)
    y1 = out[slot_of[:, 1]].reshape(b, l, d)
    return _final(x_new, gt2, y0, y1)
```

```python
import functools
import math

import jax
import jax.numpy as jnp
from jax import lax
from jax.experimental import pallas as pl
from jax.experimental.pallas import tpu as pltpu

F32 = jnp.float32
BF16 = jnp.bfloat16
HIGHEST = lax.Precision.HIGHEST

D_MODEL = 2048
GRID_W = 64
RWKV_HEADS = 16
RWKV_HEAD_DIM = 64
RWKV_DIM = RWKV_HEADS * RWKV_HEAD_DIM
DECAY_LORA = 96
AAA_LORA = 96
GATE_LORA = 256
RWKV_COLS = 3 * RWKV_DIM + 2 * DECAY_LORA + 2 * AAA_LORA + GATE_LORA
DIFF_HEADS = 8
DIFF_QK_DIM = 64
DIFF_V_DIM = 2 * DIFF_QK_DIM
DIFF_DIM = DIFF_HEADS * DIFF_V_DIM
DIFF_QK_COLS = DIFF_HEADS * 2 * DIFF_QK_DIM
DIFF_COLS = 2 * DIFF_QK_COLS + DIFF_DIM
DIFF_SCALE = DIFF_QK_DIM ** -0.5
ROPE_THETA = 10000.0
GATE_COLS = 2 * D_MODEL
N_GROUPS = 4
EXPERTS_PER_GROUP = 8
N_EXPERTS = N_GROUPS * EXPERTS_PER_GROUP
TOP_K = 2
D_EXPERT = 512
NORM_EPS = 1e-6
SUBLN_EPS = 1e-5
LNX_EPS = 64e-5

LANES = 128
SUBLANES = 8
VMEM_LIMIT_BYTES = 56 * 1024 * 1024

LORA_PAD = LANES
RWKV_PCOLS = 3 * RWKV_DIM + 4 * LORA_PAD + GATE_LORA
CHUNK = 64
V_EXT = 2 * DIFF_V_DIM
SCORE_LOG2_LIMIT = 60.0
ROUTER_PAD = LANES
MOE_TILE = 512


def _cparams(*sem):
    return pltpu.CompilerParams(dimension_semantics=sem, vmem_limit_bytes=VMEM_LIMIT_BYTES)


def _sigmoid(x):
    return 1.0 / (1.0 + jnp.exp(-x))


def _dot(a, b, dims=(((1,), (0,)), ((), ()))):
    return lax.dot_general(a.astype(BF16), b.astype(BF16), dims, preferred_element_type=F32)


def _dot_f32(a, b, dims=(((1,), (0,)), ((), ()))):
    return lax.dot_general(a, b, dims, precision=HIGHEST, preferred_element_type=F32)


_NT = (((1,), (1,)), ((), ()))
_TN = (((0,), (0,)), ((), ()))


def _group_ones(width, group):
    r = lax.broadcasted_iota(jnp.int32, (width, width), 0) // group
    c = lax.broadcasted_iota(jnp.int32, (width, width), 1) // group
    return (r == c).astype(F32)


def _group_sum(x, group):
    ones = _group_ones(LANES, group)
    parts = [_dot_f32(x[:, j * LANES:(j + 1) * LANES], ones) for j in range(x.shape[1] // LANES)]
    return parts[0] if len(parts) == 1 else jnp.concatenate(parts, axis=1)


def _mod_kernel(c_ref, w_ref, b_ref, o_ref):
    c = c_ref[...]
    o_ref[...] = _dot_f32(c * _sigmoid(c), w_ref[...]) + b_ref[...]


def _modulation(cm, ada_w, ada_b):
    rows, d = cm.shape
    n = ada_w.shape[1]
    tn = 1536
    return pl.pallas_call(
        _mod_kernel,
        grid=(n // tn,),
        in_specs=[pl.BlockSpec((rows, d), lambda j: (0, 0)),
                  pl.BlockSpec((d, tn), lambda j: (0, j)),
                  pl.BlockSpec((1, tn), lambda j: (0, j))],
        out_specs=pl.BlockSpec((rows, tn), lambda j: (0, j)),
        out_shape=jax.ShapeDtypeStruct((rows, n), F32),
        compiler_params=_cparams("parallel"),
        name="modulation",
    )(cm, ada_w, ada_b.reshape(1, n))


def _inproj_kernel(x_ref, g_ref, sc_ref, sh_ref, w_ref, o_ref, h_ref):
    @pl.when(pl.program_id(2) == 0)
    def _():
        x = x_ref[0]
        y = x * lax.rsqrt(jnp.mean(x * x, axis=-1, keepdims=True) + NORM_EPS) * g_ref[...]
        h_ref[...] = (y * (1.0 + sc_ref[0]) + sh_ref[0]).astype(BF16)

    o_ref[0] = jnp.dot(h_ref[...], w_ref[...], preferred_element_type=F32).astype(o_ref.dtype)


def _pick(n, prefs):
    for t in prefs:
        if n % t == 0:
            return t
    return n


def _norm_proj(x, g, sc, sh, w, out_dtype):
    b, l, d = x.shape
    n = w.shape[1]
    tm = _pick(l, (1024, 512, 256, 128))
    tn = _pick(n, (640, 512, 384, 256, 128))
    return pl.pallas_call(
        _inproj_kernel,
        grid=(b, l // tm, n // tn),
        in_specs=[pl.BlockSpec((1, tm, d), lambda bi, i, j: (bi, i, 0)),
                  pl.BlockSpec((1, d), lambda bi, i, j: (0, 0)),
                  pl.BlockSpec((1, 1, d), lambda bi, i, j: (bi, 0, 0)),
                  pl.BlockSpec((1, 1, d), lambda bi, i, j: (bi, 0, 0)),
                  pl.BlockSpec((d, tn), lambda bi, i, j: (0, j))],
        out_specs=pl.BlockSpec((1, tm, tn), lambda bi, i, j: (bi, i, j)),
        out_shape=jax.ShapeDtypeStruct((b, l, n), out_dtype),
        scratch_shapes=[pltpu.VMEM((tm, d), BF16)],
        compiler_params=_cparams("parallel", "parallel", "arbitrary"),
        name="norm_proj",
    )(x, g.reshape(1, d), sc, sh, w)


def _mm(a, b, dims=(((1,), (0,)), ((), ())), exact=False):
    return _dot_f32(a, b, dims) if exact else _dot(a, b, dims)


def _tri_inverse_all(a_list, exact):
    n = CHUNK
    eye = (lax.broadcasted_iota(jnp.int32, (n, n), 0) == lax.broadcasted_iota(jnp.int32, (n, n), 1)).astype(F32)
    xs = [eye + a for a in a_list]
    ps = [_mm(a, a, exact=exact) for a in a_list]
    steps = int(math.log2(n)) - 1
    for s in range(steps):
        if s < steps - 1:
            xps = [_mm(jnp.concatenate([x, p], axis=0), p, exact=exact) for x, p in zip(xs, ps)]
            xs = [x + xp[:n] for x, xp in zip(xs, xps)]
            ps = [xp[n:] for xp in xps]
        else:
            xs = [x + _mm(x, p, exact=exact) for x, p in zip(xs, ps)]
    return xs


def _rwkv_chunk_kernel(p_ref, pp_ref, pn_ref, mu_ref, w0_ref, w2_ref, a0_ref, a2_ref, g2_ref, kk_ref, ka_ref,
                       rk_ref, q_ref, y0_ref, m_ref, n_ref, bonus_ref, gate_ref):
    i = pl.program_id(1)
    last = pl.num_programs(1) - 1
    c = RWKV_DIM
    hd = RWKV_HEAD_DIM
    heads = range(RWKV_HEADS)
    p = p_ref[0]
    row = lax.broadcasted_iota(jnp.int32, (CHUNK, 1), 0)
    prev_row = jnp.where(i == 0, 0.0, pp_ref[0, SUBLANES - 1:SUBLANES, :])
    next_row = jnp.where(i == last, 0.0, pn_ref[0, 0:1, :])
    prev = jnp.where(row == 0, prev_row, pltpu.roll(p, 1, axis=0))
    nxt = jnp.where(row == CHUNK - 1, next_row, pltpu.roll(p, CHUNK - 1, axis=0))
    ps = p + mu_ref[...] * (0.5 * (prev + nxt) - p)

    r, k, v = ps[:, :c], ps[:, c:2 * c], ps[:, 2 * c:3 * c]
    o = 3 * c
    xw = (ps[:, o:o + LORA_PAD], ps[:, o + LORA_PAD:o + 2 * LORA_PAD])
    xa = (ps[:, o + 2 * LORA_PAD:o + 3 * LORA_PAD], ps[:, o + 3 * LORA_PAD:o + 4 * LORA_PAD])
    xg = ps[:, o + 4 * LORA_PAD:]

    gate_ref[0] = _dot(_sigmoid(xg), g2_ref[...])
    kk = k * kk_ref[...]
    kk = kk * lax.rsqrt(_group_sum(kk * kk, hd) + 1e-12)

    tr = lax.broadcasted_iota(jnp.int32, (CHUNK, CHUNK), 0)
    tc = lax.broadcasted_iota(jnp.int32, (CHUNK, CHUNK), 1)
    eye = tr == tc
    tr2 = lax.broadcasted_iota(jnp.int32, (CHUNK, 2 * CHUNK), 0)
    tc2 = lax.broadcasted_iota(jnp.int32, (CHUNK, 2 * CHUNK), 1) % CHUNK
    kd_sum = jnp.zeros_like(k)
    v_b = v.astype(BF16)
    hsl = [slice(h * hd, (h + 1) * hd) for h in heads]
    for d in range(2):
        before = (tc < tr) if d == 0 else (tc > tr)
        upto = (tc <= tr) if d == 0 else (tc >= tr)
        upto2 = (tc2 <= tr2) if d == 0 else (tc2 >= tr2)
        z = w0_ref[d:d + 1, :] + _dot(jnp.tanh(xw[d]), w2_ref[d])
        w_log = -(jnp.maximum(-z, 0.0) + jnp.log(1.0 + jnp.exp(-jnp.abs(z)))) - 0.5
        logw = -jnp.exp(w_log)
        a = _sigmoid(a0_ref[d:d + 1, :] + _dot(xa[d], a2_ref[d]))
        kd = k * (1.0 + (a - 1.0) * ka_ref[...])
        kd_sum = kd_sum + kd
        cum = _dot_f32(upto.astype(F32), logw)
        total = cum[CHUNK - 1:CHUNK, :] if d == 0 else cum[0:1, :]
        e_neg = jnp.exp(-cum)
        e_rest = jnp.exp(total - cum)
        p_total = jnp.exp(total)
        beta = kk * a
        al = -kk * jnp.exp(cum - logw)
        rt = r * jnp.exp(cum)
        al_b = al.astype(BF16)
        rt_b = rt.astype(BF16)
        bt_b = (beta * e_neg).astype(BF16)
        kt_b = (kd * e_neg).astype(BF16)
        bh_b = (beta * e_rest).astype(BF16)
        kh_b = (kd * e_rest).astype(BF16)
        g = [_mm(jnp.concatenate([al_b[:, s], rt_b[:, s]], axis=0),
                 jnp.concatenate([bt_b[:, s], kt_b[:, s]], axis=0), _NT) for s in hsl]
        a_ab = [jnp.where(before, gh[:CHUNK, :CHUNK], 0.0) for gh in g]
        a_ak = [jnp.where(before, gh[:CHUNK, CHUNK:], 0.0) for gh in g]
        a_rbk = [jnp.where(upto2, gh[CHUNK:, :], 0.0).astype(BF16) for gh in g]
        akv = [_mm(ak, v_b[:, s]) for ak, s in zip(a_ak, hsl)]
        t_inv = _tri_inverse_all(a_ab, exact=False)
        wu = [_mm(t, jnp.concatenate([al[:, s], u], axis=1)) for t, u, s in zip(t_inv, akv, hsl)]
        zv = jnp.zeros((CHUNK, hd), BF16)
        zmat = [jnp.concatenate([x.astype(BF16), jnp.concatenate([zv, v_b[:, s]], axis=1)], axis=0)
                for x, s in zip(wu, hsl)]
        qy = [_mm(ar, zm) for ar, zm in zip(a_rbk, zmat)]
        mn = [_mm(zm, jnp.concatenate([bh_b[:, s], kh_b[:, s]], axis=0), _TN)
              for zm, s in zip(zmat, hsl)]
        q_ref[0, d] = rt + jnp.concatenate([x[:, :hd] for x in qy], axis=1)
        y0_ref[0, d] = jnp.concatenate([x[:, hd:] for x in qy], axis=1)
        m_ref[0, d, 0] = jnp.concatenate([jnp.where(eye, p_total[:, s], 0.0) + x[:hd] for x, s in zip(mn, hsl)], axis=1)
        n_ref[0, d, 0] = jnp.concatenate([x[hd:] for x in mn], axis=1)
    bonus_ref[0] = _group_sum(r * kd_sum * rk_ref[...], hd) * v


def _rwkv_chunk_ops(p, params):
    b, l, pc = p.shape
    nc = l // CHUNK
    c = RWKV_DIM
    hb = CHUNK // SUBLANES
    nb8 = l // SUBLANES
    mu, w0, w2, a0, a2, g2, k_k, k_a, r_k = params
    const = lambda shape: pl.BlockSpec(shape, lambda bi, i: (0,) * len(shape))
    tok = lambda: pl.BlockSpec((1, CHUNK, c), lambda bi, i: (bi, i, 0))
    return pl.pallas_call(
        _rwkv_chunk_kernel,
        grid=(b, nc),
        in_specs=[pl.BlockSpec((1, CHUNK, pc), lambda bi, i: (bi, i, 0)),
                  pl.BlockSpec((1, SUBLANES, pc), lambda bi, i: (bi, jnp.maximum(i * hb - 1, 0), 0)),
                  pl.BlockSpec((1, SUBLANES, pc), lambda bi, i: (bi, jnp.minimum((i + 1) * hb, nb8 - 1), 0)),
                  const((1, pc)), const((2, c)), const((2, LORA_PAD, c)), const((2, c)), const((2, LORA_PAD, c)),
                  const((GATE_LORA, c)), const((1, c)), const((1, c)), const((1, c))],
        out_specs=[pl.BlockSpec((1, 2, CHUNK, c), lambda bi, i: (bi, 0, i, 0)),
                   pl.BlockSpec((1, 2, CHUNK, c), lambda bi, i: (bi, 0, i, 0)),
                   pl.BlockSpec((1, 2, 1, RWKV_HEAD_DIM, c), lambda bi, i: (bi, 0, i, 0, 0)),
                   pl.BlockSpec((1, 2, 1, RWKV_HEAD_DIM, c), lambda bi, i: (bi, 0, i, 0, 0)),
                   tok(), tok()],
        out_shape=[jax.ShapeDtypeStruct((b, 2, l, c), F32), jax.ShapeDtypeStruct((b, 2, l, c), F32),
                   jax.ShapeDtypeStruct((b, 2, nc, RWKV_HEAD_DIM, c), F32),
                   jax.ShapeDtypeStruct((b, 2, nc, RWKV_HEAD_DIM, c), F32),
                   jax.ShapeDtypeStruct((b, l, c), F32), jax.ShapeDtypeStruct((b, l, c), F32)],
        compiler_params=_cparams("parallel", "parallel"),
        name="rwkv_chunk_ops",
    )(p, p, p, mu, w0, w2, a0, a2, g2, k_k, k_a, r_k)


def _rwkv_sweep_kernel(qf_ref, y0f_ref, mf_ref, nf_ref, qb_ref, y0b_ref, mb_ref, nb_ref, s0_ref,
                       yf_ref, yb_ref, sfin_ref, s_ref):
    j = pl.program_id(1)
    hd = RWKV_HEAD_DIM

    @pl.when(j == 0)
    def _():
        s_ref[...] = s0_ref[0]

    dirs = ((qf_ref, y0f_ref, mf_ref, nf_ref, yf_ref), (qb_ref, y0b_ref, mb_ref, nb_ref, yb_ref))
    hsl = [slice(h * hd, (h + 1) * hd) for h in range(RWKV_HEADS)]
    for d, (q_ref, y0_ref, m_ref, n_ref, y_ref) in enumerate(dirs):
        s_b = s_ref[d].astype(BF16)
        q_b = q_ref[0, 0].astype(BF16)
        m_b = m_ref[0, 0, 0].astype(BF16)
        ys = [_dot(q_b[:, s], s_b[:, s], _NT) for s in hsl]
        sm = [_dot(s_b[:, s], m_b[:, s]) for s in hsl]
        y_ref[0, 0] = y0_ref[0, 0] + jnp.concatenate(ys, axis=1)
        s_ref[d] = n_ref[0, 0, 0] + jnp.concatenate(sm, axis=1)

    @pl.when(j == pl.num_programs(1) - 1)
    def _():
        sfin_ref[0] = s_ref[...]


def _rwkv_sweep(q, y0, m, n, s0):
    b, _, l, c = q.shape
    nc = l // CHUNK
    hd = RWKV_HEAD_DIM
    tokf = lambda: pl.BlockSpec((1, 1, CHUNK, c), lambda bi, j: (bi, 0, j, 0))
    tokb = lambda: pl.BlockSpec((1, 1, CHUNK, c), lambda bi, j: (bi, 1, nc - 1 - j, 0))
    opf = lambda: pl.BlockSpec((1, 1, 1, hd, c), lambda bi, j: (bi, 0, j, 0, 0))
    opb = lambda: pl.BlockSpec((1, 1, 1, hd, c), lambda bi, j: (bi, 1, nc - 1 - j, 0, 0))
    return pl.pallas_call(
        _rwkv_sweep_kernel,
        grid=(b, nc),
        in_specs=[tokf(), tokf(), opf(), opf(), tokb(), tokb(), opb(), opb(),
                  pl.BlockSpec((1, 2, hd, c), lambda bi, j: (bi, 0, 0, 0))],
        out_specs=[tokf(), tokb(), pl.BlockSpec((1, 2, hd, c), lambda bi, j: (bi, 0, 0, 0))],
        out_shape=[jax.ShapeDtypeStruct((b, 2, l, c), F32), jax.ShapeDtypeStruct((b, 2, l, c), F32),
                   jax.ShapeDtypeStruct((b, 2, hd, c), F32)],
        scratch_shapes=[pltpu.VMEM((2, hd, c), F32)],
        compiler_params=_cparams("parallel", "arbitrary"),
        name="rwkv_sweep",
    )(q, y0, m, n, q, y0, m, n, s0)


def _rwkv_scans(px_r, pc_r, rparams):
    b = px_r.shape[0]
    qc, y0c, mc, nc_, _, _ = _rwkv_chunk_ops(pc_r, rparams)
    s_zero = jnp.zeros((b, 2, RWKV_HEAD_DIM, RWKV_DIM), F32)
    _, _, s_ctx = _rwkv_sweep(qc, y0c, mc, nc_, s_zero)
    qx, y0x, mx, nx, bonus, gate = _rwkv_chunk_ops(px_r, rparams)
    yf, yb, _ = _rwkv_sweep(qx, y0x, mx, nx, s_ctx)
    return (yf, yb), bonus, gate


def _diff_prep_kernel(p_ref, cos_ref, sin_ref, qg_ref, kg_ref, q_ref, k_ref, v_ref, *, rope):
    lane = lax.broadcasted_iota(jnp.int32, (1, LANES), 1)
    first = (lane % 32) < 16
    for hd in range(DIFF_HEADS):
        for off, g_ref, o_ref, scale in ((0, qg_ref, q_ref, DIFF_SCALE * math.log2(math.e)),
                                         (DIFF_QK_COLS, kg_ref, k_ref, 1.0)):
            cs = slice(hd * LANES, (hd + 1) * LANES)
            xb = p_ref[0, :, off + hd * LANES:off + (hd + 1) * LANES]
            ms = _group_sum(xb * xb, DIFF_QK_DIM) * (1.0 / DIFF_QK_DIM)
            y = xb * lax.rsqrt(ms + NORM_EPS) * g_ref[...]
            if rope:
                swapped = jnp.where(first, pltpu.roll(y, LANES - 16, axis=1), pltpu.roll(y, 16, axis=1))
                y = y * cos_ref[...] + swapped * sin_ref[...]
            o_ref[0, :, cs] = (y * scale).astype(BF16)
    ones = jnp.ones((p_ref.shape[1], V_EXT - DIFF_V_DIM), BF16)
    for hd in range(DIFF_HEADS):
        vb = p_ref[0, :, 2 * DIFF_QK_COLS + hd * DIFF_V_DIM:2 * DIFF_QK_COLS + (hd + 1) * DIFF_V_DIM]
        v_ref[0, :, hd * V_EXT:(hd + 1) * V_EXT] = jnp.concatenate([vb.astype(BF16), ones], axis=1)


def _diff_prep(p, cos_t, sin_t, qg, kg, rope):
    b, l, pc = p.shape
    t = _pick(l, (512, 256, 128))
    tok = lambda w=DIFF_DIM: pl.BlockSpec((1, t, w), lambda bi, i: (bi, i, 0))
    shp = jax.ShapeDtypeStruct((b, l, DIFF_DIM), BF16)
    shp_v = jax.ShapeDtypeStruct((b, l, DIFF_HEADS * V_EXT), BF16)
    return pl.pallas_call(
        functools.partial(_diff_prep_kernel, rope=rope),
        grid=(b, l // t),
        in_specs=[pl.BlockSpec((1, t, pc), lambda bi, i: (bi, i, 0)),
                  pl.BlockSpec((t, LANES), lambda bi, i: (i, 0)),
                  pl.BlockSpec((t, LANES), lambda bi, i: (i, 0)),
                  pl.BlockSpec((1, LANES), lambda bi, i: (0, 0)),
                  pl.BlockSpec((1, LANES), lambda bi, i: (0, 0))],
        out_specs=[tok(), tok(), tok(DIFF_HEADS * V_EXT)],
        out_shape=[shp, shp, shp_v],
        compiler_params=_cparams("parallel", "parallel"),
        name="diff_prep_rope" if rope else "diff_prep",
    )(p, cos_t, sin_t, qg, kg)


def _diff_finish(acc1, acc2, lam_ref, sg_ref, o_ref):
    dv = DIFF_V_DIM
    o = acc1[:, :dv] / acc1[:, dv:] - lam_ref[...] * (acc2[:, :dv] / acc2[:, dv:])
    o = o * lax.rsqrt(jnp.mean(o * o, axis=-1, keepdims=True) + SUBLN_EPS)
    o_ref[0] = (o * sg_ref[...]).astype(o_ref.dtype)


def _flash_online_kernel(lam_ref, sg_ref, q_ref, kc_ref, vc_ref, k_ref, v_ref, o_ref, *, tk):
    qd = DIFF_QK_DIM
    tq = q_ref.shape[1]
    q = q_ref[0]
    qs = (q[:, :qd], q[:, qd:])

    def absorb(state, k, v):
        reps = k.shape[0] // LANES
        out = []
        for mp in range(2):
            m_prev, acc = state[mp]
            s = lax.dot_general(qs[mp], k[:, mp * qd:(mp + 1) * qd], _NT, preferred_element_type=F32)
            m_new = jnp.maximum(m_prev, jnp.max(s, axis=-1, keepdims=True))
            alpha = jnp.exp2(m_prev - m_new)
            pr = jnp.exp2(s - jnp.concatenate([m_new] * reps, axis=1))
            acc_new = jnp.concatenate([alpha, alpha], axis=1) * acc + jnp.dot(pr.astype(BF16), v, preferred_element_type=F32)
            out.append((m_new, acc_new))
        return tuple(out)

    init = (jnp.full((tq, LANES), -jnp.inf, F32), jnp.zeros((tq, V_EXT), F32))
    state = absorb((init, init), kc_ref[0], vc_ref[0])

    def body(j, state):
        rows = pl.ds(pl.multiple_of(j * tk, tk), tk)
        return absorb(state, k_ref[0, rows, :], v_ref[0, rows, :])

    (_, acc1), (_, acc2) = lax.fori_loop(0, k_ref.shape[1] // tk, body, state)
    _diff_finish(acc1, acc2, lam_ref, sg_ref, o_ref)


def _flash_bounded_kernel(lam_ref, sg_ref, q_ref, kc_ref, vc_ref, k_ref, v_ref, o_ref, acc_ref, p_ref, *, tk):
    qd = DIFF_QK_DIM
    q = q_ref[0]
    qs = (q[:, :qd], q[:, qd:])
    n_kv = k_ref.shape[1] // tk

    def weights(mp, k):
        s = lax.dot_general(qs[mp], k[:, mp * qd:(mp + 1) * qd], _NT, preferred_element_type=F32)
        return jnp.exp2(s).astype(BF16)

    def chunk(c):
        return pl.ds(pl.multiple_of(jnp.minimum(c, n_kv - 1) * tk, tk), tk)

    for mp in range(2):
        acc_ref[mp] = jnp.dot(weights(mp, kc_ref[0]), vc_ref[0], preferred_element_type=F32)
        p_ref[0, mp] = weights(mp, k_ref[0, pl.ds(0, tk), :])

    steps = 4 if n_kv % 4 == 0 else 2

    def body(j, carry):
        for u in range(steps):
            c = steps * j + u
            v = v_ref[0, chunk(c), :]
            k_next = k_ref[0, chunk(c + 1), :]
            for mp in range(2):
                acc_ref[mp] += jnp.dot(p_ref[u % 2, mp], v, preferred_element_type=F32)
                p_ref[1 - u % 2, mp] = weights(mp, k_next)
        return carry

    lax.fori_loop(0, n_kv // steps, body, 0)
    _diff_finish(acc_ref[0], acc_ref[1], lam_ref, sg_ref, o_ref)


def _diff_attention(q, kc, vc, k, v, lam_vec, sg_vec, bounded):
    b, l, _ = q.shape
    lc = kc.shape[1]
    tq = _pick(l, (256, 128))
    tk = _pick(l, (512, 256, 128))
    if bounded:
        tk = _pick(l // 2, (512, 256, 128))
        assert l % (2 * tk) == 0
        body = functools.partial(_flash_bounded_kernel, tk=tk)
        scratch = [pltpu.VMEM((2, tq, V_EXT), F32), pltpu.VMEM((2, 2, tq, tk), BF16)]
    else:
        body = functools.partial(_flash_online_kernel, tk=tk)
        scratch = []
    return pl.pallas_call(
        body,
        grid=(b, DIFF_HEADS, l // tq),
        in_specs=[pl.BlockSpec((1, LANES), lambda bi, h, i: (0, 0)),
                  pl.BlockSpec((1, LANES), lambda bi, h, i: (0, 0)),
                  pl.BlockSpec((1, tq, LANES), lambda bi, h, i: (bi, i, h)),
                  pl.BlockSpec((1, lc, LANES), lambda bi, h, i: (bi, 0, h)),
                  pl.BlockSpec((1, lc, V_EXT), lambda bi, h, i: (bi, 0, h)),
                  pl.BlockSpec((1, l, LANES), lambda bi, h, i: (bi, 0, h)),
                  pl.BlockSpec((1, l, V_EXT), lambda bi, h, i: (bi, 0, h))],
        out_specs=pl.BlockSpec((1, tq, LANES), lambda bi, h, i: (bi, i, h)),
        out_shape=jax.ShapeDtypeStruct((b, l, DIFF_DIM), BF16),
        scratch_shapes=scratch,
        compiler_params=_cparams("parallel", "parallel", "arbitrary"),
        name="diff_flash_bounded" if bounded else "diff_flash_online",
    )(lam_vec, sg_vec, q, kc, vc, k, v)


def _merge_kernel(yf_ref, yb_ref, bonus_ref, gate_ref, yd_ref, pg_ref, lg_ref, lb_ref, wpa_ref, wpb_ref, o_ref):
    hd = RWKV_HEAD_DIM
    y = yf_ref[0, 0] + yb_ref[0, 0]
    dev = y - _group_sum(y, hd) * (1.0 / hd)
    var = _group_sum(dev * dev, hd) * (1.0 / hd)
    yn = dev * lax.rsqrt(var + LNX_EPS) * lg_ref[...] + lb_ref[...]
    y_rwkv = (yn + bonus_ref[0]) * gate_ref[0]
    a = _dot(y_rwkv, wpa_ref[...])
    bb = jnp.dot(yd_ref[0], wpb_ref[...], preferred_element_type=F32)
    ga = _sigmoid(pg_ref[0, :, :D_MODEL])
    gb = _sigmoid(pg_ref[0, :, D_MODEL:])
    o_ref[0] = (ga * a + gb * bb).astype(BF16)


def _merge(y_sweep, bonus, gate, y_diff, p_gate, lnx_g, lnx_b, w_pa, w_pb):
    b, l, c = bonus.shape
    d = D_MODEL
    tm = _pick(l, (256, 128))
    const = lambda shape: pl.BlockSpec(shape, lambda bi, i: (0,) * len(shape))
    tok = lambda w: pl.BlockSpec((1, tm, w), lambda bi, i: (bi, i, 0))
    return pl.pallas_call(
        _merge_kernel,
        grid=(b, l // tm),
        in_specs=[pl.BlockSpec((1, 1, tm, c), lambda bi, i: (bi, 0, i, 0)),
                  pl.BlockSpec((1, 1, tm, c), lambda bi, i: (bi, 1, i, 0)),
                  tok(c), tok(c), tok(DIFF_DIM), tok(GATE_COLS),
                  const((1, c)), const((1, c)), const((c, d)), const((DIFF_DIM, d))],
        out_specs=tok(d),
        out_shape=jax.ShapeDtypeStruct((b, l, d), BF16),
        compiler_params=_cparams("parallel", "parallel"),
        name="merge",
    )(y_sweep[0], y_sweep[1], bonus, gate, y_diff, p_gate, lnx_g, lnx_b, w_pa, w_pb)


def _outproj_kernel(mx_ref, x_ref, gt_ref, g_ref, sc_ref, sh_ref, wo_ref, wr_ref, br_ref, xn_ref, h_ref, lg_ref):
    mix = jnp.dot(mx_ref[0], wo_ref[...], preferred_element_type=F32)
    xn = x_ref[0] + gt_ref[0] * mix
    xn_ref[0] = xn
    y = xn * lax.rsqrt(jnp.mean(xn * xn, axis=-1, keepdims=True) + NORM_EPS) * g_ref[...]
    h = y * (1.0 + sc_ref[0]) + sh_ref[0]
    h_ref[0] = h.astype(BF16)
    lg_ref[0] = _dot_f32(h, wr_ref[...]) + br_ref[...]


def _outproj(mixed, x, gt1, g2, sc2, sh2, w_out, w_router, b_router):
    b, l, d = x.shape
    tm = _pick(l, (256, 128))
    const = lambda shape: pl.BlockSpec(shape, lambda bi, i: (0,) * len(shape))
    tok = lambda w: pl.BlockSpec((1, tm, w), lambda bi, i: (bi, i, 0))
    per_b = lambda: pl.BlockSpec((1, 1, d), lambda bi, i: (bi, 0, 0))
    return pl.pallas_call(
        _outproj_kernel,
        grid=(b, l // tm),
        in_specs=[tok(d), tok(d), per_b(), const((1, d)), per_b(), per_b(),
                  const((d, d)), const((d, ROUTER_PAD)), const((1, ROUTER_PAD))],
        out_specs=[tok(d), tok(d), tok(ROUTER_PAD)],
        out_shape=[jax.ShapeDtypeStruct((b, l, d), F32), jax.ShapeDtypeStruct((b, l, d), BF16),
                   jax.ShapeDtypeStruct((b, l, ROUTER_PAD), F32)],
        compiler_params=_cparams("parallel", "parallel"),
        name="outproj_router",
    )(mixed, x, gt1, g2, sc2, sh2, w_out, w_router, b_router)


def _moe_kernel(be_ref, nu_ref, x_ref, sw_ref, w1_ref, w3_ref, w2_ref, o_ref):
    i = pl.program_id(0)

    @pl.when(i < nu_ref[0])
    def _():
        xb = x_ref[...]
        u = jnp.dot(xb, w1_ref[0], preferred_element_type=F32)
        g = jnp.dot(xb, w3_ref[0], preferred_element_type=F32)
        hmid = (u * _sigmoid(u) * g).astype(BF16)
        o_ref[...] = jnp.dot(hmid, w2_ref[0], preferred_element_type=F32) * sw_ref[...]

    @pl.when(i >= nu_ref[0])
    def _():
        o_ref[...] = jnp.zeros(o_ref.shape, F32)


def _moe_ffn(xs, sw, blk_expert, n_used, w1, w3, w2):
    cap, d = xs.shape
    nb = cap // MOE_TILE
    grid_spec = pltpu.PrefetchScalarGridSpec(
        num_scalar_prefetch=2,
        grid=(nb,),
        in_specs=[pl.BlockSpec((MOE_TILE, d), lambda i, be, nu: (i, 0)),
                  pl.BlockSpec((MOE_TILE, 1), lambda i, be, nu: (i, 0)),
                  pl.BlockSpec((1, d, D_EXPERT), lambda i, be, nu: (be[i], 0, 0)),
                  pl.BlockSpec((1, d, D_EXPERT), lambda i, be, nu: (be[i], 0, 0)),
                  pl.BlockSpec((1, D_EXPERT, d), lambda i, be, nu: (be[i], 0, 0))],
        out_specs=pl.BlockSpec((MOE_TILE, d), lambda i, be, nu: (i, 0)),
    )
    return pl.pallas_call(
        _moe_kernel,
        grid_spec=grid_spec,
        out_shape=jax.ShapeDtypeStruct((cap, d), F32),
        compiler_params=_cparams("arbitrary"),
        name="moe_ffn",
    )(blk_expert, n_used, xs, sw, w1, w3, w2)


def _final_kernel(x_ref, gt_ref, y0_ref, y1_ref, o_ref):
    o_ref[0] = x_ref[0] + gt_ref[0] * (y0_ref[0] + y1_ref[0])


def _final(x_new, gt2, y0, y1):
    b, l, d = x_new.shape
    tm = _pick(l, (512, 256, 128))
    tok = lambda: pl.BlockSpec((1, tm, d), lambda bi, i: (bi, i, 0))
    return pl.pallas_call(
        _final_kernel,
        grid=(b, l // tm),
        in_specs=[tok(), pl.BlockSpec((1, 1, d), lambda bi, i: (bi, 0, 0)), tok(), tok()],
        out_specs=tok(),
        out_shape=jax.ShapeDtypeStruct((b, l, d), F32),
        compiler_params=_cparams("parallel", "parallel"),
        name="moe_residual",
    )(x_new, gt2, y0, y1)


def _pad_lora_cols(w, widths):
    parts, o = [], 0
    for wd in widths:
        blk = w[..., o:o + wd]
        parts.append(jnp.pad(blk, [(0, 0)] * (w.ndim - 1) + [(0, LORA_PAD - wd)]))
        o += wd
    return jnp.concatenate(parts, axis=-1)


def _rope_tables(l):
    half = DIFF_QK_DIM // 2
    inv_freq = ROPE_THETA ** (-jnp.arange(0, half, 2, dtype=F32) / half)
    t = jnp.arange(l, dtype=jnp.int32)
    rows = (t // GRID_W).astype(F32)[:, None] * inv_freq
    cols = (t % GRID_W).astype(F32)[:, None] * inv_freq
    cos64 = jnp.concatenate([jnp.cos(rows), jnp.cos(rows), jnp.cos(cols), jnp.cos(cols)], axis=1)
    sin64 = jnp.concatenate([-jnp.sin(rows), jnp.sin(rows), -jnp.sin(cols), jnp.sin(cols)], axis=1)
    return jnp.tile(cos64, (1, 2)), jnp.tile(sin64, (1, 2))


def _route(logits, n_tok):
    g_logits = logits[:, :N_GROUPS]
    e_logits = logits[:, N_GROUPS:N_GROUPS + N_EXPERTS].reshape(n_tok, N_GROUPS, EXPERTS_PER_GROUP)
    p_group = jax.nn.softmax(g_logits, axis=-1)
    g_top = jnp.argmax(g_logits, axis=-1)
    p_g = jnp.take_along_axis(p_group, g_top[:, None], axis=1)[:, 0]
    e_sel = jnp.take_along_axis(e_logits, g_top[:, None, None], axis=1)[:, 0]
    top_p, top_i = lax.top_k(jax.nn.softmax(e_sel, axis=-1), TOP_K)
    top_p = top_p / jnp.sum(top_p, axis=-1, keepdims=True)
    gate = p_g[:, None] * top_p
    expert = g_top[:, None] * EXPERTS_PER_GROUP + top_i

    n_assign = n_tok * TOP_K
    flat_e = expert.reshape(-1).astype(jnp.int32)
    flat_tok = jnp.repeat(jnp.arange(n_tok, dtype=jnp.int32), TOP_K)
    flat_w = gate.reshape(-1)
    order = jnp.argsort(flat_e)
    se = flat_e[order]
    counts = jnp.bincount(flat_e, length=N_EXPERTS)
    padded = (counts + MOE_TILE - 1) // MOE_TILE * MOE_TILE
    pend = jnp.cumsum(padded)
    pstart = pend - padded
    start = jnp.cumsum(counts) - counts
    slot = (pstart[se] + jnp.arange(n_assign, dtype=jnp.int32) - start[se]).astype(jnp.int32)
    cap = (n_assign + MOE_TILE - 1) // MOE_TILE * MOE_TILE + N_EXPERTS * MOE_TILE
    nb = cap // MOE_TILE
    slot_tok = jnp.zeros((cap,), jnp.int32).at[slot].set(flat_tok[order])
    slot_w = jnp.zeros((cap,), F32).at[slot].set(flat_w[order])
    slot_of = jnp.zeros((n_assign,), jnp.int32).at[order].set(slot).reshape(n_tok, TOP_K)
    blk_expert = jnp.minimum(
        jnp.searchsorted(pend, jnp.arange(nb, dtype=jnp.int32) * MOE_TILE, side='right'),
        N_EXPERTS - 1).astype(jnp.int32)
    n_used = (pend[-1] // MOE_TILE).astype(jnp.int32).reshape(1)
    return slot_tok, slot_w, slot_of, blk_expert, n_used


def kernel(x, c, ctx, c_ctx, ada_w, ada_b, norm1_g, norm2_g, w_in, shift_mu, rwkv_w0, rwkv_w2, rwkv_a0, rwkv_a2,
           rwkv_g2, rwkv_k_k, rwkv_k_a, rwkv_r_k, rwkv_lnx_g, rwkv_lnx_b, qn_g, kn_g, diff_lambda, subln_g,
           w_pa, w_pb, w_out, router_g_w, router_g_b, router_e_w, router_e_b, exp_w1, exp_w3, exp_w2):
    assert ada_w.shape[0] == 1, "single-layer block"
    b, l, d = x.shape
    lc = ctx.shape[1]
    lam_init = 0.8 - 0.6 * math.exp(-0.3 * 0)
    lv = diff_lambda[0].astype(F32)
    lam = jnp.exp(jnp.sum(lv[0] * lv[1])) - jnp.exp(jnp.sum(lv[2] * lv[3])) + lam_init

    rows = (b + 1 + SUBLANES - 1) // SUBLANES * SUBLANES
    cm = jnp.zeros((rows, d), F32).at[:b].set(c).at[b].set(c_ctx)
    mod = _modulation(cm, ada_w[0], ada_b[0])
    sh1, sc1, gt1, sh2, sc2, gt2 = [mod[:b, None, k * d:(k + 1) * d] for k in range(6)]
    csh1, csc1 = [jnp.broadcast_to(mod[b, k * d:(k + 1) * d], (b, 1, d)) for k in range(2)]

    w = w_in[0]
    lora_widths = (DECAY_LORA, DECAY_LORA, AAA_LORA, AAA_LORA)
    o_lora = 3 * RWKV_DIM
    o_glora = o_lora + sum(lora_widths)
    pad_cols = lambda m: jnp.concatenate(
        [m[..., :o_lora], _pad_lora_cols(m[..., o_lora:o_glora], lora_widths), m[..., o_glora:RWKV_COLS]], axis=-1)
    w_rwkv = pad_cols(w).astype(BF16)
    w_diff = w[:, RWKV_COLS:RWKV_COLS + DIFF_COLS].astype(BF16)
    w_gate = w[:, RWKV_COLS + DIFF_COLS:].astype(BF16)
    g1 = norm1_g[0]
    px_r = _norm_proj(x, g1, sc1, sh1, w_rwkv, F32)
    px_d = _norm_proj(x, g1, sc1, sh1, w_diff, F32)
    px_g = _norm_proj(x, g1, sc1, sh1, w_gate, F32)
    pc_r = _norm_proj(ctx, g1, csc1, csh1, w_rwkv, F32)
    pc_d = _norm_proj(ctx, g1, csc1, csh1, w_diff, F32)

    pad_rows = lambda m: jnp.pad(m, ((0, 0), (0, LORA_PAD - m.shape[1]), (0, 0)))
    rparams = (pad_cols(shift_mu[0])[None], rwkv_w0[0], pad_rows(rwkv_w2[0]), rwkv_a0[0], pad_rows(rwkv_a2[0]),
               rwkv_g2[0], rwkv_k_k[0][None], rwkv_k_a[0][None], rwkv_r_k[0].reshape(1, RWKV_DIM))
    y_sweep, bonus, gate = _rwkv_scans(px_r, pc_r, rparams)

    cos_t, sin_t = _rope_tables(l)
    qg = jnp.tile(qn_g[0], 2)[None]
    kg = jnp.tile(kn_g[0], 2)[None]
    q_x, k_x, v_x = _diff_prep(px_d, cos_t, sin_t, qg, kg, True)
    _, k_c, v_c = _diff_prep(pc_d, cos_t[:lc], sin_t[:lc], qg, kg, False)
    lam_vec = jnp.full((1, LANES), lam, F32)
    sg_vec = (subln_g[0] * (1.0 - lam_init))[None]
    score_bound = (1.05 * DIFF_QK_DIM * DIFF_SCALE * math.log2(math.e)
                   * jnp.max(jnp.abs(qn_g[0])) * jnp.max(jnp.abs(kn_g[0])))
    attn_args = (q_x, k_c, v_c, k_x, v_x, lam_vec, sg_vec)
    y_diff = lax.cond(score_bound <= SCORE_LOG2_LIMIT,
                      lambda a: _diff_attention(*a, bounded=True),
                      lambda a: _diff_attention(*a, bounded=False), attn_args)

    mixed = _merge(y_sweep, bonus, gate, y_diff, px_g, rwkv_lnx_g[0][None], rwkv_lnx_b[0][None],
                   w_pa[0].astype(BF16), w_pb[0].astype(BF16))
    n_r = N_GROUPS + N_EXPERTS
    w_router = jnp.zeros((d, ROUTER_PAD), F32).at[:, :N_GROUPS].set(router_g_w[0]).at[:, N_GROUPS:n_r].set(router_e_w[0])
    b_router = jnp.zeros((1, ROUTER_PAD), F32).at[0, :N_GROUPS].set(router_g_b[0]).at[0, N_GROUPS:n_r].set(router_e_b[0])
    x_new, h2, logits = _outproj(mixed, x, gt1, norm2_g[0][None], sc2, sh2, w_out[0].astype(BF16), w_router, b_router)

    n_tok = b * l
    slot_tok, slot_w, slot_of, blk_expert, n_used = _route(logits.reshape(n_tok, ROUTER_PAD), n_tok)
    xs = h2.reshape(n_tok, d)[slot_tok]
    out = _moe_ffn(xs, slot_w[:, None], blk_expert, n_used,
                   exp_w1[0].astype(BF16), exp_w3[0].astype(BF16), exp_w2[0].astype(BF16))
    y0 = out[slot_of[:, 0]].reshape(b, l, d)
    y1 = out[slot_of[:, 1]].reshape(b, l, d)
    return _final(x_new, gt2, y0, y1)
```

```python
import functools
import math

import jax
import jax.numpy as jnp
from jax import lax
from jax.experimental import pallas as pl
from jax.experimental.pallas import tpu as pltpu

F32 = jnp.float32
BF16 = jnp.bfloat16
HIGHEST = lax.Precision.HIGHEST

D_MODEL = 2048
GRID_W = 64
RWKV_HEADS = 16
RWKV_HEAD_DIM = 64
RWKV_DIM = RWKV_HEADS * RWKV_HEAD_DIM
DECAY_LORA = 96
AAA_LORA = 96
GATE_LORA = 256
RWKV_COLS = 3 * RWKV_DIM + 2 * DECAY_LORA + 2 * AAA_LORA + GATE_LORA
DIFF_HEADS = 8
DIFF_QK_DIM = 64
DIFF_V_DIM = 2 * DIFF_QK_DIM
DIFF_DIM = DIFF_HEADS * DIFF_V_DIM
DIFF_QK_COLS = DIFF_HEADS * 2 * DIFF_QK_DIM
DIFF_COLS = 2 * DIFF_QK_COLS + DIFF_DIM
DIFF_SCALE = DIFF_QK_DIM ** -0.5
ROPE_THETA = 10000.0
GATE_COLS = 2 * D_MODEL
N_GROUPS = 4
EXPERTS_PER_GROUP = 8
N_EXPERTS = N_GROUPS * EXPERTS_PER_GROUP
TOP_K = 2
D_EXPERT = 512
NORM_EPS = 1e-6
SUBLN_EPS = 1e-5
LNX_EPS = 64e-5

LANES = 128
SUBLANES = 8
VMEM_LIMIT_BYTES = 56 * 1024 * 1024

LORA_PAD = LANES
RWKV_PCOLS = 3 * RWKV_DIM + 4 * LORA_PAD + GATE_LORA
CHUNK = 64
V_EXT = 2 * DIFF_V_DIM
SCORE_LOG2_LIMIT = 60.0
ROUTER_PAD = LANES
MOE_TILE = 512


def _cparams(*sem):
    return pltpu.CompilerParams(dimension_semantics=sem, vmem_limit_bytes=VMEM_LIMIT_BYTES)


def _sigmoid(x):
    return 1.0 / (1.0 + jnp.exp(-x))


def _dot(a, b, dims=(((1,), (0,)), ((), ()))):
    return lax.dot_general(a.astype(BF16), b.astype(BF16), dims, preferred_element_type=F32)


def _dot_f32(a, b, dims=(((1,), (0,)), ((), ()))):
    return lax.dot_general(a, b, dims, precision=HIGHEST, preferred_element_type=F32)


_NT = (((1,), (1,)), ((), ()))
_TN = (((0,), (0,)), ((), ()))


def _bf16_parts(x, n):
    parts = []
    for _ in range(n):
        p = x.astype(BF16)
        parts.append(p)
        x = x - p.astype(F32)
    return parts


def _dot_split(x, w_b, n, lhs=True):
    parts = _bf16_parts(x, n)
    outs = [jnp.dot(p, w_b, preferred_element_type=F32) if lhs else jnp.dot(w_b, p, preferred_element_type=F32)
            for p in parts]
    return functools.reduce(lambda a, b: a + b, outs)


def _group_ones(width, group):
    r = lax.broadcasted_iota(jnp.int32, (width, width), 0) // group
    c = lax.broadcasted_iota(jnp.int32, (width, width), 1) // group
    return (r == c).astype(BF16)


def _group_sum(x, group):
    ones = _group_ones(LANES, group)
    parts = [_dot_split(x[:, j * LANES:(j + 1) * LANES], ones, 2) for j in range(x.shape[1] // LANES)]
    return parts[0] if len(parts) == 1 else jnp.concatenate(parts, axis=1)


def _mod_kernel(c_ref, w_ref, b_ref, o_ref):
    c = c_ref[...]
    o_ref[...] = _dot_f32(c * _sigmoid(c), w_ref[...]) + b_ref[...]


def _modulation(cm, ada_w, ada_b):
    rows, d = cm.shape
    n = ada_w.shape[1]
    tn = 1536
    return pl.pallas_call(
        _mod_kernel,
        grid=(n // tn,),
        in_specs=[pl.BlockSpec((rows, d), lambda j: (0, 0)),
                  pl.BlockSpec((d, tn), lambda j: (0, j)),
                  pl.BlockSpec((1, tn), lambda j: (0, j))],
        out_specs=pl.BlockSpec((rows, tn), lambda j: (0, j)),
        out_shape=jax.ShapeDtypeStruct((rows, n), F32),
        compiler_params=_cparams("parallel"),
        name="modulation",
    )(cm, ada_w, ada_b.reshape(1, n))


def _inproj_kernel(x_ref, g_ref, sc_ref, sh_ref, w_ref, o_ref, h_ref):
    @pl.when(pl.program_id(2) == 0)
    def _():
        x = x_ref[0]
        y = x * lax.rsqrt(jnp.mean(x * x, axis=-1, keepdims=True) + NORM_EPS) * g_ref[...]
        h_ref[...] = (y * (1.0 + sc_ref[0]) + sh_ref[0]).astype(BF16)

    o_ref[0] = jnp.dot(h_ref[...], w_ref[...], preferred_element_type=F32).astype(o_ref.dtype)


def _pick(n, prefs):
    for t in prefs:
        if n % t == 0:
            return t
    return n


def _norm_proj(x, g, sc, sh, w, out_dtype):
    b, l, d = x.shape
    n = w.shape[1]
    tm = _pick(l, (1024, 512, 256, 128))
    tn = _pick(n, (1024, 768, 512, 256, 128))
    return pl.pallas_call(
        _inproj_kernel,
        grid=(b, l // tm, n // tn),
        in_specs=[pl.BlockSpec((1, tm, d), lambda bi, i, j: (bi, i, 0)),
                  pl.BlockSpec((1, d), lambda bi, i, j: (0, 0)),
                  pl.BlockSpec((1, 1, d), lambda bi, i, j: (bi, 0, 0)),
                  pl.BlockSpec((1, 1, d), lambda bi, i, j: (bi, 0, 0)),
                  pl.BlockSpec((d, tn), lambda bi, i, j: (0, j))],
        out_specs=pl.BlockSpec((1, tm, tn), lambda bi, i, j: (bi, i, j)),
        out_shape=jax.ShapeDtypeStruct((b, l, n), out_dtype),
        scratch_shapes=[pltpu.VMEM((tm, d), BF16)],
        compiler_params=_cparams("parallel", "parallel", "arbitrary"),
        name="norm_proj",
    )(x, g.reshape(1, d), sc, sh, w)


def _mm(a, b, dims=(((1,), (0,)), ((), ())), exact=False):
    return _dot_f32(a, b, dims) if exact else _dot(a, b, dims)


def _tri_inverse_all(a_list, exact):
    n = CHUNK
    eye = (lax.broadcasted_iota(jnp.int32, (n, n), 0) == lax.broadcasted_iota(jnp.int32, (n, n), 1)).astype(F32)
    xs = [eye + a for a in a_list]
    ps = [_mm(a, a, exact=exact) for a in a_list]
    steps = int(math.log2(n)) - 1
    for s in range(steps):
        if s < steps - 1:
            xps = [_mm(jnp.concatenate([x, p], axis=0), p, exact=exact) for x, p in zip(xs, ps)]
            xs = [x + xp[:n] for x, xp in zip(xs, xps)]
            ps = [xp[n:] for xp in xps]
        else:
            xs = [x + _mm(x, p, exact=exact) for x, p in zip(xs, ps)]
    return xs


def _rwkv_chunk_kernel(p_ref, pp_ref, pn_ref, mu_ref, w0_ref, w2_ref, a0_ref, a2_ref, g2_ref, kk_ref, ka_ref,
                       rk_ref, q_ref, y0_ref, m_ref, n_ref, bonus_ref, gate_ref):
    i = pl.program_id(1)
    last = pl.num_programs(1) - 1
    c = RWKV_DIM
    hd = RWKV_HEAD_DIM
    heads = range(RWKV_HEADS)
    p = p_ref[0]
    row = lax.broadcasted_iota(jnp.int32, (CHUNK, 1), 0)
    prev_row = jnp.where(i == 0, 0.0, pp_ref[0, SUBLANES - 1:SUBLANES, :])
    next_row = jnp.where(i == last, 0.0, pn_ref[0, 0:1, :])
    prev = jnp.where(row == 0, prev_row, pltpu.roll(p, 1, axis=0))
    nxt = jnp.where(row == CHUNK - 1, next_row, pltpu.roll(p, CHUNK - 1, axis=0))
    ps = p + mu_ref[...] * (0.5 * (prev + nxt) - p)

    r, k, v = ps[:, :c], ps[:, c:2 * c], ps[:, 2 * c:3 * c]
    o = 3 * c
    xw = (ps[:, o:o + LORA_PAD], ps[:, o + LORA_PAD:o + 2 * LORA_PAD])
    xa = (ps[:, o + 2 * LORA_PAD:o + 3 * LORA_PAD], ps[:, o + 3 * LORA_PAD:o + 4 * LORA_PAD])
    xg = ps[:, o + 4 * LORA_PAD:]

    gate_ref[0] = _dot(_sigmoid(xg), g2_ref[...])
    kk = k * kk_ref[...]
    kk = kk * lax.rsqrt(_group_sum(kk * kk, hd) + 1e-12)

    tr = lax.broadcasted_iota(jnp.int32, (CHUNK, CHUNK), 0)
    tc = lax.broadcasted_iota(jnp.int32, (CHUNK, CHUNK), 1)
    eye = tr == tc
    tr2 = lax.broadcasted_iota(jnp.int32, (CHUNK, 2 * CHUNK), 0)
    tc2 = lax.broadcasted_iota(jnp.int32, (CHUNK, 2 * CHUNK), 1) % CHUNK
    kd_sum = jnp.zeros_like(k)
    v_b = v.astype(BF16)
    hsl = [slice(h * hd, (h + 1) * hd) for h in heads]
    for d in range(2):
        before = (tc < tr) if d == 0 else (tc > tr)
        upto = (tc <= tr) if d == 0 else (tc >= tr)
        upto2 = (tc2 <= tr2) if d == 0 else (tc2 >= tr2)
        z = w0_ref[d:d + 1, :] + _dot(jnp.tanh(xw[d]), w2_ref[d])
        w_log = -(jnp.maximum(-z, 0.0) + jnp.log(1.0 + jnp.exp(-jnp.abs(z)))) - 0.5
        logw = -jnp.exp(w_log)
        a = _sigmoid(a0_ref[d:d + 1, :] + _dot(xa[d], a2_ref[d]))
        kd = k * (1.0 + (a - 1.0) * ka_ref[...])
        kd_sum = kd_sum + kd
        cum = _dot_split(logw, upto.astype(BF16), 3, lhs=False)
        total = cum[CHUNK - 1:CHUNK, :] if d == 0 else cum[0:1, :]
        e_neg = jnp.exp(-cum)
        e_rest = jnp.exp(total - cum)
        p_total = jnp.exp(total)
        beta = kk * a
        al = -kk * jnp.exp(cum - logw)
        rt = r * jnp.exp(cum)
        al_b = al.astype(BF16)
        rt_b = rt.astype(BF16)
        bt_b = (beta * e_neg).astype(BF16)
        kt_b = (kd * e_neg).astype(BF16)
        bh_b = (beta * e_rest).astype(BF16)
        kh_b = (kd * e_rest).astype(BF16)
        g = [_mm(jnp.concatenate([al_b[:, s], rt_b[:, s]], axis=0),
                 jnp.concatenate([bt_b[:, s], kt_b[:, s]], axis=0), _NT) for s in hsl]
        a_ab = [jnp.where(before, gh[:CHUNK, :CHUNK], 0.0) for gh in g]
        a_ak = [jnp.where(before, gh[:CHUNK, CHUNK:], 0.0) for gh in g]
        a_rbk = [jnp.where(upto2, gh[CHUNK:, :], 0.0).astype(BF16) for gh in g]
        akv = [_mm(ak, v_b[:, s]) for ak, s in zip(a_ak, hsl)]
        t_inv = _tri_inverse_all(a_ab, exact=False)
        wu = [_mm(t, jnp.concatenate([al[:, s], u], axis=1)) for t, u, s in zip(t_inv, akv, hsl)]
        zv = jnp.zeros((CHUNK, hd), BF16)
        zmat = [jnp.concatenate([x.astype(BF16), jnp.concatenate([zv, v_b[:, s]], axis=1)], axis=0)
                for x, s in zip(wu, hsl)]
        qy = [_mm(ar, zm) for ar, zm in zip(a_rbk, zmat)]
        mn = [_mm(zm, jnp.concatenate([bh_b[:, s], kh_b[:, s]], axis=0), _TN)
              for zm, s in zip(zmat, hsl)]
        q_ref[0, d] = rt + jnp.concatenate([x[:, :hd] for x in qy], axis=1)
        y0_ref[0, d] = jnp.concatenate([x[:, hd:] for x in qy], axis=1)
        m_ref[0, d, 0] = jnp.concatenate([jnp.where(eye, p_total[:, s], 0.0) + x[:hd] for x, s in zip(mn, hsl)], axis=1)
        n_ref[0, d, 0] = jnp.concatenate([x[hd:] for x in mn], axis=1)
    bonus_ref[0] = _group_sum(r * kd_sum * rk_ref[...], hd) * v


def _rwkv_chunk_ops(p, params):
    b, l, pc = p.shape
    nc = l // CHUNK
    c = RWKV_DIM
    hb = CHUNK // SUBLANES
    nb8 = l // SUBLANES
    mu, w0, w2, a0, a2, g2, k_k, k_a, r_k = params
    const = lambda shape: pl.BlockSpec(shape, lambda bi, i: (0,) * len(shape))
    tok = lambda: pl.BlockSpec((1, CHUNK, c), lambda bi, i: (bi, i, 0))
    return pl.pallas_call(
        _rwkv_chunk_kernel,
        grid=(b, nc),
        in_specs=[pl.BlockSpec((1, CHUNK, pc), lambda bi, i: (bi, i, 0)),
                  pl.BlockSpec((1, SUBLANES, pc), lambda bi, i: (bi, jnp.maximum(i * hb - 1, 0), 0)),
                  pl.BlockSpec((1, SUBLANES, pc), lambda bi, i: (bi, jnp.minimum((i + 1) * hb, nb8 - 1), 0)),
                  const((1, pc)), const((2, c)), const((2, LORA_PAD, c)), const((2, c)), const((2, LORA_PAD, c)),
                  const((GATE_LORA, c)), const((1, c)), const((1, c)), const((1, c))],
        out_specs=[pl.BlockSpec((1, 2, CHUNK, c), lambda bi, i: (bi, 0, i, 0)),
                   pl.BlockSpec((1, 2, CHUNK, c), lambda bi, i: (bi, 0, i, 0)),
                   pl.BlockSpec((1, 2, 1, RWKV_HEAD_DIM, c), lambda bi, i: (bi, 0, i, 0, 0)),
                   pl.BlockSpec((1, 2, 1, RWKV_HEAD_DIM, c), lambda bi, i: (bi, 0, i, 0, 0)),
                   tok(), tok()],
        out_shape=[jax.ShapeDtypeStruct((b, 2, l, c), F32), jax.ShapeDtypeStruct((b, 2, l, c), F32),
                   jax.ShapeDtypeStruct((b, 2, nc, RWKV_HEAD_DIM, c), F32),
                   jax.ShapeDtypeStruct((b, 2, nc, RWKV_HEAD_DIM, c), F32),
                   jax.ShapeDtypeStruct((b, l, c), F32), jax.ShapeDtypeStruct((b, l, c), F32)],
        compiler_params=_cparams("parallel", "parallel"),
        name="rwkv_chunk_ops",
    )(p, p, p, mu, w0, w2, a0, a2, g2, k_k, k_a, r_k)


def _rwkv_sweep_kernel(qf_ref, y0f_ref, mf_ref, nf_ref, qb_ref, y0b_ref, mb_ref, nb_ref, s0_ref,
                       yf_ref, yb_ref, sfin_ref, s_ref):
    j = pl.program_id(1)
    hd = RWKV_HEAD_DIM

    @pl.when(j == 0)
    def _():
        s_ref[...] = s0_ref[0]

    dirs = ((qf_ref, y0f_ref, mf_ref, nf_ref, yf_ref), (qb_ref, y0b_ref, mb_ref, nb_ref, yb_ref))
    hsl = [slice(h * hd, (h + 1) * hd) for h in range(RWKV_HEADS)]
    for d, (q_ref, y0_ref, m_ref, n_ref, y_ref) in enumerate(dirs):
        s_b = s_ref[d].astype(BF16)
        q_b = q_ref[0, 0].astype(BF16)
        m_b = m_ref[0, 0, 0].astype(BF16)
        ys = [_dot(q_b[:, s], s_b[:, s], _NT) for s in hsl]
        sm = [_dot(s_b[:, s], m_b[:, s]) for s in hsl]
        y_ref[0] = y0_ref[0, 0] + jnp.concatenate(ys, axis=1)
        s_ref[d] = n_ref[0, 0, 0] + jnp.concatenate(sm, axis=1)

    @pl.when(j == pl.num_programs(1) - 1)
    def _():
        sfin_ref[0] = s_ref[...]


def _rwkv_sweep(q, y0, m, n, s0):
    b, _, l, c = q.shape
    nc = l // CHUNK
    hd = RWKV_HEAD_DIM
    tokf = lambda: pl.BlockSpec((1, 1, CHUNK, c), lambda bi, j: (bi, 0, j, 0))
    tokb = lambda: pl.BlockSpec((1, 1, CHUNK, c), lambda bi, j: (bi, 1, nc - 1 - j, 0))
    opf = lambda: pl.BlockSpec((1, 1, 1, hd, c), lambda bi, j: (bi, 0, j, 0, 0))
    opb = lambda: pl.BlockSpec((1, 1, 1, hd, c), lambda bi, j: (bi, 1, nc - 1 - j, 0, 0))
    return pl.pallas_call(
        _rwkv_sweep_kernel,
        grid=(b, nc),
        in_specs=[tokf(), tokf(), opf(), opf(), tokb(), tokb(), opb(), opb(),
                  pl.BlockSpec((1, 2, hd, c), lambda bi, j: (bi, 0, 0, 0))],
        out_specs=[pl.BlockSpec((1, CHUNK, c), lambda bi, j: (bi, j, 0)),
                   pl.BlockSpec((1, CHUNK, c), lambda bi, j: (bi, nc - 1 - j, 0)),
                   pl.BlockSpec((1, 2, hd, c), lambda bi, j: (bi, 0, 0, 0))],
        out_shape=[jax.ShapeDtypeStruct((b, l, c), F32), jax.ShapeDtypeStruct((b, l, c), F32),
                   jax.ShapeDtypeStruct((b, 2, hd, c), F32)],
        scratch_shapes=[pltpu.VMEM((2, hd, c), F32)],
        compiler_params=_cparams("parallel", "arbitrary"),
        name="rwkv_sweep",
    )(q, y0, m, n, q, y0, m, n, s0)


def _rwkv_scans(px_r, pc_r, rparams):
    b = px_r.shape[0]
    qc, y0c, mc, nc_, _, _ = _rwkv_chunk_ops(pc_r, rparams)
    s_zero = jnp.zeros((b, 2, RWKV_HEAD_DIM, RWKV_DIM), F32)
    _, _, s_ctx = _rwkv_sweep(qc, y0c, mc, nc_, s_zero)
    qx, y0x, mx, nx, bonus, gate = _rwkv_chunk_ops(px_r, rparams)
    yf, yb, _ = _rwkv_sweep(qx, y0x, mx, nx, s_ctx)
    return (yf, yb), bonus, gate


def _diff_prep_kernel(p_ref, cos_ref, sin_ref, qg_ref, kg_ref, q_ref, k_ref, v_ref, *, rope):
    lane = lax.broadcasted_iota(jnp.int32, (1, LANES), 1)
    first = (lane % 32) < 16
    for hd in range(DIFF_HEADS):
        for off, g_ref, o_ref, scale in ((0, qg_ref, q_ref, DIFF_SCALE * math.log2(math.e)),
                                         (DIFF_QK_COLS, kg_ref, k_ref, 1.0)):
            cs = slice(hd * LANES, (hd + 1) * LANES)
            xb = p_ref[0, :, off + hd * LANES:off + (hd + 1) * LANES].astype(F32)
            ms = _group_sum(xb * xb, DIFF_QK_DIM) * (1.0 / DIFF_QK_DIM)
            y = xb * lax.rsqrt(ms + NORM_EPS) * g_ref[...]
            if rope:
                swapped = jnp.where(first, pltpu.roll(y, LANES - 16, axis=1), pltpu.roll(y, 16, axis=1))
                y = y * cos_ref[...] + swapped * sin_ref[...]
            o_ref[0, :, cs] = (y * scale).astype(BF16)
    ones = jnp.ones((p_ref.shape[1], V_EXT - DIFF_V_DIM), BF16)
    for hd in range(DIFF_HEADS):
        vb = p_ref[0, :, 2 * DIFF_QK_COLS + hd * DIFF_V_DIM:2 * DIFF_QK_COLS + (hd + 1) * DIFF_V_DIM]
        v_ref[0, :, hd * V_EXT:(hd + 1) * V_EXT] = jnp.concatenate([vb.astype(BF16), ones], axis=1)


def _diff_prep(p, cos_t, sin_t, qg, kg, rope):
    b, l, pc = p.shape
    t = _pick(l, (512, 256, 128))
    tok = lambda w=DIFF_DIM: pl.BlockSpec((1, t, w), lambda bi, i: (bi, i, 0))
    shp = jax.ShapeDtypeStruct((b, l, DIFF_DIM), BF16)
    shp_v = jax.ShapeDtypeStruct((b, l, DIFF_HEADS * V_EXT), BF16)
    return pl.pallas_call(
        functools.partial(_diff_prep_kernel, rope=rope),
        grid=(b, l // t),
        in_specs=[pl.BlockSpec((1, t, pc), lambda bi, i: (bi, i, 0)),
                  pl.BlockSpec((t, LANES), lambda bi, i: (i, 0)),
                  pl.BlockSpec((t, LANES), lambda bi, i: (i, 0)),
                  pl.BlockSpec((1, LANES), lambda bi, i: (0, 0)),
                  pl.BlockSpec((1, LANES), lambda bi, i: (0, 0))],
        out_specs=[tok(), tok(), tok(DIFF_HEADS * V_EXT)],
        out_shape=[shp, shp, shp_v],
        compiler_params=_cparams("parallel", "parallel"),
        name="diff_prep_rope" if rope else "diff_prep",
    )(p, cos_t, sin_t, qg, kg)


def _diff_finish(acc1, acc2, lam_ref, sg_ref, o_ref):
    dv = DIFF_V_DIM
    o = acc1[:, :dv] / acc1[:, dv:] - lam_ref[...] * (acc2[:, :dv] / acc2[:, dv:])
    o = o * lax.rsqrt(jnp.mean(o * o, axis=-1, keepdims=True) + SUBLN_EPS)
    o_ref[0] = (o * sg_ref[...]).astype(o_ref.dtype)


def _flash_online_kernel(lam_ref, sg_ref, q_ref, kc_ref, vc_ref, k_ref, v_ref, o_ref, *, tk):
    qd = DIFF_QK_DIM
    tq = q_ref.shape[1]
    q = q_ref[0]
    qs = (q[:, :qd], q[:, qd:])

    def absorb(state, k, v):
        reps = k.shape[0] // LANES
        out = []
        for mp in range(2):
            m_prev, acc = state[mp]
            s = lax.dot_general(qs[mp], k[:, mp * qd:(mp + 1) * qd], _NT, preferred_element_type=F32)
            m_new = jnp.maximum(m_prev, jnp.max(s, axis=-1, keepdims=True))
            alpha = jnp.exp2(m_prev - m_new)
            pr = jnp.exp2(s - jnp.concatenate([m_new] * reps, axis=1))
            acc_new = jnp.concatenate([alpha, alpha], axis=1) * acc + jnp.dot(pr.astype(BF16), v, preferred_element_type=F32)
            out.append((m_new, acc_new))
        return tuple(out)

    init = (jnp.full((tq, LANES), -jnp.inf, F32), jnp.zeros((tq, V_EXT), F32))
    state = absorb((init, init), kc_ref[0], vc_ref[0])

    def body(j, state):
        rows = pl.ds(pl.multiple_of(j * tk, tk), tk)
        return absorb(state, k_ref[0, rows, :], v_ref[0, rows, :])

    (_, acc1), (_, acc2) = lax.fori_loop(0, k_ref.shape[1] // tk, body, state)
    _diff_finish(acc1, acc2, lam_ref, sg_ref, o_ref)


def _flash_bounded_kernel(lam_ref, sg_ref, q_ref, kc_ref, vc_ref, k_ref, v_ref, o_ref, acc_ref, p_ref, *, tk):
    qd = DIFF_QK_DIM
    q = q_ref[0]
    qs = (q[:, :qd], q[:, qd:])
    n_kv = k_ref.shape[1] // tk

    def weights(mp, k):
        s = lax.dot_general(qs[mp], k[:, mp * qd:(mp + 1) * qd], _NT, preferred_element_type=F32)
        return jnp.exp2(s).astype(BF16)

    def chunk(c):
        return pl.ds(pl.multiple_of(jnp.minimum(c, n_kv - 1) * tk, tk), tk)

    for mp in range(2):
        acc_ref[mp] = jnp.dot(weights(mp, kc_ref[0]), vc_ref[0], preferred_element_type=F32)
        p_ref[0, mp] = weights(mp, k_ref[0, pl.ds(0, tk), :])

    steps = 4 if n_kv % 4 == 0 else 2

    def body(j, carry):
        for u in range(steps):
            c = steps * j + u
            v = v_ref[0, chunk(c), :]
            k_next = k_ref[0, chunk(c + 1), :]
            for mp in range(2):
                acc_ref[mp] += jnp.dot(p_ref[u % 2, mp], v, preferred_element_type=F32)
                p_ref[1 - u % 2, mp] = weights(mp, k_next)
        return carry

    lax.fori_loop(0, n_kv // steps, body, 0)
    _diff_finish(acc_ref[0], acc_ref[1], lam_ref, sg_ref, o_ref)


def _diff_attention(q, kc, vc, k, v, lam_vec, sg_vec, bounded):
    b, l, _ = q.shape
    lc = kc.shape[1]
    tq = _pick(l, (256, 128))
    tk = _pick(l, (512, 256, 128))
    if bounded:
        tk = _pick(l // 2, (512, 256, 128))
        assert l % (2 * tk) == 0
        body = functools.partial(_flash_bounded_kernel, tk=tk)
        scratch = [pltpu.VMEM((2, tq, V_EXT), F32), pltpu.VMEM((2, 2, tq, tk), BF16)]
    else:
        body = functools.partial(_flash_online_kernel, tk=tk)
        scratch = []
    return pl.pallas_call(
        body,
        grid=(b, DIFF_HEADS, l // tq),
        in_specs=[pl.BlockSpec((1, LANES), lambda bi, h, i: (0, 0)),
                  pl.BlockSpec((1, LANES), lambda bi, h, i: (0, 0)),
                  pl.BlockSpec((1, tq, LANES), lambda bi, h, i: (bi, i, h)),
                  pl.BlockSpec((1, lc, LANES), lambda bi, h, i: (bi, 0, h)),
                  pl.BlockSpec((1, lc, V_EXT), lambda bi, h, i: (bi, 0, h)),
                  pl.BlockSpec((1, l, LANES), lambda bi, h, i: (bi, 0, h)),
                  pl.BlockSpec((1, l, V_EXT), lambda bi, h, i: (bi, 0, h))],
        out_specs=pl.BlockSpec((1, tq, LANES), lambda bi, h, i: (bi, i, h)),
        out_shape=jax.ShapeDtypeStruct((b, l, DIFF_DIM), BF16),
        scratch_shapes=scratch,
        compiler_params=_cparams("parallel", "parallel", "arbitrary"),
        name="diff_flash_bounded" if bounded else "diff_flash_online",
    )(lam_vec, sg_vec, q, kc, vc, k, v)


def _merge_kernel(yf_ref, yb_ref, bonus_ref, gate_ref, yd_ref, pg_ref, lg_ref, lb_ref, wpa_ref, wpb_ref, o_ref):
    hd = RWKV_HEAD_DIM
    y = yf_ref[0] + yb_ref[0]
    dev = y - _group_sum(y, hd) * (1.0 / hd)
    var = _group_sum(dev * dev, hd) * (1.0 / hd)
    yn = dev * lax.rsqrt(var + LNX_EPS) * lg_ref[...] + lb_ref[...]
    y_rwkv = (yn + bonus_ref[0]) * gate_ref[0]
    a = _dot(y_rwkv, wpa_ref[...])
    bb = jnp.dot(yd_ref[0], wpb_ref[...], preferred_element_type=F32)
    ga = _sigmoid(pg_ref[0, :, :D_MODEL].astype(F32))
    gb = _sigmoid(pg_ref[0, :, D_MODEL:].astype(F32))
    o_ref[0] = (ga * a + gb * bb).astype(BF16)


def _merge(y_sweep, bonus, gate, y_diff, p_gate, lnx_g, lnx_b, w_pa, w_pb):
    b, l, c = bonus.shape
    d = D_MODEL
    tm = _pick(l, (256, 128))
    const = lambda shape: pl.BlockSpec(shape, lambda bi, i: (0,) * len(shape))
    tok = lambda w: pl.BlockSpec((1, tm, w), lambda bi, i: (bi, i, 0))
    return pl.pallas_call(
        _merge_kernel,
        grid=(b, l // tm),
        in_specs=[tok(c), tok(c), tok(c), tok(c), tok(DIFF_DIM), tok(GATE_COLS),
                  const((1, c)), const((1, c)), const((c, d)), const((DIFF_DIM, d))],
        out_specs=tok(d),
        out_shape=jax.ShapeDtypeStruct((b, l, d), BF16),
        compiler_params=_cparams("parallel", "parallel"),
        name="merge",
    )(y_sweep[0], y_sweep[1], bonus, gate, y_diff, p_gate, lnx_g, lnx_b, w_pa, w_pb)


def _outproj_kernel(mx_ref, x_ref, gt_ref, g_ref, sc_ref, sh_ref, wo_ref, wr_ref, br_ref, xn_ref, h_ref, lg_ref):
    mix = jnp.dot(mx_ref[0], wo_ref[...], preferred_element_type=F32)
    xn = x_ref[0] + gt_ref[0] * mix
    xn_ref[0] = xn
    y = xn * lax.rsqrt(jnp.mean(xn * xn, axis=-1, keepdims=True) + NORM_EPS) * g_ref[...]
    h = y * (1.0 + sc_ref[0]) + sh_ref[0]
    h_hi, h_lo = _bf16_parts(h, 2)
    h_ref[0] = h_hi
    dot = lambda a, b: jnp.dot(a, b, preferred_element_type=F32)
    lg_ref[0] = dot(h_hi, wr_ref[0]) + (dot(h_lo, wr_ref[0]) + dot(h_hi, wr_ref[1])) + br_ref[...]


def _outproj(mixed, x, gt1, g2, sc2, sh2, w_out, w_router, b_router):
    b, l, d = x.shape
    tm = _pick(l, (512, 256, 128))
    const = lambda shape: pl.BlockSpec(shape, lambda bi, i: (0,) * len(shape))
    tok = lambda w: pl.BlockSpec((1, tm, w), lambda bi, i: (bi, i, 0))
    per_b = lambda: pl.BlockSpec((1, 1, d), lambda bi, i: (bi, 0, 0))
    return pl.pallas_call(
        _outproj_kernel,
        grid=(b, l // tm),
        in_specs=[tok(d), tok(d), per_b(), const((1, d)), per_b(), per_b(),
                  const((d, d)), const((2, d, ROUTER_PAD)), const((1, ROUTER_PAD))],
        out_specs=[tok(d), tok(d), tok(ROUTER_PAD)],
        out_shape=[jax.ShapeDtypeStruct((b, l, d), F32), jax.ShapeDtypeStruct((b, l, d), BF16),
                   jax.ShapeDtypeStruct((b, l, ROUTER_PAD), F32)],
        compiler_params=_cparams("parallel", "parallel"),
        name="outproj_router",
    )(mixed, x, gt1, g2, sc2, sh2, w_out, w_router, b_router)


def _moe_kernel(be_ref, nu_ref, x_ref, sw_ref, w1_ref, w3_ref, w2_ref, o_ref, w1b_ref, w3b_ref, w2b_ref):
    i = pl.program_id(0)
    used = i < nu_ref[0]

    @pl.when(used & ((i == 0) | (be_ref[i] != be_ref[jnp.maximum(i - 1, 0)])))
    def _():
        w1b_ref[...] = w1_ref[0].astype(BF16)
        w3b_ref[...] = w3_ref[0].astype(BF16)
        w2b_ref[...] = w2_ref[0].astype(BF16)

    @pl.when(used)
    def _():
        xb = x_ref[...]
        u = jnp.dot(xb, w1b_ref[...], preferred_element_type=F32)
        g = jnp.dot(xb, w3b_ref[...], preferred_element_type=F32)
        hmid = (u * _sigmoid(u) * g).astype(BF16)
        o_ref[...] = (jnp.dot(hmid, w2b_ref[...], preferred_element_type=F32) * sw_ref[...]).astype(o_ref.dtype)

    @pl.when(jnp.logical_not(used))
    def _():
        o_ref[...] = jnp.zeros(o_ref.shape, o_ref.dtype)


def _moe_ffn(xs, sw, blk_expert, n_used, w1, w3, w2):
    cap, d = xs.shape
    nb = cap // MOE_TILE
    grid_spec = pltpu.PrefetchScalarGridSpec(
        num_scalar_prefetch=2,
        grid=(nb,),
        in_specs=[pl.BlockSpec((MOE_TILE, d), lambda i, be, nu: (i, 0)),
                  pl.BlockSpec((MOE_TILE, 1), lambda i, be, nu: (i, 0)),
                  pl.BlockSpec((1, d, D_EXPERT), lambda i, be, nu: (be[i], 0, 0)),
                  pl.BlockSpec((1, d, D_EXPERT), lambda i, be, nu: (be[i], 0, 0)),
                  pl.BlockSpec((1, D_EXPERT, d), lambda i, be, nu: (be[i], 0, 0))],
        out_specs=pl.BlockSpec((MOE_TILE, d), lambda i, be, nu: (i, 0)),
        scratch_shapes=[pltpu.VMEM((d, D_EXPERT), BF16), pltpu.VMEM((d, D_EXPERT), BF16),
                        pltpu.VMEM((D_EXPERT, d), BF16)],
    )
    return pl.pallas_call(
        _moe_kernel,
        grid_spec=grid_spec,
        out_shape=jax.ShapeDtypeStruct((cap, d), BF16),
        compiler_params=_cparams("arbitrary"),
        name="moe_ffn",
    )(blk_expert, n_used, xs, sw, w1, w3, w2)


def _final_kernel(x_ref, gt_ref, y0_ref, y1_ref, o_ref):
    o_ref[0] = x_ref[0] + gt_ref[0] * (y0_ref[0].astype(F32) + y1_ref[0].astype(F32))


def _final(x_new, gt2, y0, y1):
    b, l, d = x_new.shape
    tm = _pick(l, (512, 256, 128))
    tok = lambda: pl.BlockSpec((1, tm, d), lambda bi, i: (bi, i, 0))
    return pl.pallas_call(
        _final_kernel,
        grid=(b, l // tm),
        in_specs=[tok(), pl.BlockSpec((1, 1, d), lambda bi, i: (bi, 0, 0)), tok(), tok()],
        out_specs=tok(),
        out_shape=jax.ShapeDtypeStruct((b, l, d), F32),
        compiler_params=_cparams("parallel", "parallel"),
        name="moe_residual",
    )(x_new, gt2, y0, y1)


def _pad_lora_cols(w, widths):
    parts, o = [], 0
    for wd in widths:
        blk = w[..., o:o + wd]
        parts.append(jnp.pad(blk, [(0, 0)] * (w.ndim - 1) + [(0, LORA_PAD - wd)]))
        o += wd
    return jnp.concatenate(parts, axis=-1)


def _rope_tables(l):
    half = DIFF_QK_DIM // 2
    inv_freq = ROPE_THETA ** (-jnp.arange(0, half, 2, dtype=F32) / half)
    t = jnp.arange(l, dtype=jnp.int32)
    rows = (t // GRID_W).astype(F32)[:, None] * inv_freq
    cols = (t % GRID_W).astype(F32)[:, None] * inv_freq
    cos64 = jnp.concatenate([jnp.cos(rows), jnp.cos(rows), jnp.cos(cols), jnp.cos(cols)], axis=1)
    sin64 = jnp.concatenate([-jnp.sin(rows), jnp.sin(rows), -jnp.sin(cols), jnp.sin(cols)], axis=1)
    return jnp.tile(cos64, (1, 2)), jnp.tile(sin64, (1, 2))


def _route(logits, n_tok):
    g_logits = logits[:, :N_GROUPS]
    e_logits = logits[:, N_GROUPS:N_GROUPS + N_EXPERTS].reshape(n_tok, N_GROUPS, EXPERTS_PER_GROUP)
    p_group = jax.nn.softmax(g_logits, axis=-1)
    g_top = jnp.argmax(g_logits, axis=-1)
    p_g = jnp.take_along_axis(p_group, g_top[:, None], axis=1)[:, 0]
    e_sel = jnp.take_along_axis(e_logits, g_top[:, None, None], axis=1)[:, 0]
    top_p, top_i = lax.top_k(jax.nn.softmax(e_sel, axis=-1), TOP_K)
    top_p = top_p / jnp.sum(top_p, axis=-1, keepdims=True)
    gate = p_g[:, None] * top_p
    expert = g_top[:, None] * EXPERTS_PER_GROUP + top_i

    n_assign = n_tok * TOP_K
    flat_e = expert.reshape(-1).astype(jnp.int32)
    flat_tok = jnp.repeat(jnp.arange(n_tok, dtype=jnp.int32), TOP_K)
    flat_w = gate.reshape(-1)
    ids = jnp.arange(n_assign, dtype=jnp.int32)
    _, order = lax.sort((flat_e, ids), num_keys=1, is_stable=True)
    _, rank = lax.sort((order, ids), num_keys=1)
    counts = jnp.sum((flat_e[:, None] == jnp.arange(N_EXPERTS, dtype=jnp.int32)).astype(jnp.int32), axis=0)
    padded = (counts + MOE_TILE - 1) // MOE_TILE * MOE_TILE
    pend = jnp.cumsum(padded)
    pstart = pend - padded
    start = jnp.cumsum(counts) - counts
    cap = (n_assign + MOE_TILE - 1) // MOE_TILE * MOE_TILE + N_EXPERTS * MOE_TILE
    nb = cap // MOE_TILE
    blk_expert = jnp.minimum(
        jnp.searchsorted(pend, jnp.arange(nb, dtype=jnp.int32) * MOE_TILE, side='right'),
        N_EXPERTS - 1).astype(jnp.int32)
    n_used = (pend[-1] // MOE_TILE).astype(jnp.int32).reshape(1)
    s_ids = jnp.arange(cap, dtype=jnp.int32)
    s_exp = jnp.repeat(blk_expert, MOE_TILE)
    pos = s_ids - pstart[s_exp]
    valid = (pos < counts[s_exp]) & (s_ids < pend[-1])
    src = order[jnp.clip(start[s_exp] + pos, 0, n_assign - 1)]
    slot_tok = jnp.where(valid, flat_tok[src], 0)
    slot_w = jnp.where(valid, flat_w[src], 0.0)
    slot_of = (pstart[flat_e] + rank - start[flat_e]).astype(jnp.int32).reshape(n_tok, TOP_K)
    return slot_tok, slot_w, slot_of, blk_expert, n_used


def kernel(x, c, ctx, c_ctx, ada_w, ada_b, norm1_g, norm2_g, w_in, shift_mu, rwkv_w0, rwkv_w2, rwkv_a0, rwkv_a2,
           rwkv_g2, rwkv_k_k, rwkv_k_a, rwkv_r_k, rwkv_lnx_g, rwkv_lnx_b, qn_g, kn_g, diff_lambda, subln_g,
           w_pa, w_pb, w_out, router_g_w, router_g_b, router_e_w, router_e_b, exp_w1, exp_w3, exp_w2):
    assert ada_w.shape[0] == 1, "single-layer block"
    b, l, d = x.shape
    lc = ctx.shape[1]
    lam_init = 0.8 - 0.6 * math.exp(-0.3 * 0)
    lv = diff_lambda[0].astype(F32)
    lam = jnp.exp(jnp.sum(lv[0] * lv[1])) - jnp.exp(jnp.sum(lv[2] * lv[3])) + lam_init

    rows = (b + 1 + SUBLANES - 1) // SUBLANES * SUBLANES
    cm = jnp.zeros((rows, d), F32).at[:b].set(c).at[b].set(c_ctx)
    mod = _modulation(cm, ada_w[0], ada_b[0])
    sh1, sc1, gt1, sh2, sc2, gt2 = [mod[:b, None, k * d:(k + 1) * d] for k in range(6)]
    csh1, csc1 = [jnp.broadcast_to(mod[b, k * d:(k + 1) * d], (b, 1, d)) for k in range(2)]

    w = w_in[0]
    lora_widths = (DECAY_LORA, DECAY_LORA, AAA_LORA, AAA_LORA)
    o_lora = 3 * RWKV_DIM
    o_glora = o_lora + sum(lora_widths)
    pad_cols = lambda m: jnp.concatenate(
        [m[..., :o_lora], _pad_lora_cols(m[..., o_lora:o_glora], lora_widths), m[..., o_glora:RWKV_COLS]], axis=-1)
    w_rwkv = pad_cols(w).astype(BF16)
    w_diff = w[:, RWKV_COLS:RWKV_COLS + DIFF_COLS].astype(BF16)
    w_gate = w[:, RWKV_COLS + DIFF_COLS:].astype(BF16)
    g1 = norm1_g[0]
    px_r = _norm_proj(x, g1, sc1, sh1, w_rwkv, F32)
    px_d = _norm_proj(x, g1, sc1, sh1, w_diff, BF16)
    px_g = _norm_proj(x, g1, sc1, sh1, w_gate, BF16)
    pc_r = _norm_proj(ctx, g1, csc1, csh1, w_rwkv, F32)
    pc_d = _norm_proj(ctx, g1, csc1, csh1, w_diff, BF16)

    pad_rows = lambda m: jnp.pad(m, ((0, 0), (0, LORA_PAD - m.shape[1]), (0, 0)))
    rparams = (pad_cols(shift_mu[0])[None], rwkv_w0[0], pad_rows(rwkv_w2[0]), rwkv_a0[0], pad_rows(rwkv_a2[0]),
               rwkv_g2[0], rwkv_k_k[0][None], rwkv_k_a[0][None], rwkv_r_k[0].reshape(1, RWKV_DIM))
    y_sweep, bonus, gate = _rwkv_scans(px_r, pc_r, rparams)

    cos_t, sin_t = _rope_tables(l)
    qg = jnp.tile(qn_g[0], 2)[None]
    kg = jnp.tile(kn_g[0], 2)[None]
    q_x, k_x, v_x = _diff_prep(px_d, cos_t, sin_t, qg, kg, True)
    _, k_c, v_c = _diff_prep(pc_d, cos_t[:lc], sin_t[:lc], qg, kg, False)
    lam_vec = jnp.full((1, LANES), lam, F32)
    sg_vec = (subln_g[0] * (1.0 - lam_init))[None]
    score_bound = (1.05 * DIFF_QK_DIM * DIFF_SCALE * math.log2(math.e)
                   * jnp.max(jnp.abs(qn_g[0])) * jnp.max(jnp.abs(kn_g[0])))
    attn_args = (q_x, k_c, v_c, k_x, v_x, lam_vec, sg_vec)
    y_diff = lax.cond(score_bound <= SCORE_LOG2_LIMIT,
                      lambda a: _diff_attention(*a, bounded=True),
                      lambda a: _diff_attention(*a, bounded=False), attn_args)

    mixed = _merge(y_sweep, bonus, gate, y_diff, px_g, rwkv_lnx_g[0][None], rwkv_lnx_b[0][None],
                   w_pa[0].astype(BF16), w_pb[0].astype(BF16))
    n_r = N_GROUPS + N_EXPERTS
    w_router = jnp.zeros((d, ROUTER_PAD), F32).at[:, :N_GROUPS].set(router_g_w[0]).at[:, N_GROUPS:n_r].set(router_e_w[0])
    b_router = jnp.zeros((1, ROUTER_PAD), F32).at[0, :N_GROUPS].set(router_g_b[0]).at[0, N_GROUPS:n_r].set(router_e_b[0])
    w_router_hi = w_router.astype(BF16)
    w_router_lo = (w_router - w_router_hi.astype(F32)).astype(BF16)
    x_new, h2, logits = _outproj(mixed, x, gt1, norm2_g[0][None], sc2, sh2, w_out[0].astype(BF16),
                                 jnp.stack([w_router_hi, w_router_lo]), b_router)

    n_tok = b * l
    slot_tok, slot_w, slot_of, blk_expert, n_used = _route(logits.reshape(n_tok, ROUTER_PAD), n_tok)
    xs = h2.reshape(n_tok, d)[slot_tok]
    out = _moe_ffn(xs, slot_w[:, None], blk_expert, n_used, exp_w1[0], exp_w3[0], exp_w2[0])
    y0 = out[slot_of[:, 0]].reshape(b, l, d)
    y1 = out[slot_of[:, 1]].reshape(b, l, d)
    return _final(x_new, gt2, y0, y1)
```

```python
import functools
import math

import jax
import jax.numpy as jnp
from jax import lax
from jax.experimental import pallas as pl
from jax.experimental.pallas import tpu as pltpu

F32 = jnp.float32
BF16 = jnp.bfloat16
HIGHEST = lax.Precision.HIGHEST

D_MODEL = 2048
GRID_W = 64
RWKV_HEADS = 16
RWKV_HEAD_DIM = 64
RWKV_DIM = RWKV_HEADS * RWKV_HEAD_DIM
DECAY_LORA = 96
AAA_LORA = 96
GATE_LORA = 256
RWKV_COLS = 3 * RWKV_DIM + 2 * DECAY_LORA + 2 * AAA_LORA + GATE_LORA
DIFF_HEADS = 8
DIFF_QK_DIM = 64
DIFF_V_DIM = 2 * DIFF_QK_DIM
DIFF_DIM = DIFF_HEADS * DIFF_V_DIM
DIFF_QK_COLS = DIFF_HEADS * 2 * DIFF_QK_DIM
DIFF_COLS = 2 * DIFF_QK_COLS + DIFF_DIM
DIFF_SCALE = DIFF_QK_DIM ** -0.5
ROPE_THETA = 10000.0
GATE_COLS = 2 * D_MODEL
N_GROUPS = 4
EXPERTS_PER_GROUP = 8
N_EXPERTS = N_GROUPS * EXPERTS_PER_GROUP
TOP_K = 2
D_EXPERT = 512
NORM_EPS = 1e-6
SUBLN_EPS = 1e-5
LNX_EPS = 64e-5

LANES = 128
SUBLANES = 8
VMEM_LIMIT_BYTES = 56 * 1024 * 1024

LORA_PAD = LANES
RWKV_PCOLS = 3 * RWKV_DIM + 4 * LORA_PAD + GATE_LORA
CHUNK = 64
V_EXT = 2 * DIFF_V_DIM
SCORE_LOG2_LIMIT = 60.0
ROUTER_PAD = LANES
MOE_TILE = 512


def _cparams(*sem):
    return pltpu.CompilerParams(dimension_semantics=sem, vmem_limit_bytes=VMEM_LIMIT_BYTES)


def _sigmoid(x):
    return 1.0 / (1.0 + jnp.exp(-x))


def _dot(a, b, dims=(((1,), (0,)), ((), ()))):
    return lax.dot_general(a.astype(BF16), b.astype(BF16), dims, preferred_element_type=F32)


def _dot_f32(a, b, dims=(((1,), (0,)), ((), ()))):
    return lax.dot_general(a, b, dims, precision=HIGHEST, preferred_element_type=F32)


_NT = (((1,), (1,)), ((), ()))
_TN = (((0,), (0,)), ((), ()))


def _bf16_parts(x, n):
    parts = []
    for _ in range(n):
        p = x.astype(BF16)
        parts.append(p)
        x = x - p.astype(F32)
    return parts


def _dot_split(x, w_b, n, lhs=True):
    parts = _bf16_parts(x, n)
    outs = [jnp.dot(p, w_b, preferred_element_type=F32) if lhs else jnp.dot(w_b, p, preferred_element_type=F32)
            for p in parts]
    return functools.reduce(lambda a, b: a + b, outs)


def _group_ones(width, group):
    r = lax.broadcasted_iota(jnp.int32, (width, width), 0) // group
    c = lax.broadcasted_iota(jnp.int32, (width, width), 1) // group
    return (r == c).astype(BF16)


def _group_sum(x, group):
    ones = _group_ones(LANES, group)
    parts = [_dot_split(x[:, j * LANES:(j + 1) * LANES], ones, 2) for j in range(x.shape[1] // LANES)]
    return parts[0] if len(parts) == 1 else jnp.concatenate(parts, axis=1)


def _mod_kernel(c_ref, w_ref, b_ref, o_ref):
    c = c_ref[...]
    o_ref[...] = _dot_f32(c * _sigmoid(c), w_ref[...]) + b_ref[...]


def _modulation(cm, ada_w, ada_b):
    rows, d = cm.shape
    n = ada_w.shape[1]
    tn = 1536
    return pl.pallas_call(
        _mod_kernel,
        grid=(n // tn,),
        in_specs=[pl.BlockSpec((rows, d), lambda j: (0, 0)),
                  pl.BlockSpec((d, tn), lambda j: (0, j)),
                  pl.BlockSpec((1, tn), lambda j: (0, j))],
        out_specs=pl.BlockSpec((rows, tn), lambda j: (0, j)),
        out_shape=jax.ShapeDtypeStruct((rows, n), F32),
        compiler_params=_cparams("parallel"),
        name="modulation",
    )(cm, ada_w, ada_b.reshape(1, n))


def _inproj_kernel(x_ref, g_ref, sc_ref, sh_ref, w_ref, o_ref, h_ref):
    @pl.when(pl.program_id(2) == 0)
    def _():
        x = x_ref[0]
        y = x * lax.rsqrt(jnp.mean(x * x, axis=-1, keepdims=True) + NORM_EPS) * g_ref[...]
        h_ref[...] = (y * (1.0 + sc_ref[0]) + sh_ref[0]).astype(BF16)

    o_ref[0] = jnp.dot(h_ref[...], w_ref[...], preferred_element_type=F32).astype(o_ref.dtype)


def _pick(n, prefs):
    for t in prefs:
        if n % t == 0:
            return t
    return n


def _norm_proj(x, g, sc, sh, w, out_dtype):
    b, l, d = x.shape
    n = w.shape[1]
    tm = _pick(l, (1024, 512, 256, 128))
    tn = _pick(n, (1024, 768, 512, 256, 128))
    return pl.pallas_call(
        _inproj_kernel,
        grid=(b, l // tm, n // tn),
        in_specs=[pl.BlockSpec((1, tm, d), lambda bi, i, j: (bi, i, 0)),
                  pl.BlockSpec((1, d), lambda bi, i, j: (0, 0)),
                  pl.BlockSpec((1, 1, d), lambda bi, i, j: (bi, 0, 0)),
                  pl.BlockSpec((1, 1, d), lambda bi, i, j: (bi, 0, 0)),
                  pl.BlockSpec((d, tn), lambda bi, i, j: (0, j))],
        out_specs=pl.BlockSpec((1, tm, tn), lambda bi, i, j: (bi, i, j)),
        out_shape=jax.ShapeDtypeStruct((b, l, n), out_dtype),
        scratch_shapes=[pltpu.VMEM((tm, d), BF16)],
        compiler_params=_cparams("parallel", "parallel", "arbitrary"),
        name="norm_proj",
    )(x, g.reshape(1, d), sc, sh, w)


def _mm(a, b, dims=(((1,), (0,)), ((), ())), exact=False):
    return _dot_f32(a, b, dims) if exact else _dot(a, b, dims)


def _tri_inverse_all(a_list, exact):
    n = CHUNK
    eye = (lax.broadcasted_iota(jnp.int32, (n, n), 0) == lax.broadcasted_iota(jnp.int32, (n, n), 1)).astype(F32)
    xs = [eye + a for a in a_list]
    ps = [_mm(a, a, exact=exact) for a in a_list]
    steps = int(math.log2(n)) - 1
    for s in range(steps):
        if s < steps - 1:
            xps = [_mm(jnp.concatenate([x, p], axis=0), p, exact=exact) for x, p in zip(xs, ps)]
            xs = [x + xp[:n] for x, xp in zip(xs, xps)]
            ps = [xp[n:] for xp in xps]
        else:
            xs = [x + _mm(x, p, exact=exact) for x, p in zip(xs, ps)]
    return xs


def _rwkv_chunk_kernel(p_ref, pp_ref, pn_ref, mu_ref, w0_ref, w2_ref, a0_ref, a2_ref, g2_ref, kk_ref, ka_ref,
                       rk_ref, q_ref, y0_ref, m_ref, n_ref, bonus_ref, gate_ref):
    i = pl.program_id(1)
    last = pl.num_programs(1) - 1
    c = RWKV_DIM
    hd = RWKV_HEAD_DIM
    heads = range(RWKV_HEADS)
    p = p_ref[0]
    row = lax.broadcasted_iota(jnp.int32, (CHUNK, 1), 0)
    prev_row = jnp.where(i == 0, 0.0, pp_ref[0, SUBLANES - 1:SUBLANES, :])
    next_row = jnp.where(i == last, 0.0, pn_ref[0, 0:1, :])
    prev = jnp.where(row == 0, prev_row, pltpu.roll(p, 1, axis=0))
    nxt = jnp.where(row == CHUNK - 1, next_row, pltpu.roll(p, CHUNK - 1, axis=0))
    ps = p + mu_ref[...] * (0.5 * (prev + nxt) - p)

    r, k, v = ps[:, :c], ps[:, c:2 * c], ps[:, 2 * c:3 * c]
    o = 3 * c
    xw = (ps[:, o:o + LORA_PAD], ps[:, o + LORA_PAD:o + 2 * LORA_PAD])
    xa = (ps[:, o + 2 * LORA_PAD:o + 3 * LORA_PAD], ps[:, o + 3 * LORA_PAD:o + 4 * LORA_PAD])
    xg = ps[:, o + 4 * LORA_PAD:]

    gate_ref[0] = _dot(_sigmoid(xg), g2_ref[...])
    kk = k * kk_ref[...]
    kk = kk * lax.rsqrt(_group_sum(kk * kk, hd) + 1e-12)

    tr = lax.broadcasted_iota(jnp.int32, (CHUNK, CHUNK), 0)
    tc = lax.broadcasted_iota(jnp.int32, (CHUNK, CHUNK), 1)
    eye = tr == tc
    tr2 = lax.broadcasted_iota(jnp.int32, (CHUNK, 2 * CHUNK), 0)
    tc2 = lax.broadcasted_iota(jnp.int32, (CHUNK, 2 * CHUNK), 1) % CHUNK
    kd_sum = jnp.zeros_like(k)
    v_b = v.astype(BF16)
    hsl = [slice(h * hd, (h + 1) * hd) for h in heads]
    for d in range(2):
        before = (tc < tr) if d == 0 else (tc > tr)
        upto = (tc <= tr) if d == 0 else (tc >= tr)
        upto2 = (tc2 <= tr2) if d == 0 else (tc2 >= tr2)
        z = w0_ref[d:d + 1, :] + _dot(jnp.tanh(xw[d]), w2_ref[d])
        w_log = -(jnp.maximum(-z, 0.0) + jnp.log(1.0 + jnp.exp(-jnp.abs(z)))) - 0.5
        logw = -jnp.exp(w_log)
        a = _sigmoid(a0_ref[d:d + 1, :] + _dot(xa[d], a2_ref[d]))
        kd = k * (1.0 + (a - 1.0) * ka_ref[...])
        kd_sum = kd_sum + kd
        cum = _dot_split(logw, upto.astype(BF16), 3, lhs=False)
        total = cum[CHUNK - 1:CHUNK, :] if d == 0 else cum[0:1, :]
        e_neg = jnp.exp(-cum)
        e_rest = jnp.exp(total - cum)
        p_total = jnp.exp(total)
        beta = kk * a
        al = -kk * jnp.exp(cum - logw)
        rt = r * jnp.exp(cum)
        al_b = al.astype(BF16)
        rt_b = rt.astype(BF16)
        bt_b = (beta * e_neg).astype(BF16)
        kt_b = (kd * e_neg).astype(BF16)
        bh_b = (beta * e_rest).astype(BF16)
        kh_b = (kd * e_rest).astype(BF16)
        g = [_mm(jnp.concatenate([al_b[:, s], rt_b[:, s]], axis=0),
                 jnp.concatenate([bt_b[:, s], kt_b[:, s]], axis=0), _NT) for s in hsl]
        a_ab = [jnp.where(before, gh[:CHUNK, :CHUNK], 0.0) for gh in g]
        a_ak = [jnp.where(before, gh[:CHUNK, CHUNK:], 0.0) for gh in g]
        a_rbk = [jnp.where(upto2, gh[CHUNK:, :], 0.0).astype(BF16) for gh in g]
        akv = [_mm(ak, v_b[:, s]) for ak, s in zip(a_ak, hsl)]
        t_inv = _tri_inverse_all(a_ab, exact=False)
        wu = [_mm(t, jnp.concatenate([al[:, s], u], axis=1)) for t, u, s in zip(t_inv, akv, hsl)]
        zv = jnp.zeros((CHUNK, hd), BF16)
        zmat = [jnp.concatenate([x.astype(BF16), jnp.concatenate([zv, v_b[:, s]], axis=1)], axis=0)
                for x, s in zip(wu, hsl)]
        qy = [_mm(ar, zm) for ar, zm in zip(a_rbk, zmat)]
        mn = [_mm(zm, jnp.concatenate([bh_b[:, s], kh_b[:, s]], axis=0), _TN)
              for zm, s in zip(zmat, hsl)]
        q_ref[0, d] = rt + jnp.concatenate([x[:, :hd] for x in qy], axis=1)
        y0_ref[0, d] = jnp.concatenate([x[:, hd:] for x in qy], axis=1)
        m_ref[0, d, 0] = jnp.concatenate([jnp.where(eye, p_total[:, s], 0.0) + x[:hd] for x, s in zip(mn, hsl)], axis=1)
        n_ref[0, d, 0] = jnp.concatenate([x[hd:] for x in mn], axis=1)
    bonus_ref[0] = _group_sum(r * kd_sum * rk_ref[...], hd) * v


def _rwkv_chunk_ops(p, params):
    b, l, pc = p.shape
    nc = l // CHUNK
    c = RWKV_DIM
    hb = CHUNK // SUBLANES
    nb8 = l // SUBLANES
    mu, w0, w2, a0, a2, g2, k_k, k_a, r_k = params
    const = lambda shape: pl.BlockSpec(shape, lambda bi, i: (0,) * len(shape))
    tok = lambda: pl.BlockSpec((1, CHUNK, c), lambda bi, i: (bi, i, 0))
    return pl.pallas_call(
        _rwkv_chunk_kernel,
        grid=(b, nc),
        in_specs=[pl.BlockSpec((1, CHUNK, pc), lambda bi, i: (bi, i, 0)),
                  pl.BlockSpec((1, SUBLANES, pc), lambda bi, i: (bi, jnp.maximum(i * hb - 1, 0), 0)),
                  pl.BlockSpec((1, SUBLANES, pc), lambda bi, i: (bi, jnp.minimum((i + 1) * hb, nb8 - 1), 0)),
                  const((1, pc)), const((2, c)), const((2, LORA_PAD, c)), const((2, c)), const((2, LORA_PAD, c)),
                  const((GATE_LORA, c)), const((1, c)), const((1, c)), const((1, c))],
        out_specs=[pl.BlockSpec((1, 2, CHUNK, c), lambda bi, i: (bi, 0, i, 0)),
                   pl.BlockSpec((1, 2, CHUNK, c), lambda bi, i: (bi, 0, i, 0)),
                   pl.BlockSpec((1, 2, 1, RWKV_HEAD_DIM, c), lambda bi, i: (bi, 0, i, 0, 0)),
                   pl.BlockSpec((1, 2, 1, RWKV_HEAD_DIM, c), lambda bi, i: (bi, 0, i, 0, 0)),
                   tok(), tok()],
        out_shape=[jax.ShapeDtypeStruct((b, 2, l, c), F32), jax.ShapeDtypeStruct((b, 2, l, c), F32),
                   jax.ShapeDtypeStruct((b, 2, nc, RWKV_HEAD_DIM, c), F32),
                   jax.ShapeDtypeStruct((b, 2, nc, RWKV_HEAD_DIM, c), F32),
                   jax.ShapeDtypeStruct((b, l, c), F32), jax.ShapeDtypeStruct((b, l, c), F32)],
        compiler_params=_cparams("parallel", "parallel"),
        name="rwkv_chunk_ops",
    )(p, p, p, mu, w0, w2, a0, a2, g2, k_k, k_a, r_k)


def _rwkv_sweep_kernel(qf_ref, y0f_ref, mf_ref, nf_ref, qb_ref, y0b_ref, mb_ref, nb_ref, s0_ref,
                       yf_ref, yb_ref, sfin_ref, s_ref):
    j = pl.program_id(1)
    hd = RWKV_HEAD_DIM

    @pl.when(j == 0)
    def _():
        s_ref[...] = s0_ref[0]

    dirs = ((qf_ref, y0f_ref, mf_ref, nf_ref, yf_ref), (qb_ref, y0b_ref, mb_ref, nb_ref, yb_ref))
    hsl = [slice(h * hd, (h + 1) * hd) for h in range(RWKV_HEADS)]
    for d, (q_ref, y0_ref, m_ref, n_ref, y_ref) in enumerate(dirs):
        s_b = s_ref[d].astype(BF16)
        q_b = q_ref[0, 0].astype(BF16)
        m_b = m_ref[0, 0, 0].astype(BF16)
        ys = [_dot(q_b[:, s], s_b[:, s], _NT) for s in hsl]
        sm = [_dot(s_b[:, s], m_b[:, s]) for s in hsl]
        y_ref[0] = y0_ref[0, 0] + jnp.concatenate(ys, axis=1)
        s_ref[d] = n_ref[0, 0, 0] + jnp.concatenate(sm, axis=1)

    @pl.when(j == pl.num_programs(1) - 1)
    def _():
        sfin_ref[0] = s_ref[...]


def _rwkv_sweep(q, y0, m, n, s0):
    b, _, l, c = q.shape
    nc = l // CHUNK
    hd = RWKV_HEAD_DIM
    tokf = lambda: pl.BlockSpec((1, 1, CHUNK, c), lambda bi, j: (bi, 0, j, 0))
    tokb = lambda: pl.BlockSpec((1, 1, CHUNK, c), lambda bi, j: (bi, 1, nc - 1 - j, 0))
    opf = lambda: pl.BlockSpec((1, 1, 1, hd, c), lambda bi, j: (bi, 0, j, 0, 0))
    opb = lambda: pl.BlockSpec((1, 1, 1, hd, c), lambda bi, j: (bi, 1, nc - 1 - j, 0, 0))
    return pl.pallas_call(
        _rwkv_sweep_kernel,
        grid=(b, nc),
        in_specs=[tokf(), tokf(), opf(), opf(), tokb(), tokb(), opb(), opb(),
                  pl.BlockSpec((1, 2, hd, c), lambda bi, j: (bi, 0, 0, 0))],
        out_specs=[pl.BlockSpec((1, CHUNK, c), lambda bi, j: (bi, j, 0)),
                   pl.BlockSpec((1, CHUNK, c), lambda bi, j: (bi, nc - 1 - j, 0)),
                   pl.BlockSpec((1, 2, hd, c), lambda bi, j: (bi, 0, 0, 0))],
        out_shape=[jax.ShapeDtypeStruct((b, l, c), F32), jax.ShapeDtypeStruct((b, l, c), F32),
                   jax.ShapeDtypeStruct((b, 2, hd, c), F32)],
        scratch_shapes=[pltpu.VMEM((2, hd, c), F32)],
        compiler_params=_cparams("parallel", "arbitrary"),
        name="rwkv_sweep",
    )(q, y0, m, n, q, y0, m, n, s0)


def _rwkv_scans(px_r, pc_r, rparams):
    b = px_r.shape[0]
    qc, y0c, mc, nc_, _, _ = _rwkv_chunk_ops(pc_r, rparams)
    s_zero = jnp.zeros((b, 2, RWKV_HEAD_DIM, RWKV_DIM), F32)
    _, _, s_ctx = _rwkv_sweep(qc, y0c, mc, nc_, s_zero)
    qx, y0x, mx, nx, bonus, gate = _rwkv_chunk_ops(px_r, rparams)
    yf, yb, _ = _rwkv_sweep(qx, y0x, mx, nx, s_ctx)
    return (yf, yb), bonus, gate


def _diff_prep_kernel(p_ref, cos_ref, sin_ref, qg_ref, kg_ref, q_ref, k_ref, v_ref, *, rope):
    lane = lax.broadcasted_iota(jnp.int32, (1, LANES), 1)
    first = (lane % 32) < 16
    for hd in range(DIFF_HEADS):
        for off, g_ref, o_ref, scale in ((0, qg_ref, q_ref, DIFF_SCALE * math.log2(math.e)),
                                         (DIFF_QK_COLS, kg_ref, k_ref, 1.0)):
            cs = slice(hd * LANES, (hd + 1) * LANES)
            xb = p_ref[0, :, off + hd * LANES:off + (hd + 1) * LANES].astype(F32)
            ms = _group_sum(xb * xb, DIFF_QK_DIM) * (1.0 / DIFF_QK_DIM)
            y = xb * lax.rsqrt(ms + NORM_EPS) * g_ref[...]
            if rope:
                swapped = jnp.where(first, pltpu.roll(y, LANES - 16, axis=1), pltpu.roll(y, 16, axis=1))
                y = y * cos_ref[...] + swapped * sin_ref[...]
            o_ref[0, :, cs] = (y * scale).astype(BF16)
    ones = jnp.ones((p_ref.shape[1], V_EXT - DIFF_V_DIM), BF16)
    for hd in range(DIFF_HEADS):
        vb = p_ref[0, :, 2 * DIFF_QK_COLS + hd * DIFF_V_DIM:2 * DIFF_QK_COLS + (hd + 1) * DIFF_V_DIM]
        v_ref[0, :, hd * V_EXT:(hd + 1) * V_EXT] = jnp.concatenate([vb.astype(BF16), ones], axis=1)


def _diff_prep(p, cos_t, sin_t, qg, kg, rope):
    b, l, pc = p.shape
    t = _pick(l, (512, 256, 128))
    tok = lambda w=DIFF_DIM: pl.BlockSpec((1, t, w), lambda bi, i: (bi, i, 0))
    shp = jax.ShapeDtypeStruct((b, l, DIFF_DIM), BF16)
    shp_v = jax.ShapeDtypeStruct((b, l, DIFF_HEADS * V_EXT), BF16)
    return pl.pallas_call(
        functools.partial(_diff_prep_kernel, rope=rope),
        grid=(b, l // t),
        in_specs=[pl.BlockSpec((1, t, pc), lambda bi, i: (bi, i, 0)),
                  pl.BlockSpec((t, LANES), lambda bi, i: (i, 0)),
                  pl.BlockSpec((t, LANES), lambda bi, i: (i, 0)),
                  pl.BlockSpec((1, LANES), lambda bi, i: (0, 0)),
                  pl.BlockSpec((1, LANES), lambda bi, i: (0, 0))],
        out_specs=[tok(), tok(), tok(DIFF_HEADS * V_EXT)],
        out_shape=[shp, shp, shp_v],
        compiler_params=_cparams("parallel", "parallel"),
        name="diff_prep_rope" if rope else "diff_prep",
    )(p, cos_t, sin_t, qg, kg)


def _diff_finish(acc1, acc2, lam_ref, sg_ref, o_ref):
    dv = DIFF_V_DIM
    o = acc1[:, :dv] / acc1[:, dv:] - lam_ref[...] * (acc2[:, :dv] / acc2[:, dv:])
    o = o * lax.rsqrt(jnp.mean(o * o, axis=-1, keepdims=True) + SUBLN_EPS)
    o_ref[0] = (o * sg_ref[...]).astype(o_ref.dtype)


def _flash_online_kernel(lam_ref, sg_ref, q_ref, kc_ref, vc_ref, k_ref, v_ref, o_ref, *, tk):
    qd = DIFF_QK_DIM
    tq = q_ref.shape[1]
    q = q_ref[0]
    qs = (q[:, :qd], q[:, qd:])

    def absorb(state, k, v):
        reps = k.shape[0] // LANES
        out = []
        for mp in range(2):
            m_prev, acc = state[mp]
            s = lax.dot_general(qs[mp], k[:, mp * qd:(mp + 1) * qd], _NT, preferred_element_type=F32)
            m_new = jnp.maximum(m_prev, jnp.max(s, axis=-1, keepdims=True))
            alpha = jnp.exp2(m_prev - m_new)
            pr = jnp.exp2(s - jnp.concatenate([m_new] * reps, axis=1))
            acc_new = jnp.concatenate([alpha, alpha], axis=1) * acc + jnp.dot(pr.astype(BF16), v, preferred_element_type=F32)
            out.append((m_new, acc_new))
        return tuple(out)

    init = (jnp.full((tq, LANES), -jnp.inf, F32), jnp.zeros((tq, V_EXT), F32))
    state = absorb((init, init), kc_ref[0], vc_ref[0])

    def body(j, state):
        rows = pl.ds(pl.multiple_of(j * tk, tk), tk)
        return absorb(state, k_ref[0, rows, :], v_ref[0, rows, :])

    (_, acc1), (_, acc2) = lax.fori_loop(0, k_ref.shape[1] // tk, body, state)
    _diff_finish(acc1, acc2, lam_ref, sg_ref, o_ref)


def _flash_bounded_kernel(lam_ref, sg_ref, q_ref, kc_ref, vc_ref, k_ref, v_ref, o_ref, acc_ref, p_ref, *, tk):
    qd = DIFF_QK_DIM
    q = q_ref[0]
    qs = (q[:, :qd], q[:, qd:])
    n_kv = k_ref.shape[1] // tk

    def weights(mp, k):
        s = lax.dot_general(qs[mp], k[:, mp * qd:(mp + 1) * qd], _NT, preferred_element_type=F32)
        return jnp.exp2(s).astype(BF16)

    def chunk(c):
        return pl.ds(pl.multiple_of(jnp.minimum(c, n_kv - 1) * tk, tk), tk)

    for mp in range(2):
        acc_ref[mp] = jnp.dot(weights(mp, kc_ref[0]), vc_ref[0], preferred_element_type=F32)
        p_ref[0, mp] = weights(mp, k_ref[0, pl.ds(0, tk), :])

    steps = 4 if n_kv % 4 == 0 else 2

    def body(j, carry):
        for u in range(steps):
            c = steps * j + u
            v = v_ref[0, chunk(c), :]
            k_next = k_ref[0, chunk(c + 1), :]
            for mp in range(2):
                acc_ref[mp] += jnp.dot(p_ref[u % 2, mp], v, preferred_element_type=F32)
                p_ref[1 - u % 2, mp] = weights(mp, k_next)
        return carry

    lax.fori_loop(0, n_kv // steps, body, 0)
    _diff_finish(acc_ref[0], acc_ref[1], lam_ref, sg_ref, o_ref)


def _diff_attention(q, kc, vc, k, v, lam_vec, sg_vec, bounded):
    b, l, _ = q.shape
    lc = kc.shape[1]
    tq = _pick(l, (256, 128))
    tk = _pick(l, (512, 256, 128))
    if bounded:
        tq = _pick(l, (512, 256, 128))
        tk = _pick(l // 2, (512, 256, 128))
        assert l % (2 * tk) == 0
        body = functools.partial(_flash_bounded_kernel, tk=tk)
        scratch = [pltpu.VMEM((2, tq, V_EXT), F32), pltpu.VMEM((2, 2, tq, tk), BF16)]
    else:
        body = functools.partial(_flash_online_kernel, tk=tk)
        scratch = []
    return pl.pallas_call(
        body,
        grid=(b, DIFF_HEADS, l // tq),
        in_specs=[pl.BlockSpec((1, LANES), lambda bi, h, i: (0, 0)),
                  pl.BlockSpec((1, LANES), lambda bi, h, i: (0, 0)),
                  pl.BlockSpec((1, tq, LANES), lambda bi, h, i: (bi, i, h)),
                  pl.BlockSpec((1, lc, LANES), lambda bi, h, i: (bi, 0, h)),
                  pl.BlockSpec((1, lc, V_EXT), lambda bi, h, i: (bi, 0, h)),
                  pl.BlockSpec((1, l, LANES), lambda bi, h, i: (bi, 0, h)),
                  pl.BlockSpec((1, l, V_EXT), lambda bi, h, i: (bi, 0, h))],
        out_specs=pl.BlockSpec((1, tq, LANES), lambda bi, h, i: (bi, i, h)),
        out_shape=jax.ShapeDtypeStruct((b, l, DIFF_DIM), BF16),
        scratch_shapes=scratch,
        compiler_params=_cparams("parallel", "parallel", "arbitrary"),
        name="diff_flash_bounded" if bounded else "diff_flash_online",
    )(lam_vec, sg_vec, q, kc, vc, k, v)


def _merge_kernel(yf_ref, yb_ref, bonus_ref, gate_ref, yd_ref, pg_ref, lg_ref, lb_ref, wpa_ref, wpb_ref, o_ref):
    hd = RWKV_HEAD_DIM
    y = yf_ref[0] + yb_ref[0]
    dev = y - _group_sum(y, hd) * (1.0 / hd)
    var = _group_sum(dev * dev, hd) * (1.0 / hd)
    yn = dev * lax.rsqrt(var + LNX_EPS) * lg_ref[...] + lb_ref[...]
    y_rwkv = (yn + bonus_ref[0]) * gate_ref[0]
    a = _dot(y_rwkv, wpa_ref[...])
    bb = jnp.dot(yd_ref[0], wpb_ref[...], preferred_element_type=F32)
    ga = _sigmoid(pg_ref[0, :, :D_MODEL].astype(F32))
    gb = _sigmoid(pg_ref[0, :, D_MODEL:].astype(F32))
    o_ref[0] = (ga * a + gb * bb).astype(BF16)


def _merge(y_sweep, bonus, gate, y_diff, p_gate, lnx_g, lnx_b, w_pa, w_pb):
    b, l, c = bonus.shape
    d = D_MODEL
    tm = _pick(l, (256, 128))
    const = lambda shape: pl.BlockSpec(shape, lambda bi, i: (0,) * len(shape))
    tok = lambda w: pl.BlockSpec((1, tm, w), lambda bi, i: (bi, i, 0))
    return pl.pallas_call(
        _merge_kernel,
        grid=(b, l // tm),
        in_specs=[tok(c), tok(c), tok(c), tok(c), tok(DIFF_DIM), tok(GATE_COLS),
                  const((1, c)), const((1, c)), const((c, d)), const((DIFF_DIM, d))],
        out_specs=tok(d),
        out_shape=jax.ShapeDtypeStruct((b, l, d), BF16),
        compiler_params=_cparams("parallel", "parallel"),
        name="merge",
    )(y_sweep[0], y_sweep[1], bonus, gate, y_diff, p_gate, lnx_g, lnx_b, w_pa, w_pb)


def _outproj_kernel(mx_ref, x_ref, gt_ref, g_ref, sc_ref, sh_ref, wo_ref, wr_ref, br_ref, xn_ref, h_ref, lg_ref):
    mix = jnp.dot(mx_ref[0], wo_ref[...], preferred_element_type=F32)
    xn = x_ref[0] + gt_ref[0] * mix
    xn_ref[0] = xn
    y = xn * lax.rsqrt(jnp.mean(xn * xn, axis=-1, keepdims=True) + NORM_EPS) * g_ref[...]
    h = y * (1.0 + sc_ref[0]) + sh_ref[0]
    h_hi, h_lo = _bf16_parts(h, 2)
    h_ref[0] = h_hi
    dot = lambda a, b: jnp.dot(a, b, preferred_element_type=F32)
    lg_ref[0] = dot(h_hi, wr_ref[0]) + (dot(h_lo, wr_ref[0]) + dot(h_hi, wr_ref[1])) + br_ref[...]


def _outproj(mixed, x, gt1, g2, sc2, sh2, w_out, w_router, b_router):
    b, l, d = x.shape
    tm = _pick(l, (512, 256, 128))
    const = lambda shape: pl.BlockSpec(shape, lambda bi, i: (0,) * len(shape))
    tok = lambda w: pl.BlockSpec((1, tm, w), lambda bi, i: (bi, i, 0))
    per_b = lambda: pl.BlockSpec((1, 1, d), lambda bi, i: (bi, 0, 0))
    return pl.pallas_call(
        _outproj_kernel,
        grid=(b, l // tm),
        in_specs=[tok(d), tok(d), per_b(), const((1, d)), per_b(), per_b(),
                  const((d, d)), const((2, d, ROUTER_PAD)), const((1, ROUTER_PAD))],
        out_specs=[tok(d), tok(d), tok(ROUTER_PAD)],
        out_shape=[jax.ShapeDtypeStruct((b, l, d), F32), jax.ShapeDtypeStruct((b, l, d), BF16),
                   jax.ShapeDtypeStruct((b, l, ROUTER_PAD), F32)],
        compiler_params=_cparams("parallel", "parallel"),
        name="outproj_router",
    )(mixed, x, gt1, g2, sc2, sh2, w_out, w_router, b_router)


def _moe_kernel(wb_ref, we_ref, lo_ref, hi_ref, x_ref, sw_ref, w1_ref, w3_ref, w2_ref, o_ref,
                w1b_ref, w3b_ref, w2b_ref, cached_ref):
    i = pl.program_id(0)
    lo, hi = lo_ref[i], hi_ref[i]
    live = hi > lo

    @pl.when(i == 0)
    def _():
        cached_ref[0] = -1

    @pl.when((i == 0) | (wb_ref[i] != wb_ref[jnp.maximum(i - 1, 0)]))
    def _():
        o_ref[...] = jnp.zeros(o_ref.shape, o_ref.dtype)

    @pl.when(live & (cached_ref[0] != we_ref[i]))
    def _():
        w1b_ref[...] = w1_ref[0].astype(BF16)
        w3b_ref[...] = w3_ref[0].astype(BF16)
        w2b_ref[...] = w2_ref[0].astype(BF16)
        cached_ref[0] = we_ref[i]

    @pl.when(live)
    def _():
        xb = x_ref[...]
        u = jnp.dot(xb, w1b_ref[...], preferred_element_type=F32)
        g = jnp.dot(xb, w3b_ref[...], preferred_element_type=F32)
        hmid = (u * _sigmoid(u) * g).astype(BF16)
        res = (jnp.dot(hmid, w2b_ref[...], preferred_element_type=F32) * sw_ref[...]).astype(o_ref.dtype)
        row = lax.broadcasted_iota(jnp.int32, (o_ref.shape[0], 1), 0)
        o_ref[...] = jnp.where((row >= lo) & (row < hi), res, o_ref[...])


def _moe_ffn(xs, sw, items, w1, w3, w2):
    n_rows, d = xs.shape
    wb, we, lo, hi = items
    grid_spec = pltpu.PrefetchScalarGridSpec(
        num_scalar_prefetch=4,
        grid=(wb.shape[0],),
        in_specs=[pl.BlockSpec((MOE_TILE, d), lambda i, wb, we, lo, hi: (wb[i], 0)),
                  pl.BlockSpec((MOE_TILE, 1), lambda i, wb, we, lo, hi: (wb[i], 0)),
                  pl.BlockSpec((1, d, D_EXPERT), lambda i, wb, we, lo, hi: (we[i], 0, 0)),
                  pl.BlockSpec((1, d, D_EXPERT), lambda i, wb, we, lo, hi: (we[i], 0, 0)),
                  pl.BlockSpec((1, D_EXPERT, d), lambda i, wb, we, lo, hi: (we[i], 0, 0))],
        out_specs=pl.BlockSpec((MOE_TILE, d), lambda i, wb, we, lo, hi: (wb[i], 0)),
        scratch_shapes=[pltpu.VMEM((d, D_EXPERT), BF16), pltpu.VMEM((d, D_EXPERT), BF16),
                        pltpu.VMEM((D_EXPERT, d), BF16), pltpu.SMEM((1,), jnp.int32)],
    )
    return pl.pallas_call(
        _moe_kernel,
        grid_spec=grid_spec,
        out_shape=jax.ShapeDtypeStruct((n_rows, d), BF16),
        compiler_params=_cparams("arbitrary"),
        name="moe_ffn",
    )(wb, we, lo, hi, xs, sw, w1, w3, w2)


def _final_kernel(x_ref, gt_ref, y0_ref, y1_ref, o_ref):
    o_ref[0] = x_ref[0] + gt_ref[0] * (y0_ref[0].astype(F32) + y1_ref[0].astype(F32))


def _final(x_new, gt2, y0, y1):
    b, l, d = x_new.shape
    tm = _pick(l, (512, 256, 128))
    tok = lambda: pl.BlockSpec((1, tm, d), lambda bi, i: (bi, i, 0))
    return pl.pallas_call(
        _final_kernel,
        grid=(b, l // tm),
        in_specs=[tok(), pl.BlockSpec((1, 1, d), lambda bi, i: (bi, 0, 0)), tok(), tok()],
        out_specs=tok(),
        out_shape=jax.ShapeDtypeStruct((b, l, d), F32),
        compiler_params=_cparams("parallel", "parallel"),
        name="moe_residual",
    )(x_new, gt2, y0, y1)


def _pad_lora_cols(w, widths):
    parts, o = [], 0
    for wd in widths:
        blk = w[..., o:o + wd]
        parts.append(jnp.pad(blk, [(0, 0)] * (w.ndim - 1) + [(0, LORA_PAD - wd)]))
        o += wd
    return jnp.concatenate(parts, axis=-1)


def _rope_tables(l):
    half = DIFF_QK_DIM // 2
    inv_freq = ROPE_THETA ** (-jnp.arange(0, half, 2, dtype=F32) / half)
    t = jnp.arange(l, dtype=jnp.int32)
    rows = (t // GRID_W).astype(F32)[:, None] * inv_freq
    cols = (t % GRID_W).astype(F32)[:, None] * inv_freq
    cos64 = jnp.concatenate([jnp.cos(rows), jnp.cos(rows), jnp.cos(cols), jnp.cos(cols)], axis=1)
    sin64 = jnp.concatenate([-jnp.sin(rows), jnp.sin(rows), -jnp.sin(cols), jnp.sin(cols)], axis=1)
    return jnp.tile(cos64, (1, 2)), jnp.tile(sin64, (1, 2))


def _route(logits, n_tok):
    g_logits = logits[:, :N_GROUPS]
    e_logits = logits[:, N_GROUPS:N_GROUPS + N_EXPERTS].reshape(n_tok, N_GROUPS, EXPERTS_PER_GROUP)
    p_group = jax.nn.softmax(g_logits, axis=-1)
    g_top = jnp.argmax(g_logits, axis=-1)
    p_g = jnp.take_along_axis(p_group, g_top[:, None], axis=1)[:, 0]
    e_sel = jnp.take_along_axis(e_logits, g_top[:, None, None], axis=1)[:, 0]
    top_p, top_i = lax.top_k(jax.nn.softmax(e_sel, axis=-1), TOP_K)
    top_p = top_p / jnp.sum(top_p, axis=-1, keepdims=True)
    gate = p_g[:, None] * top_p
    expert = g_top[:, None] * EXPERTS_PER_GROUP + top_i

    n_assign = n_tok * TOP_K
    assert n_assign % MOE_TILE == 0
    flat_e = expert.reshape(-1).astype(jnp.int32)
    ids = jnp.arange(n_assign, dtype=jnp.int32)
    sorted_e, order, sorted_w = lax.sort((flat_e, ids, gate.reshape(-1)), num_keys=1, is_stable=True)
    _, rank = lax.sort((order, ids), num_keys=1)
    ends = jnp.searchsorted(sorted_e, jnp.arange(N_EXPERTS, dtype=jnp.int32), side='right').astype(jnp.int32)
    starts = jnp.concatenate([jnp.zeros((1,), jnp.int32), ends[:-1]])

    nb = n_assign // MOE_TILE
    blk_lo = jnp.arange(nb, dtype=jnp.int32) * MOE_TILE
    e_first = jnp.searchsorted(ends, blk_lo, side='right').astype(jnp.int32)
    e_last = jnp.searchsorted(ends, blk_lo + MOE_TILE - 1, side='right').astype(jnp.int32)
    per_blk = e_last - e_first + 1
    cum = jnp.cumsum(per_blk)
    it = jnp.arange(nb + N_EXPERTS - 1, dtype=jnp.int32)
    wb = jnp.minimum(jnp.searchsorted(cum, it, side='right'), nb - 1).astype(jnp.int32)
    we = jnp.clip(e_first[wb] + it - (cum[wb] - per_blk[wb]), 0, N_EXPERTS - 1)
    lo = jnp.clip(starts[we], blk_lo[wb], blk_lo[wb] + MOE_TILE) - blk_lo[wb]
    hi = jnp.clip(ends[we], blk_lo[wb], blk_lo[wb] + MOE_TILE) - blk_lo[wb]
    hi = jnp.where(it < cum[-1], hi, lo)
    return order // TOP_K, sorted_w, rank.reshape(n_tok, TOP_K), (wb, we, lo, hi)


def kernel(x, c, ctx, c_ctx, ada_w, ada_b, norm1_g, norm2_g, w_in, shift_mu, rwkv_w0, rwkv_w2, rwkv_a0, rwkv_a2,
           rwkv_g2, rwkv_k_k, rwkv_k_a, rwkv_r_k, rwkv_lnx_g, rwkv_lnx_b, qn_g, kn_g, diff_lambda, subln_g,
           w_pa, w_pb, w_out, router_g_w, router_g_b, router_e_w, router_e_b, exp_w1, exp_w3, exp_w2):
    assert ada_w.shape[0] == 1, "single-layer block"
    b, l, d = x.shape
    lc = ctx.shape[1]
    lam_init = 0.8 - 0.6 * math.exp(-0.3 * 0)
    lv = diff_lambda[0].astype(F32)
    lam = jnp.exp(jnp.sum(lv[0] * lv[1])) - jnp.exp(jnp.sum(lv[2] * lv[3])) + lam_init

    rows = (b + 1 + SUBLANES - 1) // SUBLANES * SUBLANES
    cm = jnp.zeros((rows, d), F32).at[:b].set(c).at[b].set(c_ctx)
    mod = _modulation(cm, ada_w[0], ada_b[0])
    sh1, sc1, gt1, sh2, sc2, gt2 = [mod[:b, None, k * d:(k + 1) * d] for k in range(6)]
    csh1, csc1 = [jnp.broadcast_to(mod[b, k * d:(k + 1) * d], (b, 1, d)) for k in range(2)]

    w = w_in[0]
    lora_widths = (DECAY_LORA, DECAY_LORA, AAA_LORA, AAA_LORA)
    o_lora = 3 * RWKV_DIM
    o_glora = o_lora + sum(lora_widths)
    pad_cols = lambda m: jnp.concatenate(
        [m[..., :o_lora], _pad_lora_cols(m[..., o_lora:o_glora], lora_widths), m[..., o_glora:RWKV_COLS]], axis=-1)
    w_rwkv = pad_cols(w).astype(BF16)
    w_diff = w[:, RWKV_COLS:RWKV_COLS + DIFF_COLS].astype(BF16)
    w_gate = w[:, RWKV_COLS + DIFF_COLS:].astype(BF16)
    g1 = norm1_g[0]
    px_r = _norm_proj(x, g1, sc1, sh1, w_rwkv, F32)
    px_d = _norm_proj(x, g1, sc1, sh1, w_diff, BF16)
    px_g = _norm_proj(x, g1, sc1, sh1, w_gate, BF16)
    pc_r = _norm_proj(ctx, g1, csc1, csh1, w_rwkv, F32)
    pc_d = _norm_proj(ctx, g1, csc1, csh1, w_diff, BF16)

    pad_rows = lambda m: jnp.pad(m, ((0, 0), (0, LORA_PAD - m.shape[1]), (0, 0)))
    rparams = (pad_cols(shift_mu[0])[None], rwkv_w0[0], pad_rows(rwkv_w2[0]), rwkv_a0[0], pad_rows(rwkv_a2[0]),
               rwkv_g2[0], rwkv_k_k[0][None], rwkv_k_a[0][None], rwkv_r_k[0].reshape(1, RWKV_DIM))
    y_sweep, bonus, gate = _rwkv_scans(px_r, pc_r, rparams)

    cos_t, sin_t = _rope_tables(l)
    qg = jnp.tile(qn_g[0], 2)[None]
    kg = jnp.tile(kn_g[0], 2)[None]
    q_x, k_x, v_x = _diff_prep(px_d, cos_t, sin_t, qg, kg, True)
    _, k_c, v_c = _diff_prep(pc_d, cos_t[:lc], sin_t[:lc], qg, kg, False)
    lam_vec = jnp.full((1, LANES), lam, F32)
    sg_vec = (subln_g[0] * (1.0 - lam_init))[None]
    score_bound = (1.05 * DIFF_QK_DIM * DIFF_SCALE * math.log2(math.e)
                   * jnp.max(jnp.abs(qn_g[0])) * jnp.max(jnp.abs(kn_g[0])))
    attn_args = (q_x, k_c, v_c, k_x, v_x, lam_vec, sg_vec)
    y_diff = lax.cond(score_bound <= SCORE_LOG2_LIMIT,
                      lambda a: _diff_attention(*a, bounded=True),
                      lambda a: _diff_attention(*a, bounded=False), attn_args)

    mixed = _merge(y_sweep, bonus, gate, y_diff, px_g, rwkv_lnx_g[0][None], rwkv_lnx_b[0][None],
                   w_pa[0].astype(BF16), w_pb[0].astype(BF16))
    n_r = N_GROUPS + N_EXPERTS
    w_router = jnp.zeros((d, ROUTER_PAD), F32).at[:, :N_GROUPS].set(router_g_w[0]).at[:, N_GROUPS:n_r].set(router_e_w[0])
    b_router = jnp.zeros((1, ROUTER_PAD), F32).at[0, :N_GROUPS].set(router_g_b[0]).at[0, N_GROUPS:n_r].set(router_e_b[0])
    w_router_hi = w_router.astype(BF16)
    w_router_lo = (w_router - w_router_hi.astype(F32)).astype(BF16)
    x_new, h2, logits = _outproj(mixed, x, gt1, norm2_g[0][None], sc2, sh2, w_out[0].astype(BF16),
                                 jnp.stack([w_router_hi, w_router_lo]), b_router)

    n_tok = b * l
    row_tok, row_w, row_of, items = _route(logits.reshape(n_tok, ROUTER_PAD), n_tok)
    xs = h2.reshape(n_tok, d)[row_tok]
    out = _moe_ffn(xs, row_w[:, None], items, exp_w1[0], exp_w3[0], exp_w2[0])
    y0 = out[row_of[:, 0]].reshape(b, l, d)
    y1 = out[row_of[:, 1]].reshape(b, l, d)
    return _final(x_new, gt2, y0, y1)
```

```python
import functools
import math

import jax
import jax.numpy as jnp
from jax import lax
from jax.experimental import pallas as pl
from jax.experimental.pallas import tpu as pltpu

F32 = jnp.float32
BF16 = jnp.bfloat16
HIGHEST = lax.Precision.HIGHEST

D_MODEL = 2048
GRID_W = 64
RWKV_HEADS = 16
RWKV_HEAD_DIM = 64
RWKV_DIM = RWKV_HEADS * RWKV_HEAD_DIM
DECAY_LORA = 96
AAA_LORA = 96
GATE_LORA = 256
RWKV_COLS = 3 * RWKV_DIM + 2 * DECAY_LORA + 2 * AAA_LORA + GATE_LORA
DIFF_HEADS = 8
DIFF_QK_DIM = 64
DIFF_V_DIM = 2 * DIFF_QK_DIM
DIFF_DIM = DIFF_HEADS * DIFF_V_DIM
DIFF_QK_COLS = DIFF_HEADS * 2 * DIFF_QK_DIM
DIFF_COLS = 2 * DIFF_QK_COLS + DIFF_DIM
DIFF_SCALE = DIFF_QK_DIM ** -0.5
ROPE_THETA = 10000.0
GATE_COLS = 2 * D_MODEL
N_GROUPS = 4
EXPERTS_PER_GROUP = 8
N_EXPERTS = N_GROUPS * EXPERTS_PER_GROUP
TOP_K = 2
D_EXPERT = 512
NORM_EPS = 1e-6
SUBLN_EPS = 1e-5
LNX_EPS = 64e-5

LANES = 128
SUBLANES = 8
VMEM_LIMIT_BYTES = 56 * 1024 * 1024

LORA_PAD = LANES
RWKV_PCOLS = 3 * RWKV_DIM + 4 * LORA_PAD + GATE_LORA
CHUNK = 64
V_EXT = 2 * DIFF_V_DIM
SCORE_LOG2_LIMIT = 60.0
ROUTER_PAD = LANES
MOE_TILE = 512


def _cparams(*sem):
    return pltpu.CompilerParams(dimension_semantics=sem, vmem_limit_bytes=VMEM_LIMIT_BYTES)


def _sigmoid(x):
    return 1.0 / (1.0 + jnp.exp(-x))


def _dot(a, b, dims=(((1,), (0,)), ((), ()))):
    return lax.dot_general(a.astype(BF16), b.astype(BF16), dims, preferred_element_type=F32)


def _dot_f32(a, b, dims=(((1,), (0,)), ((), ()))):
    return lax.dot_general(a, b, dims, precision=HIGHEST, preferred_element_type=F32)


_NT = (((1,), (1,)), ((), ()))
_TN = (((0,), (0,)), ((), ()))


def _bf16_parts(x, n):
    parts = []
    for _ in range(n):
        p = x.astype(BF16)
        parts.append(p)
        x = x - p.astype(F32)
    return parts


def _dot_split(x, w_b, n, lhs=True):
    parts = _bf16_parts(x, n)
    outs = [jnp.dot(p, w_b, preferred_element_type=F32) if lhs else jnp.dot(w_b, p, preferred_element_type=F32)
            for p in parts]
    return functools.reduce(lambda a, b: a + b, outs)


def _group_ones(width, group):
    r = lax.broadcasted_iota(jnp.int32, (width, width), 0) // group
    c = lax.broadcasted_iota(jnp.int32, (width, width), 1) // group
    return (r == c).astype(BF16)


def _group_sum(x, group):
    ones = _group_ones(LANES, group)
    parts = [_dot_split(x[:, j * LANES:(j + 1) * LANES], ones, 2) for j in range(x.shape[1] // LANES)]
    return parts[0] if len(parts) == 1 else jnp.concatenate(parts, axis=1)


def _mod_kernel(c_ref, w_ref, b_ref, o_ref):
    c = c_ref[...]
    o_ref[...] = _dot_f32(c * _sigmoid(c), w_ref[...]) + b_ref[...]


def _modulation(cm, ada_w, ada_b):
    rows, d = cm.shape
    n = ada_w.shape[1]
    tn = 1536
    return pl.pallas_call(
        _mod_kernel,
        grid=(n // tn,),
        in_specs=[pl.BlockSpec((rows, d), lambda j: (0, 0)),
                  pl.BlockSpec((d, tn), lambda j: (0, j)),
                  pl.BlockSpec((1, tn), lambda j: (0, j))],
        out_specs=pl.BlockSpec((rows, tn), lambda j: (0, j)),
        out_shape=jax.ShapeDtypeStruct((rows, n), F32),
        compiler_params=_cparams("parallel"),
        name="modulation",
    )(cm, ada_w, ada_b.reshape(1, n))


def _inproj_kernel(x_ref, g_ref, sc_ref, sh_ref, w_ref, o_ref, h_ref):
    @pl.when(pl.program_id(2) == 0)
    def _():
        x = x_ref[0]
        y = x * lax.rsqrt(jnp.mean(x * x, axis=-1, keepdims=True) + NORM_EPS) * g_ref[...]
        h_ref[0] = (y * (1.0 + sc_ref[0]) + sh_ref[0]).astype(BF16)

    o_ref[0] = jnp.dot(h_ref[0], w_ref[...], preferred_element_type=F32).astype(o_ref.dtype)


def _pick(n, prefs):
    for t in prefs:
        if n % t == 0:
            return t
    return n


def _norm_proj(x, g, sc, sh, w, out_dtype):
    b, l, d = x.shape
    n = w.shape[1]
    tm = _pick(l, (1024, 512, 256, 128))
    tn = _pick(n, (1024, 768, 512, 256, 128))
    return pl.pallas_call(
        _inproj_kernel,
        grid=(b, l // tm, n // tn),
        in_specs=[pl.BlockSpec((1, tm, d), lambda bi, i, j: (bi, i, 0)),
                  pl.BlockSpec((1, d), lambda bi, i, j: (0, 0)),
                  pl.BlockSpec((1, 1, d), lambda bi, i, j: (bi, 0, 0)),
                  pl.BlockSpec((1, 1, d), lambda bi, i, j: (bi, 0, 0)),
                  pl.BlockSpec((d, tn), lambda bi, i, j: (0, j))],
        out_specs=[pl.BlockSpec((1, tm, tn), lambda bi, i, j: (bi, i, j)),
                   pl.BlockSpec((1, tm, d), lambda bi, i, j: (bi, i, 0))],
        out_shape=[jax.ShapeDtypeStruct((b, l, n), out_dtype), jax.ShapeDtypeStruct((b, l, d), BF16)],
        compiler_params=_cparams("parallel", "parallel", "arbitrary"),
        name="norm_proj",
    )(x, g.reshape(1, d), sc, sh, w)


def _proj_kernel(h_ref, w_ref, o_ref):
    o_ref[0] = jnp.dot(h_ref[0], w_ref[...], preferred_element_type=F32).astype(o_ref.dtype)


def _proj(h, w, out_dtype):
    b, l, d = h.shape
    n = w.shape[1]
    tm = _pick(l, (1024, 512, 256, 128))
    tn = _pick(n, (1024, 768, 512, 256, 128))
    return pl.pallas_call(
        _proj_kernel,
        grid=(b, l // tm, n // tn),
        in_specs=[pl.BlockSpec((1, tm, d), lambda bi, i, j: (bi, i, 0)),
                  pl.BlockSpec((d, tn), lambda bi, i, j: (0, j))],
        out_specs=pl.BlockSpec((1, tm, tn), lambda bi, i, j: (bi, i, j)),
        out_shape=jax.ShapeDtypeStruct((b, l, n), out_dtype),
        compiler_params=_cparams("parallel", "parallel", "parallel"),
        name="proj",
    )(h, w)


def _tri_inverse_all(a_list):
    n = CHUNK
    eye = (lax.broadcasted_iota(jnp.int32, (n, n), 0) == lax.broadcasted_iota(jnp.int32, (n, n), 1)).astype(F32)
    xs = [eye + a for a in a_list]
    ps = [_dot(a, a) for a in a_list]
    steps = int(math.log2(n)) - 1
    for s in range(steps):
        if s < steps - 1:
            xps = [_dot(jnp.concatenate([x, p], axis=0), p) for x, p in zip(xs, ps)]
            xs = [x + xp[:n] for x, xp in zip(xs, xps)]
            ps = [xp[n:] for xp in xps]
        else:
            xs = [x + _dot(x, p) for x, p in zip(xs, ps)]
    return xs


def _rwkv_chunk_kernel(p_ref, pp_ref, pn_ref, mu_ref, w0_ref, w2_ref, a0_ref, a2_ref, g2_ref, kk_ref, ka_ref,
                       rk_ref, q_ref, y0_ref, m_ref, n_ref, bonus_ref, gate_ref):
    i = pl.program_id(1)
    last = pl.num_programs(1) - 1
    c = RWKV_DIM
    hd = RWKV_HEAD_DIM
    heads = range(RWKV_HEADS)
    p = p_ref[0]
    row = lax.broadcasted_iota(jnp.int32, (CHUNK, 1), 0)
    prev_row = jnp.where(i == 0, 0.0, pp_ref[0, SUBLANES - 1:SUBLANES, :])
    next_row = jnp.where(i == last, 0.0, pn_ref[0, 0:1, :])
    prev = jnp.where(row == 0, prev_row, pltpu.roll(p, 1, axis=0))
    nxt = jnp.where(row == CHUNK - 1, next_row, pltpu.roll(p, CHUNK - 1, axis=0))
    ps = p + mu_ref[...] * (0.5 * (prev + nxt) - p)

    r, k, v = ps[:, :c], ps[:, c:2 * c], ps[:, 2 * c:3 * c]
    o = 3 * c
    xw = (ps[:, o:o + LORA_PAD], ps[:, o + LORA_PAD:o + 2 * LORA_PAD])
    xa = (ps[:, o + 2 * LORA_PAD:o + 3 * LORA_PAD], ps[:, o + 3 * LORA_PAD:o + 4 * LORA_PAD])
    xg = ps[:, o + 4 * LORA_PAD:]

    gate_ref[0] = _dot(_sigmoid(xg), g2_ref[...]).astype(gate_ref.dtype)
    kk = k * kk_ref[...]
    kk = kk * lax.rsqrt(_group_sum(kk * kk, hd) + 1e-12)

    tr = lax.broadcasted_iota(jnp.int32, (CHUNK, CHUNK), 0)
    tc = lax.broadcasted_iota(jnp.int32, (CHUNK, CHUNK), 1)
    eye = tr == tc
    tr2 = lax.broadcasted_iota(jnp.int32, (CHUNK, 2 * CHUNK), 0)
    tc2 = lax.broadcasted_iota(jnp.int32, (CHUNK, 2 * CHUNK), 1) % CHUNK
    kd_sum = jnp.zeros_like(k)
    v_b = v.astype(BF16)
    hsl = [slice(h * hd, (h + 1) * hd) for h in range(RWKV_HEADS)]
    for d in range(2):
        before = (tc < tr) if d == 0 else (tc > tr)
        upto = (tc <= tr) if d == 0 else (tc >= tr)
        upto2 = (tc2 <= tr2) if d == 0 else (tc2 >= tr2)
        z = w0_ref[d:d + 1, :] + _dot(jnp.tanh(xw[d]), w2_ref[d])
        w_log = -(jnp.maximum(-z, 0.0) + jnp.log(1.0 + jnp.exp(-jnp.abs(z)))) - 0.5
        logw = -jnp.exp(w_log)
        a = _sigmoid(a0_ref[d:d + 1, :] + _dot(xa[d], a2_ref[d]))
        kd = k * (1.0 + (a - 1.0) * ka_ref[...])
        kd_sum = kd_sum + kd
        cum = _dot_split(logw, upto.astype(BF16), 3, lhs=False)
        total = cum[CHUNK - 1:CHUNK, :] if d == 0 else cum[0:1, :]
        e_neg = jnp.exp(-cum)
        e_rest = jnp.exp(total - cum)
        p_total = jnp.exp(total)
        beta = kk * a
        al = -kk * jnp.exp(cum - logw)
        rt = r * jnp.exp(cum)
        al_b = al.astype(BF16)
        rt_b = rt.astype(BF16)
        bt_b = (beta * e_neg).astype(BF16)
        kt_b = (kd * e_neg).astype(BF16)
        bh_b = (beta * e_rest).astype(BF16)
        kh_b = (kd * e_rest).astype(BF16)
        g = [_dot(jnp.concatenate([al_b[:, s], rt_b[:, s]], axis=0),
                  jnp.concatenate([bt_b[:, s], kt_b[:, s]], axis=0), _NT) for s in hsl]
        a_ab = [jnp.where(before, gh[:CHUNK, :CHUNK], 0.0) for gh in g]
        a_ak = [jnp.where(before, gh[:CHUNK, CHUNK:], 0.0) for gh in g]
        a_rbk = [jnp.where(upto2, gh[CHUNK:, :], 0.0).astype(BF16) for gh in g]
        akv = [_dot(ak, v_b[:, s]) for ak, s in zip(a_ak, hsl)]
        t_inv = _tri_inverse_all(a_ab)
        wu = [_dot(t, jnp.concatenate([al[:, s], u], axis=1)) for t, u, s in zip(t_inv, akv, hsl)]
        zv = jnp.zeros((CHUNK, hd), BF16)
        zmat = [jnp.concatenate([x.astype(BF16), jnp.concatenate([zv, v_b[:, s]], axis=1)], axis=0)
                for x, s in zip(wu, hsl)]
        qy = [_dot(ar, zm) for ar, zm in zip(a_rbk, zmat)]
        mn = [_dot(zm, jnp.concatenate([bh_b[:, s], kh_b[:, s]], axis=0), _TN)
              for zm, s in zip(zmat, hsl)]
        q_ref[0, d] = (rt + jnp.concatenate([x[:, :hd] for x in qy], axis=1)).astype(q_ref.dtype)
        y0_ref[0, d] = jnp.concatenate([x[:, hd:] for x in qy], axis=1).astype(y0_ref.dtype)
        m_ref[0, d, 0] = jnp.concatenate([jnp.where(eye, p_total[:, s], 0.0) + x[:hd] for x, s in zip(mn, hsl)],
                                         axis=1).astype(m_ref.dtype)
        n_ref[0, d, 0] = jnp.concatenate([x[hd:] for x in mn], axis=1)
    bonus_ref[0] = (_group_sum(r * kd_sum * rk_ref[...], hd) * v).astype(bonus_ref.dtype)


def _rwkv_chunk_ops(p, params):
    b, l, pc = p.shape
    nc = l // CHUNK
    c = RWKV_DIM
    hb = CHUNK // SUBLANES
    nb8 = l // SUBLANES
    mu, w0, w2, a0, a2, g2, k_k, k_a, r_k = params
    const = lambda shape: pl.BlockSpec(shape, lambda bi, i: (0,) * len(shape))
    tok = lambda: pl.BlockSpec((1, CHUNK, c), lambda bi, i: (bi, i, 0))
    return pl.pallas_call(
        _rwkv_chunk_kernel,
        grid=(b, nc),
        in_specs=[pl.BlockSpec((1, CHUNK, pc), lambda bi, i: (bi, i, 0)),
                  pl.BlockSpec((1, SUBLANES, pc), lambda bi, i: (bi, jnp.maximum(i * hb - 1, 0), 0)),
                  pl.BlockSpec((1, SUBLANES, pc), lambda bi, i: (bi, jnp.minimum((i + 1) * hb, nb8 - 1), 0)),
                  const((1, pc)), const((2, c)), const((2, LORA_PAD, c)), const((2, c)), const((2, LORA_PAD, c)),
                  const((GATE_LORA, c)), const((1, c)), const((1, c)), const((1, c))],
        out_specs=[pl.BlockSpec((1, 2, CHUNK, c), lambda bi, i: (bi, 0, i, 0)),
                   pl.BlockSpec((1, 2, CHUNK, c), lambda bi, i: (bi, 0, i, 0)),
                   pl.BlockSpec((1, 2, 1, RWKV_HEAD_DIM, c), lambda bi, i: (bi, 0, i, 0, 0)),
                   pl.BlockSpec((1, 2, 1, RWKV_HEAD_DIM, c), lambda bi, i: (bi, 0, i, 0, 0)),
                   tok(), tok()],
        out_shape=[jax.ShapeDtypeStruct((b, 2, l, c), BF16), jax.ShapeDtypeStruct((b, 2, l, c), BF16),
                   jax.ShapeDtypeStruct((b, 2, nc, RWKV_HEAD_DIM, c), BF16),
                   jax.ShapeDtypeStruct((b, 2, nc, RWKV_HEAD_DIM, c), F32),
                   jax.ShapeDtypeStruct((b, l, c), BF16), jax.ShapeDtypeStruct((b, l, c), BF16)],
        compiler_params=_cparams("parallel", "parallel"),
        name="rwkv_chunk_ops",
    )(p, p, p, mu, w0, w2, a0, a2, g2, k_k, k_a, r_k)


def _rwkv_sweep_kernel(qf_ref, y0f_ref, mf_ref, nf_ref, qb_ref, y0b_ref, mb_ref, nb_ref, s0_ref,
                       yf_ref, yb_ref, sfin_ref, s_ref):
    j = pl.program_id(1)
    hd = RWKV_HEAD_DIM

    @pl.when(j == 0)
    def _():
        s_ref[...] = s0_ref[0]

    dirs = ((qf_ref, y0f_ref, mf_ref, nf_ref, yf_ref), (qb_ref, y0b_ref, mb_ref, nb_ref, yb_ref))
    psl = [slice(p * LANES, (p + 1) * LANES) for p in range(RWKV_DIM // LANES)]
    mask_b = _group_ones(LANES, hd)
    pair_diag = lambda x: jnp.concatenate([x, x], axis=0) * mask_b
    for d, (q_ref, y0_ref, m_ref, n_ref, y_ref) in enumerate(dirs):
        s_b = s_ref[d].astype(BF16)
        q_b = q_ref[0, 0]
        m_b = m_ref[0, 0, 0]
        ys = [_dot(q_b[:, s], pair_diag(s_b[:, s]), _NT) for s in psl]
        sm = [_dot(s_b[:, s], pair_diag(m_b[:, s])) for s in psl]
        y_ref[0] = (y0_ref[0, 0].astype(F32) + jnp.concatenate(ys, axis=1)).astype(y_ref.dtype)
        s_ref[d] = n_ref[0, 0, 0] + jnp.concatenate(sm, axis=1)

    @pl.when(j == pl.num_programs(1) - 1)
    def _():
        sfin_ref[0] = s_ref[...]


def _rwkv_sweep(q, y0, m, n, s0):
    b, _, l, c = q.shape
    nc = l // CHUNK
    hd = RWKV_HEAD_DIM
    tokf = lambda: pl.BlockSpec((1, 1, CHUNK, c), lambda bi, j: (bi, 0, j, 0))
    tokb = lambda: pl.BlockSpec((1, 1, CHUNK, c), lambda bi, j: (bi, 1, nc - 1 - j, 0))
    opf = lambda: pl.BlockSpec((1, 1, 1, hd, c), lambda bi, j: (bi, 0, j, 0, 0))
    opb = lambda: pl.BlockSpec((1, 1, 1, hd, c), lambda bi, j: (bi, 1, nc - 1 - j, 0, 0))
    return pl.pallas_call(
        _rwkv_sweep_kernel,
        grid=(b, nc),
        in_specs=[tokf(), tokf(), opf(), opf(), tokb(), tokb(), opb(), opb(),
                  pl.BlockSpec((1, 2, hd, c), lambda bi, j: (bi, 0, 0, 0))],
        out_specs=[pl.BlockSpec((1, CHUNK, c), lambda bi, j: (bi, j, 0)),
                   pl.BlockSpec((1, CHUNK, c), lambda bi, j: (bi, nc - 1 - j, 0)),
                   pl.BlockSpec((1, 2, hd, c), lambda bi, j: (bi, 0, 0, 0))],
        out_shape=[jax.ShapeDtypeStruct((b, l, c), BF16), jax.ShapeDtypeStruct((b, l, c), BF16),
                   jax.ShapeDtypeStruct((b, 2, hd, c), F32)],
        scratch_shapes=[pltpu.VMEM((2, hd, c), F32)],
        compiler_params=_cparams("parallel", "arbitrary"),
        name="rwkv_sweep",
    )(q, y0, m, n, q, y0, m, n, s0)


def _rwkv_scans(px_r, pc_r, rparams):
    b = px_r.shape[0]
    qc, y0c, mc, nc_, _, _ = _rwkv_chunk_ops(pc_r, rparams)
    s_zero = jnp.zeros((b, 2, RWKV_HEAD_DIM, RWKV_DIM), F32)
    _, _, s_ctx = _rwkv_sweep(qc, y0c, mc, nc_, s_zero)
    qx, y0x, mx, nx, bonus, gate = _rwkv_chunk_ops(px_r, rparams)
    yf, yb, _ = _rwkv_sweep(qx, y0x, mx, nx, s_ctx)
    return (yf, yb), bonus, gate


def _diff_prep_kernel(p_ref, cos_ref, sin_ref, qg_ref, kg_ref, q_ref, k_ref, v_ref, *, rope):
    lane = lax.broadcasted_iota(jnp.int32, (1, LANES), 1)
    first = (lane % 32) < 16
    for hd in range(DIFF_HEADS):
        for off, g_ref, o_ref, scale in ((0, qg_ref, q_ref, DIFF_SCALE * math.log2(math.e)),
                                         (DIFF_QK_COLS, kg_ref, k_ref, 1.0)):
            cs = slice(hd * LANES, (hd + 1) * LANES)
            xb = p_ref[0, :, off + hd * LANES:off + (hd + 1) * LANES].astype(F32)
            ms = _group_sum(xb * xb, DIFF_QK_DIM) * (1.0 / DIFF_QK_DIM)
            y = xb * lax.rsqrt(ms + NORM_EPS) * g_ref[...]
            if rope:
                swapped = jnp.where(first, pltpu.roll(y, LANES - 16, axis=1), pltpu.roll(y, 16, axis=1))
                y = y * cos_ref[...] + swapped * sin_ref[...]
            o_ref[0, :, cs] = (y * scale).astype(BF16)
    ones = jnp.ones((p_ref.shape[1], V_EXT - DIFF_V_DIM), BF16)
    for hd in range(DIFF_HEADS):
        vb = p_ref[0, :, 2 * DIFF_QK_COLS + hd * DIFF_V_DIM:2 * DIFF_QK_COLS + (hd + 1) * DIFF_V_DIM]
        v_ref[0, :, hd * V_EXT:(hd + 1) * V_EXT] = jnp.concatenate([vb.astype(BF16), ones], axis=1)


def _diff_prep(p, cos_t, sin_t, qg, kg, rope):
    b, l, pc = p.shape
    t = _pick(l, (512, 256, 128))
    tok = lambda w=DIFF_DIM: pl.BlockSpec((1, t, w), lambda bi, i: (bi, i, 0))
    shp = jax.ShapeDtypeStruct((b, l, DIFF_DIM), BF16)
    shp_v = jax.ShapeDtypeStruct((b, l, DIFF_HEADS * V_EXT), BF16)
    return pl.pallas_call(
        functools.partial(_diff_prep_kernel, rope=rope),
        grid=(b, l // t),
        in_specs=[pl.BlockSpec((1, t, pc), lambda bi, i: (bi, i, 0)),
                  pl.BlockSpec((t, LANES), lambda bi, i: (i, 0)),
                  pl.BlockSpec((t, LANES), lambda bi, i: (i, 0)),
                  pl.BlockSpec((1, LANES), lambda bi, i: (0, 0)),
                  pl.BlockSpec((1, LANES), lambda bi, i: (0, 0))],
        out_specs=[tok(), tok(), tok(DIFF_HEADS * V_EXT)],
        out_shape=[shp, shp, shp_v],
        compiler_params=_cparams("parallel", "parallel"),
        name="diff_prep_rope" if rope else "diff_prep",
    )(p, cos_t, sin_t, qg, kg)


def _diff_finish(acc1, acc2, lam_ref, sg_ref, o_ref):
    dv = DIFF_V_DIM
    o = acc1[:, :dv] / acc1[:, dv:] - lam_ref[...] * (acc2[:, :dv] / acc2[:, dv:])
    o = o * lax.rsqrt(jnp.mean(o * o, axis=-1, keepdims=True) + SUBLN_EPS)
    o_ref[0] = (o * sg_ref[...]).astype(o_ref.dtype)


def _flash_online_kernel(lam_ref, sg_ref, q_ref, kc_ref, vc_ref, k_ref, v_ref, o_ref, *, tk):
    qd = DIFF_QK_DIM
    tq = q_ref.shape[1]
    q = q_ref[0]
    qs = (q[:, :qd], q[:, qd:])

    def absorb(state, k, v):
        reps = k.shape[0] // LANES
        out = []
        for mp in range(2):
            m_prev, acc = state[mp]
            s = lax.dot_general(qs[mp], k[:, mp * qd:(mp + 1) * qd], _NT, preferred_element_type=F32)
            m_new = jnp.maximum(m_prev, jnp.max(s, axis=-1, keepdims=True))
            alpha = jnp.exp2(m_prev - m_new)
            pr = jnp.exp2(s - jnp.concatenate([m_new] * reps, axis=1))
            acc_new = jnp.concatenate([alpha, alpha], axis=1) * acc + jnp.dot(pr.astype(BF16), v, preferred_element_type=F32)
            out.append((m_new, acc_new))
        return tuple(out)

    init = (jnp.full((tq, LANES), -jnp.inf, F32), jnp.zeros((tq, V_EXT), F32))
    state = absorb((init, init), kc_ref[0], vc_ref[0])

    def body(j, state):
        rows = pl.ds(pl.multiple_of(j * tk, tk), tk)
        return absorb(state, k_ref[0, rows, :], v_ref[0, rows, :])

    (_, acc1), (_, acc2) = lax.fori_loop(0, k_ref.shape[1] // tk, body, state)
    _diff_finish(acc1, acc2, lam_ref, sg_ref, o_ref)


def _flash_bounded_kernel(lam_ref, sg_ref, q_ref, kc_ref, vc_ref, k_ref, v_ref, o_ref, acc_ref, p_ref, *, tk):
    qd = DIFF_QK_DIM
    q = q_ref[0]
    qs = (q[:, :qd], q[:, qd:])
    n_kv = k_ref.shape[1] // tk

    def weights(mp, k):
        s = lax.dot_general(qs[mp], k[:, mp * qd:(mp + 1) * qd], _NT, preferred_element_type=F32)
        return jnp.exp2(s).astype(BF16)

    def chunk(c):
        return pl.ds(pl.multiple_of(jnp.minimum(c, n_kv - 1) * tk, tk), tk)

    for mp in range(2):
        acc_ref[mp] = jnp.dot(weights(mp, kc_ref[0]), vc_ref[0], preferred_element_type=F32)
        p_ref[0, mp] = weights(mp, k_ref[0, pl.ds(0, tk), :])

    steps = 4 if n_kv % 4 == 0 else 2

    def body(j, carry):
        for u in range(steps):
            c = steps * j + u
            v = v_ref[0, chunk(c), :]
            k_next = k_ref[0, chunk(c + 1), :]
            for mp in range(2):
                acc_ref[mp] += jnp.dot(p_ref[u % 2, mp], v, preferred_element_type=F32)
                p_ref[1 - u % 2, mp] = weights(mp, k_next)
        return carry

    lax.fori_loop(0, n_kv // steps, body, 0)
    _diff_finish(acc_ref[0], acc_ref[1], lam_ref, sg_ref, o_ref)


def _diff_attention(q, kc, vc, k, v, lam_vec, sg_vec, bounded):
    b, l, _ = q.shape
    lc = kc.shape[1]
    tq = _pick(l, (256, 128))
    tk = _pick(l, (512, 256, 128))
    if bounded:
        tq = _pick(l, (512, 256, 128))
        tk = _pick(l // 2, (512, 256, 128))
        assert l % (2 * tk) == 0
        body = functools.partial(_flash_bounded_kernel, tk=tk)
        scratch = [pltpu.VMEM((2, tq, V_EXT), F32), pltpu.VMEM((2, 2, tq, tk), BF16)]
    else:
        body = functools.partial(_flash_online_kernel, tk=tk)
        scratch = []
    return pl.pallas_call(
        body,
        grid=(b, DIFF_HEADS, l // tq),
        in_specs=[pl.BlockSpec((1, LANES), lambda bi, h, i: (0, 0)),
                  pl.BlockSpec((1, LANES), lambda bi, h, i: (0, 0)),
                  pl.BlockSpec((1, tq, LANES), lambda bi, h, i: (bi, i, h)),
                  pl.BlockSpec((1, lc, LANES), lambda bi, h, i: (bi, 0, h)),
                  pl.BlockSpec((1, lc, V_EXT), lambda bi, h, i: (bi, 0, h)),
                  pl.BlockSpec((1, l, LANES), lambda bi, h, i: (bi, 0, h)),
                  pl.BlockSpec((1, l, V_EXT), lambda bi, h, i: (bi, 0, h))],
        out_specs=pl.BlockSpec((1, tq, LANES), lambda bi, h, i: (bi, i, h)),
        out_shape=jax.ShapeDtypeStruct((b, l, DIFF_DIM), BF16),
        scratch_shapes=scratch,
        compiler_params=_cparams("parallel", "parallel", "arbitrary"),
        name="diff_flash_bounded" if bounded else "diff_flash_online",
    )(lam_vec, sg_vec, q, kc, vc, k, v)


def _merge_kernel(yf_ref, yb_ref, bonus_ref, gate_ref, yd_ref, pg_ref, lg_ref, lb_ref, wpa_ref, wpb_ref, o_ref):
    hd = RWKV_HEAD_DIM
    y = yf_ref[0].astype(F32) + yb_ref[0].astype(F32)
    dev = y - _group_sum(y, hd) * (1.0 / hd)
    var = _group_sum(dev * dev, hd) * (1.0 / hd)
    yn = dev * lax.rsqrt(var + LNX_EPS) * lg_ref[...] + lb_ref[...]
    y_rwkv = (yn + bonus_ref[0].astype(F32)) * gate_ref[0].astype(F32)
    a = _dot(y_rwkv, wpa_ref[...])
    bb = jnp.dot(yd_ref[0], wpb_ref[...], preferred_element_type=F32)
    ga = _sigmoid(pg_ref[0, :, :D_MODEL].astype(F32))
    gb = _sigmoid(pg_ref[0, :, D_MODEL:].astype(F32))
    o_ref[0] = (ga * a + gb * bb).astype(BF16)


def _merge(y_sweep, bonus, gate, y_diff, p_gate, lnx_g, lnx_b, w_pa, w_pb):
    b, l, c = bonus.shape
    d = D_MODEL
    tm = _pick(l, (256, 128))
    const = lambda shape: pl.BlockSpec(shape, lambda bi, i: (0,) * len(shape))
    tok = lambda w: pl.BlockSpec((1, tm, w), lambda bi, i: (bi, i, 0))
    return pl.pallas_call(
        _merge_kernel,
        grid=(b, l // tm),
        in_specs=[tok(c), tok(c), tok(c), tok(c), tok(DIFF_DIM), tok(GATE_COLS),
                  const((1, c)), const((1, c)), const((c, d)), const((DIFF_DIM, d))],
        out_specs=tok(d),
        out_shape=jax.ShapeDtypeStruct((b, l, d), BF16),
        compiler_params=_cparams("parallel", "parallel"),
        name="merge",
    )(y_sweep[0], y_sweep[1], bonus, gate, y_diff, p_gate, lnx_g, lnx_b, w_pa, w_pb)


def _outproj_kernel(mx_ref, x_ref, gt_ref, g_ref, sc_ref, sh_ref, wo_ref, wr_ref, br_ref, xn_ref, h_ref, lg_ref):
    mix = jnp.dot(mx_ref[0], wo_ref[...], preferred_element_type=F32)
    xn = x_ref[0] + gt_ref[0] * mix
    xn_ref[0] = xn
    y = xn * lax.rsqrt(jnp.mean(xn * xn, axis=-1, keepdims=True) + NORM_EPS) * g_ref[...]
    h = y * (1.0 + sc_ref[0]) + sh_ref[0]
    h_hi, h_lo = _bf16_parts(h, 2)
    h_ref[0] = h_hi
    dot = lambda a, b: jnp.dot(a, b, preferred_element_type=F32)
    lg_ref[0] = dot(h_hi, wr_ref[0]) + (dot(h_lo, wr_ref[0]) + dot(h_hi, wr_ref[1])) + br_ref[...]


def _outproj(mixed, x, gt1, g2, sc2, sh2, w_out, w_router, b_router):
    b, l, d = x.shape
    tm = _pick(l, (512, 256, 128))
    const = lambda shape: pl.BlockSpec(shape, lambda bi, i: (0,) * len(shape))
    tok = lambda w: pl.BlockSpec((1, tm, w), lambda bi, i: (bi, i, 0))
    per_b = lambda: pl.BlockSpec((1, 1, d), lambda bi, i: (bi, 0, 0))
    return pl.pallas_call(
        _outproj_kernel,
        grid=(b, l // tm),
        in_specs=[tok(d), tok(d), per_b(), const((1, d)), per_b(), per_b(),
                  const((d, d)), const((2, d, ROUTER_PAD)), const((1, ROUTER_PAD))],
        out_specs=[tok(d), tok(d), tok(ROUTER_PAD)],
        out_shape=[jax.ShapeDtypeStruct((b, l, d), F32), jax.ShapeDtypeStruct((b, l, d), BF16),
                   jax.ShapeDtypeStruct((b, l, ROUTER_PAD), F32)],
        compiler_params=_cparams("parallel", "parallel"),
        name="outproj_router",
    )(mixed, x, gt1, g2, sc2, sh2, w_out, w_router, b_router)


def _moe_kernel(wb_ref, we_ref, lo_ref, hi_ref, x_ref, sw_ref, w1_ref, w3_ref, w2_ref, o_ref,
                w1b_ref, w3b_ref, w2b_ref, cached_ref):
    i = pl.program_id(0)
    lo, hi = lo_ref[i], hi_ref[i]
    live = hi > lo

    @pl.when(i == 0)
    def _():
        cached_ref[0] = -1

    @pl.when((i == 0) | (wb_ref[i] != wb_ref[jnp.maximum(i - 1, 0)]))
    def _():
        o_ref[...] = jnp.zeros(o_ref.shape, o_ref.dtype)

    @pl.when(live & (cached_ref[0] != we_ref[i]))
    def _():
        w1b_ref[...] = w1_ref[0].astype(BF16)
        w3b_ref[...] = w3_ref[0].astype(BF16)
        w2b_ref[...] = w2_ref[0].astype(BF16)
        cached_ref[0] = we_ref[i]

    @pl.when(live)
    def _():
        xb = x_ref[...]
        u = jnp.dot(xb, w1b_ref[...], preferred_element_type=F32)
        g = jnp.dot(xb, w3b_ref[...], preferred_element_type=F32)
        hmid = (u * _sigmoid(u) * g).astype(BF16)
        res = (jnp.dot(hmid, w2b_ref[...], preferred_element_type=F32) * sw_ref[...]).astype(o_ref.dtype)
        row = lax.broadcasted_iota(jnp.int32, (o_ref.shape[0], 1), 0)
        o_ref[...] = jnp.where((row >= lo) & (row < hi), res, o_ref[...])


def _moe_ffn(xs, sw, items, w1, w3, w2):
    n_rows, d = xs.shape
    wb, we, lo, hi = items
    grid_spec = pltpu.PrefetchScalarGridSpec(
        num_scalar_prefetch=4,
        grid=(wb.shape[0],),
        in_specs=[pl.BlockSpec((MOE_TILE, d), lambda i, wb, we, lo, hi: (wb[i], 0)),
                  pl.BlockSpec((MOE_TILE, 1), lambda i, wb, we, lo, hi: (wb[i], 0)),
                  pl.BlockSpec((1, d, D_EXPERT), lambda i, wb, we, lo, hi: (we[i], 0, 0)),
                  pl.BlockSpec((1, d, D_EXPERT), lambda i, wb, we, lo, hi: (we[i], 0, 0)),
                  pl.BlockSpec((1, D_EXPERT, d), lambda i, wb, we, lo, hi: (we[i], 0, 0))],
        out_specs=pl.BlockSpec((MOE_TILE, d), lambda i, wb, we, lo, hi: (wb[i], 0)),
        scratch_shapes=[pltpu.VMEM((d, D_EXPERT), BF16), pltpu.VMEM((d, D_EXPERT), BF16),
                        pltpu.VMEM((D_EXPERT, d), BF16), pltpu.SMEM((1,), jnp.int32)],
    )
    return pl.pallas_call(
        _moe_kernel,
        grid_spec=grid_spec,
        out_shape=jax.ShapeDtypeStruct((n_rows, d), BF16),
        compiler_params=_cparams("arbitrary"),
        name="moe_ffn",
    )(wb, we, lo, hi, xs, sw, w1, w3, w2)


def _final_kernel(x_ref, gt_ref, y0_ref, y1_ref, o_ref):
    o_ref[0] = x_ref[0] + gt_ref[0] * (y0_ref[0].astype(F32) + y1_ref[0].astype(F32))


def _final(x_new, gt2, y0, y1):
    b, l, d = x_new.shape
    tm = _pick(l, (512, 256, 128))
    tok = lambda: pl.BlockSpec((1, tm, d), lambda bi, i: (bi, i, 0))
    return pl.pallas_call(
        _final_kernel,
        grid=(b, l // tm),
        in_specs=[tok(), pl.BlockSpec((1, 1, d), lambda bi, i: (bi, 0, 0)), tok(), tok()],
        out_specs=tok(),
        out_shape=jax.ShapeDtypeStruct((b, l, d), F32),
        compiler_params=_cparams("parallel", "parallel"),
        name="moe_residual",
    )(x_new, gt2, y0, y1)


def _pad_lora_cols(w, widths):
    parts, o = [], 0
    for wd in widths:
        blk = w[..., o:o + wd]
        parts.append(jnp.pad(blk, [(0, 0)] * (w.ndim - 1) + [(0, LORA_PAD - wd)]))
        o += wd
    return jnp.concatenate(parts, axis=-1)


def _rope_tables(l):
    half = DIFF_QK_DIM // 2
    inv_freq = ROPE_THETA ** (-jnp.arange(0, half, 2, dtype=F32) / half)
    t = jnp.arange(l, dtype=jnp.int32)
    rows = (t // GRID_W).astype(F32)[:, None] * inv_freq
    cols = (t % GRID_W).astype(F32)[:, None] * inv_freq
    cos64 = jnp.concatenate([jnp.cos(rows), jnp.cos(rows), jnp.cos(cols), jnp.cos(cols)], axis=1)
    sin64 = jnp.concatenate([-jnp.sin(rows), jnp.sin(rows), -jnp.sin(cols), jnp.sin(cols)], axis=1)
    return jnp.tile(cos64, (1, 2)), jnp.tile(sin64, (1, 2))


def _route(logits, n_tok):
    g_logits = logits[:, :N_GROUPS]
    e_logits = logits[:, N_GROUPS:N_GROUPS + N_EXPERTS].reshape(n_tok, N_GROUPS, EXPERTS_PER_GROUP)
    p_group = jax.nn.softmax(g_logits, axis=-1)
    g_top = jnp.argmax(g_logits, axis=-1)
    p_g = jnp.take_along_axis(p_group, g_top[:, None], axis=1)[:, 0]
    e_sel = jnp.take_along_axis(e_logits, g_top[:, None, None], axis=1)[:, 0]
    top_p, top_i = lax.top_k(jax.nn.softmax(e_sel, axis=-1), TOP_K)
    top_p = top_p / jnp.sum(top_p, axis=-1, keepdims=True)
    gate = p_g[:, None] * top_p
    expert = g_top[:, None] * EXPERTS_PER_GROUP + top_i

    n_assign = n_tok * TOP_K
    assert n_assign % MOE_TILE == 0
    flat_e = expert.reshape(-1).astype(jnp.int32)
    ids = jnp.arange(n_assign, dtype=jnp.int32)
    sorted_e, order, sorted_w = lax.sort((flat_e, ids, gate.reshape(-1)), num_keys=1, is_stable=True)
    _, rank = lax.sort((order, ids), num_keys=1)
    ends = jnp.searchsorted(sorted_e, jnp.arange(N_EXPERTS, dtype=jnp.int32), side='right').astype(jnp.int32)
    starts = jnp.concatenate([jnp.zeros((1,), jnp.int32), ends[:-1]])

    nb = n_assign // MOE_TILE
    blk_lo = jnp.arange(nb, dtype=jnp.int32) * MOE_TILE
    e_first = jnp.searchsorted(ends, blk_lo, side='right').astype(jnp.int32)
    e_last = jnp.searchsorted(ends, blk_lo + MOE_TILE - 1, side='right').astype(jnp.int32)
    per_blk = e_last - e_first + 1
    cum = jnp.cumsum(per_blk)
    it = jnp.arange(nb + N_EXPERTS - 1, dtype=jnp.int32)
    wb = jnp.minimum(jnp.searchsorted(cum, it, side='right'), nb - 1).astype(jnp.int32)
    we = jnp.clip(e_first[wb] + it - (cum[wb] - per_blk[wb]), 0, N_EXPERTS - 1)
    lo = jnp.clip(starts[we], blk_lo[wb], blk_lo[wb] + MOE_TILE) - blk_lo[wb]
    hi = jnp.clip(ends[we], blk_lo[wb], blk_lo[wb] + MOE_TILE) - blk_lo[wb]
    hi = jnp.where(it < cum[-1], hi, lo)
    return order // TOP_K, sorted_w, rank.reshape(n_tok, TOP_K), (wb, we, lo, hi)


def kernel(x, c, ctx, c_ctx, ada_w, ada_b, norm1_g, norm2_g, w_in, shift_mu, rwkv_w0, rwkv_w2, rwkv_a0, rwkv_a2,
           rwkv_g2, rwkv_k_k, rwkv_k_a, rwkv_r_k, rwkv_lnx_g, rwkv_lnx_b, qn_g, kn_g, diff_lambda, subln_g,
           w_pa, w_pb, w_out, router_g_w, router_g_b, router_e_w, router_e_b, exp_w1, exp_w3, exp_w2):
    assert ada_w.shape[0] == 1, "single-layer block"
    b, l, d = x.shape
    lc = ctx.shape[1]
    lam_init = 0.8 - 0.6 * math.exp(-0.3 * 0)
    lv = diff_lambda[0].astype(F32)
    lam = jnp.exp(jnp.sum(lv[0] * lv[1])) - jnp.exp(jnp.sum(lv[2] * lv[3])) + lam_init

    rows = (b + 1 + SUBLANES - 1) // SUBLANES * SUBLANES
    cm = jnp.zeros((rows, d), F32).at[:b].set(c).at[b].set(c_ctx)
    mod = _modulation(cm, ada_w[0], ada_b[0])
    sh1, sc1, gt1, sh2, sc2, gt2 = [mod[:b, None, k * d:(k + 1) * d] for k in range(6)]
    csh1, csc1 = [jnp.broadcast_to(mod[b, k * d:(k + 1) * d], (b, 1, d)) for k in range(2)]

    w = w_in[0]
    lora_widths = (DECAY_LORA, DECAY_LORA, AAA_LORA, AAA_LORA)
    o_lora = 3 * RWKV_DIM
    o_glora = o_lora + sum(lora_widths)
    pad_cols = lambda m: jnp.concatenate(
        [m[..., :o_lora], _pad_lora_cols(m[..., o_lora:o_glora], lora_widths), m[..., o_glora:RWKV_COLS]], axis=-1)
    w_rwkv = pad_cols(w).astype(BF16)
    w_diff = w[:, RWKV_COLS:RWKV_COLS + DIFF_COLS].astype(BF16)
    w_gate = w[:, RWKV_COLS + DIFF_COLS:].astype(BF16)
    g1 = norm1_g[0]
    px_r, hx = _norm_proj(x, g1, sc1, sh1, w_rwkv, F32)
    px_d = _proj(hx, w_diff, BF16)
    px_g = _proj(hx, w_gate, BF16)
    pc_r, hc = _norm_proj(ctx, g1, csc1, csh1, w_rwkv, F32)
    pc_d = _proj(hc, w_diff, BF16)

    pad_rows = lambda m: jnp.pad(m, ((0, 0), (0, LORA_PAD - m.shape[1]), (0, 0)))
    rparams = (pad_cols(shift_mu[0])[None], rwkv_w0[0], pad_rows(rwkv_w2[0]), rwkv_a0[0], pad_rows(rwkv_a2[0]),
               rwkv_g2[0], rwkv_k_k[0][None], rwkv_k_a[0][None], rwkv_r_k[0].reshape(1, RWKV_DIM))
    y_sweep, bonus, gate = _rwkv_scans(px_r, pc_r, rparams)

    cos_t, sin_t = _rope_tables(l)
    qg = jnp.tile(qn_g[0], 2)[None]
    kg = jnp.tile(kn_g[0], 2)[None]
    q_x, k_x, v_x = _diff_prep(px_d, cos_t, sin_t, qg, kg, True)
    _, k_c, v_c = _diff_prep(pc_d, cos_t[:lc], sin_t[:lc], qg, kg, False)
    lam_vec = jnp.full((1, LANES), lam, F32)
    sg_vec = (subln_g[0] * (1.0 - lam_init))[None]
    score_bound = (1.05 * DIFF_QK_DIM * DIFF_SCALE * math.log2(math.e)
                   * jnp.max(jnp.abs(qn_g[0])) * jnp.max(jnp.abs(kn_g[0])))
    attn_args = (q_x, k_c, v_c, k_x, v_x, lam_vec, sg_vec)
    y_diff = lax.cond(score_bound <= SCORE_LOG2_LIMIT,
                      lambda a: _diff_attention(*a, bounded=True),
                      lambda a: _diff_attention(*a, bounded=False), attn_args)

    mixed = _merge(y_sweep, bonus, gate, y_diff, px_g, rwkv_lnx_g[0][None], rwkv_lnx_b[0][None],
                   w_pa[0].astype(BF16), w_pb[0].astype(BF16))
    n_r = N_GROUPS + N_EXPERTS
    w_router = jnp.zeros((d, ROUTER_PAD), F32).at[:, :N_GROUPS].set(router_g_w[0]).at[:, N_GROUPS:n_r].set(router_e_w[0])
    b_router = jnp.zeros((1, ROUTER_PAD), F32).at[0, :N_GROUPS].set(router_g_b[0]).at[0, N_GROUPS:n_r].set(router_e_b[0])
    w_router_hi = w_router.astype(BF16)
    w_router_lo = (w_router - w_router_hi.astype(F32)).astype(BF16)
    x_new, h2, logits = _outproj(mixed, x, gt1, norm2_g[0][None], sc2, sh2, w_out[0].astype(BF16),
                                 jnp.stack([w_router_hi, w_router_lo]), b_router)

    n_tok = b * l
    row_tok, row_w, row_of, items = _route(logits.reshape(n_tok, ROUTER_PAD), n_tok)
    xs = h2.reshape(n_tok, d)[row_tok]
    out = _moe_ffn(xs, row_w[:, None], items, exp_w1[0], exp_w3[0], exp_w2[0])
    y0 = out[row_of[:, 0]].reshape(b, l, d)
    y1 = out[row_of[:, 1]].reshape(b, l, d)
    return _final(x_new, gt2, y0, y1)
```

```python
import functools
import math

import jax
import jax.numpy as jnp
from jax import lax
from jax.experimental import pallas as pl
from jax.experimental.pallas import tpu as pltpu

F32 = jnp.float32
BF16 = jnp.bfloat16
HIGHEST = lax.Precision.HIGHEST

D_MODEL = 2048
GRID_W = 64
RWKV_HEADS = 16
RWKV_HEAD_DIM = 64
RWKV_DIM = RWKV_HEADS * RWKV_HEAD_DIM
DECAY_LORA = 96
AAA_LORA = 96
GATE_LORA = 256
RWKV_COLS = 3 * RWKV_DIM + 2 * DECAY_LORA + 2 * AAA_LORA + GATE_LORA
DIFF_HEADS = 8
DIFF_QK_DIM = 64
DIFF_V_DIM = 2 * DIFF_QK_DIM
DIFF_DIM = DIFF_HEADS * DIFF_V_DIM
DIFF_QK_COLS = DIFF_HEADS * 2 * DIFF_QK_DIM
DIFF_COLS = 2 * DIFF_QK_COLS + DIFF_DIM
DIFF_SCALE = DIFF_QK_DIM ** -0.5
ROPE_THETA = 10000.0
GATE_COLS = 2 * D_MODEL
N_GROUPS = 4
EXPERTS_PER_GROUP = 8
N_EXPERTS = N_GROUPS * EXPERTS_PER_GROUP
TOP_K = 2
D_EXPERT = 512
NORM_EPS = 1e-6
SUBLN_EPS = 1e-5
LNX_EPS = 64e-5

LANES = 128
SUBLANES = 8
VMEM_LIMIT_BYTES = 56 * 1024 * 1024

LORA_PAD = LANES
RWKV_PCOLS = 3 * RWKV_DIM + 4 * LORA_PAD + GATE_LORA
CHUNK = 64
V_EXT =2 * DIFF_V_DIM
SCORE_LOG2_LIMIT = 60.0
ROUTER_PAD = LANES
MOE_TILE = 512


def _cparams(*sem):
    return pltpu.CompilerParams(dimension_semantics=sem, vmem_limit_bytes=VMEM_LIMIT_BYTES)


def _sigmoid(x):
    return 1.0 / (1.0 + jnp.exp(-x))


def _dot(a, b, dims=(((1,), (0,)), ((), ()))):
    return lax.dot_general(a.astype(BF16), b.astype(BF16), dims, preferred_element_type=F32)


def _dot_f32(a, b, dims=(((1,), (0,)), ((), ()))):
    return lax.dot_general(a, b, dims, precision=HIGHEST, preferred_element_type=F32)


_NT = (((1,), (1,)), ((), ()))
_TN = (((0,), (0,)), ((), ()))


def _bf16_parts(x, n):
    parts = []
    for _ in range(n):
        p = x.astype(BF16)
        parts.append(p)
        x = x - p.astype(F32)
    return parts


def _dot_split(x, w_b, n, lhs=True):
    parts = _bf16_parts(x, n)
    outs = [jnp.dot(p, w_b, preferred_element_type=F32) if lhs else jnp.dot(w_b, p, preferred_element_type=F32)
            for p in parts]
    return functools.reduce(lambda a, b: a + b, outs)


def _group_ones(width, group):
    r = lax.broadcasted_iota(jnp.int32, (width, width), 0) // group
    c = lax.broadcasted_iota(jnp.int32, (width, width), 1) // group
    return (r == c).astype(BF16)


def _group_sum(x, group):
    ones = _group_ones(LANES, group)
    parts = [_dot_split(x[:, j * LANES:(j + 1) * LANES], ones, 2) for j in range(x.shape[1] // LANES)]
    return parts[0] if len(parts) == 1 else jnp.concatenate(parts, axis=1)


def _mod_kernel(c_ref, w_ref, b_ref, o_ref):
    c = c_ref[...]
    o_ref[...] = _dot_f32(c * _sigmoid(c), w_ref[...]) + b_ref[...]


def _modulation(cm, ada_w, ada_b):
    rows, d = cm.shape
    n = ada_w.shape[1]
    tn = 1536
    return pl.pallas_call(
        _mod_kernel,
        grid=(n // tn,),
        in_specs=[pl.BlockSpec((rows, d), lambda j: (0, 0)),
                  pl.BlockSpec((d, tn), lambda j: (0, j)),
                  pl.BlockSpec((1, tn), lambda j: (0, j))],
        out_specs=pl.BlockSpec((rows, tn), lambda j: (0, j)),
        out_shape=jax.ShapeDtypeStruct((rows, n), F32),
        compiler_params=_cparams("parallel"),
        name="modulation",
    )(cm, ada_w, ada_b.reshape(1, n))


def _norm_mod_kernel(x_ref, g_ref, sc_ref, sh_ref, h_ref):
    x = x_ref[0]
    y = x * lax.rsqrt(jnp.mean(x * x, axis=-1, keepdims=True) + NORM_EPS) * g_ref[...]
    h_ref[0] = (y * (1.0 + sc_ref[0]) + sh_ref[0]).astype(BF16)


def _pick(n, prefs):
    for t in prefs:
        if n % t == 0:
            return t
    return n


def _norm_mod(x, g, sc, sh):
    b, l, d = x.shape
    tm = _pick(l, (512, 256, 128))
    return pl.pallas_call(
        _norm_mod_kernel,
        grid=(b, l // tm),
        in_specs=[pl.BlockSpec((1, tm, d), lambda bi, i: (bi, i, 0)),
                  pl.BlockSpec((1, d), lambda bi, i: (0, 0)),
                  pl.BlockSpec((1, 1, d), lambda bi, i: (bi, 0, 0)),
                  pl.BlockSpec((1, 1, d), lambda bi, i: (bi, 0, 0))],
        out_specs=pl.BlockSpec((1, tm, d), lambda bi, i: (bi, i, 0)),
        out_shape=jax.ShapeDtypeStruct((b, l, d), BF16),
        compiler_params=_cparams("parallel", "parallel"),
        name="norm_mod",
    )(x, g.reshape(1, d), sc, sh)


def _proj_kernel(h_ref, w_ref, o_ref):
    o_ref[0] = jnp.dot(h_ref[0], w_ref[...], preferred_element_type=F32).astype(o_ref.dtype)


def _proj(h, w, out_dtype):
    b, l, d = h.shape
    n = w.shape[1]
    tm = _pick(l, (1024, 512, 256, 128))
    tn = _pick(n, (1024, 768, 512, 256, 128))
    return pl.pallas_call(
        _proj_kernel,
        grid=(b, l // tm, n // tn),
        in_specs=[pl.BlockSpec((1, tm, d), lambda bi, i, j: (bi, i, 0)),
                  pl.BlockSpec((d, tn), lambda bi, i, j: (0, j))],
        out_specs=pl.BlockSpec((1, tm, tn), lambda bi, i, j: (bi, i, j)),
        out_shape=jax.ShapeDtypeStruct((b, l, n), out_dtype),
        compiler_params=_cparams("parallel", "parallel", "parallel"),
        name="proj",
    )(h, w)


def _pair_diag(x_b, mask_b):
    return jnp.concatenate([x_b, x_b], axis=0) * mask_b


def _tri_inverse_pairs(a_list, eye_f, mask_b):
    n = CHUNK
    mm = lambda l, r: jnp.dot(l.astype(BF16), _pair_diag(r.astype(BF16), mask_b), preferred_element_type=F32)
    xs = [eye_f + a for a in a_list]
    ps = [mm(a, a) for a in a_list]
    steps = int(math.log2(n)) - 1
    for s in range(steps):
        if s < steps - 1:
            xps = [mm(jnp.concatenate([x, p], axis=0), p) for x, p in zip(xs, ps)]
            xs = [x + xp[:n] for x, xp in zip(xs, xps)]
            ps = [xp[n:] for xp in xps]
        else:
            xs = [x + mm(x, p) for x, p in zip(xs, ps)]
    return xs


def _rwkv_chunk_kernel(p_ref, pp_ref, pn_ref, mu_ref, w0_ref, w2_ref, a0_ref, a2_ref, g2_ref, kk_ref, ka_ref,
                       rk_ref, q_ref, y0_ref, m_ref, n_ref, bonus_ref, gate_ref):
    i = pl.program_id(1)
    last = pl.num_programs(1) - 1
    c = RWKV_DIM
    hd = RWKV_HEAD_DIM
    heads = range(RWKV_HEADS)
    p = p_ref[0]
    row = lax.broadcasted_iota(jnp.int32, (CHUNK, 1), 0)
    prev_row = jnp.where(i == 0, 0.0, pp_ref[0, SUBLANES - 1:SUBLANES, :])
    next_row = jnp.where(i == last, 0.0, pn_ref[0, 0:1, :])
    prev = jnp.where(row == 0, prev_row, pltpu.roll(p, 1, axis=0))
    nxt = jnp.where(row == CHUNK - 1, next_row, pltpu.roll(p, CHUNK - 1, axis=0))
    ps = p + mu_ref[...] * (0.5 * (prev + nxt) - p)

    r, k, v = ps[:, :c], ps[:, c:2 * c], ps[:, 2 * c:3 * c]
    o = 3 * c
    xw = (ps[:, o:o + LORA_PAD], ps[:, o + LORA_PAD:o + 2 * LORA_PAD])
    xa = (ps[:, o + 2 * LORA_PAD:o + 3 * LORA_PAD], ps[:, o + 3 * LORA_PAD:o + 4 * LORA_PAD])
    xg = ps[:, o + 4 * LORA_PAD:]

    gate_ref[0] = _dot(_sigmoid(xg), g2_ref[...]).astype(gate_ref.dtype)
    kk = k * kk_ref[...]
    kk = kk * lax.rsqrt(_group_sum(kk * kk, hd) + 1e-12)

    tr = lax.broadcasted_iota(jnp.int32, (CHUNK, CHUNK), 0)
    tc = lax.broadcasted_iota(jnp.int32, (CHUNK, CHUNK), 1)
    tr2 = lax.broadcasted_iota(jnp.int32, (CHUNK, LANES), 0)
    lane2 = lax.broadcasted_iota(jnp.int32, (CHUNK, LANES), 1)
    tc2 = lane2 % CHUNK
    lane_head = lane2 // hd
    eye2 = tr2 == tc2
    mask_b = _group_ones(LANES, hd)
    kd_sum = jnp.zeros_like(k)
    v_b = v.astype(BF16)
    psl = [slice(p * LANES, (p + 1) * LANES) for p in range(c // LANES)]
    before2, upto2, prep = [], [], []
    for d in range(2):
        before2.append((tc2 < tr2) if d == 0 else (tc2 > tr2))
        upto = (tc <= tr) if d == 0 else (tc >= tr)
        upto2.append((tc2 <= tr2) if d == 0 else (tc2 >= tr2))
        z = w0_ref[d:d + 1, :] + _dot(jnp.tanh(xw[d]), w2_ref[d])
        w_log = -(jnp.maximum(-z, 0.0) + jnp.log(1.0 + jnp.exp(-jnp.abs(z)))) - 0.5
        logw = -jnp.exp(w_log)
        a = _sigmoid(a0_ref[d:d + 1, :] + _dot(xa[d], a2_ref[d]))
        kd = k * (1.0 + (a - 1.0) * ka_ref[...])
        kd_sum = kd_sum + kd
        cum = _dot_split(logw, upto.astype(BF16), 3, lhs=False)
        total = cum[CHUNK - 1:CHUNK, :] if d == 0 else cum[0:1, :]
        e_neg = jnp.exp(-cum)
        e_rest = jnp.exp(total - cum)
        p_total = jnp.exp(total)
        beta = kk * a
        al = -kk * jnp.exp(cum - logw)
        rt = r * jnp.exp(cum)
        al_b = al.astype(BF16)
        rt_b = rt.astype(BF16)
        bt_b = (beta * e_neg).astype(BF16)
        kt_b = (kd * e_neg).astype(BF16)
        bh_b = (beta * e_rest).astype(BF16)
        kh_b = (kd * e_rest).astype(BF16)
        prep.append((al_b, rt_b, bt_b, kt_b, bh_b, kh_b, rt, p_total))

    dp = [(d, s) for d in range(2) for s in psl]
    fdot = lambda a, b: jnp.dot(a, b, preferred_element_type=F32)
    bd = lambda x: _pair_diag(x, mask_b)
    v_bd = [bd(v_b[:, s]) for s in psl] * 2
    zeros_sq = jnp.zeros((LANES, LANES), BF16)
    zeros_tl = jnp.zeros((CHUNK, LANES), BF16)
    gm = [lax.dot_general(jnp.concatenate([prep[d][0][:, s], prep[d][1][:, s]], axis=0),
                          jnp.concatenate([bd(prep[d][2][:, s]), bd(prep[d][3][:, s])], axis=0),
                          _NT, preferred_element_type=F32) for d, s in dp]
    a_ab = [jnp.where(before2[d], x[:CHUNK, :LANES], 0.0) for x, (d, s) in zip(gm, dp)]
    a_ak = [jnp.where(before2[d], x[:CHUNK, LANES:], 0.0).astype(BF16) for x, (d, s) in zip(gm, dp)]
    a_rb = [jnp.where(upto2[d], x[CHUNK:, :LANES], 0.0).astype(BF16) for x, (d, s) in zip(gm, dp)]
    a_rk = [jnp.where(upto2[d], x[CHUNK:, LANES:], 0.0).astype(BF16) for x, (d, s) in zip(gm, dp)]
    akv = [fdot(ak, vd) for ak, vd in zip(a_ak, v_bd)]
    t_inv = _tri_inverse_pairs(a_ab, eye2.astype(F32), mask_b)
    wu = [fdot(t.astype(BF16), jnp.concatenate([bd(prep[d][0][:, s]), bd(u.astype(BF16))], axis=1))
          for t, u, (d, s) in zip(t_inv, akv, dp)]
    w_b = [x[:, :LANES].astype(BF16) for x in wu]
    u_b = [x[:, LANES:].astype(BF16) for x in wu]
    qy = [fdot(jnp.concatenate([rb, rk], axis=1),
               jnp.concatenate([jnp.concatenate([bd(w), bd(u)], axis=1),
                                jnp.concatenate([zeros_sq, vd], axis=1)], axis=0))
          for rb, rk, w, u, vd in zip(a_rb, a_rk, w_b, u_b, v_bd)]
    full = [lax.dot_general(jnp.concatenate([jnp.concatenate([w, u], axis=1),
                                             jnp.concatenate([zeros_tl, v_b[:, s]], axis=1)], axis=0),
                            jnp.concatenate([prep[d][4][:, s], prep[d][5][:, s]], axis=0),
                            _TN, preferred_element_type=F32)
            for w, u, (d, s) in zip(w_b, u_b, dp)]

    def diag_blocks(x):
        return jnp.where(lane_head == 0, x[:hd], 0.0) + jnp.where(lane_head == 1, x[hd:2 * hd], 0.0)

    npair = len(psl)
    for d in range(2):
        sl = slice(d * npair, (d + 1) * npair)
        rt, p_total = prep[d][6], prep[d][7]
        q_ref[0, d] = (rt + jnp.concatenate([x[:, :LANES] for x in qy[sl]], axis=1)).astype(q_ref.dtype)
        y0_ref[0, d] = jnp.concatenate([x[:, LANES:] for x in qy[sl]], axis=1).astype(y0_ref.dtype)
        m_ref[0, d, 0] = jnp.concatenate([jnp.where(eye2, p_total[:, s], 0.0) + diag_blocks(x[:LANES])
                                          for x, s in zip(full[sl], psl)], axis=1).astype(m_ref.dtype)
        n_ref[0, d, 0] = jnp.concatenate([diag_blocks(x[LANES:]) for x in full[sl]], axis=1)
    bonus_ref[0] = (_group_sum(r * kd_sum * rk_ref[...], hd) * v).astype(bonus_ref.dtype)


def _rwkv_chunk_ops(p, params):
    b, l, pc = p.shape
    nc = l // CHUNK
    c = RWKV_DIM
    hb = CHUNK // SUBLANES
    nb8 = l // SUBLANES
    mu, w0, w2, a0, a2, g2, k_k, k_a, r_k = params
    const = lambda shape: pl.BlockSpec(shape, lambda bi, i: (0,) * len(shape))
    tok = lambda: pl.BlockSpec((1, CHUNK, c), lambda bi, i: (bi, i, 0))
    return pl.pallas_call(
        _rwkv_chunk_kernel,
        grid=(b, nc),
        in_specs=[pl.BlockSpec((1, CHUNK, pc), lambda bi, i: (bi, i, 0)),
                  pl.BlockSpec((1, SUBLANES, pc), lambda bi, i: (bi, jnp.maximum(i * hb - 1, 0), 0)),
                  pl.BlockSpec((1, SUBLANES, pc), lambda bi, i: (bi, jnp.minimum((i + 1) * hb, nb8 - 1), 0)),
                  const((1, pc)), const((2, c)), const((2, LORA_PAD, c)), const((2, c)), const((2, LORA_PAD, c)),
                  const((GATE_LORA, c)), const((1, c)), const((1, c)), const((1, c))],
        out_specs=[pl.BlockSpec((1, 2, CHUNK, c), lambda bi, i: (bi, 0, i, 0)),
                   pl.BlockSpec((1, 2, CHUNK, c), lambda bi, i: (bi, 0, i, 0)),
                   pl.BlockSpec((1, 2, 1, RWKV_HEAD_DIM, c), lambda bi, i: (bi, 0, i, 0, 0)),
                   pl.BlockSpec((1, 2, 1, RWKV_HEAD_DIM, c), lambda bi, i: (bi, 0, i, 0, 0)),
                   tok(), tok()],
        out_shape=[jax.ShapeDtypeStruct((b, 2, l, c), BF16), jax.ShapeDtypeStruct((b, 2, l, c), BF16),
                   jax.ShapeDtypeStruct((b, 2, nc, RWKV_HEAD_DIM, c), BF16),
                   jax.ShapeDtypeStruct((b, 2, nc, RWKV_HEAD_DIM, c), F32),
                   jax.ShapeDtypeStruct((b, l, c), BF16), jax.ShapeDtypeStruct((b, l, c), BF16)],
        compiler_params=_cparams("parallel", "parallel"),
        name="rwkv_chunk_ops",
    )(p, p, p, mu, w0, w2, a0, a2, g2, k_k, k_a, r_k)


def _rwkv_sweep_kernel(qf_ref, y0f_ref, mf_ref, nf_ref, qb_ref, y0b_ref, mb_ref, nb_ref, s0_ref,
                       yf_ref, yb_ref, sfin_ref, s_ref):
    j = pl.program_id(1)
    hd = RWKV_HEAD_DIM

    @pl.when(j == 0)
    def _():
        s_ref[...] = s0_ref[0]

    dirs = ((qf_ref, y0f_ref, mf_ref, nf_ref, yf_ref), (qb_ref, y0b_ref, mb_ref, nb_ref, yb_ref))
    psl = [slice(p * LANES, (p + 1) * LANES) for p in range(RWKV_DIM // LANES)]
    mask_b = _group_ones(LANES, hd)
    pair_diag = lambda x: jnp.concatenate([x, x], axis=0) * mask_b
    for d, (q_ref, y0_ref, m_ref, n_ref, y_ref) in enumerate(dirs):
        s_b = s_ref[d].astype(BF16)
        q_b = q_ref[0, 0]
        m_b = m_ref[0, 0, 0]
        ys = [_dot(q_b[:, s], pair_diag(s_b[:, s]), _NT) for s in psl]
        sm = [_dot(s_b[:, s], pair_diag(m_b[:, s])) for s in psl]
        y_ref[0] = (y0_ref[0, 0].astype(F32) + jnp.concatenate(ys, axis=1)).astype(y_ref.dtype)
        s_ref[d] = n_ref[0, 0, 0] + jnp.concatenate(sm, axis=1)

    @pl.when(j == pl.num_programs(1) - 1)
    def _():
        sfin_ref[0] = s_ref[...]


def _rwkv_sweep(q, y0, m, n, s0):
    b, _, l, c = q.shape
    nc = l // CHUNK
    hd = RWKV_HEAD_DIM
    tokf = lambda: pl.BlockSpec((1, 1, CHUNK, c), lambda bi, j: (bi, 0, j, 0))
    tokb = lambda: pl.BlockSpec((1, 1, CHUNK, c), lambda bi, j: (bi, 1, nc - 1 - j, 0))
    opf = lambda: pl.BlockSpec((1, 1, 1, hd, c), lambda bi, j: (bi, 0, j, 0, 0))
    opb = lambda: pl.BlockSpec((1, 1, 1, hd, c), lambda bi, j: (bi, 1, nc - 1 - j, 0, 0))
    return pl.pallas_call(
        _rwkv_sweep_kernel,
        grid=(b, nc),
        in_specs=[tokf(), tokf(), opf(), opf(), tokb(), tokb(), opb(), opb(),
                  pl.BlockSpec((1, 2, hd, c), lambda bi, j: (bi, 0, 0, 0))],
        out_specs=[pl.BlockSpec((1, CHUNK, c), lambda bi, j: (bi, j, 0)),
                   pl.BlockSpec((1, CHUNK, c), lambda bi, j: (bi, nc - 1 - j, 0)),
                   pl.BlockSpec((1, 2, hd, c), lambda bi, j: (bi, 0, 0, 0))],
        out_shape=[jax.ShapeDtypeStruct((b, l, c), BF16), jax.ShapeDtypeStruct((b, l, c), BF16),
                   jax.ShapeDtypeStruct((b, 2, hd, c), F32)],
        scratch_shapes=[pltpu.VMEM((2, hd, c), F32)],
        compiler_params=_cparams("parallel", "arbitrary"),
        name="rwkv_sweep",
    )(q, y0, m, n, q, y0, m, n, s0)


def _rwkv_scans(px_r, pc_r, rparams):
    b = px_r.shape[0]
    qc, y0c, mc, nc_, _, _ = _rwkv_chunk_ops(pc_r, rparams)
    s_zero = jnp.zeros((b, 2, RWKV_HEAD_DIM, RWKV_DIM), F32)
    _, _, s_ctx = _rwkv_sweep(qc, y0c, mc, nc_, s_zero)
    qx, y0x, mx, nx, bonus, gate = _rwkv_chunk_ops(px_r, rparams)
    yf, yb, _ = _rwkv_sweep(qx, y0x, mx, nx, s_ctx)
    return (yf, yb), bonus, gate


def _diff_prep_kernel(p_ref, cos_ref, sin_ref, qg_ref, kg_ref, q_ref, k_ref, v_ref, *, rope):
    lane = lax.broadcasted_iota(jnp.int32, (1, LANES), 1)
    first = (lane % 32) < 16
    for hd in range(DIFF_HEADS):
        for off, g_ref, o_ref, scale in ((0, qg_ref, q_ref, DIFF_SCALE * math.log2(math.e)),
                                         (DIFF_QK_COLS, kg_ref, k_ref, 1.0)):
            cs = slice(hd * LANES, (hd + 1) * LANES)
            xb = p_ref[0, :, off + hd * LANES:off + (hd + 1) * LANES].astype(F32)
            ms = _group_sum(xb * xb, DIFF_QK_DIM) * (1.0 / DIFF_QK_DIM)
            y = xb * lax.rsqrt(ms + NORM_EPS) * g_ref[...]
            if rope:
                swapped = jnp.where(first, pltpu.roll(y, LANES - 16, axis=1), pltpu.roll(y, 16, axis=1))
                y = y * cos_ref[...] + swapped * sin_ref[...]
            o_ref[0, :, cs] = (y * scale).astype(BF16)
    ones = jnp.ones((p_ref.shape[1], V_EXT - DIFF_V_DIM), BF16)
    for hd in range(DIFF_HEADS):
        vb = p_ref[0, :, 2 * DIFF_QK_COLS + hd * DIFF_V_DIM:2 * DIFF_QK_COLS + (hd + 1) * DIFF_V_DIM]
        v_ref[0, :, hd * V_EXT:(hd + 1) * V_EXT] = jnp.concatenate([vb.astype(BF16), ones], axis=1)


def _diff_prep(p, cos_t, sin_t, qg, kg, rope):
    b, l, pc = p.shape
    t = _pick(l, (512, 256, 128))
    tok = lambda w=DIFF_DIM: pl.BlockSpec((1, t, w), lambda bi, i: (bi, i, 0))
    shp = jax.ShapeDtypeStruct((b, l, DIFF_DIM), BF16)
    shp_v = jax.ShapeDtypeStruct((b, l, DIFF_HEADS * V_EXT), BF16)
    return pl.pallas_call(
        functools.partial(_diff_prep_kernel, rope=rope),
        grid=(b, l // t),
        in_specs=[pl.BlockSpec((1, t, pc), lambda bi, i: (bi, i, 0)),
                  pl.BlockSpec((t, LANES), lambda bi, i: (i, 0)),
                  pl.BlockSpec((t, LANES), lambda bi, i: (i, 0)),
                  pl.BlockSpec((1, LANES), lambda bi, i: (0, 0)),
                  pl.BlockSpec((1, LANES), lambda bi, i: (0, 0))],
        out_specs=[tok(), tok(), tok(DIFF_HEADS * V_EXT)],
        out_shape=[shp, shp, shp_v],
        compiler_params=_cparams("parallel", "parallel"),
        name="diff_prep_rope" if rope else "diff_prep",
    )(p, cos_t, sin_t, qg, kg)


def _diff_finish(acc1, acc2, lam_ref, sg_ref, o_ref):
    dv = DIFF_V_DIM
    o = acc1[:, :dv] / acc1[:, dv:] - lam_ref[...] * (acc2[:, :dv] / acc2[:, dv:])
    o = o * lax.rsqrt(jnp.mean(o * o, axis=-1, keepdims=True) + SUBLN_EPS)
    o_ref[0] = (o * sg_ref[...]).astype(o_ref.dtype)


def _flash_online_kernel(lam_ref, sg_ref, q_ref, kc_ref, vc_ref, k_ref, v_ref, o_ref, *, tk):
    qd = DIFF_QK_DIM
    tq = q_ref.shape[1]
    q = q_ref[0]
    qs = (q[:, :qd], q[:, qd:])

    def absorb(state, k, v):
        reps = k.shape[0] // LANES
        out = []
        for mp in range(2):
            m_prev, acc = state[mp]
            s = lax.dot_general(qs[mp], k[:, mp * qd:(mp + 1) * qd], _NT, preferred_element_type=F32)
            m_new = jnp.maximum(m_prev, jnp.max(s, axis=-1, keepdims=True))
            alpha = jnp.exp2(m_prev - m_new)
            pr = jnp.exp2(s - jnp.concatenate([m_new] * reps, axis=1))
            acc_new = jnp.concatenate([alpha, alpha], axis=1) * acc + jnp.dot(pr.astype(BF16), v, preferred_element_type=F32)
            out.append((m_new, acc_new))
        return tuple(out)

    init = (jnp.full((tq, LANES), -jnp.inf, F32), jnp.zeros((tq, V_EXT), F32))
    state = absorb((init, init), kc_ref[0], vc_ref[0])

    def body(j, state):
        rows = pl.ds(pl.multiple_of(j * tk, tk), tk)
        return absorb(state, k_ref[0, rows, :], v_ref[0, rows, :])

    (_, acc1), (_, acc2) = lax.fori_loop(0, k_ref.shape[1] // tk, body, state)
    _diff_finish(acc1, acc2, lam_ref, sg_ref, o_ref)


def _flash_bounded_kernel(lam_ref, sg_ref, q_ref, kc_ref, vc_ref, k_ref, v_ref, o_ref, acc_ref, p_ref, *, tk):
    qd = DIFF_QK_DIM
    q = q_ref[0]
    qs = (q[:, :qd], q[:, qd:])
    n_kv = k_ref.shape[1] // tk

    def weights(mp, k):
        s = lax.dot_general(qs[mp], k[:, mp * qd:(mp + 1) * qd], _NT, preferred_element_type=F32)
        return jnp.exp2(s).astype(BF16)

    def chunk(c):
        return pl.ds(pl.multiple_of(jnp.minimum(c, n_kv - 1) * tk, tk), tk)

    for mp in range(2):
        acc_ref[mp] = jnp.dot(weights(mp, kc_ref[0]), vc_ref[0], preferred_element_type=F32)
        p_ref[0, mp] = weights(mp, k_ref[0, pl.ds(0, tk), :])

    steps = 4 if n_kv % 4 == 0 else 2

    def body(j, carry):
        for u in range(steps):
            c = steps * j + u
            v = v_ref[0, chunk(c), :]
            k_next = k_ref[0, chunk(c + 1), :]
            for mp in range(2):
                acc_ref[mp] += jnp.dot(p_ref[u % 2, mp], v, preferred_element_type=F32)
                p_ref[1 - u % 2, mp] = weights(mp, k_next)
        return carry

    lax.fori_loop(0, n_kv // steps, body, 0)
    _diff_finish(acc_ref[0], acc_ref[1], lam_ref, sg_ref, o_ref)


def _diff_attention(q, kc, vc, k, v, lam_vec, sg_vec, bounded):
    b, l, _ = q.shape
    lc = kc.shape[1]
    tq = _pick(l, (256, 128))
    tk = _pick(l, (512, 256, 128))
    if bounded:
        tq = _pick(l, (512, 256, 128))
        tk = _pick(l // 2, (512, 256, 128))
        assert l % (2 * tk) == 0
        body = functools.partial(_flash_bounded_kernel, tk=tk)
        scratch = [pltpu.VMEM((2, tq, V_EXT), F32), pltpu.VMEM((2, 2, tq, tk), BF16)]
    else:
        body = functools.partial(_flash_online_kernel, tk=tk)
        scratch = []
    return pl.pallas_call(
        body,
        grid=(b, DIFF_HEADS, l // tq),
        in_specs=[pl.BlockSpec((1, LANES), lambda bi, h, i: (0, 0)),
                  pl.BlockSpec((1, LANES), lambda bi, h, i: (0, 0)),
                  pl.BlockSpec((1, tq, LANES), lambda bi, h, i: (bi, i, h)),
                  pl.BlockSpec((1, lc, LANES), lambda bi, h, i: (bi, 0, h)),
                  pl.BlockSpec((1, lc, V_EXT), lambda bi, h, i: (bi, 0, h)),
                  pl.BlockSpec((1, l, LANES), lambda bi, h, i: (bi, 0, h)),
                  pl.BlockSpec((1, l, V_EXT), lambda bi, h, i: (bi, 0, h))],
        out_specs=pl.BlockSpec((1, tq, LANES), lambda bi, h, i: (bi, i, h)),
        out_shape=jax.ShapeDtypeStruct((b, l, DIFF_DIM), BF16),
        scratch_shapes=scratch,
        compiler_params=_cparams("parallel", "parallel", "arbitrary"),
        name="diff_flash_bounded" if bounded else "diff_flash_online",
    )(lam_vec, sg_vec, q, kc, vc, k, v)


def _merge_kernel(yf_ref, yb_ref, bonus_ref, gate_ref, yd_ref, pg_ref, lg_ref, lb_ref, wpa_ref, wpb_ref, o_ref):
    hd = RWKV_HEAD_DIM
    y = yf_ref[0].astype(F32) + yb_ref[0].astype(F32)
    dev = y - _group_sum(y, hd) * (1.0 / hd)
    var = _group_sum(dev * dev, hd) * (1.0 / hd)
    yn = dev * lax.rsqrt(var + LNX_EPS) * lg_ref[...] + lb_ref[...]
    y_rwkv = (yn + bonus_ref[0].astype(F32)) * gate_ref[0].astype(F32)
    a = _dot(y_rwkv, wpa_ref[...])
    bb = jnp.dot(yd_ref[0], wpb_ref[...], preferred_element_type=F32)
    ga = _sigmoid(pg_ref[0, :, :D_MODEL].astype(F32))
    gb = _sigmoid(pg_ref[0, :, D_MODEL:].astype(F32))
    o_ref[0] = (ga * a + gb * bb).astype(BF16)


def _merge(y_sweep, bonus, gate, y_diff, p_gate, lnx_g, lnx_b, w_pa, w_pb):
    b, l, c = bonus.shape
    d = D_MODEL
    tm = _pick(l, (256, 128))
    const = lambda shape: pl.BlockSpec(shape, lambda bi, i: (0,) * len(shape))
    tok = lambda w: pl.BlockSpec((1, tm, w), lambda bi, i: (bi, i, 0))
    return pl.pallas_call(
        _merge_kernel,
        grid=(b, l // tm),
        in_specs=[tok(c), tok(c), tok(c), tok(c), tok(DIFF_DIM), tok(GATE_COLS),
                  const((1, c)), const((1, c)), const((c, d)), const((DIFF_DIM, d))],
        out_specs=tok(d),
        out_shape=jax.ShapeDtypeStruct((b, l, d), BF16),
        compiler_params=_cparams("parallel", "parallel"),
        name="merge",
    )(y_sweep[0], y_sweep[1], bonus, gate, y_diff, p_gate, lnx_g, lnx_b, w_pa, w_pb)


def _outproj_kernel(mx_ref, x_ref, gt_ref, g_ref, sc_ref, sh_ref, wo_ref, wr_ref, br_ref, xn_ref, h_ref, lg_ref):
    mix = jnp.dot(mx_ref[0], wo_ref[...], preferred_element_type=F32)
    xn = x_ref[0] + gt_ref[0] * mix
    xn_ref[0] = xn
    y = xn * lax.rsqrt(jnp.mean(xn * xn, axis=-1, keepdims=True) + NORM_EPS) * g_ref[...]
    h = y * (1.0 + sc_ref[0]) + sh_ref[0]
    h_hi, h_lo = _bf16_parts(h, 2)
    h_ref[0] = h_hi
    dot = lambda a, b: jnp.dot(a, b, preferred_element_type=F32)
    lg_ref[0] = dot(h_hi, wr_ref[0]) + (dot(h_lo, wr_ref[0]) + dot(h_hi, wr_ref[1])) + br_ref[...]


def _outproj(mixed, x, gt1, g2, sc2, sh2, w_out, w_router, b_router):
    b, l, d = x.shape
    tm = _pick(l, (512, 256, 128))
    const = lambda shape: pl.BlockSpec(shape, lambda bi, i: (0,) * len(shape))
    tok = lambda w: pl.BlockSpec((1, tm, w), lambda bi, i: (bi, i, 0))
    per_b = lambda: pl.BlockSpec((1, 1, d), lambda bi, i: (bi, 0, 0))
    return pl.pallas_call(
        _outproj_kernel,
        grid=(b, l // tm),
        in_specs=[tok(d), tok(d), per_b(), const((1, d)), per_b(), per_b(),
                  const((d, d)), const((2, d, ROUTER_PAD)), const((1, ROUTER_PAD))],
        out_specs=[tok(d), tok(d), tok(ROUTER_PAD)],
        out_shape=[jax.ShapeDtypeStruct((b, l, d), F32), jax.ShapeDtypeStruct((b, l, d), BF16),
                   jax.ShapeDtypeStruct((b, l, ROUTER_PAD), F32)],
        compiler_params=_cparams("parallel", "parallel"),
        name="outproj_router",
    )(mixed, x, gt1, g2, sc2, sh2, w_out, w_router, b_router)


def _moe_kernel(wb_ref, we_ref, lo_ref, hi_ref, x_ref, sw_ref, w1_ref, w3_ref, w2_ref, o_ref,
                w1b_ref, w3b_ref, w2b_ref, cached_ref):
    i = pl.program_id(0)
    lo, hi = lo_ref[i], hi_ref[i]
    live = hi > lo

    @pl.when(i == 0)
    def _():
        cached_ref[0] = -1

    @pl.when((i == 0) | (wb_ref[i] != wb_ref[jnp.maximum(i - 1, 0)]))
    def _():
        o_ref[...] = jnp.zeros(o_ref.shape, o_ref.dtype)

    @pl.when(live & (cached_ref[0] != we_ref[i]))
    def _():
        w1b_ref[...] = w1_ref[0].astype(BF16)
        w3b_ref[...] = w3_ref[0].astype(BF16)
        w2b_ref[...] = w2_ref[0].astype(BF16)
        cached_ref[0] = we_ref[i]

    @pl.when(live)
    def _():
        xb = x_ref[...]
        u = jnp.dot(xb, w1b_ref[...], preferred_element_type=F32)
        g = jnp.dot(xb, w3b_ref[...], preferred_element_type=F32)
        hmid = (u * _sigmoid(u) * g).astype(BF16)
        res = (jnp.dot(hmid, w2b_ref[...], preferred_element_type=F32) * sw_ref[...]).astype(o_ref.dtype)
        row = lax.broadcasted_iota(jnp.int32, (o_ref.shape[0], 1), 0)
        o_ref[...] = jnp.where((row >= lo) & (row < hi), res, o_ref[...])


def _moe_ffn(xs, sw, items, w1, w3, w2):
    n_rows, d = xs.shape
    wb, we, lo, hi = items
    grid_spec = pltpu.PrefetchScalarGridSpec(
        num_scalar_prefetch=4,
        grid=(wb.shape[0],),
        in_specs=[pl.BlockSpec((MOE_TILE, d), lambda i, wb, we, lo, hi: (wb[i], 0)),
                  pl.BlockSpec((MOE_TILE, 1), lambda i, wb, we, lo, hi: (wb[i], 0)),
                  pl.BlockSpec((1, d, D_EXPERT), lambda i, wb, we, lo, hi: (we[i], 0, 0)),
                  pl.BlockSpec((1, d, D_EXPERT), lambda i, wb, we, lo, hi: (we[i], 0, 0)),
                  pl.BlockSpec((1, D_EXPERT, d), lambda i, wb, we, lo, hi: (we[i], 0, 0))],
        out_specs=pl.BlockSpec((MOE_TILE, d), lambda i, wb, we, lo, hi: (wb[i], 0)),
        scratch_shapes=[pltpu.VMEM((d, D_EXPERT), BF16), pltpu.VMEM((d, D_EXPERT), BF16),
                        pltpu.VMEM((D_EXPERT, d), BF16), pltpu.SMEM((1,), jnp.int32)],
    )
    return pl.pallas_call(
        _moe_kernel,
        grid_spec=grid_spec,
        out_shape=jax.ShapeDtypeStruct((n_rows, d), BF16),
        compiler_params=_cparams("arbitrary"),
        name="moe_ffn",
    )(wb, we, lo, hi, xs, sw, w1, w3, w2)


def _final_kernel(x_ref, gt_ref, y0_ref, y1_ref, o_ref):
    o_ref[0] = x_ref[0] + gt_ref[0] * (y0_ref[0].astype(F32) + y1_ref[0].astype(F32))


def _final(x_new, gt2, y0, y1):
    b, l, d = x_new.shape
    tm = _pick(l, (512, 256, 128))
    tok = lambda: pl.BlockSpec((1, tm, d), lambda bi, i: (bi, i, 0))
    return pl.pallas_call(
        _final_kernel,
        grid=(b, l // tm),
        in_specs=[tok(), pl.BlockSpec((1, 1, d), lambda bi, i: (bi, 0, 0)), tok(), tok()],
        out_specs=tok(),
        out_shape=jax.ShapeDtypeStruct((b, l, d), F32),
        compiler_params=_cparams("parallel", "parallel"),
        name="moe_residual",
    )(x_new, gt2, y0, y1)


def _pad_lora_cols(w, widths):
    parts, o = [], 0
    for wd in widths:
        blk = w[..., o:o + wd]
        parts.append(jnp.pad(blk, [(0, 0)] * (w.ndim - 1) + [(0, LORA_PAD - wd)]))
        o += wd
    return jnp.concatenate(parts, axis=-1)


def _rope_tables(l):
    half = DIFF_QK_DIM // 2
    inv_freq = ROPE_THETA ** (-jnp.arange(0, half, 2, dtype=F32) / half)
    t = jnp.arange(l, dtype=jnp.int32)
    rows = (t // GRID_W).astype(F32)[:, None] * inv_freq
    cols = (t % GRID_W).astype(F32)[:, None] * inv_freq
    cos64 = jnp.concatenate([jnp.cos(rows), jnp.cos(rows), jnp.cos(cols), jnp.cos(cols)], axis=1)
    sin64 = jnp.concatenate([-jnp.sin(rows), jnp.sin(rows), -jnp.sin(cols), jnp.sin(cols)], axis=1)
    return jnp.tile(cos64, (1, 2)), jnp.tile(sin64, (1, 2))


def _route(logits, n_tok):
    g_logits = logits[:, :N_GROUPS]
    e_logits = logits[:, N_GROUPS:N_GROUPS + N_EXPERTS].reshape(n_tok, N_GROUPS, EXPERTS_PER_GROUP)
    p_group = jax.nn.softmax(g_logits, axis=-1)
    g_top = jnp.argmax(g_logits, axis=-1)
    p_g = jnp.take_along_axis(p_group, g_top[:, None], axis=1)[:, 0]
    e_sel = jnp.take_along_axis(e_logits, g_top[:, None, None], axis=1)[:, 0]
    top_p, top_i = lax.top_k(jax.nn.softmax(e_sel, axis=-1), TOP_K)
    top_p = top_p / jnp.sum(top_p, axis=-1, keepdims=True)
    gate = p_g[:, None] * top_p
    expert = g_top[:, None] * EXPERTS_PER_GROUP + top_i

    n_assign = n_tok * TOP_K
    assert n_assign % MOE_TILE == 0
    flat_e = expert.reshape(-1).astype(jnp.int32)
    ids = jnp.arange(n_assign, dtype=jnp.int32)
    sorted_e, order, sorted_w = lax.sort((flat_e, ids, gate.reshape(-1)), num_keys=1, is_stable=True)
    _, rank = lax.sort((order, ids), num_keys=1)
    ends = jnp.searchsorted(sorted_e, jnp.arange(N_EXPERTS, dtype=jnp.int32), side='right').astype(jnp.int32)
    starts = jnp.concatenate([jnp.zeros((1,), jnp.int32), ends[:-1]])

    nb = n_assign // MOE_TILE
    blk_lo = jnp.arange(nb, dtype=jnp.int32) * MOE_TILE
    e_first = jnp.searchsorted(ends, blk_lo, side='right').astype(jnp.int32)
    e_last = jnp.searchsorted(ends, blk_lo + MOE_TILE - 1, side='right').astype(jnp.int32)
    per_blk = e_last - e_first + 1
    cum = jnp.cumsum(per_blk)
    it = jnp.arange(nb + N_EXPERTS - 1, dtype=jnp.int32)
    wb = jnp.minimum(jnp.searchsorted(cum, it, side='right'), nb - 1).astype(jnp.int32)
    we = jnp.clip(e_first[wb] + it - (cum[wb] - per_blk[wb]), 0, N_EXPERTS - 1)
    lo = jnp.clip(starts[we], blk_lo[wb], blk_lo[wb] + MOE_TILE) - blk_lo[wb]
    hi = jnp.clip(ends[we], blk_lo[wb], blk_lo[wb] + MOE_TILE) - blk_lo[wb]
    hi = jnp.where(it < cum[-1], hi, lo)
    return order // TOP_K, sorted_w, rank.reshape(n_tok, TOP_K), (wb, we, lo, hi)


def kernel(x, c, ctx, c_ctx, ada_w, ada_b, norm1_g, norm2_g, w_in, shift_mu, rwkv_w0, rwkv_w2, rwkv_a0, rwkv_a2,
           rwkv_g2, rwkv_k_k, rwkv_k_a, rwkv_r_k, rwkv_lnx_g, rwkv_lnx_b, qn_g, kn_g, diff_lambda, subln_g,
           w_pa, w_pb, w_out, router_g_w, router_g_b, router_e_w, router_e_b, exp_w1, exp_w3, exp_w2):
    assert ada_w.shape[0] == 1, "single-layer block"
    b, l, d = x.shape
    lc = ctx.shape[1]
    lam_init = 0.8 - 0.6 * math.exp(-0.3 * 0)
    lv = diff_lambda[0].astype(F32)
    lam = jnp.exp(jnp.sum(lv[0] * lv[1])) - jnp.exp(jnp.sum(lv[2] * lv[3])) + lam_init

    rows = (b + 1 + SUBLANES - 1) // SUBLANES * SUBLANES
    cm = jnp.zeros((rows, d), F32).at[:b].set(c).at[b].set(c_ctx)
    mod = _modulation(cm, ada_w[0], ada_b[0])
    sh1, sc1, gt1, sh2, sc2, gt2 = [mod[:b, None, k * d:(k + 1) * d] for k in range(6)]
    csh1, csc1 = [jnp.broadcast_to(mod[b, k * d:(k + 1) * d], (b, 1, d)) for k in range(2)]

    w = w_in[0]
    lora_widths = (DECAY_LORA, DECAY_LORA, AAA_LORA, AAA_LORA)
    o_lora = 3 * RWKV_DIM
    o_glora = o_lora + sum(lora_widths)
    pad_cols = lambda m: jnp.concatenate(
        [m[..., :o_lora], _pad_lora_cols(m[..., o_lora:o_glora], lora_widths), m[..., o_glora:RWKV_COLS]], axis=-1)
    w_rwkv = pad_cols(w).astype(BF16)
    w_diff = w[:, RWKV_COLS:RWKV_COLS + DIFF_COLS].astype(BF16)
    w_gate = w[:, RWKV_COLS + DIFF_COLS:].astype(BF16)
    g1 = norm1_g[0]
    hx = _norm_mod(x, g1, sc1, sh1)
    hc = _norm_mod(ctx, g1, csc1, csh1)
    px_r, px_d, px_g = _proj(hx, w_rwkv, F32), _proj(hx, w_diff, BF16), _proj(hx, w_gate, BF16)
    pc_r, pc_d = _proj(hc, w_rwkv, F32), _proj(hc, w_diff, BF16)

    pad_rows = lambda m: jnp.pad(m, ((0, 0), (0, LORA_PAD - m.shape[1]), (0, 0)))
    rparams = (pad_cols(shift_mu[0])[None], rwkv_w0[0], pad_rows(rwkv_w2[0]), rwkv_a0[0], pad_rows(rwkv_a2[0]),
               rwkv_g2[0], rwkv_k_k[0][None], rwkv_k_a[0][None], rwkv_r_k[0].reshape(1, RWKV_DIM))
    y_sweep, bonus, gate = _rwkv_scans(px_r, pc_r, rparams)

    cos_t, sin_t = _rope_tables(l)
    qg = jnp.tile(qn_g[0], 2)[None]
    kg = jnp.tile(kn_g[0], 2)[None]
    q_x, k_x, v_x = _diff_prep(px_d, cos_t, sin_t, qg, kg, True)
    _, k_c, v_c = _diff_prep(pc_d, cos_t[:lc], sin_t[:lc], qg, kg, False)
    lam_vec = jnp.full((1, LANES), lam, F32)
    sg_vec = (subln_g[0] * (1.0 - lam_init))[None]
    score_bound = (1.05 * DIFF_QK_DIM * DIFF_SCALE * math.log2(math.e)
                   * jnp.max(jnp.abs(qn_g[0])) * jnp.max(jnp.abs(kn_g[0])))
    attn_args = (q_x, k_c, v_c, k_x, v_x, lam_vec, sg_vec)
    y_diff = lax.cond(score_bound <= SCORE_LOG2_LIMIT,
                      lambda a: _diff_attention(*a, bounded=True),
                      lambda a: _diff_attention(*a, bounded=False), attn_args)

    mixed = _merge(y_sweep, bonus, gate, y_diff, px_g, rwkv_lnx_g[0][None], rwkv_lnx_b[0][None],
                   w_pa[0].astype(BF16), w_pb[0].astype(BF16))
    n_r = N_GROUPS + N_EXPERTS
    w_router = jnp.zeros((d, ROUTER_PAD), F32).at[:, :N_GROUPS].set(router_g_w[0]).at[:, N_GROUPS:n_r].set(router_e_w[0])
    b_router = jnp.zeros((1, ROUTER_PAD), F32).at[0, :N_GROUPS].set(router_g_b[0]).at[0, N_GROUPS:n_r].set(router_e_b[0])
    w_router_hi = w_router.astype(BF16)
    w_router_lo = (w_router - w_router_hi.astype(F32)).astype(BF16)
    x_new, h2, logits = _outproj(mixed, x, gt1, norm2_g[0][None], sc2, sh2, w_out[0].astype(BF16),
                                 jnp.stack([w_router_hi, w_router_lo]), b_router)

    n_tok = b * l
    row_tok, row_w, row_of, items = _route(logits.reshape(n_tok, ROUTER_PAD), n_tok)
    xs = h2.reshape(n_tok, d)[row_tok]
    out = _moe_ffn(xs, row_w[:, None], items, exp_w1[0], exp_w3[0], exp_w2[0])
    y0 = out[row_of[:, 0]].reshape(b, l, d)
    y1 = out[row_of[:, 1]].reshape(b, l, d)
    return _final(x_new, gt2, y0, y1)
```

```python
import functools
import math

import jax
import jax.numpy as jnp
from jax import lax
from jax.experimental import pallas as pl
from jax.experimental.pallas import tpu as pltpu

F32 = jnp.float32
BF16 = jnp.bfloat16
HIGHEST = lax.Precision.HIGHEST

D_MODEL = 2048
GRID_W = 64
RWKV_HEADS = 16
RWKV_HEAD_DIM = 64
RWKV_DIM = RWKV_HEADS * RWKV_HEAD_DIM
DECAY_LORA = 96
AAA_LORA = 96
GATE_LORA = 256
RWKV_COLS = 3 * RWKV_DIM + 2 * DECAY_LORA + 2 * AAA_LORA + GATE_LORA
DIFF_HEADS = 8
DIFF_QK_DIM = 64
DIFF_V_DIM = 2 * DIFF_QK_DIM
DIFF_DIM = DIFF_HEADS * DIFF_V_DIM
DIFF_QK_COLS = DIFF_HEADS * 2 * DIFF_QK_DIM
DIFF_COLS = 2 * DIFF_QK_COLS + DIFF_DIM
DIFF_SCALE = DIFF_QK_DIM ** -0.5
ROPE_THETA = 10000.0
GATE_COLS = 2 * D_MODEL
N_GROUPS = 4
EXPERTS_PER_GROUP = 8
N_EXPERTS = N_GROUPS * EXPERTS_PER_GROUP
TOP_K = 2
D_EXPERT = 512
NORM_EPS = 1e-6
SUBLN_EPS = 1e-5
LNX_EPS = 64e-5

LANES = 128
SUBLANES = 8
VMEM_LIMIT_BYTES = 56 * 1024 * 1024

LORA_PAD = LANES
RWKV_PCOLS = 3 * RWKV_DIM + 4 * LORA_PAD + GATE_LORA
CHUNK = 64
V_EXT =2 * DIFF_V_DIM
SCORE_LOG2_LIMIT = 60.0
ROUTER_PAD = LANES
MOE_TILE = 512


def _cparams(*sem):
    return pltpu.CompilerParams(dimension_semantics=sem, vmem_limit_bytes=VMEM_LIMIT_BYTES)


def _sigmoid(x):
    return 1.0 / (1.0 + jnp.exp(-x))


def _dot(a, b, dims=(((1,), (0,)), ((), ()))):
    return lax.dot_general(a.astype(BF16), b.astype(BF16), dims, preferred_element_type=F32)


def _dot_f32(a, b, dims=(((1,), (0,)), ((), ()))):
    return lax.dot_general(a, b, dims, precision=HIGHEST, preferred_element_type=F32)


_NT = (((1,), (1,)), ((), ()))
_TN = (((0,), (0,)), ((), ()))


def _bf16_parts(x, n):
    parts = []
    for _ in range(n):
        p = x.astype(BF16)
        parts.append(p)
        x = x - p.astype(F32)
    return parts


def _dot_split(x, w_b, n, lhs=True):
    parts = _bf16_parts(x, n)
    outs = [jnp.dot(p, w_b, preferred_element_type=F32) if lhs else jnp.dot(w_b, p, preferred_element_type=F32)
            for p in parts]
    return functools.reduce(lambda a, b: a + b, outs)


def _group_ones(width, group):
    r = lax.broadcasted_iota(jnp.int32, (width, width), 0) // group
    c = lax.broadcasted_iota(jnp.int32, (width, width), 1) // group
    return (r == c).astype(BF16)


def _group_sum(x, group, pieces=2):
    ones = _group_ones(LANES, group)
    parts = [_dot_split(x[:, j * LANES:(j + 1) * LANES], ones, pieces) for j in range(x.shape[1] // LANES)]
    return parts[0] if len(parts) == 1 else jnp.concatenate(parts, axis=1)


def _mod_kernel(c_ref, w_ref, b_ref, o_ref):
    c = c_ref[...]
    o_ref[...] = _dot_f32(c * _sigmoid(c), w_ref[...]) + b_ref[...]


def _modulation(cm, ada_w, ada_b):
    rows, d = cm.shape
    n = ada_w.shape[1]
    tn = 1536
    return pl.pallas_call(
        _mod_kernel,
        grid=(n // tn,),
        in_specs=[pl.BlockSpec((rows, d), lambda j: (0, 0)),
                  pl.BlockSpec((d, tn), lambda j: (0, j)),
                  pl.BlockSpec((1, tn), lambda j: (0, j))],
        out_specs=pl.BlockSpec((rows, tn), lambda j: (0, j)),
        out_shape=jax.ShapeDtypeStruct((rows, n), F32),
        compiler_params=_cparams("parallel"),
        name="modulation",
    )(cm, ada_w, ada_b.reshape(1, n))


def _norm_mod_kernel(x_ref, g_ref, sc_ref, sh_ref, h_ref):
    x = x_ref[0]
    y = x * lax.rsqrt(jnp.mean(x * x, axis=-1, keepdims=True) + NORM_EPS) * g_ref[...]
    h_ref[0] = (y * (1.0 + sc_ref[0]) + sh_ref[0]).astype(BF16)


def _pick(n, prefs):
    for t in prefs:
        if n % t == 0:
            return t
    return n


def _norm_mod(x, g, sc, sh):
    b, l, d = x.shape
    tm = _pick(l, (512, 256, 128))
    return pl.pallas_call(
        _norm_mod_kernel,
        grid=(b, l // tm),
        in_specs=[pl.BlockSpec((1, tm, d), lambda bi, i: (bi, i, 0)),
                  pl.BlockSpec((1, d), lambda bi, i: (0, 0)),
                  pl.BlockSpec((1, 1, d), lambda bi, i: (bi, 0, 0)),
                  pl.BlockSpec((1, 1, d), lambda bi, i: (bi, 0, 0))],
        out_specs=pl.BlockSpec((1, tm, d), lambda bi, i: (bi, i, 0)),
        out_shape=jax.ShapeDtypeStruct((b, l, d), BF16),
        compiler_params=_cparams("parallel", "parallel"),
        name="norm_mod",
    )(x, g.reshape(1, d), sc, sh)


def _proj_kernel(h_ref, w_ref, o_ref):
    o_ref[0] = jnp.dot(h_ref[0], w_ref[...], preferred_element_type=F32).astype(o_ref.dtype)


def _proj(h, w, out_dtype):
    b, l, d = h.shape
    n = w.shape[1]
    tm = _pick(l, (1024, 512, 256, 128))
    tn = _pick(n, (1024, 768, 512, 256, 128))
    return pl.pallas_call(
        _proj_kernel,
        grid=(b, l // tm, n // tn),
        in_specs=[pl.BlockSpec((1, tm, d), lambda bi, i, j: (bi, i, 0)),
                  pl.BlockSpec((d, tn), lambda bi, i, j: (0, j))],
        out_specs=pl.BlockSpec((1, tm, tn), lambda bi, i, j: (bi, i, j)),
        out_shape=jax.ShapeDtypeStruct((b, l, n), out_dtype),
        compiler_params=_cparams("parallel", "parallel", "parallel"),
        name="proj",
    )(h, w)


def _pair_diag(x_b, mask_b):
    return jnp.concatenate([x_b, x_b], axis=0) * mask_b


def _tri_inverse_pairs(a_list, eye_f, mask_b):
    n = CHUNK
    mm = lambda l, r: jnp.dot(l.astype(BF16), _pair_diag(r.astype(BF16), mask_b), preferred_element_type=F32)
    xs = [eye_f + a for a in a_list]
    ps = [mm(a, a) for a in a_list]
    steps = int(math.log2(n)) - 1
    for s in range(steps):
        if s < steps - 1:
            xps = [mm(jnp.concatenate([x, p], axis=0), p) for x, p in zip(xs, ps)]
            xs = [x + xp[:n] for x, xp in zip(xs, xps)]
            ps = [xp[n:] for xp in xps]
        else:
            xs = [x + mm(x, p) for x, p in zip(xs, ps)]
    return xs


def _rwkv_chunk_kernel(p_ref, pp_ref, pn_ref, mu_ref, w0_ref, w2_ref, a0_ref, a2_ref, g2_ref, kk_ref, ka_ref,
                       rk_ref, q_ref, y0_ref, m_ref, n_ref, bonus_ref, gate_ref):
    i = pl.program_id(1)
    last = pl.num_programs(1) - 1
    c = RWKV_DIM
    hd = RWKV_HEAD_DIM
    p = p_ref[0]
    row = lax.broadcasted_iota(jnp.int32, (CHUNK, 1), 0)
    prev_row = jnp.where(i == 0, 0.0, pp_ref[0, SUBLANES - 1:SUBLANES, :])
    next_row = jnp.where(i == last, 0.0, pn_ref[0, 0:1, :])
    prev = jnp.where(row == 0, prev_row, pltpu.roll(p, 1, axis=0))
    nxt = jnp.where(row == CHUNK - 1, next_row, pltpu.roll(p, CHUNK - 1, axis=0))
    ps = p + mu_ref[...] * (0.5 * (prev + nxt) - p)

    r, k, v = ps[:, :c], ps[:, c:2 * c], ps[:, 2 * c:3 * c]
    o = 3 * c
    xw = (ps[:, o:o + LORA_PAD], ps[:, o + LORA_PAD:o + 2 * LORA_PAD])
    xa = (ps[:, o + 2 * LORA_PAD:o + 3 * LORA_PAD], ps[:, o + 3 * LORA_PAD:o + 4 * LORA_PAD])
    xg = ps[:, o + 4 * LORA_PAD:]

    gate_ref[0] = _dot(_sigmoid(xg), g2_ref[...]).astype(gate_ref.dtype)
    kk = k * kk_ref[...]
    kk = kk * lax.rsqrt(_group_sum(kk * kk, hd, pieces=1) + 1e-12)

    tr = lax.broadcasted_iota(jnp.int32, (CHUNK, CHUNK), 0)
    tc = lax.broadcasted_iota(jnp.int32, (CHUNK, CHUNK), 1)
    tr2 = lax.broadcasted_iota(jnp.int32, (CHUNK, LANES), 0)
    lane2 = lax.broadcasted_iota(jnp.int32, (CHUNK, LANES), 1)
    tc2 = lane2 % CHUNK
    lane_head = lane2 // hd
    eye2 = tr2 == tc2
    mask_b = _group_ones(LANES, hd)
    kd_sum = jnp.zeros_like(k)
    v_b = v.astype(BF16)
    psl = [slice(p * LANES, (p + 1) * LANES) for p in range(c // LANES)]
    before2, upto2, prep = [], [], []
    for d in range(2):
        before2.append((tc2 < tr2) if d == 0 else (tc2 > tr2))
        upto = (tc <= tr) if d == 0 else (tc >= tr)
        upto2.append((tc2 <= tr2) if d == 0 else (tc2 >= tr2))
        z = w0_ref[d:d + 1, :] + _dot(jnp.tanh(xw[d]), w2_ref[d])
        w_log = -(jnp.maximum(-z, 0.0) + jnp.log(1.0 + jnp.exp(-jnp.abs(z)))) - 0.5
        logw = -jnp.exp(w_log)
        a = _sigmoid(a0_ref[d:d + 1, :] + _dot(xa[d], a2_ref[d]))
        kd = k * (1.0 + (a - 1.0) * ka_ref[...])
        kd_sum = kd_sum + kd
        cum = _dot_split(logw, upto.astype(BF16), 3, lhs=False)
        total = cum[CHUNK - 1:CHUNK, :] if d == 0 else cum[0:1, :]
        e_neg = jnp.exp(-cum)
        e_rest = jnp.exp(total - cum)
        p_total = jnp.exp(total)
        beta = kk * a
        al = -kk * jnp.exp(cum - logw)
        rt = r * jnp.exp(cum)
        al_b = al.astype(BF16)
        rt_b = rt.astype(BF16)
        bt_b = (beta * e_neg).astype(BF16)
        kt_b = (kd * e_neg).astype(BF16)
        bh_b = (beta * e_rest).astype(BF16)
        kh_b = (kd * e_rest).astype(BF16)
        prep.append((al_b, rt_b, bt_b, kt_b, bh_b, kh_b, rt, p_total))

    dp = [(d, s) for d in range(2) for s in psl]
    fdot = lambda a, b: jnp.dot(a, b, preferred_element_type=F32)
    bd = lambda x: _pair_diag(x, mask_b)
    v_bd = [bd(v_b[:, s]) for s in psl] * 2
    zeros_sq = jnp.zeros((LANES, LANES), BF16)
    zeros_tl = jnp.zeros((CHUNK, LANES), BF16)
    gm = [lax.dot_general(jnp.concatenate([prep[d][0][:, s], prep[d][1][:, s]], axis=0),
                          jnp.concatenate([bd(prep[d][2][:, s]), bd(prep[d][3][:, s])], axis=0),
                          _NT, preferred_element_type=F32) for d, s in dp]
    a_ab = [jnp.where(before2[d], x[:CHUNK, :LANES], 0.0) for x, (d, s) in zip(gm, dp)]
    a_ak = [jnp.where(before2[d], x[:CHUNK, LANES:], 0.0).astype(BF16) for x, (d, s) in zip(gm, dp)]
    a_rb = [jnp.where(upto2[d], x[CHUNK:, :LANES], 0.0).astype(BF16) for x, (d, s) in zip(gm, dp)]
    a_rk = [jnp.where(upto2[d], x[CHUNK:, LANES:], 0.0).astype(BF16) for x, (d, s) in zip(gm, dp)]
    akv = [fdot(ak, vd) for ak, vd in zip(a_ak, v_bd)]
    t_inv = _tri_inverse_pairs(a_ab, eye2.astype(F32), mask_b)
    wu = [fdot(t.astype(BF16), jnp.concatenate([bd(prep[d][0][:, s]), bd(u.astype(BF16))], axis=1))
          for t, u, (d, s) in zip(t_inv, akv, dp)]
    w_b = [x[:, :LANES].astype(BF16) for x in wu]
    u_b = [x[:, LANES:].astype(BF16) for x in wu]
    qy = [fdot(jnp.concatenate([rb, rk], axis=1),
               jnp.concatenate([jnp.concatenate([bd(w), bd(u)], axis=1),
                                jnp.concatenate([zeros_sq, vd], axis=1)], axis=0))
          for rb, rk, w, u, vd in zip(a_rb, a_rk, w_b, u_b, v_bd)]
    full = [lax.dot_general(jnp.concatenate([jnp.concatenate([w, u], axis=1),
                                             jnp.concatenate([zeros_tl, v_b[:, s]], axis=1)], axis=0),
                            jnp.concatenate([prep[d][4][:, s], prep[d][5][:, s]], axis=0),
                            _TN, preferred_element_type=F32)
            for w, u, (d, s) in zip(w_b, u_b, dp)]

    def diag_blocks(x):
        return jnp.where(lane_head == 0, x[:hd], 0.0) + jnp.where(lane_head == 1, x[hd:2 * hd], 0.0)

    npair = len(psl)
    for d in range(2):
        sl = slice(d * npair, (d + 1) * npair)
        rt, p_total = prep[d][6], prep[d][7]
        q_ref[0, d] = (rt + jnp.concatenate([x[:, :LANES] for x in qy[sl]], axis=1)).astype(q_ref.dtype)
        y0_ref[0, d] = jnp.concatenate([x[:, LANES:] for x in qy[sl]], axis=1).astype(y0_ref.dtype)
        m_ref[0, d, 0] = jnp.concatenate([jnp.where(eye2, p_total[:, s], 0.0) + diag_blocks(x[:LANES])
                                          for x, s in zip(full[sl], psl)], axis=1).astype(m_ref.dtype)
        n_ref[0, d, 0] = jnp.concatenate([diag_blocks(x[LANES:]) for x in full[sl]], axis=1)
    bonus_ref[0] = (_group_sum(r * kd_sum * rk_ref[...], hd) * v).astype(bonus_ref.dtype)


def _rwkv_chunk_ops(p, params):
    b, l, pc = p.shape
    nc = l // CHUNK
    c = RWKV_DIM
    hb = CHUNK // SUBLANES
    nb8 = l // SUBLANES
    mu, w0, w2, a0, a2, g2, k_k, k_a, r_k = params
    const = lambda shape: pl.BlockSpec(shape, lambda bi, i: (0,) * len(shape))
    tok = lambda: pl.BlockSpec((1, CHUNK, c), lambda bi, i: (bi, i, 0))
    return pl.pallas_call(
        _rwkv_chunk_kernel,
        grid=(b, nc),
        in_specs=[pl.BlockSpec((1, CHUNK, pc), lambda bi, i: (bi, i, 0)),
                  pl.BlockSpec((1, SUBLANES, pc), lambda bi, i: (bi, jnp.maximum(i * hb - 1, 0), 0)),
                  pl.BlockSpec((1, SUBLANES, pc), lambda bi, i: (bi, jnp.minimum((i + 1) * hb, nb8 - 1), 0)),
                  const((1, pc)), const((2, c)), const((2, LORA_PAD, c)), const((2, c)), const((2, LORA_PAD, c)),
                  const((GATE_LORA, c)), const((1, c)), const((1, c)), const((1, c))],
        out_specs=[pl.BlockSpec((1, 2, CHUNK, c), lambda bi, i: (bi, 0, i, 0)),
                   pl.BlockSpec((1, 2, CHUNK, c), lambda bi, i: (bi, 0, i, 0)),
                   pl.BlockSpec((1, 2, 1, RWKV_HEAD_DIM, c), lambda bi, i: (bi, 0, i, 0, 0)),
                   pl.BlockSpec((1, 2, 1, RWKV_HEAD_DIM, c), lambda bi, i: (bi, 0, i, 0, 0)),
                   tok(), tok()],
        out_shape=[jax.ShapeDtypeStruct((b, 2, l, c), BF16), jax.ShapeDtypeStruct((b, 2, l, c), BF16),
                   jax.ShapeDtypeStruct((b, 2, nc, RWKV_HEAD_DIM, c), BF16),
                   jax.ShapeDtypeStruct((b, 2, nc, RWKV_HEAD_DIM, c), F32),
                   jax.ShapeDtypeStruct((b, l, c), BF16), jax.ShapeDtypeStruct((b, l, c), BF16)],
        compiler_params=_cparams("parallel", "parallel"),
        name="rwkv_chunk_ops",
    )(p, p, p, mu, w0, w2, a0, a2, g2, k_k, k_a, r_k)


def _rwkv_sweep_kernel(qf_ref, y0f_ref, mf_ref, nf_ref, qb_ref, y0b_ref, mb_ref, nb_ref, s0_ref,
                       yf_ref, yb_ref, sfin_ref, s_ref):
    j = pl.program_id(1)
    hd = RWKV_HEAD_DIM

    @pl.when(j == 0)
    def _():
        s_ref[...] = s0_ref[0]

    dirs = ((qf_ref, y0f_ref, mf_ref, nf_ref, yf_ref), (qb_ref, y0b_ref, mb_ref, nb_ref, yb_ref))
    psl = [slice(p * LANES, (p + 1) * LANES) for p in range(RWKV_DIM // LANES)]
    mask_b = _group_ones(LANES, hd)
    pair_diag = lambda x: _pair_diag(x, mask_b)
    for d, (q_ref, y0_ref, m_ref, n_ref, y_ref) in enumerate(dirs):
        s_b = s_ref[d].astype(BF16)
        q_b = q_ref[0, 0]
        m_b = m_ref[0, 0, 0]
        ys = [_dot(q_b[:, s], pair_diag(s_b[:, s]), _NT) for s in psl]
        sm = [_dot(s_b[:, s], pair_diag(m_b[:, s])) for s in psl]
        y_ref[0] = (y0_ref[0, 0].astype(F32) + jnp.concatenate(ys, axis=1)).astype(y_ref.dtype)
        s_ref[d] = n_ref[0, 0, 0] + jnp.concatenate(sm, axis=1)

    @pl.when(j == pl.num_programs(1) - 1)
    def _():
        sfin_ref[0] = s_ref[...]


def _rwkv_sweep(q, y0, m, n, s0):
    b, _, l, c = q.shape
    nc = l // CHUNK
    hd = RWKV_HEAD_DIM
    tokf = lambda: pl.BlockSpec((1, 1, CHUNK, c), lambda bi, j: (bi, 0, j, 0))
    tokb = lambda: pl.BlockSpec((1, 1, CHUNK, c), lambda bi, j: (bi, 1, nc - 1 - j, 0))
    opf = lambda: pl.BlockSpec((1, 1, 1, hd, c), lambda bi, j: (bi, 0, j, 0, 0))
    opb = lambda: pl.BlockSpec((1, 1, 1, hd, c), lambda bi, j: (bi, 1, nc - 1 - j, 0, 0))
    return pl.pallas_call(
        _rwkv_sweep_kernel,
        grid=(b, nc),
        in_specs=[tokf(), tokf(), opf(), opf(), tokb(), tokb(), opb(), opb(),
                  pl.BlockSpec((1, 2, hd, c), lambda bi, j: (bi, 0, 0, 0))],
        out_specs=[pl.BlockSpec((1, CHUNK, c), lambda bi, j: (bi, j, 0)),
                   pl.BlockSpec((1, CHUNK, c), lambda bi, j: (bi, nc - 1 - j, 0)),
                   pl.BlockSpec((1, 2, hd, c), lambda bi, j: (bi, 0, 0, 0))],
        out_shape=[jax.ShapeDtypeStruct((b, l, c), BF16), jax.ShapeDtypeStruct((b, l, c), BF16),
                   jax.ShapeDtypeStruct((b, 2, hd, c), F32)],
        scratch_shapes=[pltpu.VMEM((2, hd, c), F32)],
        compiler_params=_cparams("parallel", "arbitrary"),
        name="rwkv_sweep",
    )(q, y0, m, n, q, y0, m, n, s0)


def _rwkv_scans(px_r, pc_r, rparams):
    b = px_r.shape[0]
    qc, y0c, mc, nc_, _, _ = _rwkv_chunk_ops(pc_r, rparams)
    s_zero = jnp.zeros((b, 2, RWKV_HEAD_DIM, RWKV_DIM), F32)
    _, _, s_ctx = _rwkv_sweep(qc, y0c, mc, nc_, s_zero)
    qx, y0x, mx, nx, bonus, gate = _rwkv_chunk_ops(px_r, rparams)
    yf, yb, _ = _rwkv_sweep(qx, y0x, mx, nx, s_ctx)
    return (yf, yb), bonus, gate


def _diff_prep_kernel(p_ref, cos_ref, sin_ref, qg_ref, kg_ref, q_ref, k_ref, v_ref, *, rope):
    lane = lax.broadcasted_iota(jnp.int32, (1, LANES), 1)
    first = (lane % 32) < 16
    for hd in range(DIFF_HEADS):
        for off, g_ref, o_ref, scale in ((0, qg_ref, q_ref, DIFF_SCALE * math.log2(math.e)),
                                         (DIFF_QK_COLS, kg_ref, k_ref, 1.0)):
            cs = slice(hd * LANES, (hd + 1) * LANES)
            xb = p_ref[0, :, off + hd * LANES:off + (hd + 1) * LANES].astype(F32)
            ms = _group_sum(xb * xb, DIFF_QK_DIM, pieces=1) * (1.0 / DIFF_QK_DIM)
            y = xb * lax.rsqrt(ms + NORM_EPS) * g_ref[...]
            if rope:
                swapped = jnp.where(first, pltpu.roll(y, LANES - 16, axis=1), pltpu.roll(y, 16, axis=1))
                y = y * cos_ref[...] + swapped * sin_ref[...]
            o_ref[0, :, cs] = (y * scale).astype(BF16)
    ones = jnp.ones((p_ref.shape[1], V_EXT - DIFF_V_DIM), BF16)
    for hd in range(DIFF_HEADS):
        vb = p_ref[0, :, 2 * DIFF_QK_COLS + hd * DIFF_V_DIM:2 * DIFF_QK_COLS + (hd + 1) * DIFF_V_DIM]
        v_ref[0, :, hd * V_EXT:(hd + 1) * V_EXT] = jnp.concatenate([vb.astype(BF16), ones], axis=1)


def _diff_prep(p, cos_t, sin_t, qg, kg, rope):
    b, l, pc = p.shape
    t = _pick(l, (512, 256, 128))
    tok = lambda w=DIFF_DIM: pl.BlockSpec((1, t, w), lambda bi, i: (bi, i, 0))
    shp = jax.ShapeDtypeStruct((b, l, DIFF_DIM), BF16)
    shp_v = jax.ShapeDtypeStruct((b, l, DIFF_HEADS * V_EXT), BF16)
    return pl.pallas_call(
        functools.partial(_diff_prep_kernel, rope=rope),
        grid=(b, l // t),
        in_specs=[pl.BlockSpec((1, t, pc), lambda bi, i: (bi, i, 0)),
                  pl.BlockSpec((t, LANES), lambda bi, i: (i, 0)),
                  pl.BlockSpec((t, LANES), lambda bi, i: (i, 0)),
                  pl.BlockSpec((1, LANES), lambda bi, i: (0, 0)),
                  pl.BlockSpec((1, LANES), lambda bi, i: (0, 0))],
        out_specs=[tok(), tok(), tok(DIFF_HEADS * V_EXT)],
        out_shape=[shp, shp, shp_v],
        compiler_params=_cparams("parallel", "parallel"),
        name="diff_prep_rope" if rope else "diff_prep",
    )(p, cos_t, sin_t, qg, kg)


def _diff_finish(acc1, acc2, lam_ref, sg_ref, o_ref):
    dv = DIFF_V_DIM
    o = acc1[:, :dv] / acc1[:, dv:] - lam_ref[...] * (acc2[:, :dv] / acc2[:, dv:])
    o = o * lax.rsqrt(jnp.mean(o * o, axis=-1, keepdims=True) + SUBLN_EPS)
    o_ref[0] = (o * sg_ref[...]).astype(o_ref.dtype)


def _flash_online_kernel(lam_ref, sg_ref, q_ref, kc_ref, vc_ref, k_ref, v_ref, o_ref, *, tk):
    qd = DIFF_QK_DIM
    tq = q_ref.shape[1]
    q = q_ref[0]
    qs = (q[:, :qd], q[:, qd:])

    def absorb(state, k, v):
        reps = k.shape[0] // LANES
        out = []
        for mp in range(2):
            m_prev, acc = state[mp]
            s = lax.dot_general(qs[mp], k[:, mp * qd:(mp + 1) * qd], _NT, preferred_element_type=F32)
            m_new = jnp.maximum(m_prev, jnp.max(s, axis=-1, keepdims=True))
            alpha = jnp.exp2(m_prev - m_new)
            pr = jnp.exp2(s - jnp.concatenate([m_new] * reps, axis=1))
            acc_new = jnp.concatenate([alpha, alpha], axis=1) * acc + jnp.dot(pr.astype(BF16), v, preferred_element_type=F32)
            out.append((m_new, acc_new))
        return tuple(out)

    init = (jnp.full((tq, LANES), -jnp.inf, F32), jnp.zeros((tq, V_EXT), F32))
    state = absorb((init, init), kc_ref[0], vc_ref[0])

    def body(j, state):
        rows = pl.ds(pl.multiple_of(j * tk, tk), tk)
        return absorb(state, k_ref[0, rows, :], v_ref[0, rows, :])

    (_, acc1), (_, acc2) = lax.fori_loop(0, k_ref.shape[1] // tk, body, state)
    _diff_finish(acc1, acc2, lam_ref, sg_ref, o_ref)


def _flash_bounded_kernel(lam_ref, sg_ref, q_ref, kc_ref, vc_ref, k_ref, v_ref, o_ref, acc_ref, p_ref, *, tk):
    qd = DIFF_QK_DIM
    q = q_ref[0]
    qs = (q[:, :qd], q[:, qd:])
    n_kv = k_ref.shape[1] // tk

    def weights(mp, k):
        s = lax.dot_general(qs[mp], k[:, mp * qd:(mp + 1) * qd], _NT, preferred_element_type=F32)
        return jnp.exp2(s).astype(BF16)

    def chunk(c):
        return pl.ds(pl.multiple_of(jnp.minimum(c, n_kv - 1) * tk, tk), tk)

    for mp in range(2):
        acc_ref[mp] = jnp.dot(weights(mp, kc_ref[0]), vc_ref[0], preferred_element_type=F32)
        p_ref[0, mp] = weights(mp, k_ref[0, pl.ds(0, tk), :])

    steps = 4 if n_kv % 4 == 0 else 2

    def body(j, carry):
        for u in range(steps):
            c = steps * j + u
            v = v_ref[0, chunk(c), :]
            k_next = k_ref[0, chunk(c + 1), :]
            for mp in range(2):
                acc_ref[mp] += jnp.dot(p_ref[u % 2, mp], v, preferred_element_type=F32)
                p_ref[1 - u % 2, mp] = weights(mp, k_next)
        return carry

    lax.fori_loop(0, n_kv // steps, body, 0)
    _diff_finish(acc_ref[0], acc_ref[1], lam_ref, sg_ref, o_ref)


def _diff_attention(q, kc, vc, k, v, lam_vec, sg_vec, bounded):
    b, l, _ = q.shape
    lc = kc.shape[1]
    tq = _pick(l, (256, 128))
    tk = _pick(l, (512, 256, 128))
    if bounded:
        tq = _pick(l, (1024, 512, 256, 128))
        tk = _pick(l // 2, (512, 256, 128))
        assert l % (2 * tk) == 0
        body = functools.partial(_flash_bounded_kernel, tk=tk)
        scratch = [pltpu.VMEM((2, tq, V_EXT), F32), pltpu.VMEM((2, 2, tq, tk), BF16)]
    else:
        body = functools.partial(_flash_online_kernel, tk=tk)
        scratch = []
    return pl.pallas_call(
        body,
        grid=(b, DIFF_HEADS, l // tq),
        in_specs=[pl.BlockSpec((1, LANES), lambda bi, h, i: (0, 0)),
                  pl.BlockSpec((1, LANES), lambda bi, h, i: (0, 0)),
                  pl.BlockSpec((1, tq, LANES), lambda bi, h, i: (bi, i, h)),
                  pl.BlockSpec((1, lc, LANES), lambda bi, h, i: (bi, 0, h)),
                  pl.BlockSpec((1, lc, V_EXT), lambda bi, h, i: (bi, 0, h)),
                  pl.BlockSpec((1, l, LANES), lambda bi, h, i: (bi, 0, h)),
                  pl.BlockSpec((1, l, V_EXT), lambda bi, h, i: (bi, 0, h))],
        out_specs=pl.BlockSpec((1, tq, LANES), lambda bi, h, i: (bi, i, h)),
        out_shape=jax.ShapeDtypeStruct((b, l, DIFF_DIM), BF16),
        scratch_shapes=scratch,
        compiler_params=_cparams("parallel", "parallel", "arbitrary"),
        name="diff_flash_bounded" if bounded else "diff_flash_online",
    )(lam_vec, sg_vec, q, kc, vc, k, v)


def _merge_kernel(yf_ref, yb_ref, bonus_ref, gate_ref, yd_ref, pg_ref, lg_ref, lb_ref, wpa_ref, wpb_ref, o_ref):
    hd = RWKV_HEAD_DIM
    y = yf_ref[0].astype(F32) + yb_ref[0].astype(F32)
    dev = y - _group_sum(y, hd) * (1.0 / hd)
    var = _group_sum(dev * dev, hd, pieces=1) * (1.0 / hd)
    yn = dev * lax.rsqrt(var + LNX_EPS) * lg_ref[...] + lb_ref[...]
    y_rwkv = (yn + bonus_ref[0].astype(F32)) * gate_ref[0].astype(F32)
    a = _dot(y_rwkv, wpa_ref[...])
    bb = jnp.dot(yd_ref[0], wpb_ref[...], preferred_element_type=F32)
    ga = _sigmoid(pg_ref[0, :, :D_MODEL].astype(F32))
    gb = _sigmoid(pg_ref[0, :, D_MODEL:].astype(F32))
    o_ref[0] = (ga * a + gb * bb).astype(BF16)


def _merge(y_sweep, bonus, gate, y_diff, p_gate, lnx_g, lnx_b, w_pa, w_pb):
    b, l, c = bonus.shape
    d = D_MODEL
    tm = _pick(l, (256, 128))
    const = lambda shape: pl.BlockSpec(shape, lambda bi, i: (0,) * len(shape))
    tok = lambda w: pl.BlockSpec((1, tm, w), lambda bi, i: (bi, i, 0))
    return pl.pallas_call(
        _merge_kernel,
        grid=(b, l // tm),
        in_specs=[tok(c), tok(c), tok(c), tok(c), tok(DIFF_DIM), tok(GATE_COLS),
                  const((1, c)), const((1, c)), const((c, d)), const((DIFF_DIM, d))],
        out_specs=tok(d),
        out_shape=jax.ShapeDtypeStruct((b, l, d), BF16),
        compiler_params=_cparams("parallel", "parallel"),
        name="merge",
    )(y_sweep[0], y_sweep[1], bonus, gate, y_diff, p_gate, lnx_g, lnx_b, w_pa, w_pb)


def _outproj_kernel(mx_ref, x_ref, gt_ref, g_ref, sc_ref, sh_ref, wo_ref, wr_ref, br_ref, xn_ref, h_ref, lg_ref):
    mix = jnp.dot(mx_ref[0], wo_ref[...], preferred_element_type=F32)
    xn = x_ref[0] + gt_ref[0] * mix
    xn_ref[0] = xn
    y = xn * lax.rsqrt(jnp.mean(xn * xn, axis=-1, keepdims=True) + NORM_EPS) * g_ref[...]
    h = y * (1.0 + sc_ref[0]) + sh_ref[0]
    h_hi, h_lo = _bf16_parts(h, 2)
    h_ref[0] = h_hi
    dot = lambda a, b: jnp.dot(a, b, preferred_element_type=F32)
    lg_ref[0] = dot(h_hi, wr_ref[0]) + (dot(h_lo, wr_ref[0]) + dot(h_hi, wr_ref[1])) + br_ref[...]


def _outproj(mixed, x, gt1, g2, sc2, sh2, w_out, w_router, b_router):
    b, l, d = x.shape
    tm = _pick(l, (512, 256, 128))
    const = lambda shape: pl.BlockSpec(shape, lambda bi, i: (0,) * len(shape))
    tok = lambda w: pl.BlockSpec((1, tm, w), lambda bi, i: (bi, i, 0))
    per_b = lambda: pl.BlockSpec((1, 1, d), lambda bi, i: (bi, 0, 0))
    return pl.pallas_call(
        _outproj_kernel,
        grid=(b, l // tm),
        in_specs=[tok(d), tok(d), per_b(), const((1, d)), per_b(), per_b(),
                  const((d, d)), const((2, d, ROUTER_PAD)), const((1, ROUTER_PAD))],
        out_specs=[tok(d), tok(d), tok(ROUTER_PAD)],
        out_shape=[jax.ShapeDtypeStruct((b, l, d), F32), jax.ShapeDtypeStruct((b, l, d), BF16),
                   jax.ShapeDtypeStruct((b, l, ROUTER_PAD), F32)],
        compiler_params=_cparams("parallel", "parallel"),
        name="outproj_router",
    )(mixed, x, gt1, g2, sc2, sh2, w_out, w_router, b_router)


def _moe_kernel(wb_ref, we_ref, lo_ref, hi_ref, x_ref, sw_ref, w1_ref, w3_ref, w2_ref, o_ref,
                w1b_ref, w3b_ref, w2b_ref, cached_ref):
    i = pl.program_id(0)
    lo, hi = lo_ref[i], hi_ref[i]
    live = hi > lo

    @pl.when(i == 0)
    def _():
        cached_ref[0] = -1

    @pl.when((i == 0) | (wb_ref[i] != wb_ref[jnp.maximum(i - 1, 0)]))
    def _():
        o_ref[...] = jnp.zeros(o_ref.shape, o_ref.dtype)

    @pl.when(live & (cached_ref[0] != we_ref[i]))
    def _():
        w1b_ref[...] = w1_ref[0].astype(BF16)
        w3b_ref[...] = w3_ref[0].astype(BF16)
        w2b_ref[...] = w2_ref[0].astype(BF16)
        cached_ref[0] = we_ref[i]

    @pl.when(live)
    def _():
        xb = x_ref[...]
        u = jnp.dot(xb, w1b_ref[...], preferred_element_type=F32)
        g = jnp.dot(xb, w3b_ref[...], preferred_element_type=F32)
        hmid = (u * _sigmoid(u) * g).astype(BF16)
        res = (jnp.dot(hmid, w2b_ref[...], preferred_element_type=F32) * sw_ref[...]).astype(o_ref.dtype)
        row = lax.broadcasted_iota(jnp.int32, (o_ref.shape[0], 1), 0)
        o_ref[...] = jnp.where((row >= lo) & (row < hi), res, o_ref[...])


def _moe_ffn(xs, sw, items, w1, w3, w2):
    n_rows, d = xs.shape
    wb, we, lo, hi = items
    grid_spec = pltpu.PrefetchScalarGridSpec(
        num_scalar_prefetch=4,
        grid=(wb.shape[0],),
        in_specs=[pl.BlockSpec((MOE_TILE, d), lambda i, wb, we, lo, hi: (wb[i], 0)),
                  pl.BlockSpec((MOE_TILE, 1), lambda i, wb, we, lo, hi: (wb[i], 0)),
                  pl.BlockSpec((1, d, D_EXPERT), lambda i, wb, we, lo, hi: (we[i], 0, 0)),
                  pl.BlockSpec((1, d, D_EXPERT), lambda i, wb, we, lo, hi: (we[i], 0, 0)),
                  pl.BlockSpec((1, D_EXPERT, d), lambda i, wb, we, lo, hi: (we[i], 0, 0))],
        out_specs=pl.BlockSpec((MOE_TILE, d), lambda i, wb, we, lo, hi: (wb[i], 0)),
        scratch_shapes=[pltpu.VMEM((d, D_EXPERT), BF16), pltpu.VMEM((d, D_EXPERT), BF16),
                        pltpu.VMEM((D_EXPERT, d), BF16), pltpu.SMEM((1,), jnp.int32)],
    )
    return pl.pallas_call(
        _moe_kernel,
        grid_spec=grid_spec,
        out_shape=jax.ShapeDtypeStruct((n_rows, d), BF16),
        compiler_params=_cparams("arbitrary"),
        name="moe_ffn",
    )(wb, we, lo, hi, xs, sw, w1, w3, w2)


def _final_kernel(x_ref, gt_ref, y0_ref, y1_ref, o_ref):
    o_ref[0] = x_ref[0] + gt_ref[0] * (y0_ref[0].astype(F32) + y1_ref[0].astype(F32))


def _final(x_new, gt2, y0, y1):
    b, l, d = x_new.shape
    tm = _pick(l, (512, 256, 128))
    tok = lambda: pl.BlockSpec((1, tm, d), lambda bi, i: (bi, i, 0))
    return pl.pallas_call(
        _final_kernel,
        grid=(b, l // tm),
        in_specs=[tok(), pl.BlockSpec((1, 1, d), lambda bi, i: (bi, 0, 0)), tok(), tok()],
        out_specs=tok(),
        out_shape=jax.ShapeDtypeStruct((b, l, d), F32),
        compiler_params=_cparams("parallel", "parallel"),
        name="moe_residual",
    )(x_new, gt2, y0, y1)


def _pad_lora_cols(w, widths):
    parts, o = [], 0
    for wd in widths:
        blk = w[..., o:o + wd]
        parts.append(jnp.pad(blk, [(0, 0)] * (w.ndim - 1) + [(0, LORA_PAD - wd)]))
        o += wd
    return jnp.concatenate(parts, axis=-1)


def _rope_tables(l):
    half = DIFF_QK_DIM // 2
    inv_freq = ROPE_THETA ** (-jnp.arange(0, half, 2, dtype=F32) / half)
    t = jnp.arange(l, dtype=jnp.int32)
    rows = (t // GRID_W).astype(F32)[:, None] * inv_freq
    cols = (t % GRID_W).astype(F32)[:, None] * inv_freq
    cos64 = jnp.concatenate([jnp.cos(rows), jnp.cos(rows), jnp.cos(cols), jnp.cos(cols)], axis=1)
    sin64 = jnp.concatenate([-jnp.sin(rows), jnp.sin(rows), -jnp.sin(cols), jnp.sin(cols)], axis=1)
    return jnp.tile(cos64, (1, 2)), jnp.tile(sin64, (1, 2))


def _route(logits, n_tok):
    g_logits = logits[:, :N_GROUPS]
    e_logits = logits[:, N_GROUPS:N_GROUPS + N_EXPERTS].reshape(n_tok, N_GROUPS, EXPERTS_PER_GROUP)
    p_group = jax.nn.softmax(g_logits, axis=-1)
    g_top = jnp.argmax(g_logits, axis=-1)
    p_g = jnp.take_along_axis(p_group, g_top[:, None], axis=1)[:, 0]
    e_sel = jnp.take_along_axis(e_logits, g_top[:, None, None], axis=1)[:, 0]
    top_p, top_i = lax.top_k(jax.nn.softmax(e_sel, axis=-1), TOP_K)
    top_p = top_p / jnp.sum(top_p, axis=-1, keepdims=True)
    gate = p_g[:, None] * top_p
    expert = g_top[:, None] * EXPERTS_PER_GROUP + top_i

    n_assign = n_tok * TOP_K
    assert n_assign % MOE_TILE == 0
    flat_e = expert.reshape(-1).astype(jnp.int32)
    ids = jnp.arange(n_assign, dtype=jnp.int32)
    sorted_e, order, sorted_w = lax.sort((flat_e, ids, gate.reshape(-1)), num_keys=1, is_stable=True)
    _, rank = lax.sort((order, ids), num_keys=1)
    ends = jnp.searchsorted(sorted_e, jnp.arange(N_EXPERTS, dtype=jnp.int32), side='right').astype(jnp.int32)
    starts = jnp.concatenate([jnp.zeros((1,), jnp.int32), ends[:-1]])

    nb = n_assign // MOE_TILE
    blk_lo = jnp.arange(nb, dtype=jnp.int32) * MOE_TILE
    e_first = jnp.searchsorted(ends, blk_lo, side='right').astype(jnp.int32)
    e_last = jnp.searchsorted(ends, blk_lo + MOE_TILE - 1, side='right').astype(jnp.int32)
    per_blk = e_last - e_first + 1
    cum = jnp.cumsum(per_blk)
    it = jnp.arange(nb + N_EXPERTS - 1, dtype=jnp.int32)
    wb = jnp.minimum(jnp.searchsorted(cum, it, side='right'), nb - 1).astype(jnp.int32)
    we = jnp.clip(e_first[wb] + it - (cum[wb] - per_blk[wb]), 0, N_EXPERTS - 1)
    lo = jnp.clip(starts[we], blk_lo[wb], blk_lo[wb] + MOE_TILE) - blk_lo[wb]
    hi = jnp.clip(ends[we], blk_lo[wb], blk_lo[wb] + MOE_TILE) - blk_lo[wb]
    hi = jnp.where(it < cum[-1], hi, lo)
    return order // TOP_K, sorted_w, rank.reshape(n_tok, TOP_K), (wb, we, lo, hi)


def kernel(x, c, ctx, c_ctx, ada_w, ada_b, norm1_g, norm2_g, w_in, shift_mu, rwkv_w0, rwkv_w2, rwkv_a0, rwkv_a2,
           rwkv_g2, rwkv_k_k, rwkv_k_a, rwkv_r_k, rwkv_lnx_g, rwkv_lnx_b, qn_g, kn_g, diff_lambda, subln_g,
           w_pa, w_pb, w_out, router_g_w, router_g_b, router_e_w, router_e_b, exp_w1, exp_w3, exp_w2):
    assert ada_w.shape[0] == 1, "single-layer block"
    b, l, d = x.shape
    lc = ctx.shape[1]
    lam_init = 0.8 - 0.6 * math.exp(-0.3 * 0)
    lv = diff_lambda[0].astype(F32)
    lam = jnp.exp(jnp.sum(lv[0] * lv[1])) - jnp.exp(jnp.sum(lv[2] * lv[3])) + lam_init

    rows = (b + 1 + SUBLANES - 1) // SUBLANES * SUBLANES
    cm = jnp.zeros((rows, d), F32).at[:b].set(c).at[b].set(c_ctx)
    mod = _modulation(cm, ada_w[0], ada_b[0])
    sh1, sc1, gt1, sh2, sc2, gt2 = [mod[:b, None, k * d:(k + 1) * d] for k in range(6)]
    csh1, csc1 = [jnp.broadcast_to(mod[b, k * d:(k + 1) * d], (b, 1, d)) for k in range(2)]

    w = w_in[0]
    lora_widths = (DECAY_LORA, DECAY_LORA, AAA_LORA, AAA_LORA)
    o_lora = 3 * RWKV_DIM
    o_glora = o_lora + sum(lora_widths)
    pad_cols = lambda m: jnp.concatenate(
        [m[..., :o_lora], _pad_lora_cols(m[..., o_lora:o_glora], lora_widths), m[..., o_glora:RWKV_COLS]], axis=-1)
    w_rwkv = pad_cols(w).astype(BF16)
    w_diff = w[:, RWKV_COLS:RWKV_COLS + DIFF_COLS].astype(BF16)
    w_gate = w[:, RWKV_COLS + DIFF_COLS:].astype(BF16)
    g1 = norm1_g[0]
    hx = _norm_mod(x, g1, sc1, sh1)
    hc = _norm_mod(ctx, g1, csc1, csh1)
    px_r, px_d, px_g = _proj(hx, w_rwkv, F32), _proj(hx, w_diff, BF16), _proj(hx, w_gate, BF16)
    pc_r, pc_d = _proj(hc, w_rwkv, F32), _proj(hc, w_diff, BF16)

    pad_rows = lambda m: jnp.pad(m, ((0, 0), (0, LORA_PAD - m.shape[1]), (0, 0)))
    rparams = (pad_cols(shift_mu[0])[None], rwkv_w0[0], pad_rows(rwkv_w2[0]), rwkv_a0[0], pad_rows(rwkv_a2[0]),
               rwkv_g2[0], rwkv_k_k[0][None], rwkv_k_a[0][None], rwkv_r_k[0].reshape(1, RWKV_DIM))
    y_sweep, bonus, gate = _rwkv_scans(px_r, pc_r, rparams)

    cos_t, sin_t = _rope_tables(l)
    qg = jnp.tile(qn_g[0], 2)[None]
    kg = jnp.tile(kn_g[0], 2)[None]
    q_x, k_x, v_x = _diff_prep(px_d, cos_t, sin_t, qg, kg, True)
    _, k_c, v_c = _diff_prep(pc_d, cos_t[:lc], sin_t[:lc], qg, kg, False)
    lam_vec = jnp.full((1, LANES), lam, F32)
    sg_vec = (subln_g[0] * (1.0 - lam_init))[None]
    score_bound = (1.05 * DIFF_QK_DIM * DIFF_SCALE * math.log2(math.e)
                   * jnp.max(jnp.abs(qn_g[0])) * jnp.max(jnp.abs(kn_g[0])))
    attn_args = (q_x, k_c, v_c, k_x, v_x, lam_vec, sg_vec)
    y_diff = lax.cond(score_bound <= SCORE_LOG2_LIMIT,
                      lambda a: _diff_attention(*a, bounded=True),
                      lambda a: _diff_attention(*a, bounded=False), attn_args)

    mixed = _merge(y_sweep, bonus, gate, y_diff, px_g, rwkv_lnx_g[0][None], rwkv_lnx_b[0][None],
                   w_pa[0].astype(BF16), w_pb[0].astype(BF16))
    n_r = N_GROUPS + N_EXPERTS
    w_router = jnp.zeros((d, ROUTER_PAD), F32).at[:, :N_GROUPS].set(router_g_w[0]).at[:, N_GROUPS:n_r].set(router_e_w[0])
    b_router = jnp.zeros((1, ROUTER_PAD), F32).at[0, :N_GROUPS].set(router_g_b[0]).at[0, N_GROUPS:n_r].set(router_e_b[0])
    w_router_hi = w_router.astype(BF16)
    w_router_lo = (w_router - w_router_hi.astype(F32)).astype(BF16)
    x_new, h2, logits = _outproj(mixed, x, gt1, norm2_g[0][None], sc2, sh2, w_out[0].astype(BF16),
                                 jnp.stack([w_router_hi, w_router_lo]), b_router)

    n_tok = b * l
    row_tok, row_w, row_of, items = _route(logits.reshape(n_tok, ROUTER_PAD), n_tok)
    xs = h2.reshape(n_tok, d)[row_tok]
    out = _moe_ffn(xs, row_w[:, None], items, exp_w1[0], exp_w3[0], exp_w2[0])
    y0 = out[row_of[:, 0]].reshape(b, l, d)
    y1 = out[row_of[:, 1]].reshape(b, l, d)
    return _final(x_new, gt2, y0, y1)
```

```python
import functools
import math

import jax
import jax.numpy as jnp
from jax import lax
from jax.experimental import pallas as pl
from jax.experimental.pallas import tpu as pltpu

F32 = jnp.float32
BF16 = jnp.bfloat16
HIGHEST = lax.Precision.HIGHEST

D_MODEL = 2048
GRID_W = 64
RWKV_HEADS = 16
RWKV_HEAD_DIM = 64
RWKV_DIM = RWKV_HEADS * RWKV_HEAD_DIM
DECAY_LORA = 96
AAA_LORA = 96
GATE_LORA = 256
RWKV_COLS = 3 * RWKV_DIM + 2 * DECAY_LORA + 2 * AAA_LORA + GATE_LORA
DIFF_HEADS = 8
DIFF_QK_DIM = 64
DIFF_V_DIM = 2 * DIFF_QK_DIM
DIFF_DIM = DIFF_HEADS * DIFF_V_DIM
DIFF_QK_COLS = DIFF_HEADS * 2 * DIFF_QK_DIM
DIFF_COLS = 2 * DIFF_QK_COLS + DIFF_DIM
DIFF_SCALE = DIFF_QK_DIM ** -0.5
ROPE_THETA = 10000.0
GATE_COLS = 2 * D_MODEL
N_GROUPS = 4
EXPERTS_PER_GROUP = 8
N_EXPERTS = N_GROUPS * EXPERTS_PER_GROUP
TOP_K = 2
D_EXPERT = 512
NORM_EPS = 1e-6
SUBLN_EPS = 1e-5
LNX_EPS = 64e-5

LANES = 128
SUBLANES = 8
VMEM_LIMIT_BYTES = 56 * 1024 * 1024

LORA_PAD = LANES
RWKV_PCOLS = 3 * RWKV_DIM + 4 * LORA_PAD + GATE_LORA
CHUNK = 64
V_EXT =2 * DIFF_V_DIM
SCORE_LOG2_LIMIT = 60.0
ROUTER_PAD = LANES
MOE_TILE = 512


def _cparams(*sem):
    return pltpu.CompilerParams(dimension_semantics=sem, vmem_limit_bytes=VMEM_LIMIT_BYTES)


def _sigmoid(x):
    return 1.0 / (1.0 + jnp.exp(-x))


def _dot(a, b, dims=(((1,), (0,)), ((), ()))):
    return lax.dot_general(a.astype(BF16), b.astype(BF16), dims, preferred_element_type=F32)


def _dot_f32(a, b, dims=(((1,), (0,)), ((), ()))):
    return lax.dot_general(a, b, dims, precision=HIGHEST, preferred_element_type=F32)


_NT = (((1,), (1,)), ((), ()))
_TN = (((0,), (0,)), ((), ()))


def _bf16_parts(x, n):
    parts = []
    for _ in range(n):
        p = x.astype(BF16)
        parts.append(p)
        x = x - p.astype(F32)
    return parts


def _dot_split(x, w_b, n, lhs=True):
    parts = _bf16_parts(x, n)
    outs = [jnp.dot(p, w_b, preferred_element_type=F32) if lhs else jnp.dot(w_b, p, preferred_element_type=F32)
            for p in parts]
    return functools.reduce(lambda a, b: a + b, outs)


def _group_ones(width, group):
    r = lax.broadcasted_iota(jnp.int32, (width, width), 0) // group
    c = lax.broadcasted_iota(jnp.int32, (width, width), 1) // group
    return (r == c).astype(BF16)


def _group_sum(x, group, pieces=2):
    ones = _group_ones(LANES, group)
    parts = [_dot_split(x[:, j * LANES:(j + 1) * LANES], ones, pieces) for j in range(x.shape[1] // LANES)]
    return parts[0] if len(parts) == 1 else jnp.concatenate(parts, axis=1)


def _mod_kernel(c_ref, w_ref, b_ref, o_ref):
    c = c_ref[...]
    o_ref[...] = _dot_f32(c * _sigmoid(c), w_ref[...]) + b_ref[...]


def _modulation(cm, ada_w, ada_b):
    rows, d = cm.shape
    n = ada_w.shape[1]
    tn = 1536
    return pl.pallas_call(
        _mod_kernel,
        grid=(n // tn,),
        in_specs=[pl.BlockSpec((rows, d), lambda j: (0, 0)),
                  pl.BlockSpec((d, tn), lambda j: (0, j)),
                  pl.BlockSpec((1, tn), lambda j: (0, j))],
        out_specs=pl.BlockSpec((rows, tn), lambda j: (0, j)),
        out_shape=jax.ShapeDtypeStruct((rows, n), F32),
        compiler_params=_cparams("parallel"),
        name="modulation",
    )(cm, ada_w, ada_b.reshape(1, n))


def _norm_mod_kernel(x_ref, g_ref, sc_ref, sh_ref, h_ref):
    x = x_ref[0]
    y = x * lax.rsqrt(jnp.mean(x * x, axis=-1, keepdims=True) + NORM_EPS) * g_ref[...]
    h_ref[0] = (y * (1.0 + sc_ref[0]) + sh_ref[0]).astype(BF16)


def _pick(n, prefs):
    for t in prefs:
        if n % t == 0:
            return t
    return n


def _norm_mod(x, g, sc, sh):
    b, l, d = x.shape
    tm = _pick(l, (512, 256, 128))
    return pl.pallas_call(
        _norm_mod_kernel,
        grid=(b, l // tm),
        in_specs=[pl.BlockSpec((1, tm, d), lambda bi, i: (bi, i, 0)),
                  pl.BlockSpec((1, d), lambda bi, i: (0, 0)),
                  pl.BlockSpec((1, 1, d), lambda bi, i: (bi, 0, 0)),
                  pl.BlockSpec((1, 1, d), lambda bi, i: (bi, 0, 0))],
        out_specs=pl.BlockSpec((1, tm, d), lambda bi, i: (bi, i, 0)),
        out_shape=jax.ShapeDtypeStruct((b, l, d), BF16),
        compiler_params=_cparams("parallel", "parallel"),
        name="norm_mod",
    )(x, g.reshape(1, d), sc, sh)


def _proj_kernel(h_ref, w_ref, o_ref):
    o_ref[0] = jnp.dot(h_ref[0], w_ref[...], preferred_element_type=F32).astype(o_ref.dtype)


def _proj(h, w, out_dtype):
    b, l, d = h.shape
    n = w.shape[1]
    tm = _pick(l, (1024, 512, 256, 128))
    tn = _pick(n, (1024, 768, 512, 256, 128))
    return pl.pallas_call(
        _proj_kernel,
        grid=(b, l // tm, n // tn),
        in_specs=[pl.BlockSpec((1, tm, d), lambda bi, i, j: (bi, i, 0)),
                  pl.BlockSpec((d, tn), lambda bi, i, j: (0, j))],
        out_specs=pl.BlockSpec((1, tm, tn), lambda bi, i, j: (bi, i, j)),
        out_shape=jax.ShapeDtypeStruct((b, l, n), out_dtype),
        compiler_params=_cparams("parallel", "parallel", "parallel"),
        name="proj",
    )(h, w)


def _pair_diag(x_b, mask_b):
    return jnp.concatenate([x_b, x_b], axis=0) * mask_b


def _tri_inverse_pairs(a_list, eye_f, mask_b):
    n = CHUNK
    mm = lambda l, r: jnp.dot(l.astype(BF16), _pair_diag(r.astype(BF16), mask_b), preferred_element_type=F32)
    xs = [eye_f + a for a in a_list]
    ps = [mm(a, a) for a in a_list]
    steps = int(math.log2(n)) - 1
    for s in range(steps):
        if s < steps - 1:
            xps = [mm(jnp.concatenate([x, p], axis=0), p) for x, p in zip(xs, ps)]
            xs = [x + xp[:n] for x, xp in zip(xs, xps)]
            ps = [xp[n:] for xp in xps]
        else:
            xs = [x + mm(x, p) for x, p in zip(xs, ps)]
    return xs


def _rwkv_chunk_kernel(p_ref, pp_ref, pn_ref, mu_ref, w0_ref, w2_ref, a0_ref, a2_ref, g2_ref, kk_ref, ka_ref,
                       rk_ref, q_ref, y0_ref, m_ref, n_ref, bonus_ref, gate_ref):
    i = pl.program_id(1)
    last = pl.num_programs(1) - 1
    c = RWKV_DIM
    hd = RWKV_HEAD_DIM
    p = p_ref[0]
    row = lax.broadcasted_iota(jnp.int32, (CHUNK, 1), 0)
    prev_row = jnp.where(i == 0, 0.0, pp_ref[0, SUBLANES - 1:SUBLANES, :])
    next_row = jnp.where(i == last, 0.0, pn_ref[0, 0:1, :])
    prev = jnp.where(row == 0, prev_row, pltpu.roll(p, 1, axis=0))
    nxt = jnp.where(row == CHUNK - 1, next_row, pltpu.roll(p, CHUNK - 1, axis=0))
    ps = p + mu_ref[...] * (0.5 * (prev + nxt) - p)

    r, k, v = ps[:, :c], ps[:, c:2 * c], ps[:, 2 * c:3 * c]
    o = 3 * c
    xw = (ps[:, o:o + LORA_PAD], ps[:, o + LORA_PAD:o + 2 * LORA_PAD])
    xa = (ps[:, o + 2 * LORA_PAD:o + 3 * LORA_PAD], ps[:, o + 3 * LORA_PAD:o + 4 * LORA_PAD])
    xg = ps[:, o + 4 * LORA_PAD:]

    gate_ref[0] = _dot(_sigmoid(xg), g2_ref[...]).astype(gate_ref.dtype)
    kk = k * kk_ref[...]
    kk = kk * lax.rsqrt(_group_sum(kk * kk, hd, pieces=1) + 1e-12)

    tr = lax.broadcasted_iota(jnp.int32, (CHUNK, CHUNK), 0)
    tc = lax.broadcasted_iota(jnp.int32, (CHUNK, CHUNK), 1)
    tr2 = lax.broadcasted_iota(jnp.int32, (CHUNK, LANES), 0)
    lane2 = lax.broadcasted_iota(jnp.int32, (CHUNK, LANES), 1)
    tc2 = lane2 % CHUNK
    lane_head = lane2 // hd
    eye2 = tr2 == tc2
    mask_b = _group_ones(LANES, hd)
    kd_sum = jnp.zeros_like(k)
    v_b = v.astype(BF16)
    psl = [slice(p * LANES, (p + 1) * LANES) for p in range(c // LANES)]
    before2, upto2, prep = [], [], []
    for d in range(2):
        before2.append((tc2 < tr2) if d == 0 else (tc2 > tr2))
        upto = (tc <= tr) if d == 0 else (tc >= tr)
        upto2.append((tc2 <= tr2) if d == 0 else (tc2 >= tr2))
        z = w0_ref[d:d + 1, :] + _dot(jnp.tanh(xw[d]), w2_ref[d])
        w_log = -(jnp.maximum(-z, 0.0) + jnp.log(1.0 + jnp.exp(-jnp.abs(z)))) - 0.5
        logw = -jnp.exp(w_log)
        a = _sigmoid(a0_ref[d:d + 1, :] + _dot(xa[d], a2_ref[d]))
        kd = k * (1.0 + (a - 1.0) * ka_ref[...])
        kd_sum = kd_sum + kd
        cum = _dot_split(logw, upto.astype(BF16), 3, lhs=False)
        total = cum[CHUNK - 1:CHUNK, :] if d == 0 else cum[0:1, :]
        e_neg = jnp.exp(-cum)
        e_rest = jnp.exp(total - cum)
        p_total = jnp.exp(total)
        beta = kk * a
        al = -kk * jnp.exp(cum - logw)
        rt = r * jnp.exp(cum)
        al_b = al.astype(BF16)
        rt_b = rt.astype(BF16)
        bt_b = (beta * e_neg).astype(BF16)
        kt_b = (kd * e_neg).astype(BF16)
        bh_b = (beta * e_rest).astype(BF16)
        kh_b = (kd * e_rest).astype(BF16)
        prep.append((al_b, rt_b, bt_b, kt_b, bh_b, kh_b, rt, p_total))

    dp = [(d, s) for d in range(2) for s in psl]
    fdot = lambda a, b: jnp.dot(a, b, preferred_element_type=F32)
    bd = lambda x: _pair_diag(x, mask_b)
    v_bd = [bd(v_b[:, s]) for s in psl] * 2
    zeros_sq = jnp.zeros((LANES, LANES), BF16)
    zeros_tl = jnp.zeros((CHUNK, LANES), BF16)
    gm = [lax.dot_general(jnp.concatenate([prep[d][0][:, s], prep[d][1][:, s]], axis=0),
                          jnp.concatenate([bd(prep[d][2][:, s]), bd(prep[d][3][:, s])], axis=0),
                          _NT, preferred_element_type=F32) for d, s in dp]
    a_ab = [jnp.where(before2[d], x[:CHUNK, :LANES], 0.0) for x, (d, s) in zip(gm, dp)]
    a_ak = [jnp.where(before2[d], x[:CHUNK, LANES:], 0.0).astype(BF16) for x, (d, s) in zip(gm, dp)]
    a_rb = [jnp.where(upto2[d], x[CHUNK:, :LANES], 0.0).astype(BF16) for x, (d, s) in zip(gm, dp)]
    a_rk = [jnp.where(upto2[d], x[CHUNK:, LANES:], 0.0).astype(BF16) for x, (d, s) in zip(gm, dp)]
    akv = [fdot(ak, vd) for ak, vd in zip(a_ak, v_bd)]
    t_inv = _tri_inverse_pairs(a_ab, eye2.astype(F32), mask_b)
    wu = [fdot(t.astype(BF16), jnp.concatenate([bd(prep[d][0][:, s]), bd(u.astype(BF16))], axis=1))
          for t, u, (d, s) in zip(t_inv, akv, dp)]
    w_b = [x[:, :LANES].astype(BF16) for x in wu]
    u_b = [x[:, LANES:].astype(BF16) for x in wu]
    qy = [fdot(jnp.concatenate([rb, rk], axis=1),
               jnp.concatenate([jnp.concatenate([bd(w), bd(u)], axis=1),
                                jnp.concatenate([zeros_sq, vd], axis=1)], axis=0))
          for rb, rk, w, u, vd in zip(a_rb, a_rk, w_b, u_b, v_bd)]
    full = [lax.dot_general(jnp.concatenate([jnp.concatenate([w, u], axis=1),
                                             jnp.concatenate([zeros_tl, v_b[:, s]], axis=1)], axis=0),
                            jnp.concatenate([prep[d][4][:, s], prep[d][5][:, s]], axis=0),
                            _TN, preferred_element_type=F32)
            for w, u, (d, s) in zip(w_b, u_b, dp)]

    def diag_blocks(x):
        return jnp.where(lane_head == 0, x[:hd], 0.0) + jnp.where(lane_head == 1, x[hd:2 * hd], 0.0)

    npair = len(psl)
    for d in range(2):
        sl = slice(d * npair, (d + 1) * npair)
        rt, p_total = prep[d][6], prep[d][7]
        q_ref[0, d] = (rt + jnp.concatenate([x[:, :LANES] for x in qy[sl]], axis=1)).astype(q_ref.dtype)
        y0_ref[0, d] = jnp.concatenate([x[:, LANES:] for x in qy[sl]], axis=1).astype(y0_ref.dtype)
        m_ref[0, d, 0] = jnp.concatenate([jnp.where(eye2, p_total[:, s], 0.0) + diag_blocks(x[:LANES])
                                          for x, s in zip(full[sl], psl)], axis=1).astype(m_ref.dtype)
        n_ref[0, d, 0] = jnp.concatenate([diag_blocks(x[LANES:]) for x in full[sl]], axis=1)
    bonus_ref[0] = (_group_sum(r * kd_sum * rk_ref[...], hd) * v).astype(bonus_ref.dtype)


def _rwkv_chunk_ops(p, params):
    b, l, pc = p.shape
    nc = l // CHUNK
    c = RWKV_DIM
    hb = CHUNK // SUBLANES
    nb8 = l // SUBLANES
    mu, w0, w2, a0, a2, g2, k_k, k_a, r_k = params
    const = lambda shape: pl.BlockSpec(shape, lambda bi, i: (0,) * len(shape))
    tok = lambda: pl.BlockSpec((1, CHUNK, c), lambda bi, i: (bi, i, 0))
    return pl.pallas_call(
        _rwkv_chunk_kernel,
        grid=(b, nc),
        in_specs=[pl.BlockSpec((1, CHUNK, pc), lambda bi, i: (bi, i, 0)),
                  pl.BlockSpec((1, SUBLANES, pc), lambda bi, i: (bi, jnp.maximum(i * hb - 1, 0), 0)),
                  pl.BlockSpec((1, SUBLANES, pc), lambda bi, i: (bi, jnp.minimum((i + 1) * hb, nb8 - 1), 0)),
                  const((1, pc)), const((2, c)), const((2, LORA_PAD, c)), const((2, c)), const((2, LORA_PAD, c)),
                  const((GATE_LORA, c)), const((1, c)), const((1, c)), const((1, c))],
        out_specs=[pl.BlockSpec((1, 2, CHUNK, c), lambda bi, i: (bi, 0, i, 0)),
                   pl.BlockSpec((1, 2, CHUNK, c), lambda bi, i: (bi, 0, i, 0)),
                   pl.BlockSpec((1, 2, 1, RWKV_HEAD_DIM, c), lambda bi, i: (bi, 0, i, 0, 0)),
                   pl.BlockSpec((1, 2, 1, RWKV_HEAD_DIM, c), lambda bi, i: (bi, 0, i, 0, 0)),
                   tok(), tok()],
        out_shape=[jax.ShapeDtypeStruct((b, 2, l, c), BF16), jax.ShapeDtypeStruct((b, 2, l, c), BF16),
                   jax.ShapeDtypeStruct((b, 2, nc, RWKV_HEAD_DIM, c), BF16),
                   jax.ShapeDtypeStruct((b, 2, nc, RWKV_HEAD_DIM, c), F32),
                   jax.ShapeDtypeStruct((b, l, c), BF16), jax.ShapeDtypeStruct((b, l, c), BF16)],
        compiler_params=_cparams("parallel", "parallel"),
        name="rwkv_chunk_ops",
    )(p, p, p, mu, w0, w2, a0, a2, g2, k_k, k_a, r_k)


def _rwkv_sweep_kernel(qf_ref, y0f_ref, mf_ref, nf_ref, qb_ref, y0b_ref, mb_ref, nb_ref, s0_ref,
                       yf_ref, yb_ref, sfin_ref, s_ref):
    j = pl.program_id(1)
    hd = RWKV_HEAD_DIM

    @pl.when(j == 0)
    def _():
        s_ref[...] = s0_ref[0]

    dirs = ((qf_ref, y0f_ref, mf_ref, nf_ref, yf_ref), (qb_ref, y0b_ref, mb_ref, nb_ref, yb_ref))
    psl = [slice(p * LANES, (p + 1) * LANES) for p in range(RWKV_DIM // LANES)]
    mask_b = _group_ones(LANES, hd)
    pair_diag = lambda x: _pair_diag(x, mask_b)
    for d, (q_ref, y0_ref, m_ref, n_ref, y_ref) in enumerate(dirs):
        s_b = s_ref[d].astype(BF16)
        q_b = q_ref[0, 0]
        m_b = m_ref[0, 0, 0]
        ys = [_dot(q_b[:, s], pair_diag(s_b[:, s]), _NT) for s in psl]
        sm = [_dot(s_b[:, s], pair_diag(m_b[:, s])) for s in psl]
        y_ref[0] = (y0_ref[0, 0].astype(F32) + jnp.concatenate(ys, axis=1)).astype(y_ref.dtype)
        s_ref[d] = n_ref[0, 0, 0] + jnp.concatenate(sm, axis=1)

    @pl.when(j == pl.num_programs(1) - 1)
    def _():
        sfin_ref[0] = s_ref[...]


def _rwkv_sweep(q, y0, m, n, s0):
    b, _, l, c = q.shape
    nc = l // CHUNK
    hd = RWKV_HEAD_DIM
    tokf = lambda: pl.BlockSpec((1, 1, CHUNK, c), lambda bi, j: (bi, 0, j, 0))
    tokb = lambda: pl.BlockSpec((1, 1, CHUNK, c), lambda bi, j: (bi, 1, nc - 1 - j, 0))
    opf = lambda: pl.BlockSpec((1, 1, 1, hd, c), lambda bi, j: (bi, 0, j, 0, 0))
    opb = lambda: pl.BlockSpec((1, 1, 1, hd, c), lambda bi, j: (bi, 1, nc - 1 - j, 0, 0))
    return pl.pallas_call(
        _rwkv_sweep_kernel,
        grid=(b, nc),
        in_specs=[tokf(), tokf(), opf(), opf(), tokb(), tokb(), opb(), opb(),
                  pl.BlockSpec((1, 2, hd, c), lambda bi, j: (bi, 0, 0, 0))],
        out_specs=[pl.BlockSpec((1, CHUNK, c), lambda bi, j: (bi, j, 0)),
                   pl.BlockSpec((1, CHUNK, c), lambda bi, j: (bi, nc - 1 - j, 0)),
                   pl.BlockSpec((1, 2, hd, c), lambda bi, j: (bi, 0, 0, 0))],
        out_shape=[jax.ShapeDtypeStruct((b, l, c), BF16), jax.ShapeDtypeStruct((b, l, c), BF16),
                   jax.ShapeDtypeStruct((b, 2, hd, c), F32)],
        scratch_shapes=[pltpu.VMEM((2, hd, c), F32)],
        compiler_params=_cparams("parallel", "arbitrary"),
        name="rwkv_sweep",
    )(q, y0, m, n, q, y0, m, n, s0)


def _rwkv_scans(px_r, pc_r, rparams):
    b = px_r.shape[0]
    qc, y0c, mc, nc_, _, _ = _rwkv_chunk_ops(pc_r, rparams)
    s_zero = jnp.zeros((b, 2, RWKV_HEAD_DIM, RWKV_DIM), F32)
    _, _, s_ctx = _rwkv_sweep(qc, y0c, mc, nc_, s_zero)
    qx, y0x, mx, nx, bonus, gate = _rwkv_chunk_ops(px_r, rparams)
    yf, yb, _ = _rwkv_sweep(qx, y0x, mx, nx, s_ctx)
    return (yf, yb), bonus, gate


def _diff_prep_kernel(p_ref, cos_ref, sin_ref, qg_ref, kg_ref, q_ref, k_ref, v_ref, *, rope):
    lane = lax.broadcasted_iota(jnp.int32, (1, LANES), 1)
    first = (lane % 32) < 16
    for hd in range(DIFF_HEADS):
        for off, g_ref, o_ref, scale in ((0, qg_ref, q_ref, DIFF_SCALE * math.log2(math.e)),
                                         (DIFF_QK_COLS, kg_ref, k_ref, 1.0)):
            cs = slice(hd * LANES, (hd + 1) * LANES)
            xb = p_ref[0, :, off + hd * LANES:off + (hd + 1) * LANES].astype(F32)
            ms = _group_sum(xb * xb, DIFF_QK_DIM, pieces=1) * (1.0 / DIFF_QK_DIM)
            y = xb * lax.rsqrt(ms + NORM_EPS) * g_ref[...]
            if rope:
                swapped = jnp.where(first, pltpu.roll(y, LANES - 16, axis=1), pltpu.roll(y, 16, axis=1))
                y = y * cos_ref[...] + swapped * sin_ref[...]
            o_ref[0, :, cs] = (y * scale).astype(BF16)
    ones = jnp.ones((p_ref.shape[1], V_EXT - DIFF_V_DIM), BF16)
    for hd in range(DIFF_HEADS):
        vb = p_ref[0, :, 2 * DIFF_QK_COLS + hd * DIFF_V_DIM:2 * DIFF_QK_COLS + (hd + 1) * DIFF_V_DIM]
        v_ref[0, :, hd * V_EXT:(hd + 1) * V_EXT] = jnp.concatenate([vb.astype(BF16), ones], axis=1)


def _diff_prep(p, cos_t, sin_t, qg, kg, rope):
    b, l, pc = p.shape
    t = _pick(l, (512, 256, 128))
    tok = lambda w=DIFF_DIM: pl.BlockSpec((1, t, w), lambda bi, i: (bi, i, 0))
    shp = jax.ShapeDtypeStruct((b, l, DIFF_DIM), BF16)
    shp_v = jax.ShapeDtypeStruct((b, l, DIFF_HEADS * V_EXT), BF16)
    return pl.pallas_call(
        functools.partial(_diff_prep_kernel, rope=rope),
        grid=(b, l // t),
        in_specs=[pl.BlockSpec((1, t, pc), lambda bi, i: (bi, i, 0)),
                  pl.BlockSpec((t, LANES), lambda bi, i: (i, 0)),
                  pl.BlockSpec((t, LANES), lambda bi, i: (i, 0)),
                  pl.BlockSpec((1, LANES), lambda bi, i: (0, 0)),
                  pl.BlockSpec((1, LANES), lambda bi, i: (0, 0))],
        out_specs=[tok(), tok(), tok(DIFF_HEADS * V_EXT)],
        out_shape=[shp, shp, shp_v],
        compiler_params=_cparams("parallel", "parallel"),
        name="diff_prep_rope" if rope else "diff_prep",
    )(p, cos_t, sin_t, qg, kg)


def _diff_finish(acc1, acc2, lam_ref, sg_ref, o_ref):
    dv = DIFF_V_DIM
    o = acc1[:, :dv] / acc1[:, dv:] - lam_ref[...] * (acc2[:, :dv] / acc2[:, dv:])
    o = o * lax.rsqrt(jnp.mean(o * o, axis=-1, keepdims=True) + SUBLN_EPS)
    o_ref[0] = (o * sg_ref[...]).astype(o_ref.dtype)


def _flash_online_kernel(lam_ref, sg_ref, q_ref, kc_ref, vc_ref, k_ref, v_ref, o_ref, *, tk):
    qd = DIFF_QK_DIM
    tq = q_ref.shape[1]
    q = q_ref[0]
    qs = (q[:, :qd], q[:, qd:])

    def absorb(state, k, v):
        reps = k.shape[0] // LANES
        out = []
        for mp in range(2):
            m_prev, acc = state[mp]
            s = lax.dot_general(qs[mp], k[:, mp * qd:(mp + 1) * qd], _NT, preferred_element_type=F32)
            m_new = jnp.maximum(m_prev, jnp.max(s, axis=-1, keepdims=True))
            alpha = jnp.exp2(m_prev - m_new)
            pr = jnp.exp2(s - jnp.concatenate([m_new] * reps, axis=1))
            acc_new = jnp.concatenate([alpha, alpha], axis=1) * acc + jnp.dot(pr.astype(BF16), v, preferred_element_type=F32)
            out.append((m_new, acc_new))
        return tuple(out)

    init = (jnp.full((tq, LANES), -jnp.inf, F32), jnp.zeros((tq, V_EXT), F32))
    state = absorb((init, init), kc_ref[0], vc_ref[0])

    def body(j, state):
        rows = pl.ds(pl.multiple_of(j * tk, tk), tk)
        return absorb(state, k_ref[0, rows, :], v_ref[0, rows, :])

    (_, acc1), (_, acc2) = lax.fori_loop(0, k_ref.shape[1] // tk, body, state)
    _diff_finish(acc1, acc2, lam_ref, sg_ref, o_ref)


def _flash_bounded_kernel(lam_ref, sg_ref, q_ref, kc_ref, vc_ref, k_ref, v_ref, o_ref, acc_ref, p_ref, *, tk):
    qd = DIFF_QK_DIM
    q = q_ref[0]
    qs = (q[:, :qd], q[:, qd:])
    n_kv = k_ref.shape[1] // tk

    def weights(mp, k):
        s = lax.dot_general(qs[mp], k[:, mp * qd:(mp + 1) * qd], _NT, preferred_element_type=F32)
        return jnp.exp2(s).astype(BF16)

    def chunk(c):
        return pl.ds(pl.multiple_of(jnp.minimum(c, n_kv - 1) * tk, tk), tk)

    for mp in range(2):
        acc_ref[mp] = jnp.dot(weights(mp, kc_ref[0]), vc_ref[0], preferred_element_type=F32)
        p_ref[0, mp] = weights(mp, k_ref[0, pl.ds(0, tk), :])

    steps = 4 if n_kv % 4 == 0 else 2

    def body(j, carry):
        for u in range(steps):
            c = steps * j + u
            v = v_ref[0, chunk(c), :]
            k_next = k_ref[0, chunk(c + 1), :]
            for mp in range(2):
                acc_ref[mp] += jnp.dot(p_ref[u % 2, mp], v, preferred_element_type=F32)
                p_ref[1 - u % 2, mp] = weights(mp, k_next)
        return carry

    lax.fori_loop(0, n_kv // steps, body, 0)
    _diff_finish(acc_ref[0], acc_ref[1], lam_ref, sg_ref, o_ref)


def _diff_attention(q, kc, vc, k, v, lam_vec, sg_vec, bounded):
    b, l, _ = q.shape
    lc = kc.shape[1]
    tq = _pick(l, (256, 128))
    tk = _pick(l, (512, 256, 128))
    if bounded:
        tq = _pick(l, (1024, 512, 256, 128))
        tk = _pick(l // 2, (512, 256, 128))
        assert l % (2 * tk) == 0
        body = functools.partial(_flash_bounded_kernel, tk=tk)
        scratch = [pltpu.VMEM((2, tq, V_EXT), F32), pltpu.VMEM((2, 2, tq, tk), BF16)]
    else:
        body = functools.partial(_flash_online_kernel, tk=tk)
        scratch = []
    return pl.pallas_call(
        body,
        grid=(b, DIFF_HEADS, l // tq),
        in_specs=[pl.BlockSpec((1, LANES), lambda bi, h, i: (0, 0)),
                  pl.BlockSpec((1, LANES), lambda bi, h, i: (0, 0)),
                  pl.BlockSpec((1, tq, LANES), lambda bi, h, i: (bi, i, h)),
                  pl.BlockSpec((1, lc, LANES), lambda bi, h, i: (bi, 0, h)),
                  pl.BlockSpec((1, lc, V_EXT), lambda bi, h, i: (bi, 0, h)),
                  pl.BlockSpec((1, l, LANES), lambda bi, h, i: (bi, 0, h)),
                  pl.BlockSpec((1, l, V_EXT), lambda bi, h, i: (bi, 0, h))],
        out_specs=pl.BlockSpec((1, tq, LANES), lambda bi, h, i: (bi, i, h)),
        out_shape=jax.ShapeDtypeStruct((b, l, DIFF_DIM), BF16),
        scratch_shapes=scratch,
        compiler_params=_cparams("parallel", "parallel", "arbitrary"),
        name="diff_flash_bounded" if bounded else "diff_flash_online",
    )(lam_vec, sg_vec, q, kc, vc, k, v)


def _merge_kernel(yf_ref, yb_ref, bonus_ref, gate_ref, yd_ref, pg_ref, lg_ref, lb_ref, wpa_ref, wpb_ref, o_ref):
    hd = RWKV_HEAD_DIM
    y = yf_ref[0].astype(F32) + yb_ref[0].astype(F32)
    dev = y - _group_sum(y, hd) * (1.0 / hd)
    var = _group_sum(dev * dev, hd, pieces=1) * (1.0 / hd)
    yn = dev * lax.rsqrt(var + LNX_EPS) * lg_ref[...] + lb_ref[...]
    y_rwkv = (yn + bonus_ref[0].astype(F32)) * gate_ref[0].astype(F32)
    a = _dot(y_rwkv, wpa_ref[...])
    bb = jnp.dot(yd_ref[0], wpb_ref[...], preferred_element_type=F32)
    ga = _sigmoid(pg_ref[0, :, :D_MODEL].astype(F32))
    gb = _sigmoid(pg_ref[0, :, D_MODEL:].astype(F32))
    o_ref[0] = (ga * a + gb * bb).astype(BF16)


def _merge(y_sweep, bonus, gate, y_diff, p_gate, lnx_g, lnx_b, w_pa, w_pb):
    b, l, c = bonus.shape
    d = D_MODEL
    tm = _pick(l, (256, 128))
    const = lambda shape: pl.BlockSpec(shape, lambda bi, i: (0,) * len(shape))
    tok = lambda w: pl.BlockSpec((1, tm, w), lambda bi, i: (bi, i, 0))
    return pl.pallas_call(
        _merge_kernel,
        grid=(b, l // tm),
        in_specs=[tok(c), tok(c), tok(c), tok(c), tok(DIFF_DIM), tok(GATE_COLS),
                  const((1, c)), const((1, c)), const((c, d)), const((DIFF_DIM, d))],
        out_specs=tok(d),
        out_shape=jax.ShapeDtypeStruct((b, l, d), BF16),
        compiler_params=_cparams("parallel", "parallel"),
        name="merge",
    )(y_sweep[0], y_sweep[1], bonus, gate, y_diff, p_gate, lnx_g, lnx_b, w_pa, w_pb)


def _outproj_kernel(mx_ref, x_ref, gt_ref, g_ref, sc_ref, sh_ref, wo_ref, wr_ref, br_ref, xn_ref, h_ref, rt_ref):
    mix = jnp.dot(mx_ref[0], wo_ref[...], preferred_element_type=F32)
    xn = x_ref[0] + gt_ref[0] * mix
    xn_ref[0] = xn
    y = xn * lax.rsqrt(jnp.mean(xn * xn, axis=-1, keepdims=True) + NORM_EPS) * g_ref[...]
    h = y * (1.0 + sc_ref[0]) + sh_ref[0]
    h_hi, h_lo = _bf16_parts(h, 2)
    h_ref[0] = h_hi
    dot = lambda a, b: jnp.dot(a, b, preferred_element_type=F32)
    lg = dot(h_hi, wr_ref[0]) + (dot(h_lo, wr_ref[0]) + dot(h_hi, wr_ref[1])) + br_ref[...]
    rt_ref[0] = _route_rows(lg)


def _route_rows(lg):
    lane = lax.broadcasted_iota(jnp.int32, lg.shape, 1).astype(F32)
    neg = jnp.float32(-3.0e38)
    far = jnp.float32(LANES)
    gl = jnp.where(lane < N_GROUPS, lg, neg)
    g_max = jnp.max(gl, axis=-1, keepdims=True)
    g_top = jnp.min(jnp.where(gl == g_max, lane, far), axis=-1, keepdims=True)
    p_g = 1.0 / jnp.sum(jnp.where(lane < N_GROUPS, jnp.exp(gl - g_max), 0.0), axis=-1, keepdims=True)
    e_lo = N_GROUPS + EXPERTS_PER_GROUP * g_top
    el = jnp.where((lane >= e_lo) & (lane < e_lo + EXPERTS_PER_GROUP), lg, neg)
    m1 = jnp.max(el, axis=-1, keepdims=True)
    i1 = jnp.min(jnp.where(el == m1, lane, far), axis=-1, keepdims=True)
    el2 = jnp.where(lane == i1, neg, el)
    m2 = jnp.max(el2, axis=-1, keepdims=True)
    i2 = jnp.min(jnp.where(el2 == m2, lane, far), axis=-1, keepdims=True)
    e2 = jnp.exp(m2 - m1)
    gate1 = p_g / (1.0 + e2)
    gate2 = p_g * e2 / (1.0 + e2)
    return jnp.where(lane == 0, i1 - N_GROUPS,
                     jnp.where(lane == 1, i2 - N_GROUPS, jnp.where(lane == 2, gate1, jnp.where(lane == 3, gate2, 0.0))))


def _outproj(mixed, x, gt1, g2, sc2, sh2, w_out, w_router, b_router):
    b, l, d = x.shape
    tm = _pick(l, (512, 256, 128))
    const = lambda shape: pl.BlockSpec(shape, lambda bi, i: (0,) * len(shape))
    tok = lambda w: pl.BlockSpec((1, tm, w), lambda bi, i: (bi, i, 0))
    per_b = lambda: pl.BlockSpec((1, 1, d), lambda bi, i: (bi, 0, 0))
    return pl.pallas_call(
        _outproj_kernel,
        grid=(b, l // tm),
        in_specs=[tok(d), tok(d), per_b(), const((1, d)), per_b(), per_b(),
                  const((d, d)), const((2, d, ROUTER_PAD)), const((1, ROUTER_PAD))],
        out_specs=[tok(d), tok(d), tok(ROUTER_PAD)],
        out_shape=[jax.ShapeDtypeStruct((b, l, d), F32), jax.ShapeDtypeStruct((b, l, d), BF16),
                   jax.ShapeDtypeStruct((b, l, ROUTER_PAD), F32)],
        compiler_params=_cparams("parallel", "parallel"),
        name="outproj_router",
    )(mixed, x, gt1, g2, sc2, sh2, w_out, w_router, b_router)


def _moe_kernel(wb_ref, we_ref, lo_ref, hi_ref, x_ref, sw_ref, w1_ref, w3_ref, w2_ref, o_ref,
                w1b_ref, w3b_ref, w2b_ref, cached_ref):
    i = pl.program_id(0)
    lo, hi = lo_ref[i], hi_ref[i]
    live = hi > lo

    @pl.when(i == 0)
    def _():
        cached_ref[0] = -1

    @pl.when((i == 0) | (wb_ref[i] != wb_ref[jnp.maximum(i - 1, 0)]))
    def _():
        o_ref[...] = jnp.zeros(o_ref.shape, o_ref.dtype)

    @pl.when(live & (cached_ref[0] != we_ref[i]))
    def _():
        w1b_ref[...] = w1_ref[0].astype(BF16)
        w3b_ref[...] = w3_ref[0].astype(BF16)
        w2b_ref[...] = w2_ref[0].astype(BF16)
        cached_ref[0] = we_ref[i]

    @pl.when(live)
    def _():
        xb = x_ref[...]
        u = jnp.dot(xb, w1b_ref[...], preferred_element_type=F32)
        g = jnp.dot(xb, w3b_ref[...], preferred_element_type=F32)
        hmid = (u * _sigmoid(u) * g).astype(BF16)
        res = (jnp.dot(hmid, w2b_ref[...], preferred_element_type=F32) * sw_ref[...]).astype(o_ref.dtype)
        row = lax.broadcasted_iota(jnp.int32, (o_ref.shape[0], 1), 0)
        o_ref[...] = jnp.where((row >= lo) & (row < hi), res, o_ref[...])


def _moe_ffn(xs, sw, items, w1, w3, w2):
    n_rows, d = xs.shape
    wb, we, lo, hi = items
    grid_spec = pltpu.PrefetchScalarGridSpec(
        num_scalar_prefetch=4,
        grid=(wb.shape[0],),
        in_specs=[pl.BlockSpec((MOE_TILE, d), lambda i, wb, we, lo, hi: (wb[i], 0)),
                  pl.BlockSpec((MOE_TILE, 1), lambda i, wb, we, lo, hi: (wb[i], 0)),
                  pl.BlockSpec((1, d, D_EXPERT), lambda i, wb, we, lo, hi: (we[i], 0, 0)),
                  pl.BlockSpec((1, d, D_EXPERT), lambda i, wb, we, lo, hi: (we[i], 0, 0)),
                  pl.BlockSpec((1, D_EXPERT, d), lambda i, wb, we, lo, hi: (we[i], 0, 0))],
        out_specs=pl.BlockSpec((MOE_TILE, d), lambda i, wb, we, lo, hi: (wb[i], 0)),
        scratch_shapes=[pltpu.VMEM((d, D_EXPERT), BF16), pltpu.VMEM((d, D_EXPERT), BF16),
                        pltpu.VMEM((D_EXPERT, d), BF16), pltpu.SMEM((1,), jnp.int32)],
    )
    return pl.pallas_call(
        _moe_kernel,
        grid_spec=grid_spec,
        out_shape=jax.ShapeDtypeStruct((n_rows, d), BF16),
        compiler_params=_cparams("arbitrary"),
        name="moe_ffn",
    )(wb, we, lo, hi, xs, sw, w1, w3, w2)


def _final_kernel(x_ref, gt_ref, y_ref, o_ref):
    d = x_ref.shape[2]
    o_ref[0] = x_ref[0] + gt_ref[0] * (y_ref[0, :, :d].astype(F32) + y_ref[0, :, d:].astype(F32))


def _final(x_new, gt2, y_pair):
    b, l, d = x_new.shape
    tm = _pick(l, (512, 256, 128))
    tok = lambda w: pl.BlockSpec((1, tm, w), lambda bi, i: (bi, i, 0))
    return pl.pallas_call(
        _final_kernel,
        grid=(b, l // tm),
        in_specs=[tok(d), pl.BlockSpec((1, 1, d), lambda bi, i: (bi, 0, 0)), tok(TOP_K * d)],
        out_specs=tok(d),
        out_shape=jax.ShapeDtypeStruct((b, l, d), F32),
        compiler_params=_cparams("parallel", "parallel"),
        name="moe_residual",
    )(x_new, gt2, y_pair)


def _pad_lora_cols(w, widths):
    parts, o = [], 0
    for wd in widths:
        blk = w[..., o:o + wd]
        parts.append(jnp.pad(blk, [(0, 0)] * (w.ndim - 1) + [(0, LORA_PAD - wd)]))
        o += wd
    return jnp.concatenate(parts, axis=-1)


def _rope_tables(l):
    half = DIFF_QK_DIM // 2
    inv_freq = ROPE_THETA ** (-jnp.arange(0, half, 2, dtype=F32) / half)
    t = jnp.arange(l, dtype=jnp.int32)
    rows = (t // GRID_W).astype(F32)[:, None] * inv_freq
    cols = (t % GRID_W).astype(F32)[:, None] * inv_freq
    cos64 = jnp.concatenate([jnp.cos(rows), jnp.cos(rows), jnp.cos(cols), jnp.cos(cols)], axis=1)
    sin64 = jnp.concatenate([-jnp.sin(rows), jnp.sin(rows), -jnp.sin(cols), jnp.sin(cols)], axis=1)
    return jnp.tile(cos64, (1, 2)), jnp.tile(sin64, (1, 2))


def _route(routed, n_tok):
    expert = routed[:, :TOP_K].astype(jnp.int32)
    gate = routed[:, TOP_K:2 * TOP_K]

    n_assign = n_tok * TOP_K
    assert n_assign % MOE_TILE == 0
    flat_e = expert.reshape(-1).astype(jnp.int32)
    ids = jnp.arange(n_assign, dtype=jnp.int32)
    sorted_e, order, sorted_w = lax.sort((flat_e, ids, gate.reshape(-1)), num_keys=1, is_stable=True)
    _, rank = lax.sort((order, ids), num_keys=1)
    ends = jnp.searchsorted(sorted_e, jnp.arange(N_EXPERTS, dtype=jnp.int32), side='right').astype(jnp.int32)
    starts = jnp.concatenate([jnp.zeros((1,), jnp.int32), ends[:-1]])

    nb = n_assign // MOE_TILE
    blk_lo = jnp.arange(nb, dtype=jnp.int32) * MOE_TILE
    e_first = jnp.searchsorted(ends, blk_lo, side='right').astype(jnp.int32)
    e_last = jnp.searchsorted(ends, blk_lo + MOE_TILE - 1, side='right').astype(jnp.int32)
    per_blk = e_last - e_first + 1
    cum = jnp.cumsum(per_blk)
    it = jnp.arange(nb + N_EXPERTS - 1, dtype=jnp.int32)
    wb = jnp.minimum(jnp.searchsorted(cum, it, side='right'), nb - 1).astype(jnp.int32)
    we = jnp.clip(e_first[wb] + it - (cum[wb] - per_blk[wb]), 0, N_EXPERTS - 1)
    lo = jnp.clip(starts[we], blk_lo[wb], blk_lo[wb] + MOE_TILE) - blk_lo[wb]
    hi = jnp.clip(ends[we], blk_lo[wb], blk_lo[wb] + MOE_TILE) - blk_lo[wb]
    hi = jnp.where(it < cum[-1], hi, lo)
    return order // TOP_K, sorted_w, rank.reshape(n_tok, TOP_K), (wb, we, lo, hi)


def kernel(x, c, ctx, c_ctx, ada_w, ada_b, norm1_g, norm2_g, w_in, shift_mu, rwkv_w0, rwkv_w2, rwkv_a0, rwkv_a2,
           rwkv_g2, rwkv_k_k, rwkv_k_a, rwkv_r_k, rwkv_lnx_g, rwkv_lnx_b, qn_g, kn_g, diff_lambda, subln_g,
           w_pa, w_pb, w_out, router_g_w, router_g_b, router_e_w, router_e_b, exp_w1, exp_w3, exp_w2):
    assert ada_w.shape[0] == 1, "single-layer block"
    b, l, d = x.shape
    lc = ctx.shape[1]
    lam_init = 0.8 - 0.6 * math.exp(-0.3 * 0)
    lv = diff_lambda[0].astype(F32)
    lam = jnp.exp(jnp.sum(lv[0] * lv[1])) - jnp.exp(jnp.sum(lv[2] * lv[3])) + lam_init

    rows = (b + 1 + SUBLANES - 1) // SUBLANES * SUBLANES
    cm = jnp.zeros((rows, d), F32).at[:b].set(c).at[b].set(c_ctx)
    mod = _modulation(cm, ada_w[0], ada_b[0])
    sh1, sc1, gt1, sh2, sc2, gt2 = [mod[:b, None, k * d:(k + 1) * d] for k in range(6)]
    csh1, csc1 = [jnp.broadcast_to(mod[b, k * d:(k + 1) * d], (b, 1, d)) for k in range(2)]

    w = w_in[0]
    lora_widths = (DECAY_LORA, DECAY_LORA, AAA_LORA, AAA_LORA)
    o_lora = 3 * RWKV_DIM
    o_glora = o_lora + sum(lora_widths)
    pad_cols = lambda m: jnp.concatenate(
        [m[..., :o_lora], _pad_lora_cols(m[..., o_lora:o_glora], lora_widths), m[..., o_glora:RWKV_COLS]], axis=-1)
    w_rwkv = pad_cols(w).astype(BF16)
    w_diff = w[:, RWKV_COLS:RWKV_COLS + DIFF_COLS].astype(BF16)
    w_gate = w[:, RWKV_COLS + DIFF_COLS:].astype(BF16)
    g1 = norm1_g[0]
    hx = _norm_mod(x, g1, sc1, sh1)
    hc = _norm_mod(ctx, g1, csc1, csh1)
    px_r, px_d, px_g = _proj(hx, w_rwkv, F32), _proj(hx, w_diff, BF16), _proj(hx, w_gate, BF16)
    pc_r, pc_d = _proj(hc, w_rwkv, F32), _proj(hc, w_diff, BF16)

    pad_rows = lambda m: jnp.pad(m, ((0, 0), (0, LORA_PAD - m.shape[1]), (0, 0)))
    rparams = (pad_cols(shift_mu[0])[None], rwkv_w0[0], pad_rows(rwkv_w2[0]), rwkv_a0[0], pad_rows(rwkv_a2[0]),
               rwkv_g2[0], rwkv_k_k[0][None], rwkv_k_a[0][None], rwkv_r_k[0].reshape(1, RWKV_DIM))
    y_sweep, bonus, gate = _rwkv_scans(px_r, pc_r, rparams)

    cos_t, sin_t = _rope_tables(l)
    qg = jnp.tile(qn_g[0], 2)[None]
    kg = jnp.tile(kn_g[0], 2)[None]
    q_x, k_x, v_x = _diff_prep(px_d, cos_t, sin_t, qg, kg, True)
    _, k_c, v_c = _diff_prep(pc_d, cos_t[:lc], sin_t[:lc], qg, kg, False)
    lam_vec = jnp.full((1, LANES), lam, F32)
    sg_vec = (subln_g[0] * (1.0 - lam_init))[None]
    score_bound = (1.05 * DIFF_QK_DIM * DIFF_SCALE * math.log2(math.e)
                   * jnp.max(jnp.abs(qn_g[0])) * jnp.max(jnp.abs(kn_g[0])))
    attn_args = (q_x, k_c, v_c, k_x, v_x, lam_vec, sg_vec)
    y_diff = lax.cond(score_bound <= SCORE_LOG2_LIMIT,
                      lambda a: _diff_attention(*a, bounded=True),
                      lambda a: _diff_attention(*a, bounded=False), attn_args)

    mixed = _merge(y_sweep, bonus, gate, y_diff, px_g, rwkv_lnx_g[0][None], rwkv_lnx_b[0][None],
                   w_pa[0].astype(BF16), w_pb[0].astype(BF16))
    n_r = N_GROUPS + N_EXPERTS
    w_router = jnp.zeros((d, ROUTER_PAD), F32).at[:, :N_GROUPS].set(router_g_w[0]).at[:, N_GROUPS:n_r].set(router_e_w[0])
    b_router = jnp.zeros((1, ROUTER_PAD), F32).at[0, :N_GROUPS].set(router_g_b[0]).at[0, N_GROUPS:n_r].set(router_e_b[0])
    w_router_hi = w_router.astype(BF16)
    w_router_lo = (w_router - w_router_hi.astype(F32)).astype(BF16)
    x_new, h2, routed = _outproj(mixed, x, gt1, norm2_g[0][None], sc2, sh2, w_out[0].astype(BF16),
                                 jnp.stack([w_router_hi, w_router_lo]), b_router)

    n_tok = b * l
    row_tok, row_w, row_of, items = _route(routed.reshape(n_tok, ROUTER_PAD), n_tok)
    xs = h2.reshape(n_tok, d)[row_tok]
    out = _moe_ffn(xs, row_w[:, None], items, exp_w1[0], exp_w3[0], exp_w2[0])
    y_pair = out[row_of.reshape(-1)].reshape(b, l, TOP_K * d)
    return _final(x_new, gt2, y_pair)
```

```python
import functools
import math

import jax
import jax.numpy as jnp
from jax import lax
from jax.experimental import pallas as pl
from jax.experimental.pallas import tpu as pltpu

F32 = jnp.float32
BF16 = jnp.bfloat16
HIGHEST = lax.Precision.HIGHEST

D_MODEL = 2048
GRID_W = 64
RWKV_HEADS = 16
RWKV_HEAD_DIM = 64
RWKV_DIM = RWKV_HEADS * RWKV_HEAD_DIM
DECAY_LORA = 96
AAA_LORA = 96
GATE_LORA = 256
RWKV_COLS = 3 * RWKV_DIM + 2 * DECAY_LORA + 2 * AAA_LORA + GATE_LORA
DIFF_HEADS = 8
DIFF_QK_DIM = 64
DIFF_V_DIM = 2 * DIFF_QK_DIM
DIFF_DIM = DIFF_HEADS * DIFF_V_DIM
DIFF_QK_COLS = DIFF_HEADS * 2 * DIFF_QK_DIM
DIFF_COLS = 2 * DIFF_QK_COLS + DIFF_DIM
DIFF_SCALE = DIFF_QK_DIM ** -0.5
ROPE_THETA = 10000.0
GATE_COLS = 2 * D_MODEL
N_GROUPS = 4
EXPERTS_PER_GROUP = 8
N_EXPERTS = N_GROUPS * EXPERTS_PER_GROUP
TOP_K = 2
D_EXPERT = 512
NORM_EPS = 1e-6
SUBLN_EPS = 1e-5
LNX_EPS = 64e-5

LANES = 128
SUBLANES = 8
VMEM_LIMIT_BYTES = 56 * 1024 * 1024

LORA_PAD = LANES
RWKV_PCOLS = 3 * RWKV_DIM + 4 * LORA_PAD + GATE_LORA
CHUNK = 64
V_EXT =2 * DIFF_V_DIM
SCORE_LOG2_LIMIT = 60.0
ROUTER_PAD = LANES
MOE_TILE = 512


def _cparams(*sem):
    return pltpu.CompilerParams(dimension_semantics=sem, vmem_limit_bytes=VMEM_LIMIT_BYTES)


def _sigmoid(x):
    return 1.0 / (1.0 + jnp.exp(-x))


def _dot(a, b, dims=(((1,), (0,)), ((), ()))):
    return lax.dot_general(a.astype(BF16), b.astype(BF16), dims, preferred_element_type=F32)


def _dot_f32(a, b, dims=(((1,), (0,)), ((), ()))):
    return lax.dot_general(a, b, dims, precision=HIGHEST, preferred_element_type=F32)


_NT = (((1,), (1,)), ((), ()))
_TN = (((0,), (0,)), ((), ()))


def _bf16_parts(x, n):
    parts = []
    for _ in range(n):
        p = x.astype(BF16)
        parts.append(p)
        x = x - p.astype(F32)
    return parts


def _dot_split(x, w_b, n, lhs=True):
    parts = _bf16_parts(x, n)
    outs = [jnp.dot(p, w_b, preferred_element_type=F32) if lhs else jnp.dot(w_b, p, preferred_element_type=F32)
            for p in parts]
    return functools.reduce(lambda a, b: a + b, outs)


def _group_ones(width, group):
    r = lax.broadcasted_iota(jnp.int32, (width, width), 0) // group
    c = lax.broadcasted_iota(jnp.int32, (width, width), 1) // group
    return (r == c).astype(BF16)


def _group_sum(x, group, pieces=2):
    ones = _group_ones(LANES, group)
    parts = [_dot_split(x[:, j * LANES:(j + 1) * LANES], ones, pieces) for j in range(x.shape[1] // LANES)]
    return parts[0] if len(parts) == 1 else jnp.concatenate(parts, axis=1)


def _mod_kernel(c_ref, w_ref, b_ref, o_ref):
    c = c_ref[...]
    o_ref[...] = _dot_f32(c * _sigmoid(c), w_ref[...]) + b_ref[...]


def _modulation(cm, ada_w, ada_b):
    rows, d = cm.shape
    n = ada_w.shape[1]
    tn = 1536
    return pl.pallas_call(
        _mod_kernel,
        grid=(n // tn,),
        in_specs=[pl.BlockSpec((rows, d), lambda j: (0, 0)),
                  pl.BlockSpec((d, tn), lambda j: (0, j)),
                  pl.BlockSpec((1, tn), lambda j: (0, j))],
        out_specs=pl.BlockSpec((rows, tn), lambda j: (0, j)),
        out_shape=jax.ShapeDtypeStruct((rows, n), F32),
        compiler_params=_cparams("parallel"),
        name="modulation",
    )(cm, ada_w, ada_b.reshape(1, n))


def _norm_mod_kernel(x_ref, g_ref, sc_ref, sh_ref, h_ref):
    x = x_ref[0]
    y = x * lax.rsqrt(jnp.mean(x * x, axis=-1, keepdims=True) + NORM_EPS) * g_ref[...]
    h_ref[0] = (y * (1.0 + sc_ref[0]) + sh_ref[0]).astype(BF16)


def _pick(n, prefs):
    for t in prefs:
        if n % t == 0:
            return t
    return n


def _norm_mod(x, g, sc, sh):
    b, l, d = x.shape
    tm = _pick(l, (512, 256, 128))
    return pl.pallas_call(
        _norm_mod_kernel,
        grid=(b, l // tm),
        in_specs=[pl.BlockSpec((1, tm, d), lambda bi, i: (bi, i, 0)),
                  pl.BlockSpec((1, d), lambda bi, i: (0, 0)),
                  pl.BlockSpec((1, 1, d), lambda bi, i: (bi, 0, 0)),
                  pl.BlockSpec((1, 1, d), lambda bi, i: (bi, 0, 0))],
        out_specs=pl.BlockSpec((1, tm, d), lambda bi, i: (bi, i, 0)),
        out_shape=jax.ShapeDtypeStruct((b, l, d), BF16),
        compiler_params=_cparams("parallel", "parallel"),
        name="norm_mod",
    )(x, g.reshape(1, d), sc, sh)


def _proj_kernel(h_ref, w_ref, o_ref):
    o_ref[0] = jnp.dot(h_ref[0], w_ref[...], preferred_element_type=F32).astype(o_ref.dtype)


def _proj(h, w, out_dtype):
    b, l, d = h.shape
    n = w.shape[1]
    tm = _pick(l, (1024, 512, 256, 128))
    tn = _pick(n, (1024, 768, 512, 256, 128))
    return pl.pallas_call(
        _proj_kernel,
        grid=(b, l // tm, n // tn),
        in_specs=[pl.BlockSpec((1, tm, d), lambda bi, i, j: (bi, i, 0)),
                  pl.BlockSpec((d, tn), lambda bi, i, j: (0, j))],
        out_specs=pl.BlockSpec((1, tm, tn), lambda bi, i, j: (bi, i, j)),
        out_shape=jax.ShapeDtypeStruct((b, l, n), out_dtype),
        compiler_params=_cparams("parallel", "parallel", "parallel"),
        name="proj",
    )(h, w)


def _pair_diag(x_b, mask_b):
    return jnp.concatenate([x_b, x_b], axis=0) * mask_b


def _tri_inverse_pairs(a_list, eye_f, mask_b):
    n = CHUNK
    mm = lambda l, r: jnp.dot(l.astype(BF16), _pair_diag(r.astype(BF16), mask_b), preferred_element_type=F32)
    xs = [eye_f + a for a in a_list]
    ps = [mm(a, a) for a in a_list]
    steps = int(math.log2(n)) - 1
    for s in range(steps):
        if s < steps - 1:
            xps = [mm(jnp.concatenate([x, p], axis=0), p) for x, p in zip(xs, ps)]
            xs = [x + xp[:n] for x, xp in zip(xs, xps)]
            ps = [xp[n:] for xp in xps]
        else:
            xs = [x + mm(x, p) for x, p in zip(xs, ps)]
    return xs


def _rwkv_chunk_kernel(p_ref, pp_ref, pn_ref, mu_ref, w0_ref, w2_ref, a0_ref, a2_ref, g2_ref, kk_ref, ka_ref,
                       rk_ref, q_ref, y0_ref, m_ref, n_ref, bonus_ref, gate_ref):
    i = pl.program_id(1)
    last = pl.num_programs(1) - 1
    c = RWKV_DIM
    hd = RWKV_HEAD_DIM
    p = p_ref[0]
    row = lax.broadcasted_iota(jnp.int32, (CHUNK, 1), 0)
    prev_row = jnp.where(i == 0, 0.0, pp_ref[0, SUBLANES - 1:SUBLANES, :])
    next_row = jnp.where(i == last, 0.0, pn_ref[0, 0:1, :])
    prev = jnp.where(row == 0, prev_row, pltpu.roll(p, 1, axis=0))
    nxt = jnp.where(row == CHUNK - 1, next_row, pltpu.roll(p, CHUNK - 1, axis=0))
    ps = p + mu_ref[...] * (0.5 * (prev + nxt) - p)

    r, k, v = ps[:, :c], ps[:, c:2 * c], ps[:, 2 * c:3 * c]
    o = 3 * c
    xw = (ps[:, o:o + LORA_PAD], ps[:, o + LORA_PAD:o + 2 * LORA_PAD])
    xa = (ps[:, o + 2 * LORA_PAD:o + 3 * LORA_PAD], ps[:, o + 3 * LORA_PAD:o + 4 * LORA_PAD])
    xg = ps[:, o + 4 * LORA_PAD:]

    gate_ref[0] = _dot(_sigmoid(xg), g2_ref[...]).astype(gate_ref.dtype)
    kk = k * kk_ref[...]
    kk = kk * lax.rsqrt(_group_sum(kk * kk, hd, pieces=1) + 1e-12)

    tr = lax.broadcasted_iota(jnp.int32, (CHUNK, CHUNK), 0)
    tc = lax.broadcasted_iota(jnp.int32, (CHUNK, CHUNK), 1)
    tr2 = lax.broadcasted_iota(jnp.int32, (CHUNK, LANES), 0)
    lane2 = lax.broadcasted_iota(jnp.int32, (CHUNK, LANES), 1)
    tc2 = lane2 % CHUNK
    lane_head = lane2 // hd
    eye2 = tr2 == tc2
    mask_b = _group_ones(LANES, hd)
    kd_sum = jnp.zeros_like(k)
    v_b = v.astype(BF16)
    psl = [slice(p * LANES, (p + 1) * LANES) for p in range(c // LANES)]
    before2, upto2, prep = [], [], []
    for d in range(2):
        before2.append((tc2 < tr2) if d == 0 else (tc2 > tr2))
        upto = (tc <= tr) if d == 0 else (tc >= tr)
        upto2.append((tc2 <= tr2) if d == 0 else (tc2 >= tr2))
        z = w0_ref[d:d + 1, :] + _dot(jnp.tanh(xw[d]), w2_ref[d])
        w_log = -(jnp.maximum(-z, 0.0) + jnp.log(1.0 + jnp.exp(-jnp.abs(z)))) - 0.5
        logw = -jnp.exp(w_log)
        a = _sigmoid(a0_ref[d:d + 1, :] + _dot(xa[d], a2_ref[d]))
        kd = k * (1.0 + (a - 1.0) * ka_ref[...])
        kd_sum = kd_sum + kd
        cum = _dot_split(logw, upto.astype(BF16), 3, lhs=False)
        total = cum[CHUNK - 1:CHUNK, :] if d == 0 else cum[0:1, :]
        e_neg = jnp.exp(-cum)
        e_rest = jnp.exp(total - cum)
        p_total = jnp.exp(total)
        beta = kk * a
        al = -kk * jnp.exp(cum - logw)
        rt = r * jnp.exp(cum)
        al_b = al.astype(BF16)
        rt_b = rt.astype(BF16)
        bt_b = (beta * e_neg).astype(BF16)
        kt_b = (kd * e_neg).astype(BF16)
        bh_b = (beta * e_rest).astype(BF16)
        kh_b = (kd * e_rest).astype(BF16)
        prep.append((al_b, rt_b, bt_b, kt_b, bh_b, kh_b, rt, p_total))

    dp = [(d, s) for d in range(2) for s in psl]
    fdot = lambda a, b: jnp.dot(a, b, preferred_element_type=F32)
    bd = lambda x: _pair_diag(x, mask_b)
    v_bd = [bd(v_b[:, s]) for s in psl] * 2
    zeros_sq = jnp.zeros((LANES, LANES), BF16)
    zeros_tl = jnp.zeros((CHUNK, LANES), BF16)
    gm = [lax.dot_general(jnp.concatenate([prep[d][0][:, s], prep[d][1][:, s]], axis=0),
                          jnp.concatenate([bd(prep[d][2][:, s]), bd(prep[d][3][:, s])], axis=0),
                          _NT, preferred_element_type=F32) for d, s in dp]
    a_ab = [jnp.where(before2[d], x[:CHUNK, :LANES], 0.0) for x, (d, s) in zip(gm, dp)]
    a_ak = [jnp.where(before2[d], x[:CHUNK, LANES:], 0.0).astype(BF16) for x, (d, s) in zip(gm, dp)]
    a_rb = [jnp.where(upto2[d], x[CHUNK:, :LANES], 0.0).astype(BF16) for x, (d, s) in zip(gm, dp)]
    a_rk = [jnp.where(upto2[d], x[CHUNK:, LANES:], 0.0).astype(BF16) for x, (d, s) in zip(gm, dp)]
    akv = [fdot(ak, vd) for ak, vd in zip(a_ak, v_bd)]
    t_inv = _tri_inverse_pairs(a_ab, eye2.astype(F32), mask_b)
    wu = [fdot(t.astype(BF16), jnp.concatenate([bd(prep[d][0][:, s]), bd(u.astype(BF16))], axis=1))
          for t, u, (d, s) in zip(t_inv, akv, dp)]
    w_b = [x[:, :LANES].astype(BF16) for x in wu]
    u_b = [x[:, LANES:].astype(BF16) for x in wu]
    qy = [fdot(jnp.concatenate([rb, rk], axis=1),
               jnp.concatenate([jnp.concatenate([bd(w), bd(u)], axis=1),
                                jnp.concatenate([zeros_sq, vd], axis=1)], axis=0))
          for rb, rk, w, u, vd in zip(a_rb, a_rk, w_b, u_b, v_bd)]
    full = [lax.dot_general(jnp.concatenate([jnp.concatenate([w, u], axis=1),
                                             jnp.concatenate([zeros_tl, v_b[:, s]], axis=1)], axis=0),
                            jnp.concatenate([prep[d][4][:, s], prep[d][5][:, s]], axis=0),
                            _TN, preferred_element_type=F32)
            for w, u, (d, s) in zip(w_b, u_b, dp)]

    def diag_blocks(x):
        return jnp.where(lane_head == 0, x[:hd], 0.0) + jnp.where(lane_head == 1, x[hd:2 * hd], 0.0)

    npair = len(psl)
    for d in range(2):
        sl = slice(d * npair, (d + 1) * npair)
        rt, p_total = prep[d][6], prep[d][7]
        q_ref[0, d] = (rt + jnp.concatenate([x[:, :LANES] for x in qy[sl]], axis=1)).astype(q_ref.dtype)
        y0_ref[0, d] = jnp.concatenate([x[:, LANES:] for x in qy[sl]], axis=1).astype(y0_ref.dtype)
        m_ref[0, d, 0] = jnp.concatenate([jnp.where(eye2, p_total[:, s], 0.0) + diag_blocks(x[:LANES])
                                          for x, s in zip(full[sl], psl)], axis=1).astype(m_ref.dtype)
        n_ref[0, d, 0] = jnp.concatenate([diag_blocks(x[LANES:]) for x in full[sl]], axis=1)
    bonus_ref[0] = (_group_sum(r * kd_sum * rk_ref[...], hd) * v).astype(bonus_ref.dtype)


def _rwkv_chunk_ops(p, params):
    b, l, pc = p.shape
    nc = l // CHUNK
    c = RWKV_DIM
    hb = CHUNK // SUBLANES
    nb8 = l // SUBLANES
    mu, w0, w2, a0, a2, g2, k_k, k_a, r_k = params
    const = lambda shape: pl.BlockSpec(shape, lambda bi, i: (0,) * len(shape))
    tok = lambda: pl.BlockSpec((1, CHUNK, c), lambda bi, i: (bi, i, 0))
    return pl.pallas_call(
        _rwkv_chunk_kernel,
        grid=(b, nc),
        in_specs=[pl.BlockSpec((1, CHUNK, pc), lambda bi, i: (bi, i, 0)),
                  pl.BlockSpec((1, SUBLANES, pc), lambda bi, i: (bi, jnp.maximum(i * hb - 1, 0), 0)),
                  pl.BlockSpec((1, SUBLANES, pc), lambda bi, i: (bi, jnp.minimum((i + 1) * hb, nb8 - 1), 0)),
                  const((1, pc)), const((2, c)), const((2, LORA_PAD, c)), const((2, c)), const((2, LORA_PAD, c)),
                  const((GATE_LORA, c)), const((1, c)), const((1, c)), const((1, c))],
        out_specs=[pl.BlockSpec((1, 2, CHUNK, c), lambda bi, i: (bi, 0, i, 0)),
                   pl.BlockSpec((1, 2, CHUNK, c), lambda bi, i: (bi, 0, i, 0)),
                   pl.BlockSpec((1, 2, 1, RWKV_HEAD_DIM, c), lambda bi, i: (bi, 0, i, 0, 0)),
                   pl.BlockSpec((1, 2, 1, RWKV_HEAD_DIM, c), lambda bi, i: (bi, 0, i, 0, 0)),
                   tok(), tok()],
        out_shape=[jax.ShapeDtypeStruct((b, 2, l, c), BF16), jax.ShapeDtypeStruct((b, 2, l, c), BF16),
                   jax.ShapeDtypeStruct((b, 2, nc, RWKV_HEAD_DIM, c), BF16),
                   jax.ShapeDtypeStruct((b, 2, nc, RWKV_HEAD_DIM, c), F32),
                   jax.ShapeDtypeStruct((b, l, c), BF16), jax.ShapeDtypeStruct((b, l, c), BF16)],
        compiler_params=_cparams("parallel", "parallel"),
        name="rwkv_chunk_ops",
    )(p, p, p, mu, w0, w2, a0, a2, g2, k_k, k_a, r_k)


def _rwkv_sweep_kernel(qf_ref, y0f_ref, mf_ref, nf_ref, qb_ref, y0b_ref, mb_ref, nb_ref, s0_ref,
                       yf_ref, yb_ref, sfin_ref, s_ref):
    j = pl.program_id(1)
    hd = RWKV_HEAD_DIM

    @pl.when(j == 0)
    def _():
        s_ref[...] = s0_ref[0]

    dirs = ((qf_ref, y0f_ref, mf_ref, nf_ref, yf_ref), (qb_ref, y0b_ref, mb_ref, nb_ref, yb_ref))
    psl = [slice(p * LANES, (p + 1) * LANES) for p in range(RWKV_DIM // LANES)]
    mask_b = _group_ones(LANES, hd)
    pair_diag = lambda x: _pair_diag(x, mask_b)
    per_step = mf_ref.shape[2]
    state = [s_ref[0], s_ref[1]]
    for u in range(per_step):
        for d, (q_ref, y0_ref, m_ref, n_ref, y_ref) in enumerate(dirs):
            cu = u if d == 0 else per_step - 1 - u
            rows = slice(cu * CHUNK, (cu + 1) * CHUNK)
            s_b = state[d].astype(BF16)
            q_b = q_ref[0, 0, rows, :]
            m_b = m_ref[0, 0, cu]
            ys = [_dot(q_b[:, s], pair_diag(s_b[:, s]), _NT) for s in psl]
            sm = [_dot(s_b[:, s], pair_diag(m_b[:, s])) for s in psl]
            y_ref[0, rows, :] = (y0_ref[0, 0, rows, :].astype(F32) + jnp.concatenate(ys, axis=1)).astype(y_ref.dtype)
            state[d] = n_ref[0, 0, cu] + jnp.concatenate(sm, axis=1)
    s_ref[0] = state[0]
    s_ref[1] = state[1]

    @pl.when(j == pl.num_programs(1) - 1)
    def _():
        sfin_ref[0] = s_ref[...]


def _rwkv_sweep(q, y0, m, n, s0):
    b, _, l, c = q.shape
    nc = l // CHUNK
    hd = RWKV_HEAD_DIM
    g = _pick(nc, (4, 2, 1))
    ns = nc // g
    tokf = lambda: pl.BlockSpec((1, 1, g * CHUNK, c), lambda bi, j: (bi, 0, j, 0))
    tokb = lambda: pl.BlockSpec((1, 1, g * CHUNK, c), lambda bi, j: (bi, 1, ns - 1 - j, 0))
    opf = lambda: pl.BlockSpec((1, 1, g, hd, c), lambda bi, j: (bi, 0, j, 0, 0))
    opb = lambda: pl.BlockSpec((1, 1, g, hd, c), lambda bi, j: (bi, 1, ns - 1 - j, 0, 0))
    return pl.pallas_call(
        _rwkv_sweep_kernel,
        grid=(b, ns),
        in_specs=[tokf(), tokf(), opf(), opf(), tokb(), tokb(), opb(), opb(),
                  pl.BlockSpec((1, 2, hd, c), lambda bi, j: (bi, 0, 0, 0))],
        out_specs=[pl.BlockSpec((1, g * CHUNK, c), lambda bi, j: (bi, j, 0)),
                   pl.BlockSpec((1, g * CHUNK, c), lambda bi, j: (bi, ns - 1 - j, 0)),
                   pl.BlockSpec((1, 2, hd, c), lambda bi, j: (bi, 0, 0, 0))],
        out_shape=[jax.ShapeDtypeStruct((b, l, c), BF16), jax.ShapeDtypeStruct((b, l, c), BF16),
                   jax.ShapeDtypeStruct((b, 2, hd, c), F32)],
        scratch_shapes=[pltpu.VMEM((2, hd, c), F32)],
        compiler_params=_cparams("parallel", "arbitrary"),
        name="rwkv_sweep",
    )(q, y0, m, n, q, y0, m, n, s0)


def _rwkv_scans(px_r, pc_r, rparams):
    b = px_r.shape[0]
    qc, y0c, mc, nc_, _, _ = _rwkv_chunk_ops(pc_r, rparams)
    s_zero = jnp.zeros((b, 2, RWKV_HEAD_DIM, RWKV_DIM), F32)
    _, _, s_ctx = _rwkv_sweep(qc, y0c, mc, nc_, s_zero)
    qx, y0x, mx, nx, bonus, gate = _rwkv_chunk_ops(px_r, rparams)
    yf, yb, _ = _rwkv_sweep(qx, y0x, mx, nx, s_ctx)
    return (yf, yb), bonus, gate


def _diff_prep_kernel(p_ref, cos_ref, sin_ref, qg_ref, kg_ref, q_ref, k_ref, v_ref, *, rope):
    lane = lax.broadcasted_iota(jnp.int32, (1, LANES), 1)
    first = (lane % 32) < 16
    for hd in range(DIFF_HEADS):
        for off, g_ref, o_ref, scale in ((0, qg_ref, q_ref, DIFF_SCALE * math.log2(math.e)),
                                         (DIFF_QK_COLS, kg_ref, k_ref, 1.0)):
            cs = slice(hd * LANES, (hd + 1) * LANES)
            xb = p_ref[0, :, off + hd * LANES:off + (hd + 1) * LANES].astype(F32)
            ms = _group_sum(xb * xb, DIFF_QK_DIM, pieces=1) * (1.0 / DIFF_QK_DIM)
            y = xb * lax.rsqrt(ms + NORM_EPS) * g_ref[...]
            if rope:
                swapped = jnp.where(first, pltpu.roll(y, LANES - 16, axis=1), pltpu.roll(y, 16, axis=1))
                y = y * cos_ref[...] + swapped * sin_ref[...]
            o_ref[0, :, cs] = (y * scale).astype(BF16)
    ones = jnp.ones((p_ref.shape[1], V_EXT - DIFF_V_DIM), BF16)
    for hd in range(DIFF_HEADS):
        vb = p_ref[0, :, 2 * DIFF_QK_COLS + hd * DIFF_V_DIM:2 * DIFF_QK_COLS + (hd + 1) * DIFF_V_DIM]
        v_ref[0, :, hd * V_EXT:(hd + 1) * V_EXT] = jnp.concatenate([vb.astype(BF16), ones], axis=1)


def _diff_prep(p, cos_t, sin_t, qg, kg, rope):
    b, l, pc = p.shape
    t = _pick(l, (512, 256, 128))
    tok = lambda w=DIFF_DIM: pl.BlockSpec((1, t, w), lambda bi, i: (bi, i, 0))
    shp = jax.ShapeDtypeStruct((b, l, DIFF_DIM), BF16)
    shp_v = jax.ShapeDtypeStruct((b, l, DIFF_HEADS * V_EXT), BF16)
    return pl.pallas_call(
        functools.partial(_diff_prep_kernel, rope=rope),
        grid=(b, l // t),
        in_specs=[pl.BlockSpec((1, t, pc), lambda bi, i: (bi, i, 0)),
                  pl.BlockSpec((t, LANES), lambda bi, i: (i, 0)),
                  pl.BlockSpec((t, LANES), lambda bi, i: (i, 0)),
                  pl.BlockSpec((1, LANES), lambda bi, i: (0, 0)),
                  pl.BlockSpec((1, LANES), lambda bi, i: (0, 0))],
        out_specs=[tok(), tok(), tok(DIFF_HEADS * V_EXT)],
        out_shape=[shp, shp, shp_v],
        compiler_params=_cparams("parallel", "parallel"),
        name="diff_prep_rope" if rope else "diff_prep",
    )(p, cos_t, sin_t, qg, kg)


def _diff_finish(acc1, acc2, lam_ref, sg_ref, o_ref):
    dv = DIFF_V_DIM
    o = acc1[:, :dv] / acc1[:, dv:] - lam_ref[...] * (acc2[:, :dv] / acc2[:, dv:])
    o = o * lax.rsqrt(jnp.mean(o * o, axis=-1, keepdims=True) + SUBLN_EPS)
    o_ref[0] = (o * sg_ref[...]).astype(o_ref.dtype)


def _flash_online_kernel(lam_ref, sg_ref, q_ref, kc_ref, vc_ref, k_ref, v_ref, o_ref, *, tk):
    qd = DIFF_QK_DIM
    tq = q_ref.shape[1]
    q = q_ref[0]
    qs = (q[:, :qd], q[:, qd:])

    def absorb(state, k, v):
        reps = k.shape[0] // LANES
        out = []
        for mp in range(2):
            m_prev, acc = state[mp]
            s = lax.dot_general(qs[mp], k[:, mp * qd:(mp + 1) * qd], _NT, preferred_element_type=F32)
            m_new = jnp.maximum(m_prev, jnp.max(s, axis=-1, keepdims=True))
            alpha = jnp.exp2(m_prev - m_new)
            pr = jnp.exp2(s - jnp.concatenate([m_new] * reps, axis=1))
            acc_new = jnp.concatenate([alpha, alpha], axis=1) * acc + jnp.dot(pr.astype(BF16), v, preferred_element_type=F32)
            out.append((m_new, acc_new))
        return tuple(out)

    init = (jnp.full((tq, LANES), -jnp.inf, F32), jnp.zeros((tq, V_EXT), F32))
    state = absorb((init, init), kc_ref[0], vc_ref[0])

    def body(j, state):
        rows = pl.ds(pl.multiple_of(j * tk, tk), tk)
        return absorb(state, k_ref[0, rows, :], v_ref[0, rows, :])

    (_, acc1), (_, acc2) = lax.fori_loop(0, k_ref.shape[1] // tk, body, state)
    _diff_finish(acc1, acc2, lam_ref, sg_ref, o_ref)


def _flash_bounded_kernel(lam_ref, sg_ref, q_ref, kc_ref, vc_ref, k_ref, v_ref, o_ref, acc_ref, p_ref, *, tk):
    qd = DIFF_QK_DIM
    q = q_ref[0]
    qs = (q[:, :qd], q[:, qd:])
    n_kv = k_ref.shape[1] // tk

    def weights(mp, k):
        s = lax.dot_general(qs[mp], k[:, mp * qd:(mp + 1) * qd], _NT, preferred_element_type=F32)
        return jnp.exp2(s).astype(BF16)

    def chunk(c):
        return pl.ds(pl.multiple_of(jnp.minimum(c, n_kv - 1) * tk, tk), tk)

    for mp in range(2):
        acc_ref[mp] = jnp.dot(weights(mp, kc_ref[0]), vc_ref[0], preferred_element_type=F32)
        p_ref[0, mp] = weights(mp, k_ref[0, pl.ds(0, tk), :])

    steps = 4 if n_kv % 4 == 0 else 2

    def body(j, carry):
        for u in range(steps):
            c = steps * j + u
            v = v_ref[0, chunk(c), :]
            k_next = k_ref[0, chunk(c + 1), :]
            for mp in range(2):
                acc_ref[mp] += jnp.dot(p_ref[u % 2, mp], v, preferred_element_type=F32)
                p_ref[1 - u % 2, mp] = weights(mp, k_next)
        return carry

    lax.fori_loop(0, n_kv // steps, body, 0)
    _diff_finish(acc_ref[0], acc_ref[1], lam_ref, sg_ref, o_ref)


def _diff_attention(q, kc, vc, k, v, lam_vec, sg_vec, bounded):
    b, l, _ = q.shape
    lc = kc.shape[1]
    tq = _pick(l, (256, 128))
    tk = _pick(l, (512, 256, 128))
    if bounded:
        tq = _pick(l, (1024, 512, 256, 128))
        tk = _pick(l // 2, (512, 256, 128))
        assert l % (2 * tk) == 0
        body = functools.partial(_flash_bounded_kernel, tk=tk)
        scratch = [pltpu.VMEM((2, tq, V_EXT), F32), pltpu.VMEM((2, 2, tq, tk), BF16)]
    else:
        body = functools.partial(_flash_online_kernel, tk=tk)
        scratch = []
    return pl.pallas_call(
        body,
        grid=(b, DIFF_HEADS, l // tq),
        in_specs=[pl.BlockSpec((1, LANES), lambda bi, h, i: (0, 0)),
                  pl.BlockSpec((1, LANES), lambda bi, h, i: (0, 0)),
                  pl.BlockSpec((1, tq, LANES), lambda bi, h, i: (bi, i, h)),
                  pl.BlockSpec((1, lc, LANES), lambda bi, h, i: (bi, 0, h)),
                  pl.BlockSpec((1, lc, V_EXT), lambda bi, h, i: (bi, 0, h)),
                  pl.BlockSpec((1, l, LANES), lambda bi, h, i: (bi, 0, h)),
                  pl.BlockSpec((1, l, V_EXT), lambda bi, h, i: (bi, 0, h))],
        out_specs=pl.BlockSpec((1, tq, LANES), lambda bi, h, i: (bi, i, h)),
        out_shape=jax.ShapeDtypeStruct((b, l, DIFF_DIM), BF16),
        scratch_shapes=scratch,
        compiler_params=_cparams("parallel", "parallel", "arbitrary"),
        name="diff_flash_bounded" if bounded else "diff_flash_online",
    )(lam_vec, sg_vec, q, kc, vc, k, v)


def _merge_kernel(yf_ref, yb_ref, bonus_ref, gate_ref, yd_ref, pg_ref, lg_ref, lb_ref, wpa_ref, wpb_ref, o_ref):
    hd = RWKV_HEAD_DIM
    y = yf_ref[0].astype(F32) + yb_ref[0].astype(F32)
    dev = y - _group_sum(y, hd) * (1.0 / hd)
    var = _group_sum(dev * dev, hd, pieces=1) * (1.0 / hd)
    yn = dev * lax.rsqrt(var + LNX_EPS) * lg_ref[...] + lb_ref[...]
    y_rwkv = (yn + bonus_ref[0].astype(F32)) * gate_ref[0].astype(F32)
    a = _dot(y_rwkv, wpa_ref[...])
    bb = jnp.dot(yd_ref[0], wpb_ref[...], preferred_element_type=F32)
    ga = _sigmoid(pg_ref[0, :, :D_MODEL].astype(F32))
    gb = _sigmoid(pg_ref[0, :, D_MODEL:].astype(F32))
    o_ref[0] = (ga * a + gb * bb).astype(BF16)


def _merge(y_sweep, bonus, gate, y_diff, p_gate, lnx_g, lnx_b, w_pa, w_pb):
    b, l, c = bonus.shape
    d = D_MODEL
    tm = _pick(l, (256, 128))
    const = lambda shape: pl.BlockSpec(shape, lambda bi, i: (0,) * len(shape))
    tok = lambda w: pl.BlockSpec((1, tm, w), lambda bi, i: (bi, i, 0))
    return pl.pallas_call(
        _merge_kernel,
        grid=(b, l // tm),
        in_specs=[tok(c), tok(c), tok(c), tok(c), tok(DIFF_DIM), tok(GATE_COLS),
                  const((1, c)), const((1, c)), const((c, d)), const((DIFF_DIM, d))],
        out_specs=tok(d),
        out_shape=jax.ShapeDtypeStruct((b, l, d), BF16),
        compiler_params=_cparams("parallel", "parallel"),
        name="merge",
    )(y_sweep[0], y_sweep[1], bonus, gate, y_diff, p_gate, lnx_g, lnx_b, w_pa, w_pb)


def _outproj_kernel(mx_ref, x_ref, gt_ref, g_ref, sc_ref, sh_ref, wo_ref, wr_ref, br_ref, xn_ref, h_ref, rt_ref):
    mix = jnp.dot(mx_ref[0], wo_ref[...], preferred_element_type=F32)
    xn = x_ref[0] + gt_ref[0] * mix
    xn_ref[0] = xn
    y = xn * lax.rsqrt(jnp.mean(xn * xn, axis=-1, keepdims=True) + NORM_EPS) * g_ref[...]
    h = y * (1.0 + sc_ref[0]) + sh_ref[0]
    h_hi, h_lo = _bf16_parts(h, 2)
    h_ref[0] = h_hi
    dot = lambda a, b: jnp.dot(a, b, preferred_element_type=F32)
    lg = dot(h_hi, wr_ref[0]) + (dot(h_lo, wr_ref[0]) + dot(h_hi, wr_ref[1])) + br_ref[...]
    rt_ref[0] = _route_rows(lg)


def _route_rows(lg):
    lane = lax.broadcasted_iota(jnp.int32, lg.shape, 1).astype(F32)
    neg = jnp.float32(-3.0e38)
    far = jnp.float32(LANES)
    gl = jnp.where(lane < N_GROUPS, lg, neg)
    g_max = jnp.max(gl, axis=-1, keepdims=True)
    g_top = jnp.min(jnp.where(gl == g_max, lane, far), axis=-1, keepdims=True)
    p_g = 1.0 / jnp.sum(jnp.where(lane < N_GROUPS, jnp.exp(gl - g_max), 0.0), axis=-1, keepdims=True)
    e_lo = N_GROUPS + EXPERTS_PER_GROUP * g_top
    el = jnp.where((lane >= e_lo) & (lane < e_lo + EXPERTS_PER_GROUP), lg, neg)
    m1 = jnp.max(el, axis=-1, keepdims=True)
    i1 = jnp.min(jnp.where(el == m1, lane, far), axis=-1, keepdims=True)
    el2 = jnp.where(lane == i1, neg, el)
    m2 = jnp.max(el2, axis=-1, keepdims=True)
    i2 = jnp.min(jnp.where(el2 == m2, lane, far), axis=-1, keepdims=True)
    e2 = jnp.exp(m2 - m1)
    gate1 = p_g / (1.0 + e2)
    gate2 = p_g * e2 / (1.0 + e2)
    return jnp.where(lane == 0, i1 - N_GROUPS,
                     jnp.where(lane == 1, i2 - N_GROUPS, jnp.where(lane == 2, gate1, jnp.where(lane == 3, gate2, 0.0))))


def _outproj(mixed, x, gt1, g2, sc2, sh2, w_out, w_router, b_router):
    b, l, d = x.shape
    tm = _pick(l, (512, 256, 128))
    const = lambda shape: pl.BlockSpec(shape, lambda bi, i: (0,) * len(shape))
    tok = lambda w: pl.BlockSpec((1, tm, w), lambda bi, i: (bi, i, 0))
    per_b = lambda: pl.BlockSpec((1, 1, d), lambda bi, i: (bi, 0, 0))
    return pl.pallas_call(
        _outproj_kernel,
        grid=(b, l // tm),
        in_specs=[tok(d), tok(d), per_b(), const((1, d)), per_b(), per_b(),
                  const((d, d)), const((2, d, ROUTER_PAD)), const((1, ROUTER_PAD))],
        out_specs=[tok(d), tok(d), tok(ROUTER_PAD)],
        out_shape=[jax.ShapeDtypeStruct((b, l, d), F32), jax.ShapeDtypeStruct((b, l, d), BF16),
                   jax.ShapeDtypeStruct((b, l, ROUTER_PAD), F32)],
        compiler_params=_cparams("parallel", "parallel"),
        name="outproj_router",
    )(mixed, x, gt1, g2, sc2, sh2, w_out, w_router, b_router)


def _moe_kernel(wb_ref, we_ref, lo_ref, hi_ref, x_ref, sw_ref, w1_ref, w3_ref, w2_ref, o_ref,
                w1b_ref, w3b_ref, w2b_ref, cached_ref):
    i = pl.program_id(0)
    lo, hi = lo_ref[i], hi_ref[i]
    live = hi > lo

    @pl.when(i == 0)
    def _():
        cached_ref[0] = -1

    @pl.when((i == 0) | (wb_ref[i] != wb_ref[jnp.maximum(i - 1, 0)]))
    def _():
        o_ref[...] = jnp.zeros(o_ref.shape, o_ref.dtype)

    @pl.when(live & (cached_ref[0] != we_ref[i]))
    def _():
        w1b_ref[...] = w1_ref[0].astype(BF16)
        w3b_ref[...] = w3_ref[0].astype(BF16)
        w2b_ref[...] = w2_ref[0].astype(BF16)
        cached_ref[0] = we_ref[i]

    @pl.when(live)
    def _():
        xb = x_ref[...]
        u = jnp.dot(xb, w1b_ref[...], preferred_element_type=F32)
        g = jnp.dot(xb, w3b_ref[...], preferred_element_type=F32)
        hmid = (u * _sigmoid(u) * g).astype(BF16)
        res = (jnp.dot(hmid, w2b_ref[...], preferred_element_type=F32) * sw_ref[...]).astype(o_ref.dtype)
        row = lax.broadcasted_iota(jnp.int32, (o_ref.shape[0], 1), 0)
        o_ref[...] = jnp.where((row >= lo) & (row < hi), res, o_ref[...])


def _moe_ffn(xs, sw, items, w1, w3, w2):
    n_rows, d = xs.shape
    wb, we, lo, hi = items
    grid_spec = pltpu.PrefetchScalarGridSpec(
        num_scalar_prefetch=4,
        grid=(wb.shape[0],),
        in_specs=[pl.BlockSpec((MOE_TILE, d), lambda i, wb, we, lo, hi: (wb[i], 0)),
                  pl.BlockSpec((MOE_TILE, 1), lambda i, wb, we, lo, hi: (wb[i], 0)),
                  pl.BlockSpec((1, d, D_EXPERT), lambda i, wb, we, lo, hi: (we[i], 0, 0)),
                  pl.BlockSpec((1, d, D_EXPERT), lambda i, wb, we, lo, hi: (we[i], 0, 0)),
                  pl.BlockSpec((1, D_EXPERT, d), lambda i, wb, we, lo, hi: (we[i], 0, 0))],
        out_specs=pl.BlockSpec((MOE_TILE, d), lambda i, wb, we, lo, hi: (wb[i], 0)),
        scratch_shapes=[pltpu.VMEM((d, D_EXPERT), BF16), pltpu.VMEM((d, D_EXPERT), BF16),
                        pltpu.VMEM((D_EXPERT, d), BF16), pltpu.SMEM((1,), jnp.int32)],
    )
    return pl.pallas_call(
        _moe_kernel,
        grid_spec=grid_spec,
        out_shape=jax.ShapeDtypeStruct((n_rows, d), BF16),
        compiler_params=_cparams("arbitrary"),
        name="moe_ffn",
    )(wb, we, lo, hi, xs, sw, w1, w3, w2)


def _final_kernel(x_ref, gt_ref, y0_ref, y1_ref, o_ref):
    o_ref[0] = x_ref[0] + gt_ref[0] * (y0_ref[0].astype(F32) + y1_ref[0].astype(F32))


def _final(x_new, gt2, y0, y1):
    b, l, d = x_new.shape
    tm = _pick(l, (512, 256, 128))
    tok = lambda: pl.BlockSpec((1, tm, d), lambda bi, i: (bi, i, 0))
    return pl.pallas_call(
        _final_kernel,
        grid=(b, l // tm),
        in_specs=[tok(), pl.BlockSpec((1, 1, d), lambda bi, i: (bi, 0, 0)), tok(), tok()],
        out_specs=tok(),
        out_shape=jax.ShapeDtypeStruct((b, l, d), F32),
        compiler_params=_cparams("parallel", "parallel"),
        name="moe_residual",
    )(x_new, gt2, y0, y1)


def _pad_lora_cols(w, widths):
    parts, o = [], 0
    for wd in widths:
        blk = w[..., o:o + wd]
        parts.append(jnp.pad(blk, [(0, 0)] * (w.ndim - 1) + [(0, LORA_PAD - wd)]))
        o += wd
    return jnp.concatenate(parts, axis=-1)


def _rope_tables(l):
    half = DIFF_QK_DIM // 2
    inv_freq = ROPE_THETA ** (-jnp.arange(0, half, 2, dtype=F32) / half)
    t = jnp.arange(l, dtype=jnp.int32)
    rows = (t // GRID_W).astype(F32)[:, None] * inv_freq
    cols = (t % GRID_W).astype(F32)[:, None] * inv_freq
    cos64 = jnp.concatenate([jnp.cos(rows), jnp.cos(rows), jnp.cos(cols), jnp.cos(cols)], axis=1)
    sin64 = jnp.concatenate([-jnp.sin(rows), jnp.sin(rows), -jnp.sin(cols), jnp.sin(cols)], axis=1)
    return jnp.tile(cos64, (1, 2)), jnp.tile(sin64, (1, 2))


def _route(routed, n_tok):
    expert = routed[:, :TOP_K].astype(jnp.int32)
    gate = routed[:, TOP_K:2 * TOP_K]

    n_assign = n_tok * TOP_K
    assert n_assign % MOE_TILE == 0
    flat_e = expert.reshape(-1).astype(jnp.int32)
    ids = jnp.arange(n_assign, dtype=jnp.int32)
    sorted_e, order, sorted_w = lax.sort((flat_e, ids, gate.reshape(-1)), num_keys=1, is_stable=True)
    _, rank = lax.sort((order, ids), num_keys=1)
    ends = jnp.searchsorted(sorted_e, jnp.arange(N_EXPERTS, dtype=jnp.int32), side='right').astype(jnp.int32)
    starts = jnp.concatenate([jnp.zeros((1,), jnp.int32), ends[:-1]])

    nb = n_assign // MOE_TILE
    blk_lo = jnp.arange(nb, dtype=jnp.int32) * MOE_TILE
    e_first = jnp.searchsorted(ends, blk_lo, side='right').astype(jnp.int32)
    e_last = jnp.searchsorted(ends, blk_lo + MOE_TILE - 1, side='right').astype(jnp.int32)
    per_blk = e_last - e_first + 1
    cum = jnp.cumsum(per_blk)
    it = jnp.arange(nb + N_EXPERTS - 1, dtype=jnp.int32)
    wb = jnp.minimum(jnp.searchsorted(cum, it, side='right'), nb - 1).astype(jnp.int32)
    we = jnp.clip(e_first[wb] + it - (cum[wb] - per_blk[wb]), 0, N_EXPERTS - 1)
    lo = jnp.clip(starts[we], blk_lo[wb], blk_lo[wb] + MOE_TILE) - blk_lo[wb]
    hi = jnp.clip(ends[we], blk_lo[wb], blk_lo[wb] + MOE_TILE) - blk_lo[wb]
    hi = jnp.where(it < cum[-1], hi, lo)
    return order // TOP_K, sorted_w, rank.reshape(n_tok, TOP_K), (wb, we, lo, hi)


def kernel(x, c, ctx, c_ctx, ada_w, ada_b, norm1_g, norm2_g, w_in, shift_mu, rwkv_w0, rwkv_w2, rwkv_a0, rwkv_a2,
           rwkv_g2, rwkv_k_k, rwkv_k_a, rwkv_r_k, rwkv_lnx_g, rwkv_lnx_b, qn_g, kn_g, diff_lambda, subln_g,
           w_pa, w_pb, w_out, router_g_w, router_g_b, router_e_w, router_e_b, exp_w1, exp_w3, exp_w2):
    assert ada_w.shape[0] == 1, "single-layer block"
    b, l, d = x.shape
    lc = ctx.shape[1]
    lam_init = 0.8 - 0.6 * math.exp(-0.3 * 0)
    lv = diff_lambda[0].astype(F32)
    lam = jnp.exp(jnp.sum(lv[0] * lv[1])) - jnp.exp(jnp.sum(lv[2] * lv[3])) + lam_init

    rows = (b + 1 + SUBLANES - 1) // SUBLANES * SUBLANES
    cm = jnp.zeros((rows, d), F32).at[:b].set(c).at[b].set(c_ctx)
    mod = _modulation(cm, ada_w[0], ada_b[0])
    sh1, sc1, gt1, sh2, sc2, gt2 = [mod[:b, None, k * d:(k + 1) * d] for k in range(6)]
    csh1, csc1 = [jnp.broadcast_to(mod[b, k * d:(k + 1) * d], (b, 1, d)) for k in range(2)]

    w = w_in[0]
    lora_widths = (DECAY_LORA, DECAY_LORA, AAA_LORA, AAA_LORA)
    o_lora = 3 * RWKV_DIM
    o_glora = o_lora + sum(lora_widths)
    pad_cols = lambda m: jnp.concatenate(
        [m[..., :o_lora], _pad_lora_cols(m[..., o_lora:o_glora], lora_widths), m[..., o_glora:RWKV_COLS]], axis=-1)
    w_rwkv = pad_cols(w).astype(BF16)
    w_diff = w[:, RWKV_COLS:RWKV_COLS + DIFF_COLS].astype(BF16)
    w_gate = w[:, RWKV_COLS + DIFF_COLS:].astype(BF16)
    g1 = norm1_g[0]
    hx = _norm_mod(x, g1, sc1, sh1)
    hc = _norm_mod(ctx, g1, csc1, csh1)
    px_r, px_d, px_g = _proj(hx, w_rwkv, F32), _proj(hx, w_diff, BF16), _proj(hx, w_gate, BF16)
    pc_r, pc_d = _proj(hc, w_rwkv, F32), _proj(hc, w_diff, BF16)

    pad_rows = lambda m: jnp.pad(m, ((0, 0), (0, LORA_PAD - m.shape[1]), (0, 0)))
    rparams = (pad_cols(shift_mu[0])[None], rwkv_w0[0], pad_rows(rwkv_w2[0]), rwkv_a0[0], pad_rows(rwkv_a2[0]),
               rwkv_g2[0], rwkv_k_k[0][None], rwkv_k_a[0][None], rwkv_r_k[0].reshape(1, RWKV_DIM))
    y_sweep, bonus, gate = _rwkv_scans(px_r, pc_r, rparams)

    cos_t, sin_t = _rope_tables(l)
    qg = jnp.tile(qn_g[0], 2)[None]
    kg = jnp.tile(kn_g[0], 2)[None]
    q_x, k_x, v_x = _diff_prep(px_d, cos_t, sin_t, qg, kg, True)
    _, k_c, v_c = _diff_prep(pc_d, cos_t[:lc], sin_t[:lc], qg, kg, False)
    lam_vec = jnp.full((1, LANES), lam, F32)
    sg_vec = (subln_g[0] * (1.0 - lam_init))[None]
    score_bound = (1.05 * DIFF_QK_DIM * DIFF_SCALE * math.log2(math.e)
                   * jnp.max(jnp.abs(qn_g[0])) * jnp.max(jnp.abs(kn_g[0])))
    attn_args = (q_x, k_c, v_c, k_x, v_x, lam_vec, sg_vec)
    y_diff = lax.cond(score_bound <= SCORE_LOG2_LIMIT,
                      lambda a: _diff_attention(*a, bounded=True),
                      lambda a: _diff_attention(*a, bounded=False), attn_args)

    mixed = _merge(y_sweep, bonus, gate, y_diff, px_g, rwkv_lnx_g[0][None], rwkv_lnx_b[0][None],
                   w_pa[0].astype(BF16), w_pb[0].astype(BF16))
    n_r = N_GROUPS + N_EXPERTS
    w_router = jnp.zeros((d, ROUTER_PAD), F32).at[:, :N_GROUPS].set(router_g_w[0]).at[:, N_GROUPS:n_r].set(router_e_w[0])
    b_router = jnp.zeros((1, ROUTER_PAD), F32).at[0, :N_GROUPS].set(router_g_b[0]).at[0, N_GROUPS:n_r].set(router_e_b[0])
    w_router_hi = w_router.astype(BF16)
    w_router_lo = (w_router - w_router_hi.astype(F32)).astype(BF16)
    x_new, h2, routed = _outproj(mixed, x, gt1, norm2_g[0][None], sc2, sh2, w_out[0].astype(BF16),
                                 jnp.stack([w_router_hi, w_router_lo]), b_router)

    n_tok = b * l
    row_tok, row_w, row_of, items = _route(routed.reshape(n_tok, ROUTER_PAD), n_tok)
    xs = h2.reshape(n_tok, d)[row_tok]
    out = _moe_ffn(xs, row_w[:, None], items, exp_w1[0], exp_w3[0], exp_w2[0])
    y0 = out[row_of[:, 0]].reshape(b, l, d)
    y1 = out[row_of[:, 1]].reshape(b, l, d)
    return _final(x_new, gt2, y0, y1)
```

```python
import functools
import math

import jax
import jax.numpy as jnp
from jax import lax
from jax.experimental import pallas as pl
from jax.experimental.pallas import tpu as pltpu

F32 = jnp.float32
BF16 = jnp.bfloat16
HIGHEST = lax.Precision.HIGHEST

D_MODEL = 2048
GRID_W = 64
RWKV_HEADS = 16
RWKV_HEAD_DIM = 64
RWKV_DIM = RWKV_HEADS * RWKV_HEAD_DIM
DECAY_LORA = 96
AAA_LORA = 96
GATE_LORA = 256
RWKV_COLS = 3 * RWKV_DIM + 2 * DECAY_LORA + 2 * AAA_LORA + GATE_LORA
DIFF_HEADS = 8
DIFF_QK_DIM = 64
DIFF_V_DIM = 2 * DIFF_QK_DIM
DIFF_DIM = DIFF_HEADS * DIFF_V_DIM
DIFF_QK_COLS = DIFF_HEADS * 2 * DIFF_QK_DIM
DIFF_COLS = 2 * DIFF_QK_COLS + DIFF_DIM
DIFF_SCALE = DIFF_QK_DIM ** -0.5
ROPE_THETA = 10000.0
GATE_COLS = 2 * D_MODEL
N_GROUPS = 4
EXPERTS_PER_GROUP = 8
N_EXPERTS = N_GROUPS * EXPERTS_PER_GROUP
TOP_K = 2
D_EXPERT = 512
NORM_EPS = 1e-6
SUBLN_EPS = 1e-5
LNX_EPS = 64e-5

LANES = 128
SUBLANES = 8
VMEM_LIMIT_BYTES = 56 * 1024 * 1024

LORA_PAD = LANES
RWKV_PCOLS = 3 * RWKV_DIM + 4 * LORA_PAD + GATE_LORA
CHUNK = 64
V_EXT =2 * DIFF_V_DIM
SCORE_LOG2_LIMIT = 60.0
ROUTER_PAD = LANES
MOE_TILE = 512


def _cparams(*sem):
    return pltpu.CompilerParams(dimension_semantics=sem, vmem_limit_bytes=VMEM_LIMIT_BYTES)


def _sigmoid(x):
    return 1.0 / (1.0 + jnp.exp(-x))


def _dot(a, b, dims=(((1,), (0,)), ((), ()))):
    return lax.dot_general(a.astype(BF16), b.astype(BF16), dims, preferred_element_type=F32)


def _dot_f32(a, b, dims=(((1,), (0,)), ((), ()))):
    return lax.dot_general(a, b, dims, precision=HIGHEST, preferred_element_type=F32)


_NT = (((1,), (1,)), ((), ()))
_TN = (((0,), (0,)), ((), ()))


def _bf16_parts(x, n):
    parts = []
    for _ in range(n):
        p = x.astype(BF16)
        parts.append(p)
        x = x - p.astype(F32)
    return parts


def _dot_split(x, w_b, n, lhs=True):
    parts = _bf16_parts(x, n)
    outs = [jnp.dot(p, w_b, preferred_element_type=F32) if lhs else jnp.dot(w_b, p, preferred_element_type=F32)
            for p in parts]
    return functools.reduce(lambda a, b: a + b, outs)


def _group_ones(width, group):
    r = lax.broadcasted_iota(jnp.int32, (width, width), 0) // group
    c = lax.broadcasted_iota(jnp.int32, (width, width), 1) // group
    return (r == c).astype(BF16)


def _group_sum(x, group, pieces=2):
    ones = _group_ones(LANES, group)
    parts = [_dot_split(x[:, j * LANES:(j + 1) * LANES], ones, pieces) for j in range(x.shape[1] // LANES)]
    return parts[0] if len(parts) == 1 else jnp.concatenate(parts, axis=1)


def _mod_kernel(c_ref, w_ref, b_ref, o_ref):
    c = c_ref[...]
    o_ref[...] = _dot_f32(c * _sigmoid(c), w_ref[...]) + b_ref[...]


def _modulation(cm, ada_w, ada_b):
    rows, d = cm.shape
    n = ada_w.shape[1]
    tn = 1536
    return pl.pallas_call(
        _mod_kernel,
        grid=(n // tn,),
        in_specs=[pl.BlockSpec((rows, d), lambda j: (0, 0)),
                  pl.BlockSpec((d, tn), lambda j: (0, j)),
                  pl.BlockSpec((1, tn), lambda j: (0, j))],
        out_specs=pl.BlockSpec((rows, tn), lambda j: (0, j)),
        out_shape=jax.ShapeDtypeStruct((rows, n), F32),
        compiler_params=_cparams("parallel"),
        name="modulation",
    )(cm, ada_w, ada_b.reshape(1, n))


def _norm_mod_kernel(x_ref, g_ref, sc_ref, sh_ref, h_ref):
    x = x_ref[0]
    y = x * lax.rsqrt(jnp.mean(x * x, axis=-1, keepdims=True) + NORM_EPS) * g_ref[...]
    h_ref[0] = (y * (1.0 + sc_ref[0]) + sh_ref[0]).astype(BF16)


def _pick(n, prefs):
    for t in prefs:
        if n % t == 0:
            return t
    return n


def _norm_mod(x, g, sc, sh):
    b, l, d = x.shape
    tm = _pick(l, (512, 256, 128))
    return pl.pallas_call(
        _norm_mod_kernel,
        grid=(b, l // tm),
        in_specs=[pl.BlockSpec((1, tm, d), lambda bi, i: (bi, i, 0)),
                  pl.BlockSpec((1, d), lambda bi, i: (0, 0)),
                  pl.BlockSpec((1, 1, d), lambda bi, i: (bi, 0, 0)),
                  pl.BlockSpec((1, 1, d), lambda bi, i: (bi, 0, 0))],
        out_specs=pl.BlockSpec((1, tm, d), lambda bi, i: (bi, i, 0)),
        out_shape=jax.ShapeDtypeStruct((b, l, d), BF16),
        compiler_params=_cparams("parallel", "parallel"),
        name="norm_mod",
    )(x, g.reshape(1, d), sc, sh)


def _proj_kernel(h_ref, w_ref, o_ref):
    o_ref[0] = jnp.dot(h_ref[0], w_ref[...], preferred_element_type=F32).astype(o_ref.dtype)


def _proj(h, w, out_dtype):
    b, l, d = h.shape
    n = w.shape[1]
    tm = _pick(l, (1024, 512, 256, 128))
    tn = _pick(n, (1024, 768, 512, 256, 128))
    return pl.pallas_call(
        _proj_kernel,
        grid=(b, l // tm, n // tn),
        in_specs=[pl.BlockSpec((1, tm, d), lambda bi, i, j: (bi, i, 0)),
                  pl.BlockSpec((d, tn), lambda bi, i, j: (0, j))],
        out_specs=pl.BlockSpec((1, tm, tn), lambda bi, i, j: (bi, i, j)),
        out_shape=jax.ShapeDtypeStruct((b, l, n), out_dtype),
        compiler_params=_cparams("parallel", "parallel", "parallel"),
        name="proj",
    )(h, w)


def _pair_diag(x_b, mask_b):
    return jnp.concatenate([x_b, x_b], axis=0) * mask_b


def _tri_inverse_pairs(a_list, eye_f, mask_b):
    n = CHUNK
    mm = lambda l, r: jnp.dot(l.astype(BF16), _pair_diag(r.astype(BF16), mask_b), preferred_element_type=F32)
    xs = [eye_f + a for a in a_list]
    ps = [mm(a, a) for a in a_list]
    steps = int(math.log2(n)) - 1
    for s in range(steps):
        if s < steps - 1:
            xps = [mm(jnp.concatenate([x, p], axis=0), p) for x, p in zip(xs, ps)]
            xs = [x + xp[:n] for x, xp in zip(xs, xps)]
            ps = [xp[n:] for xp in xps]
        else:
            xs = [x + mm(x, p) for x, p in zip(xs, ps)]
    return xs


def _rwkv_chunk_kernel(p_ref, pp_ref, pn_ref, mu_ref, w0_ref, w2_ref, a0_ref, a2_ref, g2_ref, kk_ref, ka_ref,
                       rk_ref, q_ref, y0_ref, m_ref, n_ref, bonus_ref, gate_ref):
    i = pl.program_id(1)
    last = pl.num_programs(1) - 1
    c = RWKV_DIM
    hd = RWKV_HEAD_DIM
    p = p_ref[0]
    t_rows = p.shape[0]
    n_sub = t_rows // CHUNK
    row = lax.broadcasted_iota(jnp.int32, (t_rows, 1), 0)
    prev_row = jnp.where(i == 0, 0.0, pp_ref[0, SUBLANES - 1:SUBLANES, :])
    next_row = jnp.where(i == last, 0.0, pn_ref[0, 0:1, :])
    prev = jnp.where(row == 0, prev_row, pltpu.roll(p, 1, axis=0))
    nxt = jnp.where(row == t_rows - 1, next_row, pltpu.roll(p, t_rows - 1, axis=0))
    ps = p + mu_ref[...] * (0.5 * (prev + nxt) - p)

    r, k, v = ps[:, :c], ps[:, c:2 * c], ps[:, 2 * c:3 * c]
    o = 3 * c
    xw = (ps[:, o:o + LORA_PAD], ps[:, o + LORA_PAD:o + 2 * LORA_PAD])
    xa = (ps[:, o + 2 * LORA_PAD:o + 3 * LORA_PAD], ps[:, o + 3 * LORA_PAD:o + 4 * LORA_PAD])
    xg = ps[:, o + 4 * LORA_PAD:]

    gate_ref[0] = _dot(_sigmoid(xg), g2_ref[...]).astype(gate_ref.dtype)
    kk = k * kk_ref[...]
    kk = kk * lax.rsqrt(_group_sum(kk * kk, hd, pieces=1) + 1e-12)

    tr = lax.broadcasted_iota(jnp.int32, (t_rows, t_rows), 0)
    tc = lax.broadcasted_iota(jnp.int32, (t_rows, t_rows), 1)
    same_chunk = (tr // CHUNK) == (tc // CHUNK)
    tr2 = lax.broadcasted_iota(jnp.int32, (CHUNK, LANES), 0)
    lane2 = lax.broadcasted_iota(jnp.int32, (CHUNK, LANES), 1)
    tc2 = lane2 % CHUNK
    lane_head = lane2 // hd
    eye2 = tr2 == tc2
    mask_b = _group_ones(LANES, hd)
    kd_sum = jnp.zeros_like(k)
    v_b = v.astype(BF16)
    psl = [slice(p * LANES, (p + 1) * LANES) for p in range(c // LANES)]
    before2, upto2, prep = [], [], []
    for d in range(2):
        before2.append((tc2 < tr2) if d == 0 else (tc2 > tr2))
        upto = same_chunk & ((tc <= tr) if d == 0 else (tc >= tr))
        upto2.append((tc2 <= tr2) if d == 0 else (tc2 >= tr2))
        z = w0_ref[d:d + 1, :] + _dot(jnp.tanh(xw[d]), w2_ref[d])
        w_log = -(jnp.maximum(-z, 0.0) + jnp.log(1.0 + jnp.exp(-jnp.abs(z)))) - 0.5
        logw = -jnp.exp(w_log)
        a = _sigmoid(a0_ref[d:d + 1, :] + _dot(xa[d], a2_ref[d]))
        kd = k * (1.0 + (a - 1.0) * ka_ref[...])
        kd_sum = kd_sum + kd
        cum = _dot_split(logw, upto.astype(BF16), 3, lhs=False)
        last_row = [u * CHUNK + (CHUNK - 1 if d == 0 else 0) for u in range(n_sub)]
        totals = [cum[t:t + 1, :] for t in last_row]
        total = totals[0]
        for u in range(1, n_sub):
            total = jnp.where(row >= u * CHUNK, totals[u], total)
        e_neg = jnp.exp(-cum)
        e_rest = jnp.exp(total - cum)
        p_total = [jnp.exp(t) for t in totals]
        beta = kk * a
        al = -kk * jnp.exp(cum - logw)
        rt = r * jnp.exp(cum)
        al_b = al.astype(BF16)
        rt_b = rt.astype(BF16)
        bt_b = (beta * e_neg).astype(BF16)
        kt_b = (kd * e_neg).astype(BF16)
        bh_b = (beta * e_rest).astype(BF16)
        kh_b = (kd * e_rest).astype(BF16)
        prep.append((al_b, rt_b, bt_b, kt_b, bh_b, kh_b, rt, p_total))

    rsl = [slice(u * CHUNK, (u + 1) * CHUNK) for u in range(n_sub)]
    dp = [(d, rs, s) for rs in rsl for d in range(2) for s in psl]
    fdot = lambda a, b: jnp.dot(a, b, preferred_element_type=F32)
    bd = lambda x: _pair_diag(x, mask_b)
    v_bd = [bd(v_b[rs, s]) for d, rs, s in dp]
    zeros_sq = jnp.zeros((LANES, LANES), BF16)
    zeros_tl = jnp.zeros((CHUNK, LANES), BF16)
    gm = [lax.dot_general(jnp.concatenate([prep[d][0][rs, s], prep[d][1][rs, s]], axis=0),
                          jnp.concatenate([bd(prep[d][2][rs, s]), bd(prep[d][3][rs, s])], axis=0),
                          _NT, preferred_element_type=F32) for d, rs, s in dp]
    a_ab = [jnp.where(before2[d], x[:CHUNK, :LANES], 0.0) for x, (d, rs, s) in zip(gm, dp)]
    a_ak = [jnp.where(before2[d], x[:CHUNK, LANES:], 0.0).astype(BF16) for x, (d, rs, s) in zip(gm, dp)]
    a_rb = [jnp.where(upto2[d], x[CHUNK:, :LANES], 0.0).astype(BF16) for x, (d, rs, s) in zip(gm, dp)]
    a_rk = [jnp.where(upto2[d], x[CHUNK:, LANES:], 0.0).astype(BF16) for x, (d, rs, s) in zip(gm, dp)]
    akv = [fdot(ak, vd) for ak, vd in zip(a_ak, v_bd)]
    t_inv = _tri_inverse_pairs(a_ab, eye2.astype(F32), mask_b)
    wu = [fdot(t.astype(BF16), jnp.concatenate([bd(prep[d][0][rs, s]), bd(u.astype(BF16))], axis=1))
          for t, u, (d, rs, s) in zip(t_inv, akv, dp)]
    w_b = [x[:, :LANES].astype(BF16) for x in wu]
    u_b = [x[:, LANES:].astype(BF16) for x in wu]
    qy = [fdot(jnp.concatenate([rb, rk], axis=1),
               jnp.concatenate([jnp.concatenate([bd(w), bd(u)], axis=1),
                                jnp.concatenate([zeros_sq, vd], axis=1)], axis=0))
          for rb, rk, w, u, vd in zip(a_rb, a_rk, w_b, u_b, v_bd)]
    full = [lax.dot_general(jnp.concatenate([jnp.concatenate([w, u], axis=1),
                                             jnp.concatenate([zeros_tl, v_b[rs, s]], axis=1)], axis=0),
                            jnp.concatenate([prep[d][4][rs, s], prep[d][5][rs, s]], axis=0),
                            _TN, preferred_element_type=F32)
            for w, u, (d, rs, s) in zip(w_b, u_b, dp)]

    def diag_blocks(x):
        return jnp.where(lane_head == 0, x[:hd], 0.0) + jnp.where(lane_head == 1, x[hd:2 * hd], 0.0)

    npair = len(psl)
    for u, rs in enumerate(rsl):
        for d in range(2):
            sl = slice((2 * u + d) * npair, (2 * u + d + 1) * npair)
            rt, p_total = prep[d][6][rs], prep[d][7][u]
            q_ref[0, d, rs, :] = (rt + jnp.concatenate([x[:, :LANES] for x in qy[sl]], axis=1)).astype(q_ref.dtype)
            y0_ref[0, d, rs, :] = jnp.concatenate([x[:, LANES:] for x in qy[sl]], axis=1).astype(y0_ref.dtype)
            m_ref[0, d, u] = jnp.concatenate([jnp.where(eye2, p_total[:, s], 0.0) + diag_blocks(x[:LANES])
                                              for x, s in zip(full[sl], psl)], axis=1).astype(m_ref.dtype)
            n_ref[0, d, u] = jnp.concatenate([diag_blocks(x[LANES:]) for x in full[sl]], axis=1)
    bonus_ref[0] = (_group_sum(r * kd_sum * rk_ref[...], hd) * v).astype(bonus_ref.dtype)


def _rwkv_chunk_ops(p, params):
    b, l, pc = p.shape
    nc = l // CHUNK
    c = RWKV_DIM
    g = _pick(nc, (2, 1))
    t = g * CHUNK
    hb = t // SUBLANES
    nb8 = l // SUBLANES
    mu, w0, w2, a0, a2, g2, k_k, k_a, r_k = params
    const = lambda shape: pl.BlockSpec(shape, lambda bi, i: (0,) * len(shape))
    tok = lambda: pl.BlockSpec((1, t, c), lambda bi, i: (bi, i, 0))
    return pl.pallas_call(
        _rwkv_chunk_kernel,
        grid=(b, nc // g),
        in_specs=[pl.BlockSpec((1, t, pc), lambda bi, i: (bi, i, 0)),
                  pl.BlockSpec((1, SUBLANES, pc), lambda bi, i: (bi, jnp.maximum(i * hb - 1, 0), 0)),
                  pl.BlockSpec((1, SUBLANES, pc), lambda bi, i: (bi, jnp.minimum((i + 1) * hb, nb8 - 1), 0)),
                  const((1, pc)), const((2, c)), const((2, LORA_PAD, c)), const((2, c)), const((2, LORA_PAD, c)),
                  const((GATE_LORA, c)), const((1, c)), const((1, c)), const((1, c))],
        out_specs=[pl.BlockSpec((1, 2, t, c), lambda bi, i: (bi, 0, i, 0)),
                   pl.BlockSpec((1, 2, t, c), lambda bi, i: (bi, 0, i, 0)),
                   pl.BlockSpec((1, 2, g, RWKV_HEAD_DIM, c), lambda bi, i: (bi, 0, i, 0, 0)),
                   pl.BlockSpec((1, 2, g, RWKV_HEAD_DIM, c), lambda bi, i: (bi, 0, i, 0, 0)),
                   tok(), tok()],
        out_shape=[jax.ShapeDtypeStruct((b, 2, l, c), BF16), jax.ShapeDtypeStruct((b, 2, l, c), BF16),
                   jax.ShapeDtypeStruct((b, 2, nc, RWKV_HEAD_DIM, c), BF16),
                   jax.ShapeDtypeStruct((b, 2, nc, RWKV_HEAD_DIM, c), F32),
                   jax.ShapeDtypeStruct((b, l, c), BF16), jax.ShapeDtypeStruct((b, l, c), BF16)],
        compiler_params=_cparams("parallel", "parallel"),
        name="rwkv_chunk_ops",
    )(p, p, p, mu, w0, w2, a0, a2, g2, k_k, k_a, r_k)


def _rwkv_sweep_kernel(qf_ref, y0f_ref, mf_ref, nf_ref, qb_ref, y0b_ref, mb_ref, nb_ref, s0_ref,
                       yf_ref, yb_ref, sfin_ref, s_ref):
    j = pl.program_id(1)
    hd = RWKV_HEAD_DIM

    @pl.when(j == 0)
    def _():
        s_ref[...] = s0_ref[0]

    dirs = ((qf_ref, y0f_ref, mf_ref, nf_ref, yf_ref), (qb_ref, y0b_ref, mb_ref, nb_ref, yb_ref))
    psl = [slice(p * LANES, (p + 1) * LANES) for p in range(RWKV_DIM // LANES)]
    mask_b = _group_ones(LANES, hd)
    pair_diag = lambda x: _pair_diag(x, mask_b)
    per_step = mf_ref.shape[2]
    state = [s_ref[0], s_ref[1]]
    for u in range(per_step):
        for d, (q_ref, y0_ref, m_ref, n_ref, y_ref) in enumerate(dirs):
            cu = u if d == 0 else per_step - 1 - u
            rows = slice(cu * CHUNK, (cu + 1) * CHUNK)
            s_b = state[d].astype(BF16)
            q_b = q_ref[0, 0, rows, :]
            m_b = m_ref[0, 0, cu]
            ys = [_dot(q_b[:, s], pair_diag(s_b[:, s]), _NT) for s in psl]
            sm = [_dot(s_b[:, s], pair_diag(m_b[:, s])) for s in psl]
            y_ref[0, rows, :] = (y0_ref[0, 0, rows, :].astype(F32) + jnp.concatenate(ys, axis=1)).astype(y_ref.dtype)
            state[d] = n_ref[0, 0, cu] + jnp.concatenate(sm, axis=1)
    s_ref[0] = state[0]
    s_ref[1] = state[1]

    @pl.when(j == pl.num_programs(1) - 1)
    def _():
        sfin_ref[0] = s_ref[...]


def _rwkv_sweep(q, y0, m, n, s0):
    b, _, l, c = q.shape
    nc = l // CHUNK
    hd = RWKV_HEAD_DIM
    g = _pick(nc, (4, 2, 1))
    ns = nc // g
    tokf = lambda: pl.BlockSpec((1, 1, g * CHUNK, c), lambda bi, j: (bi, 0, j, 0))
    tokb = lambda: pl.BlockSpec((1, 1, g * CHUNK, c), lambda bi, j: (bi, 1, ns - 1 - j, 0))
    opf = lambda: pl.BlockSpec((1, 1, g, hd, c), lambda bi, j: (bi, 0, j, 0, 0))
    opb = lambda: pl.BlockSpec((1, 1, g, hd, c), lambda bi, j: (bi, 1, ns - 1 - j, 0, 0))
    return pl.pallas_call(
        _rwkv_sweep_kernel,
        grid=(b, ns),
        in_specs=[tokf(), tokf(), opf(), opf(), tokb(), tokb(), opb(), opb(),
                  pl.BlockSpec((1, 2, hd, c), lambda bi, j: (bi, 0, 0, 0))],
        out_specs=[pl.BlockSpec((1, g * CHUNK, c), lambda bi, j: (bi, j, 0)),
                   pl.BlockSpec((1, g * CHUNK, c), lambda bi, j: (bi, ns - 1 - j, 0)),
                   pl.BlockSpec((1, 2, hd, c), lambda bi, j: (bi, 0, 0, 0))],
        out_shape=[jax.ShapeDtypeStruct((b, l, c), BF16), jax.ShapeDtypeStruct((b, l, c), BF16),
                   jax.ShapeDtypeStruct((b, 2, hd, c), F32)],
        scratch_shapes=[pltpu.VMEM((2, hd, c), F32)],
        compiler_params=_cparams("parallel", "arbitrary"),
        name="rwkv_sweep",
    )(q, y0, m, n, q, y0, m, n, s0)


def _rwkv_scans(px_r, pc_r, rparams):
    b = px_r.shape[0]
    qc, y0c, mc, nc_, _, _ = _rwkv_chunk_ops(pc_r, rparams)
    s_zero = jnp.zeros((b, 2, RWKV_HEAD_DIM, RWKV_DIM), F32)
    _, _, s_ctx = _rwkv_sweep(qc, y0c, mc, nc_, s_zero)
    qx, y0x, mx, nx, bonus, gate = _rwkv_chunk_ops(px_r, rparams)
    yf, yb, _ = _rwkv_sweep(qx, y0x, mx, nx, s_ctx)
    return (yf, yb), bonus, gate


def _diff_prep_kernel(p_ref, cos_ref, sin_ref, qg_ref, kg_ref, q_ref, k_ref, v_ref, *, rope):
    lane = lax.broadcasted_iota(jnp.int32, (1, LANES), 1)
    first = (lane % 32) < 16
    for hd in range(DIFF_HEADS):
        for off, g_ref, o_ref, scale in ((0, qg_ref, q_ref, DIFF_SCALE * math.log2(math.e)),
                                         (DIFF_QK_COLS, kg_ref, k_ref, 1.0)):
            cs = slice(hd * LANES, (hd + 1) * LANES)
            xb = p_ref[0, :, off + hd * LANES:off + (hd + 1) * LANES].astype(F32)
            ms = _group_sum(xb * xb, DIFF_QK_DIM, pieces=1) * (1.0 / DIFF_QK_DIM)
            y = xb * lax.rsqrt(ms + NORM_EPS) * g_ref[...]
            if rope:
                swapped = jnp.where(first, pltpu.roll(y, LANES - 16, axis=1), pltpu.roll(y, 16, axis=1))
                y = y * cos_ref[...] + swapped * sin_ref[...]
            o_ref[0, :, cs] = (y * scale).astype(BF16)
    ones = jnp.ones((p_ref.shape[1], V_EXT - DIFF_V_DIM), BF16)
    for hd in range(DIFF_HEADS):
        vb = p_ref[0, :, 2 * DIFF_QK_COLS + hd * DIFF_V_DIM:2 * DIFF_QK_COLS + (hd + 1) * DIFF_V_DIM]
        v_ref[0, :, hd * V_EXT:(hd + 1) * V_EXT] = jnp.concatenate([vb.astype(BF16), ones], axis=1)


def _diff_prep(p, cos_t, sin_t, qg, kg, rope):
    b, l, pc = p.shape
    t = _pick(l, (512, 256, 128))
    tok = lambda w=DIFF_DIM: pl.BlockSpec((1, t, w), lambda bi, i: (bi, i, 0))
    shp = jax.ShapeDtypeStruct((b, l, DIFF_DIM), BF16)
    shp_v = jax.ShapeDtypeStruct((b, l, DIFF_HEADS * V_EXT), BF16)
    return pl.pallas_call(
        functools.partial(_diff_prep_kernel, rope=rope),
        grid=(b, l // t),
        in_specs=[pl.BlockSpec((1, t, pc), lambda bi, i: (bi, i, 0)),
                  pl.BlockSpec((t, LANES), lambda bi, i: (i, 0)),
                  pl.BlockSpec((t, LANES), lambda bi, i: (i, 0)),
                  pl.BlockSpec((1, LANES), lambda bi, i: (0, 0)),
                  pl.BlockSpec((1, LANES), lambda bi, i: (0, 0))],
        out_specs=[tok(), tok(), tok(DIFF_HEADS * V_EXT)],
        out_shape=[shp, shp, shp_v],
        compiler_params=_cparams("parallel", "parallel"),
        name="diff_prep_rope" if rope else "diff_prep",
    )(p, cos_t, sin_t, qg, kg)


def _diff_finish(acc1, acc2, lam_ref, sg_ref, o_ref):
    dv = DIFF_V_DIM
    o = acc1[:, :dv] / acc1[:, dv:] - lam_ref[...] * (acc2[:, :dv] / acc2[:, dv:])
    o = o * lax.rsqrt(jnp.mean(o * o, axis=-1, keepdims=True) + SUBLN_EPS)
    o_ref[0] = (o * sg_ref[...]).astype(o_ref.dtype)


def _flash_online_kernel(lam_ref, sg_ref, q_ref, kc_ref, vc_ref, k_ref, v_ref, o_ref, *, tk):
    qd = DIFF_QK_DIM
    tq = q_ref.shape[1]
    q = q_ref[0]
    qs = (q[:, :qd], q[:, qd:])

    def absorb(state, k, v):
        reps = k.shape[0] // LANES
        out = []
        for mp in range(2):
            m_prev, acc = state[mp]
            s = lax.dot_general(qs[mp], k[:, mp * qd:(mp + 1) * qd], _NT, preferred_element_type=F32)
            m_new = jnp.maximum(m_prev, jnp.max(s, axis=-1, keepdims=True))
            alpha = jnp.exp2(m_prev - m_new)
            pr = jnp.exp2(s - jnp.concatenate([m_new] * reps, axis=1))
            acc_new = jnp.concatenate([alpha, alpha], axis=1) * acc + jnp.dot(pr.astype(BF16), v, preferred_element_type=F32)
            out.append((m_new, acc_new))
        return tuple(out)

    init = (jnp.full((tq, LANES), -jnp.inf, F32), jnp.zeros((tq, V_EXT), F32))
    state = absorb((init, init), kc_ref[0], vc_ref[0])

    def body(j, state):
        rows = pl.ds(pl.multiple_of(j * tk, tk), tk)
        return absorb(state, k_ref[0, rows, :], v_ref[0, rows, :])

    (_, acc1), (_, acc2) = lax.fori_loop(0, k_ref.shape[1] // tk, body, state)
    _diff_finish(acc1, acc2, lam_ref, sg_ref, o_ref)


def _flash_bounded_kernel(lam_ref, sg_ref, q_ref, kc_ref, vc_ref, k_ref, v_ref, o_ref, acc_ref, p_ref, *, tk):
    qd = DIFF_QK_DIM
    q = q_ref[0]
    qs = (q[:, :qd], q[:, qd:])
    n_kv = k_ref.shape[1] // tk

    def weights(mp, k):
        s = lax.dot_general(qs[mp], k[:, mp * qd:(mp + 1) * qd], _NT, preferred_element_type=F32)
        return jnp.exp2(s).astype(BF16)

    def chunk(c):
        return pl.ds(pl.multiple_of(jnp.minimum(c, n_kv - 1) * tk, tk), tk)

    for mp in range(2):
        acc_ref[mp] = jnp.dot(weights(mp, kc_ref[0]), vc_ref[0], preferred_element_type=F32)
        p_ref[0, mp] = weights(mp, k_ref[0, pl.ds(0, tk), :])

    steps = 4 if n_kv % 4 == 0 else 2

    def body(j, carry):
        for u in range(steps):
            c = steps * j + u
            v = v_ref[0, chunk(c), :]
            k_next = k_ref[0, chunk(c + 1), :]
            for mp in range(2):
                acc_ref[mp] += jnp.dot(p_ref[u % 2, mp], v, preferred_element_type=F32)
                p_ref[1 - u % 2, mp] = weights(mp, k_next)
        return carry

    lax.fori_loop(0, n_kv // steps, body, 0)
    _diff_finish(acc_ref[0], acc_ref[1], lam_ref, sg_ref, o_ref)


def _diff_attention(q, kc, vc, k, v, lam_vec, sg_vec, bounded):
    b, l, _ = q.shape
    lc = kc.shape[1]
    tq = _pick(l, (256, 128))
    tk = _pick(l, (512, 256, 128))
    if bounded:
        tq = _pick(l, (1024, 512, 256, 128))
        tk = _pick(l // 2, (512, 256, 128))
        assert l % (2 * tk) == 0
        body = functools.partial(_flash_bounded_kernel, tk=tk)
        scratch = [pltpu.VMEM((2, tq, V_EXT), F32), pltpu.VMEM((2, 2, tq, tk), BF16)]
    else:
        body = functools.partial(_flash_online_kernel, tk=tk)
        scratch = []
    return pl.pallas_call(
        body,
        grid=(b, DIFF_HEADS, l // tq),
        in_specs=[pl.BlockSpec((1, LANES), lambda bi, h, i: (0, 0)),
                  pl.BlockSpec((1, LANES), lambda bi, h, i: (0, 0)),
                  pl.BlockSpec((1, tq, LANES), lambda bi, h, i: (bi, i, h)),
                  pl.BlockSpec((1, lc, LANES), lambda bi, h, i: (bi, 0, h)),
                  pl.BlockSpec((1, lc, V_EXT), lambda bi, h, i: (bi, 0, h)),
                  pl.BlockSpec((1, l, LANES), lambda bi, h, i: (bi, 0, h)),
                  pl.BlockSpec((1, l, V_EXT), lambda bi, h, i: (bi, 0, h))],
        out_specs=pl.BlockSpec((1, tq, LANES), lambda bi, h, i: (bi, i, h)),
        out_shape=jax.ShapeDtypeStruct((b, l, DIFF_DIM), BF16),
        scratch_shapes=scratch,
        compiler_params=_cparams("parallel", "parallel", "arbitrary"),
        name="diff_flash_bounded" if bounded else "diff_flash_online",
    )(lam_vec, sg_vec, q, kc, vc, k, v)


def _merge_kernel(yf_ref, yb_ref, bonus_ref, gate_ref, yd_ref, pg_ref, lg_ref, lb_ref, wpa_ref, wpb_ref, o_ref):
    hd = RWKV_HEAD_DIM
    y = yf_ref[0].astype(F32) + yb_ref[0].astype(F32)
    dev = y - _group_sum(y, hd) * (1.0 / hd)
    var = _group_sum(dev * dev, hd, pieces=1) * (1.0 / hd)
    yn = dev * lax.rsqrt(var + LNX_EPS) * lg_ref[...] + lb_ref[...]
    y_rwkv = (yn + bonus_ref[0].astype(F32)) * gate_ref[0].astype(F32)
    a = _dot(y_rwkv, wpa_ref[...])
    bb = jnp.dot(yd_ref[0], wpb_ref[...], preferred_element_type=F32)
    ga = _sigmoid(pg_ref[0, :, :D_MODEL].astype(F32))
    gb = _sigmoid(pg_ref[0, :, D_MODEL:].astype(F32))
    o_ref[0] = (ga * a + gb * bb).astype(BF16)


def _merge(y_sweep, bonus, gate, y_diff, p_gate, lnx_g, lnx_b, w_pa, w_pb):
    b, l, c = bonus.shape
    d = D_MODEL
    tm = _pick(l, (256, 128))
    const = lambda shape: pl.BlockSpec(shape, lambda bi, i: (0,) * len(shape))
    tok = lambda w: pl.BlockSpec((1, tm, w), lambda bi, i: (bi, i, 0))
    return pl.pallas_call(
        _merge_kernel,
        grid=(b, l // tm),
        in_specs=[tok(c), tok(c), tok(c), tok(c), tok(DIFF_DIM), tok(GATE_COLS),
                  const((1, c)), const((1, c)), const((c, d)), const((DIFF_DIM, d))],
        out_specs=tok(d),
        out_shape=jax.ShapeDtypeStruct((b, l, d), BF16),
        compiler_params=_cparams("parallel", "parallel"),
        name="merge",
    )(y_sweep[0], y_sweep[1], bonus, gate, y_diff, p_gate, lnx_g, lnx_b, w_pa, w_pb)


def _outproj_kernel(mx_ref, x_ref, gt_ref, g_ref, sc_ref, sh_ref, wo_ref, wr_ref, br_ref, xn_ref, h_ref, rt_ref):
    mix = jnp.dot(mx_ref[0], wo_ref[...], preferred_element_type=F32)
    xn = x_ref[0] + gt_ref[0] * mix
    xn_ref[0] = xn
    y = xn * lax.rsqrt(jnp.mean(xn * xn, axis=-1, keepdims=True) + NORM_EPS) * g_ref[...]
    h = y * (1.0 + sc_ref[0]) + sh_ref[0]
    h_hi, h_lo = _bf16_parts(h, 2)
    h_ref[0] = h_hi
    dot = lambda a, b: jnp.dot(a, b, preferred_element_type=F32)
    lg = dot(h_hi, wr_ref[0]) + (dot(h_lo, wr_ref[0]) + dot(h_hi, wr_ref[1])) + br_ref[...]
    rt_ref[0] = _route_rows(lg)


def _route_rows(lg):
    lane = lax.broadcasted_iota(jnp.int32, lg.shape, 1).astype(F32)
    neg = jnp.float32(-3.0e38)
    far = jnp.float32(LANES)
    gl = jnp.where(lane < N_GROUPS, lg, neg)
    g_max = jnp.max(gl, axis=-1, keepdims=True)
    g_top = jnp.min(jnp.where(gl == g_max, lane, far), axis=-1, keepdims=True)
    p_g = 1.0 / jnp.sum(jnp.where(lane < N_GROUPS, jnp.exp(gl - g_max), 0.0), axis=-1, keepdims=True)
    e_lo = N_GROUPS + EXPERTS_PER_GROUP * g_top
    el = jnp.where((lane >= e_lo) & (lane < e_lo + EXPERTS_PER_GROUP), lg, neg)
    m1 = jnp.max(el, axis=-1, keepdims=True)
    i1 = jnp.min(jnp.where(el == m1, lane, far), axis=-1, keepdims=True)
    el2 = jnp.where(lane == i1, neg, el)
    m2 = jnp.max(el2, axis=-1, keepdims=True)
    i2 = jnp.min(jnp.where(el2 == m2, lane, far), axis=-1, keepdims=True)
    e2 = jnp.exp(m2 - m1)
    gate1 = p_g / (1.0 + e2)
    gate2 = p_g * e2 / (1.0 + e2)
    return jnp.where(lane == 0, i1 - N_GROUPS,
                     jnp.where(lane == 1, i2 - N_GROUPS, jnp.where(lane == 2, gate1, jnp.where(lane == 3, gate2, 0.0))))


def _outproj(mixed, x, gt1, g2, sc2, sh2, w_out, w_router, b_router):
    b, l, d = x.shape
    tm = _pick(l, (512, 256, 128))
    const = lambda shape: pl.BlockSpec(shape, lambda bi, i: (0,) * len(shape))
    tok = lambda w: pl.BlockSpec((1, tm, w), lambda bi, i: (bi, i, 0))
    per_b = lambda: pl.BlockSpec((1, 1, d), lambda bi, i: (bi, 0, 0))
    return pl.pallas_call(
        _outproj_kernel,
        grid=(b, l // tm),
        in_specs=[tok(d), tok(d), per_b(), const((1, d)), per_b(), per_b(),
                  const((d, d)), const((2, d, ROUTER_PAD)), const((1, ROUTER_PAD))],
        out_specs=[tok(d), tok(d), tok(ROUTER_PAD)],
        out_shape=[jax.ShapeDtypeStruct((b, l, d), F32), jax.ShapeDtypeStruct((b, l, d), BF16),
                   jax.ShapeDtypeStruct((b, l, ROUTER_PAD), F32)],
        compiler_params=_cparams("parallel", "parallel"),
        name="outproj_router",
    )(mixed, x, gt1, g2, sc2, sh2, w_out, w_router, b_router)


def _moe_kernel(wb_ref, we_ref, lo_ref, hi_ref, x_ref, sw_ref, w1_ref, w3_ref, w2_ref, o_ref,
                w1b_ref, w3b_ref, w2b_ref, cached_ref):
    i = pl.program_id(0)
    lo, hi = lo_ref[i], hi_ref[i]
    live = hi > lo

    @pl.when(i == 0)
    def _():
        cached_ref[0] = -1

    @pl.when((i == 0) | (wb_ref[i] != wb_ref[jnp.maximum(i - 1, 0)]))
    def _():
        o_ref[...] = jnp.zeros(o_ref.shape, o_ref.dtype)

    @pl.when(live & (cached_ref[0] != we_ref[i]))
    def _():
        w1b_ref[...] = w1_ref[0].astype(BF16)
        w3b_ref[...] = w3_ref[0].astype(BF16)
        w2b_ref[...] = w2_ref[0].astype(BF16)
        cached_ref[0] = we_ref[i]

    @pl.when(live)
    def _():
        xb = x_ref[...]
        u = jnp.dot(xb, w1b_ref[...], preferred_element_type=F32)
        g = jnp.dot(xb, w3b_ref[...], preferred_element_type=F32)
        hmid = (u * _sigmoid(u) * g).astype(BF16)
        res = (jnp.dot(hmid, w2b_ref[...], preferred_element_type=F32) * sw_ref[...]).astype(o_ref.dtype)
        row = lax.broadcasted_iota(jnp.int32, (o_ref.shape[0], 1), 0)
        o_ref[...] = jnp.where((row >= lo) & (row < hi), res, o_ref[...])


def _moe_ffn(xs, sw, items, w1, w3, w2):
    n_rows, d = xs.shape
    wb, we, lo, hi = items
    grid_spec = pltpu.PrefetchScalarGridSpec(
        num_scalar_prefetch=4,
        grid=(wb.shape[0],),
        in_specs=[pl.BlockSpec((MOE_TILE, d), lambda i, wb, we, lo, hi: (wb[i], 0)),
                  pl.BlockSpec((MOE_TILE, 1), lambda i, wb, we, lo, hi: (wb[i], 0)),
                  pl.BlockSpec((1, d, D_EXPERT), lambda i, wb, we, lo, hi: (we[i], 0, 0)),
                  pl.BlockSpec((1, d, D_EXPERT), lambda i, wb, we, lo, hi: (we[i], 0, 0)),
                  pl.BlockSpec((1, D_EXPERT, d), lambda i, wb, we, lo, hi: (we[i], 0, 0))],
        out_specs=pl.BlockSpec((MOE_TILE, d), lambda i, wb, we, lo, hi: (wb[i], 0)),
        scratch_shapes=[pltpu.VMEM((d, D_EXPERT), BF16), pltpu.VMEM((d, D_EXPERT), BF16),
                        pltpu.VMEM((D_EXPERT, d), BF16), pltpu.SMEM((1,), jnp.int32)],
    )
    return pl.pallas_call(
        _moe_kernel,
        grid_spec=grid_spec,
        out_shape=jax.ShapeDtypeStruct((n_rows, d), BF16),
        compiler_params=_cparams("arbitrary"),
        name="moe_ffn",
    )(wb, we, lo, hi, xs, sw, w1, w3, w2)


def _final_kernel(x_ref, gt_ref, y0_ref, y1_ref, o_ref):
    o_ref[0] = x_ref[0] + gt_ref[0] * (y0_ref[0].astype(F32) + y1_ref[0].astype(F32))


def _final(x_new, gt2, y0, y1):
    b, l, d = x_new.shape
    tm = _pick(l, (512, 256, 128))
    tok = lambda: pl.BlockSpec((1, tm, d), lambda bi, i: (bi, i, 0))
    return pl.pallas_call(
        _final_kernel,
        grid=(b, l // tm),
        in_specs=[tok(), pl.BlockSpec((1, 1, d), lambda bi, i: (bi, 0, 0)), tok(), tok()],
        out_specs=tok(),
        out_shape=jax.ShapeDtypeStruct((b, l, d), F32),
        compiler_params=_cparams("parallel", "parallel"),
        name="moe_residual",
    )(x_new, gt2, y0, y1)


def _pad_lora_cols(w, widths):
    parts, o = [], 0
    for wd in widths:
        blk = w[..., o:o + wd]
        parts.append(jnp.pad(blk, [(0, 0)] * (w.ndim - 1) + [(0, LORA_PAD - wd)]))
        o += wd
    return jnp.concatenate(parts, axis=-1)


def _rope_tables(l):
    half = DIFF_QK_DIM // 2
    inv_freq = ROPE_THETA ** (-jnp.arange(0, half, 2, dtype=F32) / half)
    t = jnp.arange(l, dtype=jnp.int32)
    rows = (t // GRID_W).astype(F32)[:, None] * inv_freq
    cols = (t % GRID_W).astype(F32)[:, None] * inv_freq
    cos64 = jnp.concatenate([jnp.cos(rows), jnp.cos(rows), jnp.cos(cols), jnp.cos(cols)], axis=1)
    sin64 = jnp.concatenate([-jnp.sin(rows), jnp.sin(rows), -jnp.sin(cols), jnp.sin(cols)], axis=1)
    return jnp.tile(cos64, (1, 2)), jnp.tile(sin64, (1, 2))


def _route(routed, n_tok):
    expert = routed[:, :TOP_K].astype(jnp.int32)
    gate = routed[:, TOP_K:2 * TOP_K]

    n_assign = n_tok * TOP_K
    assert n_assign % MOE_TILE == 0
    flat_e = expert.reshape(-1).astype(jnp.int32)
    ids = jnp.arange(n_assign, dtype=jnp.int32)
    sorted_e, order, sorted_w = lax.sort((flat_e, ids, gate.reshape(-1)), num_keys=1, is_stable=True)
    _, rank = lax.sort((order, ids), num_keys=1)
    ends = jnp.searchsorted(sorted_e, jnp.arange(N_EXPERTS, dtype=jnp.int32), side='right').astype(jnp.int32)
    starts = jnp.concatenate([jnp.zeros((1,), jnp.int32), ends[:-1]])

    nb = n_assign // MOE_TILE
    blk_lo = jnp.arange(nb, dtype=jnp.int32) * MOE_TILE
    e_first = jnp.searchsorted(ends, blk_lo, side='right').astype(jnp.int32)
    e_last = jnp.searchsorted(ends, blk_lo + MOE_TILE - 1, side='right').astype(jnp.int32)
    per_blk = e_last - e_first + 1
    cum = jnp.cumsum(per_blk)
    it = jnp.arange(nb + N_EXPERTS - 1, dtype=jnp.int32)
    wb = jnp.minimum(jnp.searchsorted(cum, it, side='right'), nb - 1).astype(jnp.int32)
    we = jnp.clip(e_first[wb] + it - (cum[wb] - per_blk[wb]), 0, N_EXPERTS - 1)
    lo = jnp.clip(starts[we], blk_lo[wb], blk_lo[wb] + MOE_TILE) - blk_lo[wb]
    hi = jnp.clip(ends[we], blk_lo[wb], blk_lo[wb] + MOE_TILE) - blk_lo[wb]
    hi = jnp.where(it < cum[-1], hi, lo)
    return order // TOP_K, sorted_w, rank.reshape(n_tok, TOP_K), (wb, we, lo, hi)


def kernel(x, c, ctx, c_ctx, ada_w, ada_b, norm1_g, norm2_g, w_in, shift_mu, rwkv_w0, rwkv_w2, rwkv_a0, rwkv_a2,
           rwkv_g2, rwkv_k_k, rwkv_k_a, rwkv_r_k, rwkv_lnx_g, rwkv_lnx_b, qn_g, kn_g, diff_lambda, subln_g,
           w_pa, w_pb, w_out, router_g_w, router_g_b, router_e_w, router_e_b, exp_w1, exp_w3, exp_w2):
    assert ada_w.shape[0] == 1, "single-layer block"
    b, l, d = x.shape
    lc = ctx.shape[1]
    lam_init = 0.8 - 0.6 * math.exp(-0.3 * 0)
    lv = diff_lambda[0].astype(F32)
    lam = jnp.exp(jnp.sum(lv[0] * lv[1])) - jnp.exp(jnp.sum(lv[2] * lv[3])) + lam_init

    rows = (b + 1 + SUBLANES - 1) // SUBLANES * SUBLANES
    cm = jnp.zeros((rows, d), F32).at[:b].set(c).at[b].set(c_ctx)
    mod = _modulation(cm, ada_w[0], ada_b[0])
    sh1, sc1, gt1, sh2, sc2, gt2 = [mod[:b, None, k * d:(k + 1) * d] for k in range(6)]
    csh1, csc1 = [jnp.broadcast_to(mod[b, k * d:(k + 1) * d], (b, 1, d)) for k in range(2)]

    w = w_in[0]
    lora_widths = (DECAY_LORA, DECAY_LORA, AAA_LORA, AAA_LORA)
    o_lora = 3 * RWKV_DIM
    o_glora = o_lora + sum(lora_widths)
    pad_cols = lambda m: jnp.concatenate(
        [m[..., :o_lora], _pad_lora_cols(m[..., o_lora:o_glora], lora_widths), m[..., o_glora:RWKV_COLS]], axis=-1)
    w_rwkv = pad_cols(w).astype(BF16)
    w_diff = w[:, RWKV_COLS:RWKV_COLS + DIFF_COLS].astype(BF16)
    w_gate = w[:, RWKV_COLS + DIFF_COLS:].astype(BF16)
    g1 = norm1_g[0]
    hx = _norm_mod(x, g1, sc1, sh1)
    hc = _norm_mod(ctx, g1, csc1, csh1)
    px_r, px_d, px_g = _proj(hx, w_rwkv, F32), _proj(hx, w_diff, BF16), _proj(hx, w_gate, BF16)
    pc_r, pc_d = _proj(hc, w_rwkv, F32), _proj(hc, w_diff, BF16)

    pad_rows = lambda m: jnp.pad(m, ((0, 0), (0, LORA_PAD - m.shape[1]), (0, 0)))
    rparams = (pad_cols(shift_mu[0])[None], rwkv_w0[0], pad_rows(rwkv_w2[0]), rwkv_a0[0], pad_rows(rwkv_a2[0]),
               rwkv_g2[0], rwkv_k_k[0][None], rwkv_k_a[0][None], rwkv_r_k[0].reshape(1, RWKV_DIM))
    y_sweep, bonus, gate = _rwkv_scans(px_r, pc_r, rparams)

    cos_t, sin_t = _rope_tables(l)
    qg = jnp.tile(qn_g[0], 2)[None]
    kg = jnp.tile(kn_g[0], 2)[None]
    q_x, k_x, v_x = _diff_prep(px_d, cos_t, sin_t, qg, kg, True)
    _, k_c, v_c = _diff_prep(pc_d, cos_t[:lc], sin_t[:lc], qg, kg, False)
    lam_vec = jnp.full((1, LANES), lam, F32)
    sg_vec = (subln_g[0] * (1.0 - lam_init))[None]
    score_bound = (1.05 * DIFF_QK_DIM * DIFF_SCALE * math.log2(math.e)
                   * jnp.max(jnp.abs(qn_g[0])) * jnp.max(jnp.abs(kn_g[0])))
    attn_args = (q_x, k_c, v_c, k_x, v_x, lam_vec, sg_vec)
    y_diff = lax.cond(score_bound <= SCORE_LOG2_LIMIT,
                      lambda a: _diff_attention(*a, bounded=True),
                      lambda a: _diff_attention(*a, bounded=False), attn_args)

    mixed = _merge(y_sweep, bonus, gate, y_diff, px_g, rwkv_lnx_g[0][None], rwkv_lnx_b[0][None],
                   w_pa[0].astype(BF16), w_pb[0].astype(BF16))
    n_r = N_GROUPS + N_EXPERTS
    w_router = jnp.zeros((d, ROUTER_PAD), F32).at[:, :N_GROUPS].set(router_g_w[0]).at[:, N_GROUPS:n_r].set(router_e_w[0])
    b_router = jnp.zeros((1, ROUTER_PAD), F32).at[0, :N_GROUPS].set(router_g_b[0]).at[0, N_GROUPS:n_r].set(router_e_b[0])
    w_router_hi = w_router.astype(BF16)
    w_router_lo = (w_router - w_router_hi.astype(F32)).astype(BF16)
    x_new, h2, routed = _outproj(mixed, x, gt1, norm2_g[0][None], sc2, sh2, w_out[0].astype(BF16),
                                 jnp.stack([w_router_hi, w_router_lo]), b_router)

    n_tok = b * l
    row_tok, row_w, row_of, items = _route(routed.reshape(n_tok, ROUTER_PAD), n_tok)
    xs = h2.reshape(n_tok, d)[row_tok]
    out = _moe_ffn(xs, row_w[:, None], items, exp_w1[0], exp_w3[0], exp_w2[0])
    y0 = out[row_of[:, 0]].reshape(b, l, d)
    y1 = out[row_of[:, 1]].reshape(b, l, d)
    return _final(x_new, gt2, y0, y1)
```

```python
import functools
import math

import jax
import jax.numpy as jnp
from jax import lax
from jax.experimental import pallas as pl
from jax.experimental.pallas import tpu as pltpu

F32 = jnp.float32
BF16 = jnp.bfloat16
HIGHEST = lax.Precision.HIGHEST

D_MODEL = 2048
GRID_W = 64
RWKV_HEADS = 16
RWKV_HEAD_DIM = 64
RWKV_DIM = RWKV_HEADS * RWKV_HEAD_DIM
DECAY_LORA = 96
AAA_LORA = 96
GATE_LORA = 256
RWKV_COLS = 3 * RWKV_DIM + 2 * DECAY_LORA + 2 * AAA_LORA + GATE_LORA
DIFF_HEADS = 8
DIFF_QK_DIM = 64
DIFF_V_DIM = 2 * DIFF_QK_DIM
DIFF_DIM = DIFF_HEADS * DIFF_V_DIM
DIFF_QK_COLS = DIFF_HEADS * 2 * DIFF_QK_DIM
DIFF_COLS = 2 * DIFF_QK_COLS + DIFF_DIM
DIFF_SCALE = DIFF_QK_DIM ** -0.5
ROPE_THETA = 10000.0
ROPE_SUB = DIFF_QK_DIM // 4
GATE_COLS = 2 * D_MODEL
N_GROUPS = 4
EXPERTS_PER_GROUP = 8
N_EXPERTS = N_GROUPS * EXPERTS_PER_GROUP
TOP_K = 2
D_EXPERT = 512
NORM_EPS = 1e-6
SUBLN_EPS = 1e-5
LNX_EPS = 64e-5

LANES = 128
SUBLANES = 8
VMEM_LIMIT_BYTES = 56 * 1024 * 1024

LORA_PAD = LANES
RWKV_PCOLS = 3 * RWKV_DIM + 4 * LORA_PAD + GATE_LORA
CHUNK = 64
V_EXT = 2 * DIFF_V_DIM
SCORE_LOG2_LIMIT = 60.0
ROUTER_PAD = LANES
MOE_TILE = 512


def _cparams(*sem):
    return pltpu.CompilerParams(dimension_semantics=sem, vmem_limit_bytes=VMEM_LIMIT_BYTES)


def _sigmoid(x):
    return 1.0 / (1.0 + jnp.exp(-x))


def _dot(a, b, dims=(((1,), (0,)), ((), ()))):
    return lax.dot_general(a.astype(BF16), b.astype(BF16), dims, preferred_element_type=F32)


def _dot_f32(a, b, dims=(((1,), (0,)), ((), ()))):
    return lax.dot_general(a, b, dims, precision=HIGHEST, preferred_element_type=F32)


_NT = (((1,), (1,)), ((), ()))
_TN = (((0,), (0,)), ((), ()))


def _bf16_parts(x, n):
    parts = []
    for _ in range(n):
        p = x.astype(BF16)
        parts.append(p)
        x = x - p.astype(F32)
    return parts


def _dot_split(x, w_b, n, lhs=True):
    parts = _bf16_parts(x, n)
    if n == 1:
        return jnp.dot(parts[0], w_b, preferred_element_type=F32) if lhs else jnp.dot(w_b, parts[0], preferred_element_type=F32)
    if lhs:
        return jnp.dot(jnp.concatenate(parts, axis=1), jnp.concatenate([w_b] * n, axis=0), preferred_element_type=F32)
    return jnp.dot(jnp.concatenate([w_b] * n, axis=1), jnp.concatenate(parts, axis=0), preferred_element_type=F32)


def _group_ones(width, group):
    r = lax.broadcasted_iota(jnp.int32, (width, width), 0) // group
    c = lax.broadcasted_iota(jnp.int32, (width, width), 1) // group
    return (r == c).astype(BF16)


def _group_sum(x, group, pieces=2):
    ones = _group_ones(LANES, group)
    parts = [_dot_split(x[:, j * LANES:(j + 1) * LANES], ones, pieces) for j in range(x.shape[1] // LANES)]
    return parts[0] if len(parts) == 1 else jnp.concatenate(parts, axis=1)


def _mod_kernel(c_ref, w_ref, b_ref, o_ref):
    c = c_ref[...]
    o_ref[...] = _dot_f32(c * _sigmoid(c), w_ref[...]) + b_ref[...]


def _modulation(cm, ada_w, ada_b):
    rows, d = cm.shape
    n = ada_w.shape[1]
    tn = _pick(n, (1536, 1024, 512, 256, 128))
    return pl.pallas_call(
        _mod_kernel,
        grid=(n // tn,),
        in_specs=[pl.BlockSpec((rows, d), lambda j: (0, 0)),
                  pl.BlockSpec((d, tn), lambda j: (0, j)),
                  pl.BlockSpec((1, tn), lambda j: (0, j))],
        out_specs=pl.BlockSpec((rows, tn), lambda j: (0, j)),
        out_shape=jax.ShapeDtypeStruct((rows, n), F32),
        compiler_params=_cparams("parallel"),
        name="modulation",
    )(cm, ada_w, ada_b.reshape(1, n))


def _norm_mod_kernel(x_ref, g_ref, sc_ref, sh_ref, h_ref):
    x = x_ref[0]
    y = x * lax.rsqrt(jnp.mean(x * x, axis=-1, keepdims=True) + NORM_EPS) * g_ref[...]
    h_ref[0] = (y * (1.0 + sc_ref[0]) + sh_ref[0]).astype(BF16)


def _pick(n, prefs):
    for t in prefs:
        if n % t == 0:
            return t
    return n


def _norm_mod(x, g, sc, sh):
    b, l, d = x.shape
    tm = _pick(l, (512, 256, 128))
    return pl.pallas_call(
        _norm_mod_kernel,
        grid=(b, l // tm),
        in_specs=[pl.BlockSpec((1, tm, d), lambda bi, i: (bi, i, 0)),
                  pl.BlockSpec((1, d), lambda bi, i: (0, 0)),
                  pl.BlockSpec((1, 1, d), lambda bi, i: (bi, 0, 0)),
                  pl.BlockSpec((1, 1, d), lambda bi, i: (bi, 0, 0))],
        out_specs=pl.BlockSpec((1, tm, d), lambda bi, i: (bi, i, 0)),
        out_shape=jax.ShapeDtypeStruct((b, l, d), BF16),
        compiler_params=_cparams("parallel", "parallel"),
        name="norm_mod",
    )(x, g.reshape(1, d), sc, sh)


def _proj_kernel(h_ref, w_ref, o_ref):
    o_ref[0] = jnp.dot(h_ref[0], w_ref[...], preferred_element_type=F32).astype(o_ref.dtype)


def _proj(h, w, out_dtype):
    b, l, d = h.shape
    n = w.shape[1]
    tm = _pick(l, (1024, 512, 256, 128))
    tn = _pick(n, (1024, 768, 512, 256, 128))
    return pl.pallas_call(
        _proj_kernel,
        grid=(b, l // tm, n // tn),
        in_specs=[pl.BlockSpec((1, tm, d), lambda bi, i, j: (bi, i, 0)),
                  pl.BlockSpec((d, tn), lambda bi, i, j: (0, j))],
        out_specs=pl.BlockSpec((1, tm, tn), lambda bi, i, j: (bi, i, j)),
        out_shape=jax.ShapeDtypeStruct((b, l, n), out_dtype),
        compiler_params=_cparams("parallel", "parallel", "parallel"),
        name="proj",
    )(h, w)


def _pair_diag(x_b, mask_b):
    return jnp.concatenate([x_b, x_b], axis=0) * mask_b


def _tri_inverse_pairs(a_list, eye_f, mask_b):
    n = CHUNK
    mm = lambda l, r: jnp.dot(l.astype(BF16), _pair_diag(r.astype(BF16), mask_b), preferred_element_type=F32)
    xs = [eye_f + a for a in a_list]
    ps = [mm(a, a) for a in a_list]
    steps = int(math.log2(n)) - 1
    for s in range(steps):
        if s < steps - 1:
            xps = [mm(jnp.concatenate([x, p], axis=0), p) for x, p in zip(xs, ps)]
            xs = [x + xp[:n] for x, xp in zip(xs, xps)]
            ps = [xp[n:] for xp in xps]
        else:
            xs = [x + mm(x, p) for x, p in zip(xs, ps)]
    return xs


def _rwkv_chunk_kernel(p_ref, pp_ref, pn_ref, mu_ref, w0_ref, w2_ref, a0_ref, a2_ref, g2_ref, kk_ref, ka_ref,
                       rk_ref, q_ref, y0_ref, m_ref, n_ref, bonus_ref, gate_ref):
    i = pl.program_id(1)
    last = pl.num_programs(1) - 1
    c = RWKV_DIM
    hd = RWKV_HEAD_DIM
    p = p_ref[0]
    t_rows = p.shape[0]
    n_sub = t_rows // CHUNK
    row = lax.broadcasted_iota(jnp.int32, (t_rows, 1), 0)
    prev_row = jnp.where(i == 0, 0.0, pp_ref[0, SUBLANES - 1:SUBLANES, :])
    next_row = jnp.where(i == last, 0.0, pn_ref[0, 0:1, :])
    prev = jnp.where(row == 0, prev_row, pltpu.roll(p, 1, axis=0))
    nxt = jnp.where(row == t_rows - 1, next_row, pltpu.roll(p, t_rows - 1, axis=0))
    ps = p + mu_ref[...] * (0.5 * (prev + nxt) - p)

    r, k, v = ps[:, :c], ps[:, c:2 * c], ps[:, 2 * c:3 * c]
    o = 3 * c
    xw = (ps[:, o:o + LORA_PAD], ps[:, o + LORA_PAD:o + 2 * LORA_PAD])
    xa = (ps[:, o + 2 * LORA_PAD:o + 3 * LORA_PAD], ps[:, o + 3 * LORA_PAD:o + 4 * LORA_PAD])
    xg = ps[:, o + 4 * LORA_PAD:]

    gate_ref[0] = _dot(_sigmoid(xg), g2_ref[...]).astype(gate_ref.dtype)
    kk = k * kk_ref[...]
    kk = kk * lax.rsqrt(_group_sum(kk * kk, hd, pieces=1) + 1e-12)

    tr = lax.broadcasted_iota(jnp.int32, (t_rows, t_rows), 0)
    tc = lax.broadcasted_iota(jnp.int32, (t_rows, t_rows), 1)
    same_chunk = (tr // CHUNK) == (tc // CHUNK)
    tr2 = lax.broadcasted_iota(jnp.int32, (CHUNK, LANES), 0)
    lane2 = lax.broadcasted_iota(jnp.int32, (CHUNK, LANES), 1)
    tc2 = lane2 % CHUNK
    lane_head = lane2 // hd
    eye2 = tr2 == tc2
    mask_b = _group_ones(LANES, hd)
    kd_sum = jnp.zeros_like(k)
    v_b = v.astype(BF16)
    psl = [slice(p * LANES, (p + 1) * LANES) for p in range(c // LANES)]
    before2, upto2, prep = [], [], []
    for d in range(2):
        before2.append((tc2 < tr2) if d == 0 else (tc2 > tr2))
        upto = same_chunk & ((tc <= tr) if d == 0 else (tc >= tr))
        upto2.append((tc2 <= tr2) if d == 0 else (tc2 >= tr2))
        z = w0_ref[d:d + 1, :] + _dot(jnp.tanh(xw[d]), w2_ref[d])
        w_log = -(jnp.maximum(-z, 0.0) + jnp.log(1.0 + jnp.exp(-jnp.abs(z)))) - 0.5
        logw = -jnp.exp(w_log)
        a = _sigmoid(a0_ref[d:d + 1, :] + _dot(xa[d], a2_ref[d]))
        kd = k * (1.0 + (a - 1.0) * ka_ref[...])
        kd_sum = kd_sum + kd
        cum = _dot_split(logw, upto.astype(BF16), 3, lhs=False)
        last_row = [u * CHUNK + (CHUNK - 1 if d == 0 else 0) for u in range(n_sub)]
        totals = [cum[t:t + 1, :] for t in last_row]
        total = totals[0]
        for u in range(1, n_sub):
            total = jnp.where(row >= u * CHUNK, totals[u], total)
        e_neg = jnp.exp(-cum)
        e_rest = jnp.exp(total - cum)
        p_total = [jnp.exp(t) for t in totals]
        beta = kk * a
        al = -kk * jnp.exp(cum - logw)
        rt = r * jnp.exp(cum)
        al_b = al.astype(BF16)
        rt_b = rt.astype(BF16)
        bt_b = (beta * e_neg).astype(BF16)
        kt_b = (kd * e_neg).astype(BF16)
        bh_b = (beta * e_rest).astype(BF16)
        kh_b = (kd * e_rest).astype(BF16)
        prep.append((al_b, rt_b, bt_b, kt_b, bh_b, kh_b, rt, p_total))

    rsl = [slice(u * CHUNK, (u + 1) * CHUNK) for u in range(n_sub)]
    dp = [(d, rs, s) for rs in rsl for d in range(2) for s in psl]
    fdot = lambda a, b: jnp.dot(a, b, preferred_element_type=F32)
    bd = lambda x: _pair_diag(x, mask_b)
    v_bd = [bd(v_b[rs, s]) for d, rs, s in dp]
    zeros_sq = jnp.zeros((LANES, LANES), BF16)
    zeros_tl = jnp.zeros((CHUNK, LANES), BF16)
    gm = [lax.dot_general(jnp.concatenate([prep[d][0][rs, s], prep[d][1][rs, s]], axis=0),
                          jnp.concatenate([bd(prep[d][2][rs, s]), bd(prep[d][3][rs, s])], axis=0),
                          _NT, preferred_element_type=F32) for d, rs, s in dp]
    a_ab = [jnp.where(before2[d], x[:CHUNK, :LANES], 0.0) for x, (d, rs, s) in zip(gm, dp)]
    a_ak = [jnp.where(before2[d], x[:CHUNK, LANES:], 0.0).astype(BF16) for x, (d, rs, s) in zip(gm, dp)]
    a_rb = [jnp.where(upto2[d], x[CHUNK:, :LANES], 0.0).astype(BF16) for x, (d, rs, s) in zip(gm, dp)]
    a_rk = [jnp.where(upto2[d], x[CHUNK:, LANES:], 0.0).astype(BF16) for x, (d, rs, s) in zip(gm, dp)]
    akv = [fdot(ak, vd) for ak, vd in zip(a_ak, v_bd)]
    t_inv = _tri_inverse_pairs(a_ab, eye2.astype(F32), mask_b)
    wu = [fdot(t.astype(BF16), jnp.concatenate([bd(prep[d][0][rs, s]), bd(u.astype(BF16))], axis=1))
          for t, u, (d, rs, s) in zip(t_inv, akv, dp)]
    w_b = [x[:, :LANES].astype(BF16) for x in wu]
    u_b = [x[:, LANES:].astype(BF16) for x in wu]
    qy = [fdot(jnp.concatenate([rb, rk], axis=1),
               jnp.concatenate([jnp.concatenate([bd(w), bd(u)], axis=1),
                                jnp.concatenate([zeros_sq, vd], axis=1)], axis=0))
          for rb, rk, w, u, vd in zip(a_rb, a_rk, w_b, u_b, v_bd)]
    full = [lax.dot_general(jnp.concatenate([jnp.concatenate([w, u], axis=1),
                                             jnp.concatenate([zeros_tl, v_b[rs, s]], axis=1)], axis=0),
                            jnp.concatenate([prep[d][4][rs, s], prep[d][5][rs, s]], axis=0),
                            _TN, preferred_element_type=F32)
            for w, u, (d, rs, s) in zip(w_b, u_b, dp)]

    def diag_blocks(x):
        return jnp.where(lane_head == 0, x[:hd], 0.0) + jnp.where(lane_head == 1, x[hd:2 * hd], 0.0)

    npair = len(psl)
    for u, rs in enumerate(rsl):
        for d in range(2):
            sl = slice((2 * u + d) * npair, (2 * u + d + 1) * npair)
            rt, p_total = prep[d][6][rs], prep[d][7][u]
            q_ref[0, d, rs, :] = (rt + jnp.concatenate([x[:, :LANES] for x in qy[sl]], axis=1)).astype(q_ref.dtype)
            y0_ref[0, d, rs, :] = jnp.concatenate([x[:, LANES:] for x in qy[sl]], axis=1).astype(y0_ref.dtype)
            m_ref[0, d, u] = jnp.concatenate([jnp.where(eye2, p_total[:, s], 0.0) + diag_blocks(x[:LANES])
                                              for x, s in zip(full[sl], psl)], axis=1).astype(m_ref.dtype)
            n_ref[0, d, u] = jnp.concatenate([diag_blocks(x[LANES:]) for x in full[sl]], axis=1)
    bonus_ref[0] = (_group_sum(r * kd_sum * rk_ref[...], hd) * v).astype(bonus_ref.dtype)


def _rwkv_chunk_ops(p, params):
    b, l, pc = p.shape
    nc = l // CHUNK
    c = RWKV_DIM
    g = _pick(nc, (2, 1))
    t = g * CHUNK
    hb = t // SUBLANES
    nb8 = l // SUBLANES
    mu, w0, w2, a0, a2, g2, k_k, k_a, r_k = params
    const = lambda shape: pl.BlockSpec(shape, lambda bi, i: (0,) * len(shape))
    tok = lambda: pl.BlockSpec((1, t, c), lambda bi, i: (bi, i, 0))
    return pl.pallas_call(
        _rwkv_chunk_kernel,
        grid=(b, nc // g),
        in_specs=[pl.BlockSpec((1, t, pc), lambda bi, i: (bi, i, 0)),
                  pl.BlockSpec((1, SUBLANES, pc), lambda bi, i: (bi, jnp.maximum(i * hb - 1, 0), 0)),
                  pl.BlockSpec((1, SUBLANES, pc), lambda bi, i: (bi, jnp.minimum((i + 1) * hb, nb8 - 1), 0)),
                  const((1, pc)), const((2, c)), const((2, LORA_PAD, c)), const((2, c)), const((2, LORA_PAD, c)),
                  const((GATE_LORA, c)), const((1, c)), const((1, c)), const((1, c))],
        out_specs=[pl.BlockSpec((1, 2, t, c), lambda bi, i: (bi, 0, i, 0)),
                   pl.BlockSpec((1, 2, t, c), lambda bi, i: (bi, 0, i, 0)),
                   pl.BlockSpec((1, 2, g, RWKV_HEAD_DIM, c), lambda bi, i: (bi, 0, i, 0, 0)),
                   pl.BlockSpec((1, 2, g, RWKV_HEAD_DIM, c), lambda bi, i: (bi, 0, i, 0, 0)),
                   tok(), tok()],
        out_shape=[jax.ShapeDtypeStruct((b, 2, l, c), BF16), jax.ShapeDtypeStruct((b, 2, l, c), BF16),
                   jax.ShapeDtypeStruct((b, 2, nc, RWKV_HEAD_DIM, c), BF16),
                   jax.ShapeDtypeStruct((b, 2, nc, RWKV_HEAD_DIM, c), F32),
                   jax.ShapeDtypeStruct((b, l, c), BF16), jax.ShapeDtypeStruct((b, l, c), BF16)],
        compiler_params=_cparams("parallel", "parallel"),
        name="rwkv_chunk_ops",
    )(p, p, p, mu, w0, w2, a0, a2, g2, k_k, k_a, r_k)


def _rwkv_sweep_kernel(qf_ref, y0f_ref, mf_ref, nf_ref, qb_ref, y0b_ref, mb_ref, nb_ref, s0_ref,
                       yf_ref, yb_ref, sfin_ref, s_ref):
    j = pl.program_id(1)
    hd = RWKV_HEAD_DIM

    @pl.when(j == 0)
    def _():
        s_ref[...] = s0_ref[0]

    dirs = ((qf_ref, y0f_ref, mf_ref, nf_ref, yf_ref), (qb_ref, y0b_ref, mb_ref, nb_ref, yb_ref))
    psl = [slice(p * LANES, (p + 1) * LANES) for p in range(RWKV_DIM // LANES)]
    mask_b = _group_ones(LANES, hd)
    pair_diag = lambda x: _pair_diag(x, mask_b)
    per_step = mf_ref.shape[2]
    state = [s_ref[0], s_ref[1]]
    for u in range(per_step):
        for d, (q_ref, y0_ref, m_ref, n_ref, y_ref) in enumerate(dirs):
            cu = u if d == 0 else per_step - 1 - u
            rows = slice(cu * CHUNK, (cu + 1) * CHUNK)
            s_b = state[d].astype(BF16)
            q_b = q_ref[0, 0, rows, :]
            m_b = m_ref[0, 0, cu]
            ys = [_dot(q_b[:, s], pair_diag(s_b[:, s]), _NT) for s in psl]
            sm = [_dot(s_b[:, s], pair_diag(m_b[:, s])) for s in psl]
            y_ref[0, rows, :] = (y0_ref[0, 0, rows, :].astype(F32) + jnp.concatenate(ys, axis=1)).astype(y_ref.dtype)
            state[d] = n_ref[0, 0, cu] + jnp.concatenate(sm, axis=1)
    s_ref[0] = state[0]
    s_ref[1] = state[1]

    @pl.when(j == pl.num_programs(1) - 1)
    def _():
        sfin_ref[0] = s_ref[...]


def _rwkv_sweep(q, y0, m, n, s0):
    b, _, l, c = q.shape
    nc = l // CHUNK
    hd = RWKV_HEAD_DIM
    g = _pick(nc, (4, 2, 1))
    ns = nc // g
    tokf = lambda: pl.BlockSpec((1, 1, g * CHUNK, c), lambda bi, j: (bi, 0, j, 0))
    tokb = lambda: pl.BlockSpec((1, 1, g * CHUNK, c), lambda bi, j: (bi, 1, ns - 1 - j, 0))
    opf = lambda: pl.BlockSpec((1, 1, g, hd, c), lambda bi, j: (bi, 0, j, 0, 0))
    opb = lambda: pl.BlockSpec((1, 1, g, hd, c), lambda bi, j: (bi, 1, ns - 1 - j, 0, 0))
    return pl.pallas_call(
        _rwkv_sweep_kernel,
        grid=(b, ns),
        in_specs=[tokf(), tokf(), opf(), opf(), tokb(), tokb(), opb(), opb(),
                  pl.BlockSpec((1, 2, hd, c), lambda bi, j: (bi, 0, 0, 0))],
        out_specs=[pl.BlockSpec((1, g * CHUNK, c), lambda bi, j: (bi, j, 0)),
                   pl.BlockSpec((1, g * CHUNK, c), lambda bi, j: (bi, ns - 1 - j, 0)),
                   pl.BlockSpec((1, 2, hd, c), lambda bi, j: (bi, 0, 0, 0))],
        out_shape=[jax.ShapeDtypeStruct((b, l, c), BF16), jax.ShapeDtypeStruct((b, l, c), BF16),
                   jax.ShapeDtypeStruct((b, 2, hd, c), F32)],
        scratch_shapes=[pltpu.VMEM((2, hd, c), F32)],
        compiler_params=_cparams("parallel", "arbitrary"),
        name="rwkv_sweep",
    )(q, y0, m, n, q, y0, m, n, s0)


def _rwkv_scans(px_r, pc_r, rparams):
    b = px_r.shape[0]
    qc, y0c, mc, nc_, _, _ = _rwkv_chunk_ops(pc_r, rparams)
    s_zero = jnp.zeros((b, 2, RWKV_HEAD_DIM, RWKV_DIM), F32)
    _, _, s_ctx = _rwkv_sweep(qc, y0c, mc, nc_, s_zero)
    qx, y0x, mx, nx, bonus, gate = _rwkv_chunk_ops(px_r, rparams)
    yf, yb, _ = _rwkv_sweep(qx, y0x, mx, nx, s_ctx)
    return (yf, yb), bonus, gate


def _diff_prep_kernel(p_ref, cos_ref, sin_ref, qg_ref, kg_ref, q_ref, k_ref, v_ref, *, rope):
    lane = lax.broadcasted_iota(jnp.int32, (1, LANES), 1)
    first = (lane % (2 * ROPE_SUB)) < ROPE_SUB
    for hd in range(DIFF_HEADS):
        for off, g_ref, o_ref, scale in ((0, qg_ref, q_ref, DIFF_SCALE * math.log2(math.e)),
                                         (DIFF_QK_COLS, kg_ref, k_ref, 1.0)):
            cs = slice(hd * LANES, (hd + 1) * LANES)
            xb = p_ref[0, :, off + hd * LANES:off + (hd + 1) * LANES].astype(F32)
            ms = _group_sum(xb * xb, DIFF_QK_DIM, pieces=1) * (1.0 / DIFF_QK_DIM)
            y = xb * lax.rsqrt(ms + NORM_EPS) * g_ref[...]
            if rope:
                swapped = jnp.where(first, pltpu.roll(y, LANES - ROPE_SUB, axis=1), pltpu.roll(y, ROPE_SUB, axis=1))
                y = y * cos_ref[...] + swapped * sin_ref[...]
            o_ref[0, :, cs] = (y * scale).astype(BF16)
    ones = jnp.ones((p_ref.shape[1], V_EXT - DIFF_V_DIM), BF16)
    for hd in range(DIFF_HEADS):
        vb = p_ref[0, :, 2 * DIFF_QK_COLS + hd * DIFF_V_DIM:2 * DIFF_QK_COLS + (hd + 1) * DIFF_V_DIM]
        v_ref[0, :, hd * V_EXT:(hd + 1) * V_EXT] = jnp.concatenate([vb.astype(BF16), ones], axis=1)


def _diff_prep(p, cos_t, sin_t, qg, kg, rope):
    b, l, pc = p.shape
    t = _pick(l, (512, 256, 128))
    tok = lambda w=DIFF_DIM: pl.BlockSpec((1, t, w), lambda bi, i: (bi, i, 0))
    shp = jax.ShapeDtypeStruct((b, l, DIFF_DIM), BF16)
    shp_v = jax.ShapeDtypeStruct((b, l, DIFF_HEADS * V_EXT), BF16)
    return pl.pallas_call(
        functools.partial(_diff_prep_kernel, rope=rope),
        grid=(b, l // t),
        in_specs=[pl.BlockSpec((1, t, pc), lambda bi, i: (bi, i, 0)),
                  pl.BlockSpec((t, LANES), lambda bi, i: (i, 0)),
                  pl.BlockSpec((t, LANES), lambda bi, i: (i, 0)),
                  pl.BlockSpec((1, LANES), lambda bi, i: (0, 0)),
                  pl.BlockSpec((1, LANES), lambda bi, i: (0, 0))],
        out_specs=[tok(), tok(), tok(DIFF_HEADS * V_EXT)],
        out_shape=[shp, shp, shp_v],
        compiler_params=_cparams("parallel", "parallel"),
        name="diff_prep_rope" if rope else "diff_prep",
    )(p, cos_t, sin_t, qg, kg)


def _diff_finish(acc1, acc2, lam_ref, sg_ref, o_ref):
    dv = DIFF_V_DIM
    o = acc1[:, :dv] / acc1[:, dv:] - lam_ref[...] * (acc2[:, :dv] / acc2[:, dv:])
    o = o * lax.rsqrt(jnp.mean(o * o, axis=-1, keepdims=True) + SUBLN_EPS)
    o_ref[0] = (o * sg_ref[...]).astype(o_ref.dtype)


def _flash_online_kernel(lam_ref, sg_ref, q_ref, kc_ref, vc_ref, k_ref, v_ref, o_ref, *, tk):
    qd = DIFF_QK_DIM
    tq = q_ref.shape[1]
    q = q_ref[0]
    qs = (q[:, :qd], q[:, qd:])

    def absorb(state, k, v):
        reps = k.shape[0] // LANES
        out = []
        for mp in range(2):
            m_prev, acc = state[mp]
            s = lax.dot_general(qs[mp], k[:, mp * qd:(mp + 1) * qd], _NT, preferred_element_type=F32)
            m_new = jnp.maximum(m_prev, jnp.max(s, axis=-1, keepdims=True))
            alpha = jnp.exp2(m_prev - m_new)
            pr = jnp.exp2(s - jnp.concatenate([m_new] * reps, axis=1))
            acc_new = jnp.concatenate([alpha, alpha], axis=1) * acc + jnp.dot(pr.astype(BF16), v, preferred_element_type=F32)
            out.append((m_new, acc_new))
        return tuple(out)

    init = (jnp.full((tq, LANES), -jnp.inf, F32), jnp.zeros((tq, V_EXT), F32))
    state = absorb((init, init), kc_ref[0], vc_ref[0])

    def body(j, state):
        rows = pl.ds(pl.multiple_of(j * tk, tk), tk)
        return absorb(state, k_ref[0, rows, :], v_ref[0, rows, :])

    (_, acc1), (_, acc2) = lax.fori_loop(0, k_ref.shape[1] // tk, body, state)
    _diff_finish(acc1, acc2, lam_ref, sg_ref, o_ref)


def _flash_bounded_kernel(lam_ref, sg_ref, q_ref, kc_ref, vc_ref, k_ref, v_ref, o_ref, acc_ref, p_ref, *, tk):
    qd = DIFF_QK_DIM
    q = q_ref[0]
    qs = (q[:, :qd], q[:, qd:])
    n_kv = k_ref.shape[1] // tk

    def weights(mp, k):
        s = lax.dot_general(qs[mp], k[:, mp * qd:(mp + 1) * qd], _NT, preferred_element_type=F32)
        return jnp.exp2(s).astype(BF16)

    def chunk(c):
        return pl.ds(pl.multiple_of(jnp.minimum(c, n_kv - 1) * tk, tk), tk)

    for mp in range(2):
        acc_ref[mp] = jnp.dot(weights(mp, kc_ref[0]), vc_ref[0], preferred_element_type=F32)
        p_ref[0, mp] = weights(mp, k_ref[0, pl.ds(0, tk), :])

    steps = 4 if n_kv % 4 == 0 else 2

    def body(j, carry):
        for u in range(steps):
            c = steps * j + u
            v = v_ref[0, chunk(c), :]
            k_next = k_ref[0, chunk(c + 1), :]
            for mp in range(2):
                acc_ref[mp] += jnp.dot(p_ref[u % 2, mp], v, preferred_element_type=F32)
                p_ref[1 - u % 2, mp] = weights(mp, k_next)
        return carry

    lax.fori_loop(0, n_kv // steps, body, 0)
    _diff_finish(acc_ref[0], acc_ref[1], lam_ref, sg_ref, o_ref)


def _diff_attention(q, kc, vc, k, v, lam_vec, sg_vec, bounded):
    b, l, _ = q.shape
    lc = kc.shape[1]
    tq = _pick(l, (256, 128))
    tk = _pick(l, (512, 256, 128))
    if bounded:
        tq = _pick(l, (1024, 512, 256, 128))
        tk = _pick(l // 2, (512, 256, 128))
        assert l % (2 * tk) == 0
        body = functools.partial(_flash_bounded_kernel, tk=tk)
        scratch = [pltpu.VMEM((2, tq, V_EXT), F32), pltpu.VMEM((2, 2, tq, tk), BF16)]
    else:
        body = functools.partial(_flash_online_kernel, tk=tk)
        scratch = []
    return pl.pallas_call(
        body,
        grid=(b, DIFF_HEADS, l // tq),
        in_specs=[pl.BlockSpec((1, LANES), lambda bi, h, i: (0, 0)),
                  pl.BlockSpec((1, LANES), lambda bi, h, i: (0, 0)),
                  pl.BlockSpec((1, tq, LANES), lambda bi, h, i: (bi, i, h)),
                  pl.BlockSpec((1, lc, LANES), lambda bi, h, i: (bi, 0, h)),
                  pl.BlockSpec((1, lc, V_EXT), lambda bi, h, i: (bi, 0, h)),
                  pl.BlockSpec((1, l, LANES), lambda bi, h, i: (bi, 0, h)),
                  pl.BlockSpec((1, l, V_EXT), lambda bi, h, i: (bi, 0, h))],
        out_specs=pl.BlockSpec((1, tq, LANES), lambda bi, h, i: (bi, i, h)),
        out_shape=jax.ShapeDtypeStruct((b, l, DIFF_DIM), BF16),
        scratch_shapes=scratch,
        compiler_params=_cparams("parallel", "parallel", "arbitrary"),
        name="diff_flash_bounded" if bounded else "diff_flash_online",
    )(lam_vec, sg_vec, q, kc, vc, k, v)


def _merge_kernel(yf_ref, yb_ref, bonus_ref, gate_ref, yd_ref, pg_ref, lg_ref, lb_ref, wpa_ref, wpb_ref, o_ref):
    hd = RWKV_HEAD_DIM
    y = yf_ref[0].astype(F32) + yb_ref[0].astype(F32)
    dev = y - _group_sum(y, hd) * (1.0 / hd)
    var = _group_sum(dev * dev, hd, pieces=1) * (1.0 / hd)
    yn = dev * lax.rsqrt(var + LNX_EPS) * lg_ref[...] + lb_ref[...]
    y_rwkv = (yn + bonus_ref[0].astype(F32)) * gate_ref[0].astype(F32)
    a = _dot(y_rwkv, wpa_ref[...])
    bb = jnp.dot(yd_ref[0], wpb_ref[...], preferred_element_type=F32)
    ga = _sigmoid(pg_ref[0, :, :D_MODEL].astype(F32))
    gb = _sigmoid(pg_ref[0, :, D_MODEL:].astype(F32))
    o_ref[0] = (ga * a + gb * bb).astype(BF16)


def _merge(y_sweep, bonus, gate, y_diff, p_gate, lnx_g, lnx_b, w_pa, w_pb):
    b, l, c = bonus.shape
    d = D_MODEL
    tm = _pick(l, (256, 128))
    const = lambda shape: pl.BlockSpec(shape, lambda bi, i: (0,) * len(shape))
    tok = lambda w: pl.BlockSpec((1, tm, w), lambda bi, i: (bi, i, 0))
    return pl.pallas_call(
        _merge_kernel,
        grid=(b, l // tm),
        in_specs=[tok(c), tok(c), tok(c), tok(c), tok(DIFF_DIM), tok(GATE_COLS),
                  const((1, c)), const((1, c)), const((c, d)), const((DIFF_DIM, d))],
        out_specs=tok(d),
        out_shape=jax.ShapeDtypeStruct((b, l, d), BF16),
        compiler_params=_cparams("parallel", "parallel"),
        name="merge",
    )(y_sweep[0], y_sweep[1], bonus, gate, y_diff, p_gate, lnx_g, lnx_b, w_pa, w_pb)


def _outproj_kernel(mx_ref, x_ref, gt_ref, g_ref, sc_ref, sh_ref, wo_ref, wr_ref, br_ref, xn_ref, h_ref, rt_ref):
    mix = jnp.dot(mx_ref[0], wo_ref[...], preferred_element_type=F32)
    xn = x_ref[0] + gt_ref[0] * mix
    xn_ref[0] = xn
    y = xn * lax.rsqrt(jnp.mean(xn * xn, axis=-1, keepdims=True) + NORM_EPS) * g_ref[...]
    h = y * (1.0 + sc_ref[0]) + sh_ref[0]
    h_hi, h_lo = _bf16_parts(h, 2)
    h_ref[0] = h_hi
    dot = lambda a, b: jnp.dot(a, b, preferred_element_type=F32)
    hh = dot(h_hi, wr_ref[...])
    lg = hh[:, :ROUTER_PAD] + (dot(h_lo, wr_ref[:, :ROUTER_PAD]) + hh[:, ROUTER_PAD:]) + br_ref[...]
    rt_ref[0] = _route_rows(lg)


def _route_rows(lg):
    lane = lax.broadcasted_iota(jnp.int32, lg.shape, 1).astype(F32)
    neg = jnp.float32(-3.0e38)
    far = jnp.float32(LANES)
    gl = jnp.where(lane < N_GROUPS, lg, neg)
    g_max = jnp.max(gl, axis=-1, keepdims=True)
    g_top = jnp.min(jnp.where(gl == g_max, lane, far), axis=-1, keepdims=True)
    p_g = 1.0 / jnp.sum(jnp.where(lane < N_GROUPS, jnp.exp(gl - g_max), 0.0), axis=-1, keepdims=True)
    e_lo = N_GROUPS + EXPERTS_PER_GROUP * g_top
    el = jnp.where((lane >= e_lo) & (lane < e_lo + EXPERTS_PER_GROUP), lg, neg)
    m1 = jnp.max(el, axis=-1, keepdims=True)
    i1 = jnp.min(jnp.where(el == m1, lane, far), axis=-1, keepdims=True)
    el2 = jnp.where(lane == i1, neg, el)
    m2 = jnp.max(el2, axis=-1, keepdims=True)
    i2 = jnp.min(jnp.where(el2 == m2, lane, far), axis=-1, keepdims=True)
    e2 = jnp.exp(m2 - m1)
    gate1 = p_g / (1.0 + e2)
    gate2 = p_g * e2 / (1.0 + e2)
    return jnp.where(lane == 0, i1 - N_GROUPS,
                     jnp.where(lane == 1, i2 - N_GROUPS, jnp.where(lane == 2, gate1, jnp.where(lane == 3, gate2, 0.0))))


def _outproj(mixed, x, gt1, g2, sc2, sh2, w_out, w_router, b_router):
    b, l, d = x.shape
    tm = _pick(l, (512, 256, 128))
    const = lambda shape: pl.BlockSpec(shape, lambda bi, i: (0,) * len(shape))
    tok = lambda w: pl.BlockSpec((1, tm, w), lambda bi, i: (bi, i, 0))
    per_b = lambda: pl.BlockSpec((1, 1, d), lambda bi, i: (bi, 0, 0))
    return pl.pallas_call(
        _outproj_kernel,
        grid=(b, l // tm),
        in_specs=[tok(d), tok(d), per_b(), const((1, d)), per_b(), per_b(),
                  const((d, d)), const((d, 2 * ROUTER_PAD)), const((1, ROUTER_PAD))],
        out_specs=[tok(d), tok(d), tok(ROUTER_PAD)],
        out_shape=[jax.ShapeDtypeStruct((b, l, d), F32), jax.ShapeDtypeStruct((b, l, d), BF16),
                   jax.ShapeDtypeStruct((b, l, ROUTER_PAD), F32)],
        compiler_params=_cparams("parallel", "parallel"),
        name="outproj_router",
    )(mixed, x, gt1, g2, sc2, sh2, w_out, w_router, b_router)


def _moe_kernel(wb_ref, we_ref, lo_ref, hi_ref, x_ref, sw_ref, w1_ref, w3_ref, w2_ref, o_ref,
                w1b_ref, w3b_ref, w2b_ref, cached_ref):
    i = pl.program_id(0)
    lo, hi = lo_ref[i], hi_ref[i]
    live = hi > lo

    @pl.when(i == 0)
    def _():
        cached_ref[0] = -1

    @pl.when((i == 0) | (wb_ref[i] != wb_ref[jnp.maximum(i - 1, 0)]))
    def _():
        o_ref[...] = jnp.zeros(o_ref.shape, o_ref.dtype)

    @pl.when(live & (cached_ref[0] != we_ref[i]))
    def _():
        w1b_ref[...] = w1_ref[0].astype(BF16)
        w3b_ref[...] = w3_ref[0].astype(BF16)
        w2b_ref[...] = w2_ref[0].astype(BF16)
        cached_ref[0] = we_ref[i]

    @pl.when(live)
    def _():
        xb = x_ref[...]
        u = jnp.dot(xb, w1b_ref[...], preferred_element_type=F32)
        g = jnp.dot(xb, w3b_ref[...], preferred_element_type=F32)
        hmid = (u * _sigmoid(u) * g).astype(BF16)
        res = (jnp.dot(hmid, w2b_ref[...], preferred_element_type=F32) * sw_ref[...]).astype(o_ref.dtype)
        row = lax.broadcasted_iota(jnp.int32, (o_ref.shape[0], 1), 0)
        o_ref[...] = jnp.where((row >= lo) & (row < hi), res, o_ref[...])


def _moe_ffn(xs, sw, items, w1, w3, w2):
    n_rows, d = xs.shape
    wb, we, lo, hi = items
    grid_spec = pltpu.PrefetchScalarGridSpec(
        num_scalar_prefetch=4,
        grid=(wb.shape[0],),
        in_specs=[pl.BlockSpec((MOE_TILE, d), lambda i, wb, we, lo, hi: (wb[i], 0)),
                  pl.BlockSpec((MOE_TILE, 1), lambda i, wb, we, lo, hi: (wb[i], 0)),
                  pl.BlockSpec((1, d, D_EXPERT), lambda i, wb, we, lo, hi: (we[i], 0, 0)),
                  pl.BlockSpec((1, d, D_EXPERT), lambda i, wb, we, lo, hi: (we[i], 0, 0)),
                  pl.BlockSpec((1, D_EXPERT, d), lambda i, wb, we, lo, hi: (we[i], 0, 0))],
        out_specs=pl.BlockSpec((MOE_TILE, d), lambda i, wb, we, lo, hi: (wb[i], 0)),
        scratch_shapes=[pltpu.VMEM((d, D_EXPERT), BF16), pltpu.VMEM((d, D_EXPERT), BF16),
                        pltpu.VMEM((D_EXPERT, d), BF16), pltpu.SMEM((1,), jnp.int32)],
    )
    return pl.pallas_call(
        _moe_kernel,
        grid_spec=grid_spec,
        out_shape=jax.ShapeDtypeStruct((n_rows, d), BF16),
        compiler_params=_cparams("arbitrary"),
        name="moe_ffn",
    )(wb, we, lo, hi, xs, sw, w1, w3, w2)


def _final_kernel(x_ref, gt_ref, y0_ref, y1_ref, o_ref):
    o_ref[0] = x_ref[0] + gt_ref[0] * (y0_ref[0].astype(F32) + y1_ref[0].astype(F32))


def _final(x_new, gt2, y0, y1):
    b, l, d = x_new.shape
    tm = _pick(l, (512, 256, 128))
    tok = lambda: pl.BlockSpec((1, tm, d), lambda bi, i: (bi, i, 0))
    return pl.pallas_call(
        _final_kernel,
        grid=(b, l // tm),
        in_specs=[tok(), pl.BlockSpec((1, 1, d), lambda bi, i: (bi, 0, 0)), tok(), tok()],
        out_specs=tok(),
        out_shape=jax.ShapeDtypeStruct((b, l, d), F32),
        compiler_params=_cparams("parallel", "parallel"),
        name="moe_residual",
    )(x_new, gt2, y0, y1)


def _pad_lora_cols(w, widths):
    parts, o = [], 0
    for wd in widths:
        blk = w[..., o:o + wd]
        parts.append(jnp.pad(blk, [(0, 0)] * (w.ndim - 1) + [(0, LORA_PAD - wd)]))
        o += wd
    return jnp.concatenate(parts, axis=-1)


def _rope_tables(l):
    half = DIFF_QK_DIM // 2
    inv_freq = ROPE_THETA ** (-jnp.arange(0, half, 2, dtype=F32) / half)
    t = jnp.arange(l, dtype=jnp.int32)
    rows = (t // GRID_W).astype(F32)[:, None] * inv_freq
    cols = (t % GRID_W).astype(F32)[:, None] * inv_freq
    cos64 = jnp.concatenate([jnp.cos(rows), jnp.cos(rows), jnp.cos(cols), jnp.cos(cols)], axis=1)
    sin64 = jnp.concatenate([-jnp.sin(rows), jnp.sin(rows), -jnp.sin(cols), jnp.sin(cols)], axis=1)
    return jnp.tile(cos64, (1, 2)), jnp.tile(sin64, (1, 2))


def _route(routed, n_tok):
    expert = routed[:, :TOP_K].astype(jnp.int32)
    gate = routed[:, TOP_K:2 * TOP_K]

    n_assign = n_tok * TOP_K
    assert n_assign % MOE_TILE == 0
    flat_e = expert.reshape(-1).astype(jnp.int32)
    ids = jnp.arange(n_assign, dtype=jnp.int32)
    sorted_e, order, sorted_w = lax.sort((flat_e, ids, gate.reshape(-1)), num_keys=1, is_stable=True)
    _, rank = lax.sort((order, ids), num_keys=1)
    ends = jnp.searchsorted(sorted_e, jnp.arange(N_EXPERTS, dtype=jnp.int32), side='right').astype(jnp.int32)
    starts = jnp.concatenate([jnp.zeros((1,), jnp.int32), ends[:-1]])

    nb = n_assign // MOE_TILE
    blk_lo = jnp.arange(nb, dtype=jnp.int32) * MOE_TILE
    e_first = jnp.searchsorted(ends, blk_lo, side='right').astype(jnp.int32)
    e_last = jnp.searchsorted(ends, blk_lo + MOE_TILE - 1, side='right').astype(jnp.int32)
    per_blk = e_last - e_first + 1
    cum = jnp.cumsum(per_blk)
    it = jnp.arange(nb + N_EXPERTS - 1, dtype=jnp.int32)
    wb = jnp.minimum(jnp.searchsorted(cum, it, side='right'), nb - 1).astype(jnp.int32)
    we = jnp.clip(e_first[wb] + it - (cum[wb] - per_blk[wb]), 0, N_EXPERTS - 1)
    lo = jnp.clip(starts[we], blk_lo[wb], blk_lo[wb] + MOE_TILE) - blk_lo[wb]
    hi = jnp.clip(ends[we], blk_lo[wb], blk_lo[wb] + MOE_TILE) - blk_lo[wb]
    hi = jnp.where(it < cum[-1], hi, lo)
    return order // TOP_K, sorted_w, rank.reshape(n_tok, TOP_K), (wb, we, lo, hi)


def kernel(x, c, ctx, c_ctx, ada_w, ada_b, norm1_g, norm2_g, w_in, shift_mu, rwkv_w0, rwkv_w2, rwkv_a0, rwkv_a2,
           rwkv_g2, rwkv_k_k, rwkv_k_a, rwkv_r_k, rwkv_lnx_g, rwkv_lnx_b, qn_g, kn_g, diff_lambda, subln_g,
           w_pa, w_pb, w_out, router_g_w, router_g_b, router_e_w, router_e_b, exp_w1, exp_w3, exp_w2):
    assert ada_w.shape[0] == 1, "single-layer block"
    b, l, d = x.shape
    lc = ctx.shape[1]
    lam_init = 0.8 - 0.6 * math.exp(-0.3 * 0)
    lv = diff_lambda[0].astype(F32)
    lam = jnp.exp(jnp.sum(lv[0] * lv[1])) - jnp.exp(jnp.sum(lv[2] * lv[3])) + lam_init

    rows = (b + 1 + SUBLANES - 1) // SUBLANES * SUBLANES
    cm = jnp.zeros((rows, d), F32).at[:b].set(c).at[b].set(c_ctx)
    mod = _modulation(cm, ada_w[0], ada_b[0])
    sh1, sc1, gt1, sh2, sc2, gt2 = [mod[:b, None, k * d:(k + 1) * d] for k in range(6)]
    csh1, csc1 = [jnp.broadcast_to(mod[b, k * d:(k + 1) * d], (b, 1, d)) for k in range(2)]

    w = w_in[0]
    lora_widths = (DECAY_LORA, DECAY_LORA, AAA_LORA, AAA_LORA)
    o_lora = 3 * RWKV_DIM
    o_glora = o_lora + sum(lora_widths)
    pad_cols = lambda m: jnp.concatenate(
        [m[..., :o_lora], _pad_lora_cols(m[..., o_lora:o_glora], lora_widths), m[..., o_glora:RWKV_COLS]], axis=-1)
    w_rwkv = pad_cols(w).astype(BF16)
    w_diff = w[:, RWKV_COLS:RWKV_COLS + DIFF_COLS].astype(BF16)
    w_gate = w[:, RWKV_COLS + DIFF_COLS:].astype(BF16)
    g1 = norm1_g[0]
    hx = _norm_mod(x, g1, sc1, sh1)
    hc = _norm_mod(ctx, g1, csc1, csh1)
    px_r, px_d, px_g = _proj(hx, w_rwkv, F32), _proj(hx, w_diff, BF16), _proj(hx, w_gate, BF16)
    pc_r, pc_d = _proj(hc, w_rwkv, F32), _proj(hc, w_diff, BF16)

    pad_rows = lambda m: jnp.pad(m, ((0, 0), (0, LORA_PAD - m.shape[1]), (0, 0)))
    rparams = (pad_cols(shift_mu[0])[None], rwkv_w0[0], pad_rows(rwkv_w2[0]), rwkv_a0[0], pad_rows(rwkv_a2[0]),
               rwkv_g2[0], rwkv_k_k[0][None], rwkv_k_a[0][None], rwkv_r_k[0].reshape(1, RWKV_DIM))
    y_sweep, bonus, gate = _rwkv_scans(px_r, pc_r, rparams)

    cos_t, sin_t = _rope_tables(l)
    qg = jnp.tile(qn_g[0], 2)[None]
    kg = jnp.tile(kn_g[0], 2)[None]
    q_x, k_x, v_x = _diff_prep(px_d, cos_t, sin_t, qg, kg, True)
    _, k_c, v_c = _diff_prep(pc_d, cos_t[:lc], sin_t[:lc], qg, kg, False)
    lam_vec = jnp.full((1, LANES), lam, F32)
    sg_vec = (subln_g[0] * (1.0 - lam_init))[None]
    score_bound = (1.05 * DIFF_QK_DIM * DIFF_SCALE * math.log2(math.e)
                   * jnp.max(jnp.abs(qn_g[0])) * jnp.max(jnp.abs(kn_g[0])))
    attn_args = (q_x, k_c, v_c, k_x, v_x, lam_vec, sg_vec)
    y_diff = lax.cond(score_bound <= SCORE_LOG2_LIMIT,
                      lambda a: _diff_attention(*a, bounded=True),
                      lambda a: _diff_attention(*a, bounded=False), attn_args)

    mixed = _merge(y_sweep, bonus, gate, y_diff, px_g, rwkv_lnx_g[0][None], rwkv_lnx_b[0][None],
                   w_pa[0].astype(BF16), w_pb[0].astype(BF16))
    n_r = N_GROUPS + N_EXPERTS
    w_router = jnp.zeros((d, ROUTER_PAD), F32).at[:, :N_GROUPS].set(router_g_w[0]).at[:, N_GROUPS:n_r].set(router_e_w[0])
    b_router = jnp.zeros((1, ROUTER_PAD), F32).at[0, :N_GROUPS].set(router_g_b[0]).at[0, N_GROUPS:n_r].set(router_e_b[0])
    w_router_hi = w_router.astype(BF16)
    w_router_lo = (w_router - w_router_hi.astype(F32)).astype(BF16)
    x_new, h2, routed = _outproj(mixed, x, gt1, norm2_g[0][None], sc2, sh2, w_out[0].astype(BF16),
                                 jnp.concatenate([w_router_hi, w_router_lo], axis=1), b_router)

    n_tok = b * l
    row_tok, row_w, row_of, items = _route(routed.reshape(n_tok, ROUTER_PAD), n_tok)
    xs = h2.reshape(n_tok, d)[row_tok]
    out = _moe_ffn(xs, row_w[:, None], items, exp_w1[0], exp_w3[0], exp_w2[0])
    y0 = out[row_of[:, 0]].reshape(b, l, d)
    y1 = out[row_of[:, 1]].reshape(b, l, d)
    return _final(x_new, gt2, y0, y1)
```

```python
import functools
import math

import jax
import jax.numpy as jnp
from jax import lax
from jax.experimental import pallas as pl
from jax.experimental.pallas import tpu as pltpu

F32 = jnp.float32
BF16 = jnp.bfloat16
HIGHEST = lax.Precision.HIGHEST

D_MODEL = 2048
GRID_W = 64
RWKV_HEADS = 16
RWKV_HEAD_DIM = 64
RWKV_DIM = RWKV_HEADS * RWKV_HEAD_DIM
DECAY_LORA = 96
AAA_LORA = 96
GATE_LORA = 256
RWKV_COLS = 3 * RWKV_DIM + 2 * DECAY_LORA + 2 * AAA_LORA + GATE_LORA
DIFF_HEADS = 8
DIFF_QK_DIM = 64
DIFF_V_DIM = 2 * DIFF_QK_DIM
DIFF_DIM = DIFF_HEADS * DIFF_V_DIM
DIFF_QK_COLS = DIFF_HEADS * 2 * DIFF_QK_DIM
DIFF_COLS = 2 * DIFF_QK_COLS + DIFF_DIM
DIFF_SCALE = DIFF_QK_DIM ** -0.5
ROPE_THETA = 10000.0
ROPE_SUB = DIFF_QK_DIM // 4
GATE_COLS = 2 * D_MODEL
N_GROUPS = 4
EXPERTS_PER_GROUP = 8
N_EXPERTS = N_GROUPS * EXPERTS_PER_GROUP
TOP_K = 2
D_EXPERT = 512
NORM_EPS = 1e-6
SUBLN_EPS = 1e-5
LNX_EPS = 64e-5

LANES = 128
SUBLANES = 8
VMEM_LIMIT_BYTES = 56 * 1024 * 1024

LORA_PAD = LANES
RWKV_PCOLS = 3 * RWKV_DIM + 4 * LORA_PAD + GATE_LORA
CHUNK = 64
V_EXT = 2 * DIFF_V_DIM
SCORE_LOG2_LIMIT = 60.0
ROUTER_PAD = LANES
MOE_TILE = 512


def _cparams(*sem):
    return pltpu.CompilerParams(dimension_semantics=sem, vmem_limit_bytes=VMEM_LIMIT_BYTES)


def _sigmoid(x):
    return 1.0 / (1.0 + jnp.exp(-x))


def _dot(a, b, dims=(((1,), (0,)), ((), ()))):
    return lax.dot_general(a.astype(BF16), b.astype(BF16), dims, preferred_element_type=F32)


def _dot_f32(a, b, dims=(((1,), (0,)), ((), ()))):
    return lax.dot_general(a, b, dims, precision=HIGHEST, preferred_element_type=F32)


_NT = (((1,), (1,)), ((), ()))
_TN = (((0,), (0,)), ((), ()))


def _bf16_parts(x, n):
    parts = []
    for _ in range(n):
        p = x.astype(BF16)
        parts.append(p)
        x = x - p.astype(F32)
    return parts


def _dot_split(x, w_b, n, lhs=True):
    parts = _bf16_parts(x, n)
    if n == 1:
        return jnp.dot(parts[0], w_b, preferred_element_type=F32) if lhs else jnp.dot(w_b, parts[0], preferred_element_type=F32)
    if lhs:
        return jnp.dot(jnp.concatenate(parts, axis=1), jnp.concatenate([w_b] * n, axis=0), preferred_element_type=F32)
    return jnp.dot(jnp.concatenate([w_b] * n, axis=1), jnp.concatenate(parts, axis=0), preferred_element_type=F32)


def _group_ones(width, group):
    r = lax.broadcasted_iota(jnp.int32, (width, width), 0) // group
    c = lax.broadcasted_iota(jnp.int32, (width, width), 1) // group
    return (r == c).astype(BF16)


def _group_sum(x, group, pieces=2):
    ones = _group_ones(LANES, group)
    parts = [_dot_split(x[:, j * LANES:(j + 1) * LANES], ones, pieces) for j in range(x.shape[1] // LANES)]
    return parts[0] if len(parts) == 1 else jnp.concatenate(parts, axis=1)


def _mod_kernel(c_ref, w_ref, b_ref, o_ref):
    c = c_ref[...]
    o_ref[...] = _dot_f32(c * _sigmoid(c), w_ref[...]) + b_ref[...]


def _modulation(cm, ada_w, ada_b):
    rows, d = cm.shape
    n = ada_w.shape[1]
    tn = _pick(n, (1536, 1024, 512, 256, 128))
    return pl.pallas_call(
        _mod_kernel,
        grid=(n // tn,),
        in_specs=[pl.BlockSpec((rows, d), lambda j: (0, 0)),
                  pl.BlockSpec((d, tn), lambda j: (0, j)),
                  pl.BlockSpec((1, tn), lambda j: (0, j))],
        out_specs=pl.BlockSpec((rows, tn), lambda j: (0, j)),
        out_shape=jax.ShapeDtypeStruct((rows, n), F32),
        compiler_params=_cparams("parallel"),
        name="modulation",
    )(cm, ada_w, ada_b.reshape(1, n))


def _norm_mod_kernel(x_ref, g_ref, sc_ref, sh_ref, h_ref):
    x = x_ref[0]
    y = x * lax.rsqrt(jnp.mean(x * x, axis=-1, keepdims=True) + NORM_EPS) * g_ref[...]
    h_ref[0] = (y * (1.0 + sc_ref[0]) + sh_ref[0]).astype(BF16)


def _pick(n, prefs):
    for t in prefs:
        if n % t == 0:
            return t
    return n


def _norm_mod(x, g, sc, sh):
    b, l, d = x.shape
    tm = _pick(l, (512, 256, 128))
    return pl.pallas_call(
        _norm_mod_kernel,
        grid=(b, l // tm),
        in_specs=[pl.BlockSpec((1, tm, d), lambda bi, i: (bi, i, 0)),
                  pl.BlockSpec((1, d), lambda bi, i: (0, 0)),
                  pl.BlockSpec((1, 1, d), lambda bi, i: (bi, 0, 0)),
                  pl.BlockSpec((1, 1, d), lambda bi, i: (bi, 0, 0))],
        out_specs=pl.BlockSpec((1, tm, d), lambda bi, i: (bi, i, 0)),
        out_shape=jax.ShapeDtypeStruct((b, l, d), BF16),
        compiler_params=_cparams("parallel", "parallel"),
        name="norm_mod",
    )(x, g.reshape(1, d), sc, sh)


def _proj_kernel(h_ref, w_ref, o_ref):
    o_ref[0] = jnp.dot(h_ref[0], w_ref[...], preferred_element_type=F32).astype(o_ref.dtype)


def _proj(h, w, out_dtype):
    b, l, d = h.shape
    n = w.shape[1]
    tm = _pick(l, (1024, 512, 256, 128))
    tn = _pick(n, (1024, 768, 512, 256, 128))
    return pl.pallas_call(
        _proj_kernel,
        grid=(b, l // tm, n // tn),
        in_specs=[pl.BlockSpec((1, tm, d), lambda bi, i, j: (bi, i, 0)),
                  pl.BlockSpec((d, tn), lambda bi, i, j: (0, j))],
        out_specs=pl.BlockSpec((1, tm, tn), lambda bi, i, j: (bi, i, j)),
        out_shape=jax.ShapeDtypeStruct((b, l, n), out_dtype),
        compiler_params=_cparams("parallel", "parallel", "parallel"),
        name="proj",
    )(h, w)


def _pair_diag(x_b, mask_b):
    return jnp.concatenate([x_b, x_b], axis=0) * mask_b


def _tri_inverse_pairs(a_list, eye_f, mask_b):
    n = CHUNK
    mm = lambda l, r: jnp.dot(l.astype(BF16), _pair_diag(r.astype(BF16), mask_b), preferred_element_type=F32)
    xs = [eye_f + a for a in a_list]
    ps = [mm(a, a) for a in a_list]
    steps = int(math.log2(n)) - 1
    for s in range(steps):
        if s < steps - 1:
            xps = [mm(jnp.concatenate([x, p], axis=0), p) for x, p in zip(xs, ps)]
            xs = [x + xp[:n] for x, xp in zip(xs, xps)]
            ps = [xp[n:] for xp in xps]
        else:
            xs = [x + mm(x, p) for x, p in zip(xs, ps)]
    return xs


def _rwkv_chunk_kernel(p_ref, pp_ref, pn_ref, mu_ref, w0_ref, w2_ref, a0_ref, a2_ref, g2_ref, kk_ref, ka_ref,
                       rk_ref, q_ref, y0_ref, m_ref, n_ref, bonus_ref, gate_ref):
    i = pl.program_id(1)
    last = pl.num_programs(1) - 1
    c = RWKV_DIM
    hd = RWKV_HEAD_DIM
    p = p_ref[0]
    t_rows = p.shape[0]
    n_sub = t_rows // CHUNK
    row = lax.broadcasted_iota(jnp.int32, (t_rows, 1), 0)
    prev_row = jnp.where(i == 0, 0.0, pp_ref[0, SUBLANES - 1:SUBLANES, :])
    next_row = jnp.where(i == last, 0.0, pn_ref[0, 0:1, :])
    prev = jnp.where(row == 0, prev_row, pltpu.roll(p, 1, axis=0))
    nxt = jnp.where(row == t_rows - 1, next_row, pltpu.roll(p, t_rows - 1, axis=0))
    ps = p + mu_ref[...] * (0.5 * (prev + nxt) - p)

    r, k, v = ps[:, :c], ps[:, c:2 * c], ps[:, 2 * c:3 * c]
    o = 3 * c
    xw = (ps[:, o:o + LORA_PAD], ps[:, o + LORA_PAD:o + 2 * LORA_PAD])
    xa = (ps[:, o + 2 * LORA_PAD:o + 3 * LORA_PAD], ps[:, o + 3 * LORA_PAD:o + 4 * LORA_PAD])
    xg = ps[:, o + 4 * LORA_PAD:]

    gate_ref[0] = _dot(_sigmoid(xg), g2_ref[...]).astype(gate_ref.dtype)
    kk = k * kk_ref[...]
    kk = kk * lax.rsqrt(_group_sum(kk * kk, hd, pieces=1) + 1e-12)

    tr = lax.broadcasted_iota(jnp.int32, (t_rows, t_rows), 0)
    tc = lax.broadcasted_iota(jnp.int32, (t_rows, t_rows), 1)
    same_chunk = (tr // CHUNK) == (tc // CHUNK)
    tr2 = lax.broadcasted_iota(jnp.int32, (CHUNK, LANES), 0)
    lane2 = lax.broadcasted_iota(jnp.int32, (CHUNK, LANES), 1)
    tc2 = lane2 % CHUNK
    lane_head = lane2 // hd
    eye2 = tr2 == tc2
    mask_b = _group_ones(LANES, hd)
    kd_sum = jnp.zeros_like(k)
    v_b = v.astype(BF16)
    psl = [slice(p * LANES, (p + 1) * LANES) for p in range(c // LANES)]
    before2, upto2, prep = [], [], []
    for d in range(2):
        before2.append((tc2 < tr2) if d == 0 else (tc2 > tr2))
        upto = same_chunk & ((tc <= tr) if d == 0 else (tc >= tr))
        upto2.append((tc2 <= tr2) if d == 0 else (tc2 >= tr2))
        z = w0_ref[d:d + 1, :] + _dot(jnp.tanh(xw[d]), w2_ref[d])
        w_log = -(jnp.maximum(-z, 0.0) + jnp.log(1.0 + jnp.exp(-jnp.abs(z)))) - 0.5
        logw = -jnp.exp(w_log)
        a = _sigmoid(a0_ref[d:d + 1, :] + _dot(xa[d], a2_ref[d]))
        kd = k * (1.0 + (a - 1.0) * ka_ref[...])
        kd_sum = kd_sum + kd
        cum = _dot_split(logw, upto.astype(BF16), 3, lhs=False)
        last_row = [u * CHUNK + (CHUNK - 1 if d == 0 else 0) for u in range(n_sub)]
        totals = [cum[t:t + 1, :] for t in last_row]
        total = totals[0]
        for u in range(1, n_sub):
            total = jnp.where(row >= u * CHUNK, totals[u], total)
        e_neg = jnp.exp(-cum)
        e_rest = jnp.exp(total - cum)
        p_total = [jnp.exp(t) for t in totals]
        beta = kk * a
        al = -kk * jnp.exp(cum - logw)
        rt = r * jnp.exp(cum)
        al_b = al.astype(BF16)
        rt_b = rt.astype(BF16)
        bt_b = (beta * e_neg).astype(BF16)
        kt_b = (kd * e_neg).astype(BF16)
        bh_b = (beta * e_rest).astype(BF16)
        kh_b = (kd * e_rest).astype(BF16)
        prep.append((al_b, rt_b, bt_b, kt_b, bh_b, kh_b, rt, p_total))

    rsl = [slice(u * CHUNK, (u + 1) * CHUNK) for u in range(n_sub)]
    dp = [(d, rs, s) for rs in rsl for d in range(2) for s in psl]
    fdot = lambda a, b: jnp.dot(a, b, preferred_element_type=F32)
    bd = lambda x: _pair_diag(x, mask_b)
    v_bd = [bd(v_b[rs, s]) for d, rs, s in dp]
    zeros_sq = jnp.zeros((LANES, LANES), BF16)
    zeros_tl = jnp.zeros((CHUNK, LANES), BF16)
    gm = [lax.dot_general(jnp.concatenate([prep[d][0][rs, s], prep[d][1][rs, s]], axis=0),
                          jnp.concatenate([bd(prep[d][2][rs, s]), bd(prep[d][3][rs, s])], axis=0),
                          _NT, preferred_element_type=F32) for d, rs, s in dp]
    a_ab = [jnp.where(before2[d], x[:CHUNK, :LANES], 0.0) for x, (d, rs, s) in zip(gm, dp)]
    a_ak = [jnp.where(before2[d], x[:CHUNK, LANES:], 0.0).astype(BF16) for x, (d, rs, s) in zip(gm, dp)]
    a_rb = [jnp.where(upto2[d], x[CHUNK:, :LANES], 0.0).astype(BF16) for x, (d, rs, s) in zip(gm, dp)]
    a_rk = [jnp.where(upto2[d], x[CHUNK:, LANES:], 0.0).astype(BF16) for x, (d, rs, s) in zip(gm, dp)]
    akv = [fdot(ak, vd) for ak, vd in zip(a_ak, v_bd)]
    t_inv = _tri_inverse_pairs(a_ab, eye2.astype(F32), mask_b)
    wu = [fdot(t.astype(BF16), jnp.concatenate([bd(prep[d][0][rs, s]), bd(u.astype(BF16))], axis=1))
          for t, u, (d, rs, s) in zip(t_inv, akv, dp)]
    w_b = [x[:, :LANES].astype(BF16) for x in wu]
    u_b = [x[:, LANES:].astype(BF16) for x in wu]
    qy = [fdot(jnp.concatenate([rb, rk], axis=1),
               jnp.concatenate([jnp.concatenate([bd(w), bd(u)], axis=1),
                                jnp.concatenate([zeros_sq, vd], axis=1)], axis=0))
          for rb, rk, w, u, vd in zip(a_rb, a_rk, w_b, u_b, v_bd)]
    full = [lax.dot_general(jnp.concatenate([jnp.concatenate([w, u], axis=1),
                                             jnp.concatenate([zeros_tl, v_b[rs, s]], axis=1)], axis=0),
                            jnp.concatenate([prep[d][4][rs, s], prep[d][5][rs, s]], axis=0),
                            _TN, preferred_element_type=F32)
            for w, u, (d, rs, s) in zip(w_b, u_b, dp)]

    def diag_blocks(x):
        return jnp.where(lane_head == 0, x[:hd], 0.0) + jnp.where(lane_head == 1, x[hd:2 * hd], 0.0)

    npair = len(psl)
    for u, rs in enumerate(rsl):
        for d in range(2):
            sl = slice((2 * u + d) * npair, (2 * u + d + 1) * npair)
            rt, p_total = prep[d][6][rs], prep[d][7][u]
            q_ref[0, d, rs, :] = (rt + jnp.concatenate([x[:, :LANES] for x in qy[sl]], axis=1)).astype(q_ref.dtype)
            y0_ref[0, d, rs, :] = jnp.concatenate([x[:, LANES:] for x in qy[sl]], axis=1).astype(y0_ref.dtype)
            m_ref[0, d, u] = jnp.concatenate([jnp.where(eye2, p_total[:, s], 0.0) + diag_blocks(x[:LANES])
                                              for x, s in zip(full[sl], psl)], axis=1).astype(m_ref.dtype)
            n_ref[0, d, u] = jnp.concatenate([diag_blocks(x[LANES:]) for x in full[sl]], axis=1)
    bonus_ref[0] = (_group_sum(r * kd_sum * rk_ref[...], hd) * v).astype(bonus_ref.dtype)


def _rwkv_chunk_ops(p, params):
    b, l, pc = p.shape
    nc = l // CHUNK
    c = RWKV_DIM
    g = _pick(nc, (2, 1))
    t = g * CHUNK
    hb = t // SUBLANES
    nb8 = l // SUBLANES
    mu, w0, w2, a0, a2, g2, k_k, k_a, r_k = params
    const = lambda shape: pl.BlockSpec(shape, lambda bi, i: (0,) * len(shape))
    tok = lambda: pl.BlockSpec((1, t, c), lambda bi, i: (bi, i, 0))
    return pl.pallas_call(
        _rwkv_chunk_kernel,
        grid=(b, nc // g),
        in_specs=[pl.BlockSpec((1, t, pc), lambda bi, i: (bi, i, 0)),
                  pl.BlockSpec((1, SUBLANES, pc), lambda bi, i: (bi, jnp.maximum(i * hb - 1, 0), 0)),
                  pl.BlockSpec((1, SUBLANES, pc), lambda bi, i: (bi, jnp.minimum((i + 1) * hb, nb8 - 1), 0)),
                  const((1, pc)), const((2, c)), const((2, LORA_PAD, c)), const((2, c)), const((2, LORA_PAD, c)),
                  const((GATE_LORA, c)), const((1, c)), const((1, c)), const((1, c))],
        out_specs=[pl.BlockSpec((1, 2, t, c), lambda bi, i: (bi, 0, i, 0)),
                   pl.BlockSpec((1, 2, t, c), lambda bi, i: (bi, 0, i, 0)),
                   pl.BlockSpec((1, 2, g, RWKV_HEAD_DIM, c), lambda bi, i: (bi, 0, i, 0, 0)),
                   pl.BlockSpec((1, 2, g, RWKV_HEAD_DIM, c), lambda bi, i: (bi, 0, i, 0, 0)),
                   tok(), tok()],
        out_shape=[jax.ShapeDtypeStruct((b, 2, l, c), BF16), jax.ShapeDtypeStruct((b, 2, l, c), BF16),
                   jax.ShapeDtypeStruct((b, 2, nc, RWKV_HEAD_DIM, c), BF16),
                   jax.ShapeDtypeStruct((b, 2, nc, RWKV_HEAD_DIM, c), F32),
                   jax.ShapeDtypeStruct((b, l, c), BF16), jax.ShapeDtypeStruct((b, l, c), BF16)],
        compiler_params=_cparams("parallel", "parallel"),
        name="rwkv_chunk_ops",
    )(p, p, p, mu, w0, w2, a0, a2, g2, k_k, k_a, r_k)


def _rwkv_sweep_kernel(qf_ref, y0f_ref, mf_ref, nf_ref, qb_ref, y0b_ref, mb_ref, nb_ref, s0_ref,
                       yf_ref, yb_ref, sfin_ref, s_ref):
    j = pl.program_id(1)
    hd = RWKV_HEAD_DIM

    @pl.when(j == 0)
    def _():
        s_ref[...] = s0_ref[0]

    dirs = ((qf_ref, y0f_ref, mf_ref, nf_ref, yf_ref), (qb_ref, y0b_ref, mb_ref, nb_ref, yb_ref))
    psl = [slice(p * LANES, (p + 1) * LANES) for p in range(RWKV_DIM // LANES)]
    mask_b = _group_ones(LANES, hd)
    pair_diag = lambda x: _pair_diag(x, mask_b)
    per_step = mf_ref.shape[2]
    state = [s_ref[0], s_ref[1]]
    for u in range(per_step):
        for d, (q_ref, y0_ref, m_ref, n_ref, y_ref) in enumerate(dirs):
            cu = u if d == 0 else per_step - 1 - u
            rows = slice(cu * CHUNK, (cu + 1) * CHUNK)
            s_b = state[d].astype(BF16)
            q_b = q_ref[0, 0, rows, :]
            m_b = m_ref[0, 0, cu]
            ys = [_dot(q_b[:, s], pair_diag(s_b[:, s]), _NT) for s in psl]
            sm = [_dot(s_b[:, s], pair_diag(m_b[:, s])) for s in psl]
            y_ref[0, rows, :] = (y0_ref[0, 0, rows, :].astype(F32) + jnp.concatenate(ys, axis=1)).astype(y_ref.dtype)
            state[d] = n_ref[0, 0, cu] + jnp.concatenate(sm, axis=1)
    s_ref[0] = state[0]
    s_ref[1] = state[1]

    @pl.when(j == pl.num_programs(1) - 1)
    def _():
        sfin_ref[0] = s_ref[...]


def _rwkv_sweep(q, y0, m, n, s0):
    b, _, l, c = q.shape
    nc = l // CHUNK
    hd = RWKV_HEAD_DIM
    g = _pick(nc, (4, 2, 1))
    ns = nc // g
    tokf = lambda: pl.BlockSpec((1, 1, g * CHUNK, c), lambda bi, j: (bi, 0, j, 0))
    tokb = lambda: pl.BlockSpec((1, 1, g * CHUNK, c), lambda bi, j: (bi, 1, ns - 1 - j, 0))
    opf = lambda: pl.BlockSpec((1, 1, g, hd, c), lambda bi, j: (bi, 0, j, 0, 0))
    opb = lambda: pl.BlockSpec((1, 1, g, hd, c), lambda bi, j: (bi, 1, ns - 1 - j, 0, 0))
    return pl.pallas_call(
        _rwkv_sweep_kernel,
        grid=(b, ns),
        in_specs=[tokf(), tokf(), opf(), opf(), tokb(), tokb(), opb(), opb(),
                  pl.BlockSpec((1, 2, hd, c), lambda bi, j: (bi, 0, 0, 0))],
        out_specs=[pl.BlockSpec((1, g * CHUNK, c), lambda bi, j: (bi, j, 0)),
                   pl.BlockSpec((1, g * CHUNK, c), lambda bi, j: (bi, ns - 1 - j, 0)),
                   pl.BlockSpec((1, 2, hd, c), lambda bi, j: (bi, 0, 0, 0))],
        out_shape=[jax.ShapeDtypeStruct((b, l, c), BF16), jax.ShapeDtypeStruct((b, l, c), BF16),
                   jax.ShapeDtypeStruct((b, 2, hd, c), F32)],
        scratch_shapes=[pltpu.VMEM((2, hd, c), F32)],
        compiler_params=_cparams("parallel", "arbitrary"),
        name="rwkv_sweep",
    )(q, y0, m, n, q, y0, m, n, s0)


def _rwkv_scans(px_r, pc_r, rparams):
    b = px_r.shape[0]
    qc, y0c, mc, nc_, _, _ = _rwkv_chunk_ops(pc_r, rparams)
    s_zero = jnp.zeros((b, 2, RWKV_HEAD_DIM, RWKV_DIM), F32)
    _, _, s_ctx = _rwkv_sweep(qc, y0c, mc, nc_, s_zero)
    qx, y0x, mx, nx, bonus, gate = _rwkv_chunk_ops(px_r, rparams)
    yf, yb, _ = _rwkv_sweep(qx, y0x, mx, nx, s_ctx)
    return (yf, yb), bonus, gate


def _diff_prep_kernel(p_ref, cos_ref, sin_ref, qg_ref, kg_ref, q_ref, k_ref, v_ref, *, rope):
    lane = lax.broadcasted_iota(jnp.int32, (1, LANES), 1)
    first = (lane % (2 * ROPE_SUB)) < ROPE_SUB
    for hd in range(DIFF_HEADS):
        for off, g_ref, o_ref, scale in ((0, qg_ref, q_ref, DIFF_SCALE * math.log2(math.e)),
                                         (DIFF_QK_COLS, kg_ref, k_ref, 1.0)):
            cs = slice(hd * LANES, (hd + 1) * LANES)
            xb = p_ref[0, :, off + hd * LANES:off + (hd + 1) * LANES].astype(F32)
            ms = _group_sum(xb * xb, DIFF_QK_DIM, pieces=1) * (1.0 / DIFF_QK_DIM)
            y = xb * lax.rsqrt(ms + NORM_EPS) * g_ref[...]
            if rope:
                swapped = jnp.where(first, pltpu.roll(y, LANES - ROPE_SUB, axis=1), pltpu.roll(y, ROPE_SUB, axis=1))
                y = y * cos_ref[...] + swapped * sin_ref[...]
            o_ref[0, :, cs] = (y * scale).astype(BF16)
    ones = jnp.ones((p_ref.shape[1], V_EXT - DIFF_V_DIM), BF16)
    for hd in range(DIFF_HEADS):
        vb = p_ref[0, :, 2 * DIFF_QK_COLS + hd * DIFF_V_DIM:2 * DIFF_QK_COLS + (hd + 1) * DIFF_V_DIM]
        v_ref[0, :, hd * V_EXT:(hd + 1) * V_EXT] = jnp.concatenate([vb.astype(BF16), ones], axis=1)


def _diff_prep(p, cos_t, sin_t, qg, kg, rope):
    b, l, pc = p.shape
    t = _pick(l, (512, 256, 128))
    tok = lambda w=DIFF_DIM: pl.BlockSpec((1, t, w), lambda bi, i: (bi, i, 0))
    shp = jax.ShapeDtypeStruct((b, l, DIFF_DIM), BF16)
    shp_v = jax.ShapeDtypeStruct((b, l, DIFF_HEADS * V_EXT), BF16)
    return pl.pallas_call(
        functools.partial(_diff_prep_kernel, rope=rope),
        grid=(b, l // t),
        in_specs=[pl.BlockSpec((1, t, pc), lambda bi, i: (bi, i, 0)),
                  pl.BlockSpec((t, LANES), lambda bi, i: (i, 0)),
                  pl.BlockSpec((t, LANES), lambda bi, i: (i, 0)),
                  pl.BlockSpec((1, LANES), lambda bi, i: (0, 0)),
                  pl.BlockSpec((1, LANES), lambda bi, i: (0, 0))],
        out_specs=[tok(), tok(), tok(DIFF_HEADS * V_EXT)],
        out_shape=[shp, shp, shp_v],
        compiler_params=_cparams("parallel", "parallel"),
        name="diff_prep_rope" if rope else "diff_prep",
    )(p, cos_t, sin_t, qg, kg)


def _diff_finish(acc1, acc2, lam_ref, sg_ref, o_ref):
    dv = DIFF_V_DIM
    o = acc1[:, :dv] / acc1[:, dv:] - lam_ref[...] * (acc2[:, :dv] / acc2[:, dv:])
    o = o * lax.rsqrt(jnp.mean(o * o, axis=-1, keepdims=True) + SUBLN_EPS)
    o_ref[0] = (o * sg_ref[...]).astype(o_ref.dtype)


def _flash_online_kernel(lam_ref, sg_ref, q_ref, kc_ref, vc_ref, k_ref, v_ref, o_ref, *, tk):
    qd = DIFF_QK_DIM
    tq = q_ref.shape[1]
    q = q_ref[0]
    qs = (q[:, :qd], q[:, qd:])

    def absorb(state, k, v):
        reps = k.shape[0] // LANES
        out = []
        for mp in range(2):
            m_prev, acc = state[mp]
            s = lax.dot_general(qs[mp], k[:, mp * qd:(mp + 1) * qd], _NT, preferred_element_type=F32)
            m_new = jnp.maximum(m_prev, jnp.max(s, axis=-1, keepdims=True))
            alpha = jnp.exp2(m_prev - m_new)
            pr = jnp.exp2(s - jnp.concatenate([m_new] * reps, axis=1))
            acc_new = jnp.concatenate([alpha, alpha], axis=1) * acc + jnp.dot(pr.astype(BF16), v, preferred_element_type=F32)
            out.append((m_new, acc_new))
        return tuple(out)

    init = (jnp.full((tq, LANES), -jnp.inf, F32), jnp.zeros((tq, V_EXT), F32))
    state = absorb((init, init), kc_ref[0], vc_ref[0])

    def body(j, state):
        rows = pl.ds(pl.multiple_of(j * tk, tk), tk)
        return absorb(state, k_ref[0, rows, :], v_ref[0, rows, :])

    (_, acc1), (_, acc2) = lax.fori_loop(0, k_ref.shape[1] // tk, body, state)
    _diff_finish(acc1, acc2, lam_ref, sg_ref, o_ref)


def _flash_bounded_kernel(lam_ref, sg_ref, q_ref, kc_ref, vc_ref, k_ref, v_ref, o_ref, acc_ref, p_ref, *, tk):
    qd = DIFF_QK_DIM
    q = q_ref[0]
    qs = (q[:, :qd], q[:, qd:])
    n_kv = k_ref.shape[1] // tk

    def weights(mp, k):
        s = lax.dot_general(qs[mp], k[:, mp * qd:(mp + 1) * qd], _NT, preferred_element_type=F32)
        return jnp.exp2(s).astype(BF16)

    def chunk(c):
        return pl.ds(pl.multiple_of(jnp.minimum(c, n_kv - 1) * tk, tk), tk)

    for mp in range(2):
        acc_ref[mp] = jnp.dot(weights(mp, kc_ref[0]), vc_ref[0], preferred_element_type=F32)
        p_ref[0, mp] = weights(mp, k_ref[0, pl.ds(0, tk), :])

    steps = _pick(n_kv, (8, 4, 2))

    def body(j, carry):
        for u in range(steps):
            c = steps * j + u
            v = v_ref[0, chunk(c), :]
            k_next = k_ref[0, chunk(c + 1), :]
            for mp in range(2):
                acc_ref[mp] += jnp.dot(p_ref[u % 2, mp], v, preferred_element_type=F32)
                p_ref[1 - u % 2, mp] = weights(mp, k_next)
        return carry

    lax.fori_loop(0, n_kv // steps, body, 0)
    _diff_finish(acc_ref[0], acc_ref[1], lam_ref, sg_ref, o_ref)


def _diff_attention(q, kc, vc, k, v, lam_vec, sg_vec, bounded):
    b, l, _ = q.shape
    lc = kc.shape[1]
    tq = _pick(l, (256, 128))
    tk = _pick(l, (512, 256, 128))
    if bounded:
        tq = _pick(l, (1024, 512, 256, 128))
        tk = _pick(l // 2, (512, 256, 128))
        assert l % (2 * tk) == 0
        body = functools.partial(_flash_bounded_kernel, tk=tk)
        scratch = [pltpu.VMEM((2, tq, V_EXT), F32), pltpu.VMEM((2, 2, tq, tk), BF16)]
    else:
        body = functools.partial(_flash_online_kernel, tk=tk)
        scratch = []
    return pl.pallas_call(
        body,
        grid=(b, DIFF_HEADS, l // tq),
        in_specs=[pl.BlockSpec((1, LANES), lambda bi, h, i: (0, 0)),
                  pl.BlockSpec((1, LANES), lambda bi, h, i: (0, 0)),
                  pl.BlockSpec((1, tq, LANES), lambda bi, h, i: (bi, i, h)),
                  pl.BlockSpec((1, lc, LANES), lambda bi, h, i: (bi, 0, h)),
                  pl.BlockSpec((1, lc, V_EXT), lambda bi, h, i: (bi, 0, h)),
                  pl.BlockSpec((1, l, LANES), lambda bi, h, i: (bi, 0, h)),
                  pl.BlockSpec((1, l, V_EXT), lambda bi, h, i: (bi, 0, h))],
        out_specs=pl.BlockSpec((1, tq, LANES), lambda bi, h, i: (bi, i, h)),
        out_shape=jax.ShapeDtypeStruct((b, l, DIFF_DIM), BF16),
        scratch_shapes=scratch,
        compiler_params=_cparams("parallel", "parallel", "arbitrary"),
        name="diff_flash_bounded" if bounded else "diff_flash_online",
    )(lam_vec, sg_vec, q, kc, vc, k, v)


def _merge_kernel(yf_ref, yb_ref, bonus_ref, gate_ref, yd_ref, pg_ref, lg_ref, lb_ref, wpa_ref, wpb_ref, o_ref):
    hd = RWKV_HEAD_DIM
    y = yf_ref[0].astype(F32) + yb_ref[0].astype(F32)
    dev = y - _group_sum(y, hd) * (1.0 / hd)
    var = _group_sum(dev * dev, hd, pieces=1) * (1.0 / hd)
    yn = dev * lax.rsqrt(var + LNX_EPS) * lg_ref[...] + lb_ref[...]
    y_rwkv = (yn + bonus_ref[0].astype(F32)) * gate_ref[0].astype(F32)
    a = _dot(y_rwkv, wpa_ref[...])
    bb = jnp.dot(yd_ref[0], wpb_ref[...], preferred_element_type=F32)
    ga = _sigmoid(pg_ref[0, :, :D_MODEL].astype(F32))
    gb = _sigmoid(pg_ref[0, :, D_MODEL:].astype(F32))
    o_ref[0] = (ga * a + gb * bb).astype(BF16)


def _merge(y_sweep, bonus, gate, y_diff, p_gate, lnx_g, lnx_b, w_pa, w_pb):
    b, l, c = bonus.shape
    d = D_MODEL
    tm = _pick(l, (256, 128))
    const = lambda shape: pl.BlockSpec(shape, lambda bi, i: (0,) * len(shape))
    tok = lambda w: pl.BlockSpec((1, tm, w), lambda bi, i: (bi, i, 0))
    return pl.pallas_call(
        _merge_kernel,
        grid=(b, l // tm),
        in_specs=[tok(c), tok(c), tok(c), tok(c), tok(DIFF_DIM), tok(GATE_COLS),
                  const((1, c)), const((1, c)), const((c, d)), const((DIFF_DIM, d))],
        out_specs=tok(d),
        out_shape=jax.ShapeDtypeStruct((b, l, d), BF16),
        compiler_params=_cparams("parallel", "parallel"),
        name="merge",
    )(y_sweep[0], y_sweep[1], bonus, gate, y_diff, p_gate, lnx_g, lnx_b, w_pa, w_pb)


def _outproj_kernel(mx_ref, x_ref, gt_ref, g_ref, sc_ref, sh_ref, wo_ref, wr_ref, br_ref, xn_ref, h_ref, rt_ref):
    mix = jnp.dot(mx_ref[0], wo_ref[...], preferred_element_type=F32)
    xn = x_ref[0] + gt_ref[0] * mix
    xn_ref[0] = xn
    y = xn * lax.rsqrt(jnp.mean(xn * xn, axis=-1, keepdims=True) + NORM_EPS) * g_ref[...]
    h = y * (1.0 + sc_ref[0]) + sh_ref[0]
    h_hi, h_lo = _bf16_parts(h, 2)
    h_ref[0] = h_hi
    dot = lambda a, b: jnp.dot(a, b, preferred_element_type=F32)
    hh = dot(h_hi, wr_ref[...])
    lg = hh[:, :ROUTER_PAD] + (dot(h_lo, wr_ref[:, :ROUTER_PAD]) + hh[:, ROUTER_PAD:]) + br_ref[...]
    rt_ref[0] = _route_rows(lg)


def _route_rows(lg):
    lane = lax.broadcasted_iota(jnp.int32, lg.shape, 1).astype(F32)
    neg = jnp.float32(-3.0e38)
    far = jnp.float32(LANES)
    gl = jnp.where(lane < N_GROUPS, lg, neg)
    g_max = jnp.max(gl, axis=-1, keepdims=True)
    g_top = jnp.min(jnp.where(gl == g_max, lane, far), axis=-1, keepdims=True)
    p_g = 1.0 / jnp.sum(jnp.where(lane < N_GROUPS, jnp.exp(gl - g_max), 0.0), axis=-1, keepdims=True)
    e_lo = N_GROUPS + EXPERTS_PER_GROUP * g_top
    el = jnp.where((lane >= e_lo) & (lane < e_lo + EXPERTS_PER_GROUP), lg, neg)
    m1 = jnp.max(el, axis=-1, keepdims=True)
    i1 = jnp.min(jnp.where(el == m1, lane, far), axis=-1, keepdims=True)
    el2 = jnp.where(lane == i1, neg, el)
    m2 = jnp.max(el2, axis=-1, keepdims=True)
    i2 = jnp.min(jnp.where(el2 == m2, lane, far), axis=-1, keepdims=True)
    e2 = jnp.exp(m2 - m1)
    gate1 = p_g / (1.0 + e2)
    gate2 = p_g * e2 / (1.0 + e2)
    return jnp.where(lane == 0, i1 - N_GROUPS,
                     jnp.where(lane == 1, i2 - N_GROUPS, jnp.where(lane == 2, gate1, jnp.where(lane == 3, gate2, 0.0))))


def _outproj(mixed, x, gt1, g2, sc2, sh2, w_out, w_router, b_router):
    b, l, d = x.shape
    tm = _pick(l, (512, 256, 128))
    const = lambda shape: pl.BlockSpec(shape, lambda bi, i: (0,) * len(shape))
    tok = lambda w: pl.BlockSpec((1, tm, w), lambda bi, i: (bi, i, 0))
    per_b = lambda: pl.BlockSpec((1, 1, d), lambda bi, i: (bi, 0, 0))
    return pl.pallas_call(
        _outproj_kernel,
        grid=(b, l // tm),
        in_specs=[tok(d), tok(d), per_b(), const((1, d)), per_b(), per_b(),
                  const((d, d)), const((d, 2 * ROUTER_PAD)), const((1, ROUTER_PAD))],
        out_specs=[tok(d), tok(d), tok(ROUTER_PAD)],
        out_shape=[jax.ShapeDtypeStruct((b, l, d), F32), jax.ShapeDtypeStruct((b, l, d), BF16),
                   jax.ShapeDtypeStruct((b, l, ROUTER_PAD), F32)],
        compiler_params=_cparams("parallel", "parallel"),
        name="outproj_router",
    )(mixed, x, gt1, g2, sc2, sh2, w_out, w_router, b_router)


def _moe_kernel(wb_ref, we_ref, lo_ref, hi_ref, x_ref, sw_ref, w1_ref, w3_ref, w2_ref, o_ref,
                w1b_ref, w3b_ref, w2b_ref, cached_ref):
    i = pl.program_id(0)
    lo, hi = lo_ref[i], hi_ref[i]
    live = hi > lo

    @pl.when(i == 0)
    def _():
        cached_ref[0] = -1

    @pl.when((i == 0) | (wb_ref[i] != wb_ref[jnp.maximum(i - 1, 0)]))
    def _():
        o_ref[...] = jnp.zeros(o_ref.shape, o_ref.dtype)

    @pl.when(live & (cached_ref[0] != we_ref[i]))
    def _():
        w1b_ref[...] = w1_ref[0].astype(BF16)
        w3b_ref[...] = w3_ref[0].astype(BF16)
        w2b_ref[...] = w2_ref[0].astype(BF16)
        cached_ref[0] = we_ref[i]

    @pl.when(live)
    def _():
        xb = x_ref[...]
        u = jnp.dot(xb, w1b_ref[...], preferred_element_type=F32)
        g = jnp.dot(xb, w3b_ref[...], preferred_element_type=F32)
        hmid = (u * _sigmoid(u) * g).astype(BF16)
        res = (jnp.dot(hmid, w2b_ref[...], preferred_element_type=F32) * sw_ref[...]).astype(o_ref.dtype)
        row = lax.broadcasted_iota(jnp.int32, (o_ref.shape[0], 1), 0)
        o_ref[...] = jnp.where((row >= lo) & (row < hi), res, o_ref[...])


def _moe_ffn(xs, sw, items, w1, w3, w2):
    n_rows, d = xs.shape
    wb, we, lo, hi = items
    grid_spec = pltpu.PrefetchScalarGridSpec(
        num_scalar_prefetch=4,
        grid=(wb.shape[0],),
        in_specs=[pl.BlockSpec((MOE_TILE, d), lambda i, wb, we, lo, hi: (wb[i], 0)),
                  pl.BlockSpec((MOE_TILE, 1), lambda i, wb, we, lo, hi: (wb[i], 0)),
                  pl.BlockSpec((1, d, D_EXPERT), lambda i, wb, we, lo, hi: (we[i], 0, 0)),
                  pl.BlockSpec((1, d, D_EXPERT), lambda i, wb, we, lo, hi: (we[i], 0, 0)),
                  pl.BlockSpec((1, D_EXPERT, d), lambda i, wb, we, lo, hi: (we[i], 0, 0))],
        out_specs=pl.BlockSpec((MOE_TILE, d), lambda i, wb, we, lo, hi: (wb[i], 0)),
        scratch_shapes=[pltpu.VMEM((d, D_EXPERT), BF16), pltpu.VMEM((d, D_EXPERT), BF16),
                        pltpu.VMEM((D_EXPERT, d), BF16), pltpu.SMEM((1,), jnp.int32)],
    )
    return pl.pallas_call(
        _moe_kernel,
        grid_spec=grid_spec,
        out_shape=jax.ShapeDtypeStruct((n_rows, d), BF16),
        compiler_params=_cparams("arbitrary"),
        name="moe_ffn",
    )(wb, we, lo, hi, xs, sw, w1, w3, w2)


def _final_kernel(x_ref, gt_ref, y0_ref, y1_ref, o_ref):
    o_ref[0] = x_ref[0] + gt_ref[0] * (y0_ref[0].astype(F32) + y1_ref[0].astype(F32))


def _final(x_new, gt2, y0, y1):
    b, l, d = x_new.shape
    tm = _pick(l, (512, 256, 128))
    tok = lambda: pl.BlockSpec((1, tm, d), lambda bi, i: (bi, i, 0))
    return pl.pallas_call(
        _final_kernel,
        grid=(b, l // tm),
        in_specs=[tok(), pl.BlockSpec((1, 1, d), lambda bi, i: (bi, 0, 0)), tok(), tok()],
        out_specs=tok(),
        out_shape=jax.ShapeDtypeStruct((b, l, d), F32),
        compiler_params=_cparams("parallel", "parallel"),
        name="moe_residual",
    )(x_new, gt2, y0, y1)


def _pad_lora_cols(w, widths):
    parts, o = [], 0
    for wd in widths:
        blk = w[..., o:o + wd]
        parts.append(jnp.pad(blk, [(0, 0)] * (w.ndim - 1) + [(0, LORA_PAD - wd)]))
        o += wd
    return jnp.concatenate(parts, axis=-1)


def _rope_tables(l):
    half = DIFF_QK_DIM // 2
    inv_freq = ROPE_THETA ** (-jnp.arange(0, half, 2, dtype=F32) / half)
    t = jnp.arange(l, dtype=jnp.int32)
    rows = (t // GRID_W).astype(F32)[:, None] * inv_freq
    cols = (t % GRID_W).astype(F32)[:, None] * inv_freq
    cos64 = jnp.concatenate([jnp.cos(rows), jnp.cos(rows), jnp.cos(cols), jnp.cos(cols)], axis=1)
    sin64 = jnp.concatenate([-jnp.sin(rows), jnp.sin(rows), -jnp.sin(cols), jnp.sin(cols)], axis=1)
    return jnp.tile(cos64, (1, 2)), jnp.tile(sin64, (1, 2))


def _route(routed, n_tok):
    expert = routed[:, :TOP_K].astype(jnp.int32)
    gate = routed[:, TOP_K:2 * TOP_K]

    n_assign = n_tok * TOP_K
    assert n_assign % MOE_TILE == 0
    flat_e = expert.reshape(-1).astype(jnp.int32)
    ids = jnp.arange(n_assign, dtype=jnp.int32)
    sorted_e, order, sorted_w = lax.sort((flat_e, ids, gate.reshape(-1)), num_keys=1, is_stable=True)
    _, rank = lax.sort((order, ids), num_keys=1)
    ends = jnp.searchsorted(sorted_e, jnp.arange(N_EXPERTS, dtype=jnp.int32), side='right').astype(jnp.int32)
    starts = jnp.concatenate([jnp.zeros((1,), jnp.int32), ends[:-1]])

    nb = n_assign // MOE_TILE
    blk_lo = jnp.arange(nb, dtype=jnp.int32) * MOE_TILE
    e_first = jnp.searchsorted(ends, blk_lo, side='right').astype(jnp.int32)
    e_last = jnp.searchsorted(ends, blk_lo + MOE_TILE - 1, side='right').astype(jnp.int32)
    per_blk = e_last - e_first + 1
    cum = jnp.cumsum(per_blk)
    it = jnp.arange(nb + N_EXPERTS - 1, dtype=jnp.int32)
    wb = jnp.minimum(jnp.searchsorted(cum, it, side='right'), nb - 1).astype(jnp.int32)
    we = jnp.clip(e_first[wb] + it - (cum[wb] - per_blk[wb]), 0, N_EXPERTS - 1)
    lo = jnp.clip(starts[we], blk_lo[wb], blk_lo[wb] + MOE_TILE) - blk_lo[wb]
    hi = jnp.clip(ends[we], blk_lo[wb], blk_lo[wb] + MOE_TILE) - blk_lo[wb]
    hi = jnp.where(it < cum[-1], hi, lo)
    return order // TOP_K, sorted_w, rank.reshape(n_tok, TOP_K), (wb, we, lo, hi)


def kernel(x, c, ctx, c_ctx, ada_w, ada_b, norm1_g, norm2_g, w_in, shift_mu, rwkv_w0, rwkv_w2, rwkv_a0, rwkv_a2,
           rwkv_g2, rwkv_k_k, rwkv_k_a, rwkv_r_k, rwkv_lnx_g, rwkv_lnx_b, qn_g, kn_g, diff_lambda, subln_g,
           w_pa, w_pb, w_out, router_g_w, router_g_b, router_e_w, router_e_b, exp_w1, exp_w3, exp_w2):
    assert ada_w.shape[0] == 1, "single-layer block"
    b, l, d = x.shape
    lc = ctx.shape[1]
    lam_init = 0.8 - 0.6 * math.exp(-0.3 * 0)
    lv = diff_lambda[0].astype(F32)
    lam = jnp.exp(jnp.sum(lv[0] * lv[1])) - jnp.exp(jnp.sum(lv[2] * lv[3])) + lam_init

    rows = (b + 1 + SUBLANES - 1) // SUBLANES * SUBLANES
    cm = jnp.zeros((rows, d), F32).at[:b].set(c).at[b].set(c_ctx)
    mod = _modulation(cm, ada_w[0], ada_b[0])
    sh1, sc1, gt1, sh2, sc2, gt2 = [mod[:b, None, k * d:(k + 1) * d] for k in range(6)]
    csh1, csc1 = [jnp.broadcast_to(mod[b, k * d:(k + 1) * d], (b, 1, d)) for k in range(2)]

    w = w_in[0]
    lora_widths = (DECAY_LORA, DECAY_LORA, AAA_LORA, AAA_LORA)
    o_lora = 3 * RWKV_DIM
    o_glora = o_lora + sum(lora_widths)
    pad_cols = lambda m: jnp.concatenate(
        [m[..., :o_lora], _pad_lora_cols(m[..., o_lora:o_glora], lora_widths), m[..., o_glora:RWKV_COLS]], axis=-1)
    w_rwkv = pad_cols(w).astype(BF16)
    w_diff = w[:, RWKV_COLS:RWKV_COLS + DIFF_COLS].astype(BF16)
    w_gate = w[:, RWKV_COLS + DIFF_COLS:].astype(BF16)
    g1 = norm1_g[0]
    hx = _norm_mod(x, g1, sc1, sh1)
    hc = _norm_mod(ctx, g1, csc1, csh1)
    px_r, px_d, px_g = _proj(hx, w_rwkv, F32), _proj(hx, w_diff, BF16), _proj(hx, w_gate, BF16)
    pc_r, pc_d = _proj(hc, w_rwkv, F32), _proj(hc, w_diff, BF16)

    pad_rows = lambda m: jnp.pad(m, ((0, 0), (0, LORA_PAD - m.shape[1]), (0, 0)))
    rparams = (pad_cols(shift_mu[0])[None], rwkv_w0[0], pad_rows(rwkv_w2[0]), rwkv_a0[0], pad_rows(rwkv_a2[0]),
               rwkv_g2[0], rwkv_k_k[0][None], rwkv_k_a[0][None], rwkv_r_k[0].reshape(1, RWKV_DIM))
    y_sweep, bonus, gate = _rwkv_scans(px_r, pc_r, rparams)

    cos_t, sin_t = _rope_tables(l)
    qg = jnp.tile(qn_g[0], 2)[None]
    kg = jnp.tile(kn_g[0], 2)[None]
    q_x, k_x, v_x = _diff_prep(px_d, cos_t, sin_t, qg, kg, True)
    _, k_c, v_c = _diff_prep(pc_d, cos_t[:lc], sin_t[:lc], qg, kg, False)
    lam_vec = jnp.full((1, LANES), lam, F32)
    sg_vec = (subln_g[0] * (1.0 - lam_init))[None]
    score_bound = (1.05 * DIFF_QK_DIM * DIFF_SCALE * math.log2(math.e)
                   * jnp.max(jnp.abs(qn_g[0])) * jnp.max(jnp.abs(kn_g[0])))
    attn_args = (q_x, k_c, v_c, k_x, v_x, lam_vec, sg_vec)
    y_diff = lax.cond(score_bound <= SCORE_LOG2_LIMIT,
                      lambda a: _diff_attention(*a, bounded=True),
                      lambda a: _diff_attention(*a, bounded=False), attn_args)

    mixed = _merge(y_sweep, bonus, gate, y_diff, px_g, rwkv_lnx_g[0][None], rwkv_lnx_b[0][None],
                   w_pa[0].astype(BF16), w_pb[0].astype(BF16))
    n_r = N_GROUPS + N_EXPERTS
    w_router = jnp.zeros((d, ROUTER_PAD), F32).at[:, :N_GROUPS].set(router_g_w[0]).at[:, N_GROUPS:n_r].set(router_e_w[0])
    b_router = jnp.zeros((1, ROUTER_PAD), F32).at[0, :N_GROUPS].set(router_g_b[0]).at[0, N_GROUPS:n_r].set(router_e_b[0])
    w_router_hi = w_router.astype(BF16)
    w_router_lo = (w_router - w_router_hi.astype(F32)).astype(BF16)
    x_new, h2, routed = _outproj(mixed, x, gt1, norm2_g[0][None], sc2, sh2, w_out[0].astype(BF16),
                                 jnp.concatenate([w_router_hi, w_router_lo], axis=1), b_router)

    n_tok = b * l
    row_tok, row_w, row_of, items = _route(routed.reshape(n_tok, ROUTER_PAD), n_tok)
    xs = h2.reshape(n_tok, d)[row_tok]
    out = _moe_ffn(xs, row_w[:, None], items, exp_w1[0], exp_w3[0], exp_w2[0])
    y0 = out[row_of[:, 0]].reshape(b, l, d)
    y1 = out[row_of[:, 1]].reshape(b, l, d)
    return _final(x_new, gt2, y0, y1)
```

```python
import functools
import math

import jax
import jax.numpy as jnp
from jax import lax
from jax.experimental import pallas as pl
from jax.experimental.pallas import tpu as pltpu

F32 = jnp.float32
BF16 = jnp.bfloat16
HIGHEST = lax.Precision.HIGHEST

D_MODEL = 2048
GRID_W = 64
RWKV_HEADS = 16
RWKV_HEAD_DIM = 64
RWKV_DIM = RWKV_HEADS * RWKV_HEAD_DIM
DECAY_LORA = 96
AAA_LORA = 96
GATE_LORA = 256
RWKV_COLS = 3 * RWKV_DIM + 2 * DECAY_LORA + 2 * AAA_LORA + GATE_LORA
DIFF_HEADS = 8
DIFF_QK_DIM = 64
DIFF_V_DIM = 2 * DIFF_QK_DIM
DIFF_DIM = DIFF_HEADS * DIFF_V_DIM
DIFF_QK_COLS = DIFF_HEADS * 2 * DIFF_QK_DIM
DIFF_COLS = 2 * DIFF_QK_COLS + DIFF_DIM
DIFF_SCALE = DIFF_QK_DIM ** -0.5
ROPE_THETA = 10000.0
ROPE_SUB = DIFF_QK_DIM // 4
GATE_COLS = 2 * D_MODEL
N_GROUPS = 4
EXPERTS_PER_GROUP = 8
N_EXPERTS = N_GROUPS * EXPERTS_PER_GROUP
TOP_K = 2
D_EXPERT = 512
NORM_EPS = 1e-6
SUBLN_EPS = 1e-5
LNX_EPS = 64e-5

LANES = 128
SUBLANES = 8
VMEM_LIMIT_BYTES = 56 * 1024 * 1024

LORA_PAD = LANES
RWKV_PCOLS = 3 * RWKV_DIM + 4 * LORA_PAD + GATE_LORA
CHUNK = 64
V_EXT = 2 * DIFF_V_DIM
SCORE_LOG2_LIMIT = 60.0
ROUTER_PAD = LANES
MOE_TILE = 512


def _cparams(*sem):
    return pltpu.CompilerParams(dimension_semantics=sem, vmem_limit_bytes=VMEM_LIMIT_BYTES)


def _sigmoid(x):
    return 1.0 / (1.0 + jnp.exp(-x))


def _dot(a, b, dims=(((1,), (0,)), ((), ()))):
    return lax.dot_general(a.astype(BF16), b.astype(BF16), dims, preferred_element_type=F32)


def _dot_f32(a, b, dims=(((1,), (0,)), ((), ()))):
    return lax.dot_general(a, b, dims, precision=HIGHEST, preferred_element_type=F32)


_NT = (((1,), (1,)), ((), ()))
_TN = (((0,), (0,)), ((), ()))


def _bf16_parts(x, n):
    parts = []
    for _ in range(n):
        p = x.astype(BF16)
        parts.append(p)
        x = x - p.astype(F32)
    return parts


def _dot_split(x, w_b, n, lhs=True):
    parts = _bf16_parts(x, n)
    if n == 1:
        return jnp.dot(parts[0], w_b, preferred_element_type=F32) if lhs else jnp.dot(w_b, parts[0], preferred_element_type=F32)
    if lhs:
        return jnp.dot(jnp.concatenate(parts, axis=1), jnp.concatenate([w_b] * n, axis=0), preferred_element_type=F32)
    return jnp.dot(jnp.concatenate([w_b] * n, axis=1), jnp.concatenate(parts, axis=0), preferred_element_type=F32)


def _group_ones(width, group):
    r = lax.broadcasted_iota(jnp.int32, (width, width), 0) // group
    c = lax.broadcasted_iota(jnp.int32, (width, width), 1) // group
    return (r == c).astype(BF16)


def _group_sum(x, group, pieces=2):
    ones = _group_ones(LANES, group)
    parts = [_dot_split(x[:, j * LANES:(j + 1) * LANES], ones, pieces) for j in range(x.shape[1] // LANES)]
    return parts[0] if len(parts) == 1 else jnp.concatenate(parts, axis=1)


def _mod_kernel(c_ref, w_ref, b_ref, o_ref):
    c = c_ref[...]
    o_ref[...] = _dot_f32(c * _sigmoid(c), w_ref[...]) + b_ref[...]


def _modulation(cm, ada_w, ada_b):
    rows, d = cm.shape
    n = ada_w.shape[1]
    tn = _pick(n, (1536, 1024, 512, 256, 128))
    return pl.pallas_call(
        _mod_kernel,
        grid=(n // tn,),
        in_specs=[pl.BlockSpec((rows, d), lambda j: (0, 0)),
                  pl.BlockSpec((d, tn), lambda j: (0, j)),
                  pl.BlockSpec((1, tn), lambda j: (0, j))],
        out_specs=pl.BlockSpec((rows, tn), lambda j: (0, j)),
        out_shape=jax.ShapeDtypeStruct((rows, n), F32),
        compiler_params=_cparams("parallel"),
        name="modulation",
    )(cm, ada_w, ada_b.reshape(1, n))


def _norm_mod_kernel(x_ref, g_ref, sc_ref, sh_ref, h_ref):
    x = x_ref[0]
    y = x * lax.rsqrt(jnp.mean(x * x, axis=-1, keepdims=True) + NORM_EPS) * g_ref[...]
    h_ref[0] = (y * (1.0 + sc_ref[0]) + sh_ref[0]).astype(BF16)


def _pick(n, prefs):
    for t in prefs:
        if n % t == 0:
            return t
    return n


def _norm_mod(x, g, sc, sh):
    b, l, d = x.shape
    tm = _pick(l, (512, 256, 128))
    return pl.pallas_call(
        _norm_mod_kernel,
        grid=(b, l // tm),
        in_specs=[pl.BlockSpec((1, tm, d), lambda bi, i: (bi, i, 0)),
                  pl.BlockSpec((1, d), lambda bi, i: (0, 0)),
                  pl.BlockSpec((1, 1, d), lambda bi, i: (bi, 0, 0)),
                  pl.BlockSpec((1, 1, d), lambda bi, i: (bi, 0, 0))],
        out_specs=pl.BlockSpec((1, tm, d), lambda bi, i: (bi, i, 0)),
        out_shape=jax.ShapeDtypeStruct((b, l, d), BF16),
        compiler_params=_cparams("parallel", "parallel"),
        name="norm_mod",
    )(x, g.reshape(1, d), sc, sh)


def _proj_kernel(h_ref, w_ref, o_ref):
    o_ref[0] = jnp.dot(h_ref[0], w_ref[...], preferred_element_type=F32).astype(o_ref.dtype)


def _proj(h, w, out_dtype):
    b, l, d = h.shape
    n = w.shape[1]
    tm = _pick(l, (1024, 512, 256, 128))
    tn = _pick(n, (1024, 768, 512, 256, 128))
    return pl.pallas_call(
        _proj_kernel,
        grid=(b, l // tm, n // tn),
        in_specs=[pl.BlockSpec((1, tm, d), lambda bi, i, j: (bi, i, 0)),
                  pl.BlockSpec((d, tn), lambda bi, i, j: (0, j))],
        out_specs=pl.BlockSpec((1, tm, tn), lambda bi, i, j: (bi, i, j)),
        out_shape=jax.ShapeDtypeStruct((b, l, n), out_dtype),
        compiler_params=_cparams("parallel", "parallel", "parallel"),
        name="proj",
    )(h, w)


def _pair_diag(x_b, mask_b):
    return jnp.concatenate([x_b, x_b], axis=0) * mask_b


def _tri_inverse_pairs(a_list, eye_f, mask_b):
    n = CHUNK
    mm = lambda l, r: jnp.dot(l.astype(BF16), _pair_diag(r.astype(BF16), mask_b), preferred_element_type=F32)
    xs = [eye_f + a for a in a_list]
    ps = [mm(a, a) for a in a_list]
    steps = int(math.log2(n)) - 1
    for s in range(steps):
        if s < steps - 1:
            xps = [mm(jnp.concatenate([x, p], axis=0), p) for x, p in zip(xs, ps)]
            xs = [x + xp[:n] for x, xp in zip(xs, xps)]
            ps = [xp[n:] for xp in xps]
        else:
            xs = [x + mm(x, p) for x, p in zip(xs, ps)]
    return xs


def _rwkv_chunk_kernel(p_ref, pp_ref, pn_ref, mu_ref, w0_ref, w2_ref, a0_ref, a2_ref, g2_ref, kk_ref, ka_ref,
                       rk_ref, q_ref, y0_ref, m_ref, n_ref, bonus_ref, gate_ref):
    i = pl.program_id(1)
    last = pl.num_programs(1) - 1
    c = RWKV_DIM
    hd = RWKV_HEAD_DIM
    p = p_ref[0].astype(F32)
    t_rows = p.shape[0]
    n_sub = t_rows // CHUNK
    row = lax.broadcasted_iota(jnp.int32, (t_rows, 1), 0)
    halo = pp_ref.shape[1]
    prev_row = jnp.where(i == 0, 0.0, pp_ref[0, halo - 1:halo, :].astype(F32))
    next_row = jnp.where(i == last, 0.0, pn_ref[0, 0:1, :].astype(F32))
    prev = jnp.where(row == 0, prev_row, pltpu.roll(p, 1, axis=0))
    nxt = jnp.where(row == t_rows - 1, next_row, pltpu.roll(p, t_rows - 1, axis=0))
    ps = p + mu_ref[...] * (0.5 * (prev + nxt) - p)

    r, k, v = ps[:, :c], ps[:, c:2 * c], ps[:, 2 * c:3 * c]
    o = 3 * c
    xw = (ps[:, o:o + LORA_PAD], ps[:, o + LORA_PAD:o + 2 * LORA_PAD])
    xa = (ps[:, o + 2 * LORA_PAD:o + 3 * LORA_PAD], ps[:, o + 3 * LORA_PAD:o + 4 * LORA_PAD])
    xg = ps[:, o + 4 * LORA_PAD:]

    gate_ref[0] = _dot(_sigmoid(xg), g2_ref[...]).astype(gate_ref.dtype)
    kk = k * kk_ref[...]
    kk = kk * lax.rsqrt(_group_sum(kk * kk, hd, pieces=1) + 1e-12)

    tr = lax.broadcasted_iota(jnp.int32, (t_rows, t_rows), 0)
    tc = lax.broadcasted_iota(jnp.int32, (t_rows, t_rows), 1)
    same_chunk = (tr // CHUNK) == (tc // CHUNK)
    tr2 = lax.broadcasted_iota(jnp.int32, (CHUNK, LANES), 0)
    lane2 = lax.broadcasted_iota(jnp.int32, (CHUNK, LANES), 1)
    tc2 = lane2 % CHUNK
    lane_head = lane2 // hd
    eye2 = tr2 == tc2
    mask_b = _group_ones(LANES, hd)
    kd_sum = jnp.zeros_like(k)
    v_b = v.astype(BF16)
    psl = [slice(p * LANES, (p + 1) * LANES) for p in range(c // LANES)]
    before2, upto2, prep = [], [], []
    for d in range(2):
        before2.append((tc2 < tr2) if d == 0 else (tc2 > tr2))
        upto = same_chunk & ((tc <= tr) if d == 0 else (tc >= tr))
        upto2.append((tc2 <= tr2) if d == 0 else (tc2 >= tr2))
        z = w0_ref[d:d + 1, :] + _dot(jnp.tanh(xw[d]), w2_ref[d])
        w_log = -(jnp.maximum(-z, 0.0) + jnp.log(1.0 + jnp.exp(-jnp.abs(z)))) - 0.5
        logw = -jnp.exp(w_log)
        a = _sigmoid(a0_ref[d:d + 1, :] + _dot(xa[d], a2_ref[d]))
        kd = k * (1.0 + (a - 1.0) * ka_ref[...])
        kd_sum = kd_sum + kd
        cum = _dot_split(logw, upto.astype(BF16), 3, lhs=False)
        last_row = [u * CHUNK + (CHUNK - 1 if d == 0 else 0) for u in range(n_sub)]
        totals = [cum[t:t + 1, :] for t in last_row]
        total = totals[0]
        for u in range(1, n_sub):
            total = jnp.where(row >= u * CHUNK, totals[u], total)
        e_neg = jnp.exp(-cum)
        e_rest = jnp.exp(total - cum)
        p_total = [jnp.exp(t) for t in totals]
        beta = kk * a
        al = -kk * jnp.exp(cum - logw)
        rt = r * jnp.exp(cum)
        al_b = al.astype(BF16)
        rt_b = rt.astype(BF16)
        bt_b = (beta * e_neg).astype(BF16)
        kt_b = (kd * e_neg).astype(BF16)
        bh_b = (beta * e_rest).astype(BF16)
        kh_b = (kd * e_rest).astype(BF16)
        prep.append((al_b, rt_b, bt_b, kt_b, bh_b, kh_b, rt, p_total))

    rsl = [slice(u * CHUNK, (u + 1) * CHUNK) for u in range(n_sub)]
    dp = [(d, rs, s) for rs in rsl for d in range(2) for s in psl]
    fdot = lambda a, b: jnp.dot(a, b, preferred_element_type=F32)
    bd = lambda x: _pair_diag(x, mask_b)
    v_bd = [bd(v_b[rs, s]) for d, rs, s in dp]
    zeros_sq = jnp.zeros((LANES, LANES), BF16)
    zeros_tl = jnp.zeros((CHUNK, LANES), BF16)
    gm = [lax.dot_general(jnp.concatenate([prep[d][0][rs, s], prep[d][1][rs, s]], axis=0),
                          jnp.concatenate([bd(prep[d][2][rs, s]), bd(prep[d][3][rs, s])], axis=0),
                          _NT, preferred_element_type=F32) for d, rs, s in dp]
    a_ab = [jnp.where(before2[d], x[:CHUNK, :LANES], 0.0) for x, (d, rs, s) in zip(gm, dp)]
    a_ak = [jnp.where(before2[d], x[:CHUNK, LANES:], 0.0).astype(BF16) for x, (d, rs, s) in zip(gm, dp)]
    a_rb = [jnp.where(upto2[d], x[CHUNK:, :LANES], 0.0).astype(BF16) for x, (d, rs, s) in zip(gm, dp)]
    a_rk = [jnp.where(upto2[d], x[CHUNK:, LANES:], 0.0).astype(BF16) for x, (d, rs, s) in zip(gm, dp)]
    akv = [fdot(ak, vd) for ak, vd in zip(a_ak, v_bd)]
    t_inv = _tri_inverse_pairs(a_ab, eye2.astype(F32), mask_b)
    wu = [fdot(t.astype(BF16), jnp.concatenate([bd(prep[d][0][rs, s]), bd(u.astype(BF16))], axis=1))
          for t, u, (d, rs, s) in zip(t_inv, akv, dp)]
    w_b = [x[:, :LANES].astype(BF16) for x in wu]
    u_b = [x[:, LANES:].astype(BF16) for x in wu]
    qy = [fdot(jnp.concatenate([rb, rk], axis=1),
               jnp.concatenate([jnp.concatenate([bd(w), bd(u)], axis=1),
                                jnp.concatenate([zeros_sq, vd], axis=1)], axis=0))
          for rb, rk, w, u, vd in zip(a_rb, a_rk, w_b, u_b, v_bd)]
    full = [lax.dot_general(jnp.concatenate([jnp.concatenate([w, u], axis=1),
                                             jnp.concatenate([zeros_tl, v_b[rs, s]], axis=1)], axis=0),
                            jnp.concatenate([prep[d][4][rs, s], prep[d][5][rs, s]], axis=0),
                            _TN, preferred_element_type=F32)
            for w, u, (d, rs, s) in zip(w_b, u_b, dp)]

    def diag_blocks(x):
        return jnp.where(lane_head == 0, x[:hd], 0.0) + jnp.where(lane_head == 1, x[hd:2 * hd], 0.0)

    npair = len(psl)
    for u, rs in enumerate(rsl):
        for d in range(2):
            sl = slice((2 * u + d) * npair, (2 * u + d + 1) * npair)
            rt, p_total = prep[d][6][rs], prep[d][7][u]
            q_ref[0, d, rs, :] = (rt + jnp.concatenate([x[:, :LANES] for x in qy[sl]], axis=1)).astype(q_ref.dtype)
            y0_ref[0, d, rs, :] = jnp.concatenate([x[:, LANES:] for x in qy[sl]], axis=1).astype(y0_ref.dtype)
            m_ref[0, d, u] = jnp.concatenate([jnp.where(eye2, p_total[:, s], 0.0) + diag_blocks(x[:LANES])
                                              for x, s in zip(full[sl], psl)], axis=1).astype(m_ref.dtype)
            n_ref[0, d, u] = jnp.concatenate([diag_blocks(x[LANES:]) for x in full[sl]], axis=1)
    bonus_ref[0] = (_group_sum(r * kd_sum * rk_ref[...], hd) * v).astype(bonus_ref.dtype)


def _rwkv_chunk_ops(p, params):
    b, l, pc = p.shape
    nc = l // CHUNK
    c = RWKV_DIM
    g = _pick(nc, (2, 1))
    t = g * CHUNK
    halo = SUBLANES * (4 // p.dtype.itemsize)
    hb = t // halo
    nhb = l // halo
    mu, w0, w2, a0, a2, g2, k_k, k_a, r_k = params
    const = lambda shape: pl.BlockSpec(shape, lambda bi, i: (0,) * len(shape))
    tok = lambda: pl.BlockSpec((1, t, c), lambda bi, i: (bi, i, 0))
    return pl.pallas_call(
        _rwkv_chunk_kernel,
        grid=(b, nc // g),
        in_specs=[pl.BlockSpec((1, t, pc), lambda bi, i: (bi, i, 0)),
                  pl.BlockSpec((1, halo, pc), lambda bi, i: (bi, jnp.maximum(i * hb - 1, 0), 0)),
                  pl.BlockSpec((1, halo, pc), lambda bi, i: (bi, jnp.minimum((i + 1) * hb, nhb - 1), 0)),
                  const((1, pc)), const((2, c)), const((2, LORA_PAD, c)), const((2, c)), const((2, LORA_PAD, c)),
                  const((GATE_LORA, c)), const((1, c)), const((1, c)), const((1, c))],
        out_specs=[pl.BlockSpec((1, 2, t, c), lambda bi, i: (bi, 0, i, 0)),
                   pl.BlockSpec((1, 2, t, c), lambda bi, i: (bi, 0, i, 0)),
                   pl.BlockSpec((1, 2, g, RWKV_HEAD_DIM, c), lambda bi, i: (bi, 0, i, 0, 0)),
                   pl.BlockSpec((1, 2, g, RWKV_HEAD_DIM, c), lambda bi, i: (bi, 0, i, 0, 0)),
                   tok(), tok()],
        out_shape=[jax.ShapeDtypeStruct((b, 2, l, c), BF16), jax.ShapeDtypeStruct((b, 2, l, c), BF16),
                   jax.ShapeDtypeStruct((b, 2, nc, RWKV_HEAD_DIM, c), BF16),
                   jax.ShapeDtypeStruct((b, 2, nc, RWKV_HEAD_DIM, c), F32),
                   jax.ShapeDtypeStruct((b, l, c), BF16), jax.ShapeDtypeStruct((b, l, c), BF16)],
        compiler_params=_cparams("parallel", "parallel"),
        name="rwkv_chunk_ops",
    )(p, p, p, mu, w0, w2, a0, a2, g2, k_k, k_a, r_k)


def _rwkv_sweep_kernel(qf_ref, y0f_ref, mf_ref, nf_ref, qb_ref, y0b_ref, mb_ref, nb_ref, s0_ref,
                       yf_ref, yb_ref, sfin_ref, s_ref):
    j = pl.program_id(1)
    hd = RWKV_HEAD_DIM

    @pl.when(j == 0)
    def _():
        s_ref[...] = s0_ref[0]

    dirs = ((qf_ref, y0f_ref, mf_ref, nf_ref, yf_ref), (qb_ref, y0b_ref, mb_ref, nb_ref, yb_ref))
    psl = [slice(p * LANES, (p + 1) * LANES) for p in range(RWKV_DIM // LANES)]
    mask_b = _group_ones(LANES, hd)
    pair_diag = lambda x: _pair_diag(x, mask_b)
    per_step = mf_ref.shape[2]
    state = [s_ref[0], s_ref[1]]
    for u in range(per_step):
        for d, (q_ref, y0_ref, m_ref, n_ref, y_ref) in enumerate(dirs):
            cu = u if d == 0 else per_step - 1 - u
            rows = slice(cu * CHUNK, (cu + 1) * CHUNK)
            s_b = state[d].astype(BF16)
            q_b = q_ref[0, 0, rows, :]
            m_b = m_ref[0, 0, cu]
            ys = [_dot(q_b[:, s], pair_diag(s_b[:, s]), _NT) for s in psl]
            sm = [_dot(s_b[:, s], pair_diag(m_b[:, s])) for s in psl]
            y_ref[0, rows, :] = (y0_ref[0, 0, rows, :].astype(F32) + jnp.concatenate(ys, axis=1)).astype(y_ref.dtype)
            state[d] = n_ref[0, 0, cu] + jnp.concatenate(sm, axis=1)
    s_ref[0] = state[0]
    s_ref[1] = state[1]

    @pl.when(j == pl.num_programs(1) - 1)
    def _():
        sfin_ref[0] = s_ref[...]


def _rwkv_sweep(q, y0, m, n, s0):
    b, _, l, c = q.shape
    nc = l // CHUNK
    hd = RWKV_HEAD_DIM
    g = _pick(nc, (4, 2, 1))
    ns = nc // g
    tokf = lambda: pl.BlockSpec((1, 1, g * CHUNK, c), lambda bi, j: (bi, 0, j, 0))
    tokb = lambda: pl.BlockSpec((1, 1, g * CHUNK, c), lambda bi, j: (bi, 1, ns - 1 - j, 0))
    opf = lambda: pl.BlockSpec((1, 1, g, hd, c), lambda bi, j: (bi, 0, j, 0, 0))
    opb = lambda: pl.BlockSpec((1, 1, g, hd, c), lambda bi, j: (bi, 1, ns - 1 - j, 0, 0))
    return pl.pallas_call(
        _rwkv_sweep_kernel,
        grid=(b, ns),
        in_specs=[tokf(), tokf(), opf(), opf(), tokb(), tokb(), opb(), opb(),
                  pl.BlockSpec((1, 2, hd, c), lambda bi, j: (bi, 0, 0, 0))],
        out_specs=[pl.BlockSpec((1, g * CHUNK, c), lambda bi, j: (bi, j, 0)),
                   pl.BlockSpec((1, g * CHUNK, c), lambda bi, j: (bi, ns - 1 - j, 0)),
                   pl.BlockSpec((1, 2, hd, c), lambda bi, j: (bi, 0, 0, 0))],
        out_shape=[jax.ShapeDtypeStruct((b, l, c), BF16), jax.ShapeDtypeStruct((b, l, c), BF16),
                   jax.ShapeDtypeStruct((b, 2, hd, c), F32)],
        scratch_shapes=[pltpu.VMEM((2, hd, c), F32)],
        compiler_params=_cparams("parallel", "arbitrary"),
        name="rwkv_sweep",
    )(q, y0, m, n, q, y0, m, n, s0)


def _rwkv_scans(px_r, pc_r, rparams):
    b = px_r.shape[0]
    qc, y0c, mc, nc_, _, _ = _rwkv_chunk_ops(pc_r, rparams)
    s_zero = jnp.zeros((b, 2, RWKV_HEAD_DIM, RWKV_DIM), F32)
    _, _, s_ctx = _rwkv_sweep(qc, y0c, mc, nc_, s_zero)
    qx, y0x, mx, nx, bonus, gate = _rwkv_chunk_ops(px_r, rparams)
    yf, yb, _ = _rwkv_sweep(qx, y0x, mx, nx, s_ctx)
    return (yf, yb), bonus, gate


def _diff_prep_kernel(p_ref, cos_ref, sin_ref, qg_ref, kg_ref, q_ref, k_ref, v_ref, *, rope):
    lane = lax.broadcasted_iota(jnp.int32, (1, LANES), 1)
    first = (lane % (2 * ROPE_SUB)) < ROPE_SUB
    for hd in range(DIFF_HEADS):
        for off, g_ref, o_ref, scale in ((0, qg_ref, q_ref, DIFF_SCALE * math.log2(math.e)),
                                         (DIFF_QK_COLS, kg_ref, k_ref, 1.0)):
            cs = slice(hd * LANES, (hd + 1) * LANES)
            xb = p_ref[0, :, off + hd * LANES:off + (hd + 1) * LANES].astype(F32)
            ms = _group_sum(xb * xb, DIFF_QK_DIM, pieces=1) * (1.0 / DIFF_QK_DIM)
            y = xb * lax.rsqrt(ms + NORM_EPS) * g_ref[...]
            if rope:
                swapped = jnp.where(first, pltpu.roll(y, LANES - ROPE_SUB, axis=1), pltpu.roll(y, ROPE_SUB, axis=1))
                y = y * cos_ref[...] + swapped * sin_ref[...]
            o_ref[0, :, cs] = (y * scale).astype(BF16)
    ones = jnp.ones((p_ref.shape[1], V_EXT - DIFF_V_DIM), BF16)
    for hd in range(DIFF_HEADS):
        vb = p_ref[0, :, 2 * DIFF_QK_COLS + hd * DIFF_V_DIM:2 * DIFF_QK_COLS + (hd + 1) * DIFF_V_DIM]
        v_ref[0, :, hd * V_EXT:(hd + 1) * V_EXT] = jnp.concatenate([vb.astype(BF16), ones], axis=1)


def _diff_prep(p, cos_t, sin_t, qg, kg, rope):
    b, l, pc = p.shape
    t = _pick(l, (512, 256, 128))
    tok = lambda w=DIFF_DIM: pl.BlockSpec((1, t, w), lambda bi, i: (bi, i, 0))
    shp = jax.ShapeDtypeStruct((b, l, DIFF_DIM), BF16)
    shp_v = jax.ShapeDtypeStruct((b, l, DIFF_HEADS * V_EXT), BF16)
    return pl.pallas_call(
        functools.partial(_diff_prep_kernel, rope=rope),
        grid=(b, l // t),
        in_specs=[pl.BlockSpec((1, t, pc), lambda bi, i: (bi, i, 0)),
                  pl.BlockSpec((t, LANES), lambda bi, i: (i, 0)),
                  pl.BlockSpec((t, LANES), lambda bi, i: (i, 0)),
                  pl.BlockSpec((1, LANES), lambda bi, i: (0, 0)),
                  pl.BlockSpec((1, LANES), lambda bi, i: (0, 0))],
        out_specs=[tok(), tok(), tok(DIFF_HEADS * V_EXT)],
        out_shape=[shp, shp, shp_v],
        compiler_params=_cparams("parallel", "parallel"),
        name="diff_prep_rope" if rope else "diff_prep",
    )(p, cos_t, sin_t, qg, kg)


def _diff_finish(acc1, acc2, lam_ref, sg_ref, o_ref):
    dv = DIFF_V_DIM
    o = acc1[:, :dv] / acc1[:, dv:] - lam_ref[...] * (acc2[:, :dv] / acc2[:, dv:])
    o = o * lax.rsqrt(jnp.mean(o * o, axis=-1, keepdims=True) + SUBLN_EPS)
    o_ref[0] = (o * sg_ref[...]).astype(o_ref.dtype)


def _flash_online_kernel(lam_ref, sg_ref, q_ref, kc_ref, vc_ref, k_ref, v_ref, o_ref, *, tk):
    qd = DIFF_QK_DIM
    tq = q_ref.shape[1]
    q = q_ref[0]
    qs = (q[:, :qd], q[:, qd:])

    def absorb(state, k, v):
        reps = k.shape[0] // LANES
        out = []
        for mp in range(2):
            m_prev, acc = state[mp]
            s = lax.dot_general(qs[mp], k[:, mp * qd:(mp + 1) * qd], _NT, preferred_element_type=F32)
            m_new = jnp.maximum(m_prev, jnp.max(s, axis=-1, keepdims=True))
            alpha = jnp.exp2(m_prev - m_new)
            pr = jnp.exp2(s - jnp.concatenate([m_new] * reps, axis=1))
            acc_new = jnp.concatenate([alpha, alpha], axis=1) * acc + jnp.dot(pr.astype(BF16), v, preferred_element_type=F32)
            out.append((m_new, acc_new))
        return tuple(out)

    init = (jnp.full((tq, LANES), -jnp.inf, F32), jnp.zeros((tq, V_EXT), F32))
    state = absorb((init, init), kc_ref[0], vc_ref[0])

    def body(j, state):
        rows = pl.ds(pl.multiple_of(j * tk, tk), tk)
        return absorb(state, k_ref[0, rows, :], v_ref[0, rows, :])

    (_, acc1), (_, acc2) = lax.fori_loop(0, k_ref.shape[1] // tk, body, state)
    _diff_finish(acc1, acc2, lam_ref, sg_ref, o_ref)


def _flash_bounded_kernel(lam_ref, sg_ref, q_ref, kc_ref, vc_ref, k_ref, v_ref, o_ref, acc_ref, p_ref, *, tk):
    qd = DIFF_QK_DIM
    q = q_ref[0]
    qs = (q[:, :qd], q[:, qd:])
    n_kv = k_ref.shape[1] // tk

    def weights(mp, k):
        s = lax.dot_general(qs[mp], k[:, mp * qd:(mp + 1) * qd], _NT, preferred_element_type=F32)
        return jnp.exp2(s).astype(BF16)

    def chunk(c):
        return pl.ds(pl.multiple_of(jnp.minimum(c, n_kv - 1) * tk, tk), tk)

    for mp in range(2):
        acc_ref[mp] = jnp.dot(weights(mp, kc_ref[0]), vc_ref[0], preferred_element_type=F32)
        p_ref[0, mp] = weights(mp, k_ref[0, pl.ds(0, tk), :])

    steps = _pick(n_kv, (8, 4, 2))

    def body(j, carry):
        for u in range(steps):
            c = steps * j + u
            v = v_ref[0, chunk(c), :]
            k_next = k_ref[0, chunk(c + 1), :]
            for mp in range(2):
                acc_ref[mp] += jnp.dot(p_ref[u % 2, mp], v, preferred_element_type=F32)
                p_ref[1 - u % 2, mp] = weights(mp, k_next)
        return carry

    lax.fori_loop(0, n_kv // steps, body, 0)
    _diff_finish(acc_ref[0], acc_ref[1], lam_ref, sg_ref, o_ref)


def _diff_attention(q, kc, vc, k, v, lam_vec, sg_vec, bounded):
    b, l, _ = q.shape
    lc = kc.shape[1]
    tq = _pick(l, (256, 128))
    tk = _pick(l, (512, 256, 128))
    if bounded:
        tq = _pick(l, (1024, 512, 256, 128))
        tk = _pick(l // 2, (512, 256, 128))
        assert l % (2 * tk) == 0
        body = functools.partial(_flash_bounded_kernel, tk=tk)
        scratch = [pltpu.VMEM((2, tq, V_EXT), F32), pltpu.VMEM((2, 2, tq, tk), BF16)]
    else:
        body = functools.partial(_flash_online_kernel, tk=tk)
        scratch = []
    return pl.pallas_call(
        body,
        grid=(b, DIFF_HEADS, l // tq),
        in_specs=[pl.BlockSpec((1, LANES), lambda bi, h, i: (0, 0)),
                  pl.BlockSpec((1, LANES), lambda bi, h, i: (0, 0)),
                  pl.BlockSpec((1, tq, LANES), lambda bi, h, i: (bi, i, h)),
                  pl.BlockSpec((1, lc, LANES), lambda bi, h, i: (bi, 0, h)),
                  pl.BlockSpec((1, lc, V_EXT), lambda bi, h, i: (bi, 0, h)),
                  pl.BlockSpec((1, l, LANES), lambda bi, h, i: (bi, 0, h)),
                  pl.BlockSpec((1, l, V_EXT), lambda bi, h, i: (bi, 0, h))],
        out_specs=pl.BlockSpec((1, tq, LANES), lambda bi, h, i: (bi, i, h)),
        out_shape=jax.ShapeDtypeStruct((b, l, DIFF_DIM), BF16),
        scratch_shapes=scratch,
        compiler_params=_cparams("parallel", "parallel", "arbitrary"),
        name="diff_flash_bounded" if bounded else "diff_flash_online",
    )(lam_vec, sg_vec, q, kc, vc, k, v)


def _merge_kernel(yf_ref, yb_ref, bonus_ref, gate_ref, yd_ref, pg_ref, lg_ref, lb_ref, wpa_ref, wpb_ref, o_ref):
    hd = RWKV_HEAD_DIM
    y = yf_ref[0].astype(F32) + yb_ref[0].astype(F32)
    dev = y - _group_sum(y, hd) * (1.0 / hd)
    var = _group_sum(dev * dev, hd, pieces=1) * (1.0 / hd)
    yn = dev * lax.rsqrt(var + LNX_EPS) * lg_ref[...] + lb_ref[...]
    y_rwkv = (yn + bonus_ref[0].astype(F32)) * gate_ref[0].astype(F32)
    a = _dot(y_rwkv, wpa_ref[...])
    bb = jnp.dot(yd_ref[0], wpb_ref[...], preferred_element_type=F32)
    ga = _sigmoid(pg_ref[0, :, :D_MODEL].astype(F32))
    gb = _sigmoid(pg_ref[0, :, D_MODEL:].astype(F32))
    o_ref[0] = (ga * a + gb * bb).astype(BF16)


def _merge(y_sweep, bonus, gate, y_diff, p_gate, lnx_g, lnx_b, w_pa, w_pb):
    b, l, c = bonus.shape
    d = D_MODEL
    tm = _pick(l, (256, 128))
    const = lambda shape: pl.BlockSpec(shape, lambda bi, i: (0,) * len(shape))
    tok = lambda w: pl.BlockSpec((1, tm, w), lambda bi, i: (bi, i, 0))
    return pl.pallas_call(
        _merge_kernel,
        grid=(b, l // tm),
        in_specs=[tok(c), tok(c), tok(c), tok(c), tok(DIFF_DIM), tok(GATE_COLS),
                  const((1, c)), const((1, c)), const((c, d)), const((DIFF_DIM, d))],
        out_specs=tok(d),
        out_shape=jax.ShapeDtypeStruct((b, l, d), BF16),
        compiler_params=_cparams("parallel", "parallel"),
        name="merge",
    )(y_sweep[0], y_sweep[1], bonus, gate, y_diff, p_gate, lnx_g, lnx_b, w_pa, w_pb)


def _outproj_kernel(mx_ref, x_ref, gt_ref, g_ref, sc_ref, sh_ref, wo_ref, wr_ref, br_ref, xn_ref, h_ref, rt_ref):
    mix = jnp.dot(mx_ref[0], wo_ref[...], preferred_element_type=F32)
    xn = x_ref[0] + gt_ref[0] * mix
    xn_ref[0] = xn
    y = xn * lax.rsqrt(jnp.mean(xn * xn, axis=-1, keepdims=True) + NORM_EPS) * g_ref[...]
    h = y * (1.0 + sc_ref[0]) + sh_ref[0]
    h_hi, h_lo = _bf16_parts(h, 2)
    h_ref[0] = h_hi
    dot = lambda a, b: jnp.dot(a, b, preferred_element_type=F32)
    hh = dot(h_hi, wr_ref[...])
    lg = hh[:, :ROUTER_PAD] + (dot(h_lo, wr_ref[:, :ROUTER_PAD]) + hh[:, ROUTER_PAD:]) + br_ref[...]
    rt_ref[0] = _route_rows(lg)


def _route_rows(lg):
    lane = lax.broadcasted_iota(jnp.int32, lg.shape, 1).astype(F32)
    neg = jnp.float32(-3.0e38)
    far = jnp.float32(LANES)
    gl = jnp.where(lane < N_GROUPS, lg, neg)
    g_max = jnp.max(gl, axis=-1, keepdims=True)
    g_top = jnp.min(jnp.where(gl == g_max, lane, far), axis=-1, keepdims=True)
    p_g = 1.0 / jnp.sum(jnp.where(lane < N_GROUPS, jnp.exp(gl - g_max), 0.0), axis=-1, keepdims=True)
    e_lo = N_GROUPS + EXPERTS_PER_GROUP * g_top
    el = jnp.where((lane >= e_lo) & (lane < e_lo + EXPERTS_PER_GROUP), lg, neg)
    m1 = jnp.max(el, axis=-1, keepdims=True)
    i1 = jnp.min(jnp.where(el == m1, lane, far), axis=-1, keepdims=True)
    el2 = jnp.where(lane == i1, neg, el)
    m2 = jnp.max(el2, axis=-1, keepdims=True)
    i2 = jnp.min(jnp.where(el2 == m2, lane, far), axis=-1, keepdims=True)
    e2 = jnp.exp(m2 - m1)
    gate1 = p_g / (1.0 + e2)
    gate2 = p_g * e2 / (1.0 + e2)
    return jnp.where(lane == 0, i1 - N_GROUPS,
                     jnp.where(lane == 1, i2 - N_GROUPS, jnp.where(lane == 2, gate1, jnp.where(lane == 3, gate2, 0.0))))


def _outproj(mixed, x, gt1, g2, sc2, sh2, w_out, w_router, b_router):
    b, l, d = x.shape
    tm = _pick(l, (512, 256, 128))
    const = lambda shape: pl.BlockSpec(shape, lambda bi, i: (0,) * len(shape))
    tok = lambda w: pl.BlockSpec((1, tm, w), lambda bi, i: (bi, i, 0))
    per_b = lambda: pl.BlockSpec((1, 1, d), lambda bi, i: (bi, 0, 0))
    return pl.pallas_call(
        _outproj_kernel,
        grid=(b, l // tm),
        in_specs=[tok(d), tok(d), per_b(), const((1, d)), per_b(), per_b(),
                  const((d, d)), const((d, 2 * ROUTER_PAD)), const((1, ROUTER_PAD))],
        out_specs=[tok(d), tok(d), tok(ROUTER_PAD)],
        out_shape=[jax.ShapeDtypeStruct((b, l, d), F32), jax.ShapeDtypeStruct((b, l, d), BF16),
                   jax.ShapeDtypeStruct((b, l, ROUTER_PAD), F32)],
        compiler_params=_cparams("parallel", "parallel"),
        name="outproj_router",
    )(mixed, x, gt1, g2, sc2, sh2, w_out, w_router, b_router)


def _moe_kernel(wb_ref, we_ref, lo_ref, hi_ref, x_ref, sw_ref, w1_ref, w3_ref, w2_ref, o_ref,
                w1b_ref, w3b_ref, w2b_ref, cached_ref):
    i = pl.program_id(0)
    lo, hi = lo_ref[i], hi_ref[i]
    live = hi > lo

    @pl.when(i == 0)
    def _():
        cached_ref[0] = -1

    @pl.when((i == 0) | (wb_ref[i] != wb_ref[jnp.maximum(i - 1, 0)]))
    def _():
        o_ref[...] = jnp.zeros(o_ref.shape, o_ref.dtype)

    @pl.when(live & (cached_ref[0] != we_ref[i]))
    def _():
        w1b_ref[...] = w1_ref[0].astype(BF16)
        w3b_ref[...] = w3_ref[0].astype(BF16)
        w2b_ref[...] = w2_ref[0].astype(BF16)
        cached_ref[0] = we_ref[i]

    @pl.when(live)
    def _():
        xb = x_ref[...]
        u = jnp.dot(xb, w1b_ref[...], preferred_element_type=F32)
        g = jnp.dot(xb, w3b_ref[...], preferred_element_type=F32)
        hmid = (u * _sigmoid(u) * g).astype(BF16)
        res = (jnp.dot(hmid, w2b_ref[...], preferred_element_type=F32) * sw_ref[...]).astype(o_ref.dtype)
        row = lax.broadcasted_iota(jnp.int32, (o_ref.shape[0], 1), 0)
        o_ref[...] = jnp.where((row >= lo) & (row < hi), res, o_ref[...])


def _moe_ffn(xs, sw, items, w1, w3, w2):
    n_rows, d = xs.shape
    wb, we, lo, hi = items
    grid_spec = pltpu.PrefetchScalarGridSpec(
        num_scalar_prefetch=4,
        grid=(wb.shape[0],),
        in_specs=[pl.BlockSpec((MOE_TILE, d), lambda i, wb, we, lo, hi: (wb[i], 0)),
                  pl.BlockSpec((MOE_TILE, 1), lambda i, wb, we, lo, hi: (wb[i], 0)),
                  pl.BlockSpec((1, d, D_EXPERT), lambda i, wb, we, lo, hi: (we[i], 0, 0)),
                  pl.BlockSpec((1, d, D_EXPERT), lambda i, wb, we, lo, hi: (we[i], 0, 0)),
                  pl.BlockSpec((1, D_EXPERT, d), lambda i, wb, we, lo, hi: (we[i], 0, 0))],
        out_specs=pl.BlockSpec((MOE_TILE, d), lambda i, wb, we, lo, hi: (wb[i], 0)),
        scratch_shapes=[pltpu.VMEM((d, D_EXPERT), BF16), pltpu.VMEM((d, D_EXPERT), BF16),
                        pltpu.VMEM((D_EXPERT, d), BF16), pltpu.SMEM((1,), jnp.int32)],
    )
    return pl.pallas_call(
        _moe_kernel,
        grid_spec=grid_spec,
        out_shape=jax.ShapeDtypeStruct((n_rows, d), BF16),
        compiler_params=_cparams("arbitrary"),
        name="moe_ffn",
    )(wb, we, lo, hi, xs, sw, w1, w3, w2)


def _final_kernel(x_ref, gt_ref, y0_ref, y1_ref, o_ref):
    o_ref[0] = x_ref[0] + gt_ref[0] * (y0_ref[0].astype(F32) + y1_ref[0].astype(F32))


def _final(x_new, gt2, y0, y1):
    b, l, d = x_new.shape
    tm = _pick(l, (512, 256, 128))
    tok = lambda: pl.BlockSpec((1, tm, d), lambda bi, i: (bi, i, 0))
    return pl.pallas_call(
        _final_kernel,
        grid=(b, l // tm),
        in_specs=[tok(), pl.BlockSpec((1, 1, d), lambda bi, i: (bi, 0, 0)), tok(), tok()],
        out_specs=tok(),
        out_shape=jax.ShapeDtypeStruct((b, l, d), F32),
        compiler_params=_cparams("parallel", "parallel"),
        name="moe_residual",
    )(x_new, gt2, y0, y1)


def _pad_lora_cols(w, widths):
    parts, o = [], 0
    for wd in widths:
        blk = w[..., o:o + wd]
        parts.append(jnp.pad(blk, [(0, 0)] * (w.ndim - 1) + [(0, LORA_PAD - wd)]))
        o += wd
    return jnp.concatenate(parts, axis=-1)


def _rope_tables(l):
    half = DIFF_QK_DIM // 2
    inv_freq = ROPE_THETA ** (-jnp.arange(0, half, 2, dtype=F32) / half)
    t = jnp.arange(l, dtype=jnp.int32)
    rows = (t // GRID_W).astype(F32)[:, None] * inv_freq
    cols = (t % GRID_W).astype(F32)[:, None] * inv_freq
    cos64 = jnp.concatenate([jnp.cos(rows), jnp.cos(rows), jnp.cos(cols), jnp.cos(cols)], axis=1)
    sin64 = jnp.concatenate([-jnp.sin(rows), jnp.sin(rows), -jnp.sin(cols), jnp.sin(cols)], axis=1)
    return jnp.tile(cos64, (1, 2)), jnp.tile(sin64, (1, 2))


def _route(routed, n_tok):
    expert = routed[:, :TOP_K].astype(jnp.int32)
    gate = routed[:, TOP_K:2 * TOP_K]

    n_assign = n_tok * TOP_K
    assert n_assign % MOE_TILE == 0
    flat_e = expert.reshape(-1).astype(jnp.int32)
    ids = jnp.arange(n_assign, dtype=jnp.int32)
    sorted_e, order, sorted_w = lax.sort((flat_e, ids, gate.reshape(-1)), num_keys=1, is_stable=True)
    _, rank = lax.sort((order, ids), num_keys=1)
    ends = jnp.searchsorted(sorted_e, jnp.arange(N_EXPERTS, dtype=jnp.int32), side='right').astype(jnp.int32)
    starts = jnp.concatenate([jnp.zeros((1,), jnp.int32), ends[:-1]])

    nb = n_assign // MOE_TILE
    blk_lo = jnp.arange(nb, dtype=jnp.int32) * MOE_TILE
    e_first = jnp.searchsorted(ends, blk_lo, side='right').astype(jnp.int32)
    e_last = jnp.searchsorted(ends, blk_lo + MOE_TILE - 1, side='right').astype(jnp.int32)
    per_blk = e_last - e_first + 1
    cum = jnp.cumsum(per_blk)
    it = jnp.arange(nb + N_EXPERTS - 1, dtype=jnp.int32)
    wb = jnp.minimum(jnp.searchsorted(cum, it, side='right'), nb - 1).astype(jnp.int32)
    we = jnp.clip(e_first[wb] + it - (cum[wb] - per_blk[wb]), 0, N_EXPERTS - 1)
    lo = jnp.clip(starts[we], blk_lo[wb], blk_lo[wb] + MOE_TILE) - blk_lo[wb]
    hi = jnp.clip(ends[we], blk_lo[wb], blk_lo[wb] + MOE_TILE) - blk_lo[wb]
    hi = jnp.where(it < cum[-1], hi, lo)
    return order // TOP_K, sorted_w, rank.reshape(n_tok, TOP_K), (wb, we, lo, hi)


def kernel(x, c, ctx, c_ctx, ada_w, ada_b, norm1_g, norm2_g, w_in, shift_mu, rwkv_w0, rwkv_w2, rwkv_a0, rwkv_a2,
           rwkv_g2, rwkv_k_k, rwkv_k_a, rwkv_r_k, rwkv_lnx_g, rwkv_lnx_b, qn_g, kn_g, diff_lambda, subln_g,
           w_pa, w_pb, w_out, router_g_w, router_g_b, router_e_w, router_e_b, exp_w1, exp_w3, exp_w2):
    assert ada_w.shape[0] == 1, "single-layer block"
    b, l, d = x.shape
    lc = ctx.shape[1]
    lam_init = 0.8 - 0.6 * math.exp(-0.3 * 0)
    lv = diff_lambda[0].astype(F32)
    lam = jnp.exp(jnp.sum(lv[0] * lv[1])) - jnp.exp(jnp.sum(lv[2] * lv[3])) + lam_init

    rows = (b + 1 + SUBLANES - 1) // SUBLANES * SUBLANES
    cm = jnp.zeros((rows, d), F32).at[:b].set(c).at[b].set(c_ctx)
    mod = _modulation(cm, ada_w[0], ada_b[0])
    sh1, sc1, gt1, sh2, sc2, gt2 = [mod[:b, None, k * d:(k + 1) * d] for k in range(6)]
    csh1, csc1 = [jnp.broadcast_to(mod[b, k * d:(k + 1) * d], (b, 1, d)) for k in range(2)]

    w = w_in[0]
    lora_widths = (DECAY_LORA, DECAY_LORA, AAA_LORA, AAA_LORA)
    o_lora = 3 * RWKV_DIM
    o_glora = o_lora + sum(lora_widths)
    pad_cols = lambda m: jnp.concatenate(
        [m[..., :o_lora], _pad_lora_cols(m[..., o_lora:o_glora], lora_widths), m[..., o_glora:RWKV_COLS]], axis=-1)
    w_rwkv = pad_cols(w).astype(BF16)
    w_diff = w[:, RWKV_COLS:RWKV_COLS + DIFF_COLS].astype(BF16)
    w_gate = w[:, RWKV_COLS + DIFF_COLS:].astype(BF16)
    g1 = norm1_g[0]
    hx = _norm_mod(x, g1, sc1, sh1)
    hc = _norm_mod(ctx, g1, csc1, csh1)
    px_r, px_d, px_g = _proj(hx, w_rwkv, BF16), _proj(hx, w_diff, BF16), _proj(hx, w_gate, BF16)
    pc_r, pc_d = _proj(hc, w_rwkv, BF16), _proj(hc, w_diff, BF16)

    pad_rows = lambda m: jnp.pad(m, ((0, 0), (0, LORA_PAD - m.shape[1]), (0, 0)))
    rparams = (pad_cols(shift_mu[0])[None], rwkv_w0[0], pad_rows(rwkv_w2[0]), rwkv_a0[0], pad_rows(rwkv_a2[0]),
               rwkv_g2[0], rwkv_k_k[0][None], rwkv_k_a[0][None], rwkv_r_k[0].reshape(1, RWKV_DIM))
    y_sweep, bonus, gate = _rwkv_scans(px_r, pc_r, rparams)

    cos_t, sin_t = _rope_tables(l)
    qg = jnp.tile(qn_g[0], 2)[None]
    kg = jnp.tile(kn_g[0], 2)[None]
    q_x, k_x, v_x = _diff_prep(px_d, cos_t, sin_t, qg, kg, True)
    _, k_c, v_c = _diff_prep(pc_d, cos_t[:lc], sin_t[:lc], qg, kg, False)
    lam_vec = jnp.full((1, LANES), lam, F32)
    sg_vec = (subln_g[0] * (1.0 - lam_init))[None]
    score_bound = (1.05 * DIFF_QK_DIM * DIFF_SCALE * math.log2(math.e)
                   * jnp.max(jnp.abs(qn_g[0])) * jnp.max(jnp.abs(kn_g[0])))
    attn_args = (q_x, k_c, v_c, k_x, v_x, lam_vec, sg_vec)
    y_diff = lax.cond(score_bound <= SCORE_LOG2_LIMIT,
                      lambda a: _diff_attention(*a, bounded=True),
                      lambda a: _diff_attention(*a, bounded=False), attn_args)

    mixed = _merge(y_sweep, bonus, gate, y_diff, px_g, rwkv_lnx_g[0][None], rwkv_lnx_b[0][None],
                   w_pa[0].astype(BF16), w_pb[0].astype(BF16))
    n_r = N_GROUPS + N_EXPERTS
    w_router = jnp.zeros((d, ROUTER_PAD), F32).at[:, :N_GROUPS].set(router_g_w[0]).at[:, N_GROUPS:n_r].set(router_e_w[0])
    b_router = jnp.zeros((1, ROUTER_PAD), F32).at[0, :N_GROUPS].set(router_g_b[0]).at[0, N_GROUPS:n_r].set(router_e_b[0])
    w_router_hi = w_router.astype(BF16)
    w_router_lo = (w_router - w_router_hi.astype(F32)).astype(BF16)
    x_new, h2, routed = _outproj(mixed, x, gt1, norm2_g[0][None], sc2, sh2, w_out[0].astype(BF16),
                                 jnp.concatenate([w_router_hi, w_router_lo], axis=1), b_router)

    n_tok = b * l
    row_tok, row_w, row_of, items = _route(routed.reshape(n_tok, ROUTER_PAD), n_tok)
    xs = h2.reshape(n_tok, d)[row_tok]
    out = _moe_ffn(xs, row_w[:, None], items, exp_w1[0], exp_w3[0], exp_w2[0])
    y0 = out[row_of[:, 0]].reshape(b, l, d)
    y1 = out[row_of[:, 1]].reshape(b, l, d)
    return _final(x_new, gt2, y0, y1)
```

```python
import functools
import math

import jax
import jax.numpy as jnp
from jax import lax
from jax.experimental import pallas as pl
from jax.experimental.pallas import tpu as pltpu

F32 = jnp.float32
BF16 = jnp.bfloat16
HIGHEST = lax.Precision.HIGHEST

D_MODEL = 2048
GRID_W = 64
RWKV_HEADS = 16
RWKV_HEAD_DIM = 64
RWKV_DIM = RWKV_HEADS * RWKV_HEAD_DIM
DECAY_LORA = 96
AAA_LORA = 96
GATE_LORA = 256
RWKV_COLS = 3 * RWKV_DIM + 2 * DECAY_LORA + 2 * AAA_LORA + GATE_LORA
DIFF_HEADS = 8
DIFF_QK_DIM = 64
DIFF_V_DIM = 2 * DIFF_QK_DIM
DIFF_DIM = DIFF_HEADS * DIFF_V_DIM
DIFF_QK_COLS = DIFF_HEADS * 2 * DIFF_QK_DIM
DIFF_COLS = 2 * DIFF_QK_COLS + DIFF_DIM
DIFF_SCALE = DIFF_QK_DIM ** -0.5
ROPE_THETA = 10000.0
ROPE_SUB = DIFF_QK_DIM // 4
GATE_COLS = 2 * D_MODEL
N_GROUPS = 4
EXPERTS_PER_GROUP = 8
N_EXPERTS = N_GROUPS * EXPERTS_PER_GROUP
TOP_K = 2
D_EXPERT = 512
NORM_EPS = 1e-6
SUBLN_EPS = 1e-5
LNX_EPS = 64e-5

LANES = 128
SUBLANES = 8
VMEM_LIMIT_BYTES = 56 * 1024 * 1024

LORA_PAD = LANES
RWKV_PCOLS = 3 * RWKV_DIM + 4 * LORA_PAD + GATE_LORA
CHUNK = 64
V_EXT = 2 * DIFF_V_DIM
SCORE_LOG2_LIMIT = 60.0
ROUTER_PAD = LANES
MOE_TILE = 512


def _cparams(*sem):
    return pltpu.CompilerParams(dimension_semantics=sem, vmem_limit_bytes=VMEM_LIMIT_BYTES)


def _sigmoid(x):
    return 1.0 / (1.0 + jnp.exp(-x))


def _dot(a, b, dims=(((1,), (0,)), ((), ()))):
    return lax.dot_general(a.astype(BF16), b.astype(BF16), dims, preferred_element_type=F32)


def _dot_f32(a, b, dims=(((1,), (0,)), ((), ()))):
    return lax.dot_general(a, b, dims, precision=HIGHEST, preferred_element_type=F32)


_NT = (((1,), (1,)), ((), ()))
_TN = (((0,), (0,)), ((), ()))


def _bf16_parts(x, n):
    parts = []
    for _ in range(n):
        p = x.astype(BF16)
        parts.append(p)
        x = x - p.astype(F32)
    return parts


def _dot_split(x, w_b, n, lhs=True):
    parts = _bf16_parts(x, n)
    if n == 1:
        return jnp.dot(parts[0], w_b, preferred_element_type=F32) if lhs else jnp.dot(w_b, parts[0], preferred_element_type=F32)
    if lhs:
        return jnp.dot(jnp.concatenate(parts, axis=1), jnp.concatenate([w_b] * n, axis=0), preferred_element_type=F32)
    return jnp.dot(jnp.concatenate([w_b] * n, axis=1), jnp.concatenate(parts, axis=0), preferred_element_type=F32)


def _group_ones(width, group):
    r = lax.broadcasted_iota(jnp.int32, (width, width), 0) // group
    c = lax.broadcasted_iota(jnp.int32, (width, width), 1) // group
    return (r == c).astype(BF16)


def _group_sum(x, group, pieces=2):
    ones = _group_ones(LANES, group)
    parts = [_dot_split(x[:, j * LANES:(j + 1) * LANES], ones, pieces) for j in range(x.shape[1] // LANES)]
    return parts[0] if len(parts) == 1 else jnp.concatenate(parts, axis=1)


def _mod_kernel(c_ref, w_ref, b_ref, o_ref):
    c = c_ref[...]
    o_ref[...] = _dot_f32(c * _sigmoid(c), w_ref[...]) + b_ref[...]


def _modulation(cm, ada_w, ada_b):
    rows, d = cm.shape
    n = ada_w.shape[1]
    tn = _pick(n, (1536, 1024, 512, 256, 128))
    return pl.pallas_call(
        _mod_kernel,
        grid=(n // tn,),
        in_specs=[pl.BlockSpec((rows, d), lambda j: (0, 0)),
                  pl.BlockSpec((d, tn), lambda j: (0, j)),
                  pl.BlockSpec((1, tn), lambda j: (0, j))],
        out_specs=pl.BlockSpec((rows, tn), lambda j: (0, j)),
        out_shape=jax.ShapeDtypeStruct((rows, n), F32),
        compiler_params=_cparams("parallel"),
        name="modulation",
    )(cm, ada_w, ada_b.reshape(1, n))


def _norm_mod_kernel(x_ref, g_ref, sc_ref, sh_ref, h_ref):
    x = x_ref[0]
    y = x * lax.rsqrt(jnp.mean(x * x, axis=-1, keepdims=True) + NORM_EPS) * g_ref[...]
    h_ref[0] = (y * (1.0 + sc_ref[0]) + sh_ref[0]).astype(BF16)


def _pick(n, prefs):
    for t in prefs:
        if n % t == 0:
            return t
    return n


def _norm_mod(x, g, sc, sh):
    b, l, d = x.shape
    tm = _pick(l, (512, 256, 128))
    return pl.pallas_call(
        _norm_mod_kernel,
        grid=(b, l // tm),
        in_specs=[pl.BlockSpec((1, tm, d), lambda bi, i: (bi, i, 0)),
                  pl.BlockSpec((1, d), lambda bi, i: (0, 0)),
                  pl.BlockSpec((1, 1, d), lambda bi, i: (bi, 0, 0)),
                  pl.BlockSpec((1, 1, d), lambda bi, i: (bi, 0, 0))],
        out_specs=pl.BlockSpec((1, tm, d), lambda bi, i: (bi, i, 0)),
        out_shape=jax.ShapeDtypeStruct((b, l, d), BF16),
        compiler_params=_cparams("parallel", "parallel"),
        name="norm_mod",
    )(x, g.reshape(1, d), sc, sh)


def _proj_kernel(h_ref, w_ref, o_ref):
    o_ref[0] = jnp.dot(h_ref[0], w_ref[...], preferred_element_type=F32).astype(o_ref.dtype)


def _proj(h, w, out_dtype):
    b, l, d = h.shape
    n = w.shape[1]
    tm = _pick(l, (1024, 512, 256, 128))
    tn = _pick(n, (1024, 768, 512, 256, 128))
    return pl.pallas_call(
        _proj_kernel,
        grid=(b, l // tm, n // tn),
        in_specs=[pl.BlockSpec((1, tm, d), lambda bi, i, j: (bi, i, 0)),
                  pl.BlockSpec((d, tn), lambda bi, i, j: (0, j))],
        out_specs=pl.BlockSpec((1, tm, tn), lambda bi, i, j: (bi, i, j)),
        out_shape=jax.ShapeDtypeStruct((b, l, n), out_dtype),
        compiler_params=_cparams("parallel", "parallel", "parallel"),
        name="proj",
    )(h, w)


def _pair_diag(x_b, mask_b):
    return jnp.concatenate([x_b, x_b], axis=0) * mask_b


def _tri_inverse_pairs(a_list, eye_f, mask_b):
    n = CHUNK
    mm = lambda l, r: jnp.dot(l.astype(BF16), _pair_diag(r.astype(BF16), mask_b), preferred_element_type=F32)
    xs = [eye_f + a for a in a_list]
    ps = [mm(a, a) for a in a_list]
    steps = int(math.log2(n)) - 1
    for s in range(steps):
        if s < steps - 1:
            xps = [mm(jnp.concatenate([x, p], axis=0), p) for x, p in zip(xs, ps)]
            xs = [x + xp[:n] for x, xp in zip(xs, xps)]
            ps = [xp[n:] for xp in xps]
        else:
            xs = [x + mm(x, p) for x, p in zip(xs, ps)]
    return xs


def _rwkv_chunk_kernel(p_ref, pp_ref, pn_ref, mu_ref, w0_ref, w2_ref, a0_ref, a2_ref, g2_ref, kk_ref, ka_ref,
                       rk_ref, q_ref, y0_ref, m_ref, n_ref, bonus_ref, gate_ref):
    i = pl.program_id(1)
    last = pl.num_programs(1) - 1
    c = RWKV_DIM
    hd = RWKV_HEAD_DIM
    p = p_ref[0]
    t_rows = p.shape[0]
    n_sub = t_rows // CHUNK
    row = lax.broadcasted_iota(jnp.int32, (t_rows, 1), 0)
    prev_row = jnp.where(i == 0, 0.0, pp_ref[0, SUBLANES - 1:SUBLANES, :])
    next_row = jnp.where(i == last, 0.0, pn_ref[0, 0:1, :])
    prev = jnp.where(row == 0, prev_row, pltpu.roll(p, 1, axis=0))
    nxt = jnp.where(row == t_rows - 1, next_row, pltpu.roll(p, t_rows - 1, axis=0))
    ps = p + mu_ref[...] * (0.5 * (prev + nxt) - p)

    r, k, v = ps[:, :c], ps[:, c:2 * c], ps[:, 2 * c:3 * c]
    o = 3 * c
    xw = (ps[:, o:o + LORA_PAD], ps[:, o + LORA_PAD:o + 2 * LORA_PAD])
    xa = (ps[:, o + 2 * LORA_PAD:o + 3 * LORA_PAD], ps[:, o + 3 * LORA_PAD:o + 4 * LORA_PAD])
    xg = ps[:, o + 4 * LORA_PAD:]

    gate_ref[0] = _dot(_sigmoid(xg), g2_ref[...]).astype(gate_ref.dtype)
    kk = k * kk_ref[...]
    kk = kk * lax.rsqrt(_group_sum(kk * kk, hd, pieces=1) + 1e-12)

    tr = lax.broadcasted_iota(jnp.int32, (t_rows, t_rows), 0)
    tc = lax.broadcasted_iota(jnp.int32, (t_rows, t_rows), 1)
    same_chunk = (tr // CHUNK) == (tc // CHUNK)
    tr2 = lax.broadcasted_iota(jnp.int32, (CHUNK, LANES), 0)
    lane2 = lax.broadcasted_iota(jnp.int32, (CHUNK, LANES), 1)
    tc2 = lane2 % CHUNK
    lane_head = lane2 // hd
    eye2 = tr2 == tc2
    mask_b = _group_ones(LANES, hd)
    kd_sum = jnp.zeros_like(k)
    v_b = v.astype(BF16)
    psl = [slice(p * LANES, (p + 1) * LANES) for p in range(c // LANES)]
    before2, upto2, prep = [], [], []
    for d in range(2):
        before2.append((tc2 < tr2) if d == 0 else (tc2 > tr2))
        upto = same_chunk & ((tc <= tr) if d == 0 else (tc >= tr))
        upto2.append((tc2 <= tr2) if d == 0 else (tc2 >= tr2))
        z = w0_ref[d:d + 1, :] + _dot(jnp.tanh(xw[d]), w2_ref[d])
        w_log = -(jnp.maximum(-z, 0.0) + jnp.log(1.0 + jnp.exp(-jnp.abs(z)))) - 0.5
        logw = -jnp.exp(w_log)
        a = _sigmoid(a0_ref[d:d + 1, :] + _dot(xa[d], a2_ref[d]))
        kd = k * (1.0 + (a - 1.0) * ka_ref[...])
        kd_sum = kd_sum + kd
        cum = _dot_split(logw, upto.astype(BF16), 3, lhs=False)
        last_row = [u * CHUNK + (CHUNK - 1 if d == 0 else 0) for u in range(n_sub)]
        totals = [cum[t:t + 1, :] for t in last_row]
        total = totals[0]
        for u in range(1, n_sub):
            total = jnp.where(row >= u * CHUNK, totals[u], total)
        e_neg = jnp.exp(-cum)
        e_rest = jnp.exp(total - cum)
        p_total = [jnp.exp(t) for t in totals]
        beta = kk * a
        al = -kk * jnp.exp(cum - logw)
        rt = r * jnp.exp(cum)
        al_b = al.astype(BF16)
        rt_b = rt.astype(BF16)
        bt_b = (beta * e_neg).astype(BF16)
        kt_b = (kd * e_neg).astype(BF16)
        bh_b = (beta * e_rest).astype(BF16)
        kh_b = (kd * e_rest).astype(BF16)
        prep.append((al_b, rt_b, bt_b, kt_b, bh_b, kh_b, rt, p_total))

    rsl = [slice(u * CHUNK, (u + 1) * CHUNK) for u in range(n_sub)]
    dp = [(d, rs, s) for rs in rsl for d in range(2) for s in psl]
    fdot = lambda a, b: jnp.dot(a, b, preferred_element_type=F32)
    bd = lambda x: _pair_diag(x, mask_b)
    v_bd = [bd(v_b[rs, s]) for d, rs, s in dp]
    zeros_sq = jnp.zeros((LANES, LANES), BF16)
    zeros_tl = jnp.zeros((CHUNK, LANES), BF16)
    gm = [lax.dot_general(jnp.concatenate([prep[d][0][rs, s], prep[d][1][rs, s]], axis=0),
                          jnp.concatenate([bd(prep[d][2][rs, s]), bd(prep[d][3][rs, s])], axis=0),
                          _NT, preferred_element_type=F32) for d, rs, s in dp]
    a_ab = [jnp.where(before2[d], x[:CHUNK, :LANES], 0.0) for x, (d, rs, s) in zip(gm, dp)]
    a_ak = [jnp.where(before2[d], x[:CHUNK, LANES:], 0.0).astype(BF16) for x, (d, rs, s) in zip(gm, dp)]
    a_rb = [jnp.where(upto2[d], x[CHUNK:, :LANES], 0.0).astype(BF16) for x, (d, rs, s) in zip(gm, dp)]
    a_rk = [jnp.where(upto2[d], x[CHUNK:, LANES:], 0.0).astype(BF16) for x, (d, rs, s) in zip(gm, dp)]
    akv = [fdot(ak, vd) for ak, vd in zip(a_ak, v_bd)]
    t_inv = _tri_inverse_pairs(a_ab, eye2.astype(F32), mask_b)
    wu = [fdot(t.astype(BF16), jnp.concatenate([bd(prep[d][0][rs, s]), bd(u.astype(BF16))], axis=1))
          for t, u, (d, rs, s) in zip(t_inv, akv, dp)]
    w_b = [x[:, :LANES].astype(BF16) for x in wu]
    u_b = [x[:, LANES:].astype(BF16) for x in wu]
    qy = [fdot(jnp.concatenate([rb, rk], axis=1),
               jnp.concatenate([jnp.concatenate([bd(w), bd(u)], axis=1),
                                jnp.concatenate([zeros_sq, vd], axis=1)], axis=0))
          for rb, rk, w, u, vd in zip(a_rb, a_rk, w_b, u_b, v_bd)]
    full = [lax.dot_general(jnp.concatenate([jnp.concatenate([w, u], axis=1),
                                             jnp.concatenate([zeros_tl, v_b[rs, s]], axis=1)], axis=0),
                            jnp.concatenate([prep[d][4][rs, s], prep[d][5][rs, s]], axis=0),
                            _TN, preferred_element_type=F32)
            for w, u, (d, rs, s) in zip(w_b, u_b, dp)]

    def diag_blocks(x):
        return jnp.where(lane_head == 0, x[:hd], 0.0) + jnp.where(lane_head == 1, x[hd:2 * hd], 0.0)

    npair = len(psl)
    for u, rs in enumerate(rsl):
        for d in range(2):
            sl = slice((2 * u + d) * npair, (2 * u + d + 1) * npair)
            rt, p_total = prep[d][6][rs], prep[d][7][u]
            q_ref[0, d, rs, :] = (rt + jnp.concatenate([x[:, :LANES] for x in qy[sl]], axis=1)).astype(q_ref.dtype)
            y0_ref[0, d, rs, :] = jnp.concatenate([x[:, LANES:] for x in qy[sl]], axis=1).astype(y0_ref.dtype)
            m_ref[0, d, u] = jnp.concatenate([jnp.where(eye2, p_total[:, s], 0.0) + diag_blocks(x[:LANES])
                                              for x, s in zip(full[sl], psl)], axis=1).astype(m_ref.dtype)
            n_ref[0, d, u] = jnp.concatenate([diag_blocks(x[LANES:]) for x in full[sl]], axis=1)
    bonus_ref[0] = (_group_sum(r * kd_sum * rk_ref[...], hd) * v).astype(bonus_ref.dtype)


def _rwkv_chunk_ops(p, params):
    b, l, pc = p.shape
    nc = l // CHUNK
    c = RWKV_DIM
    g = _pick(nc, (2, 1))
    t = g * CHUNK
    hb = t // SUBLANES
    nb8 = l // SUBLANES
    mu, w0, w2, a0, a2, g2, k_k, k_a, r_k = params
    const = lambda shape: pl.BlockSpec(shape, lambda bi, i: (0,) * len(shape))
    tok = lambda: pl.BlockSpec((1, t, c), lambda bi, i: (bi, i, 0))
    return pl.pallas_call(
        _rwkv_chunk_kernel,
        grid=(b, nc // g),
        in_specs=[pl.BlockSpec((1, t, pc), lambda bi, i: (bi, i, 0)),
                  pl.BlockSpec((1, SUBLANES, pc), lambda bi, i: (bi, jnp.maximum(i * hb - 1, 0), 0)),
                  pl.BlockSpec((1, SUBLANES, pc), lambda bi, i: (bi, jnp.minimum((i + 1) * hb, nb8 - 1), 0)),
                  const((1, pc)), const((2, c)), const((2, LORA_PAD, c)), const((2, c)), const((2, LORA_PAD, c)),
                  const((GATE_LORA, c)), const((1, c)), const((1, c)), const((1, c))],
        out_specs=[pl.BlockSpec((1, 2, t, c), lambda bi, i: (bi, 0, i, 0)),
                   pl.BlockSpec((1, 2, t, c), lambda bi, i: (bi, 0, i, 0)),
                   pl.BlockSpec((1, 2, g, RWKV_HEAD_DIM, c), lambda bi, i: (bi, 0, i, 0, 0)),
                   pl.BlockSpec((1, 2, g, RWKV_HEAD_DIM, c), lambda bi, i: (bi, 0, i, 0, 0)),
                   tok(), tok()],
        out_shape=[jax.ShapeDtypeStruct((b, 2, l, c), BF16), jax.ShapeDtypeStruct((b, 2, l, c), BF16),
                   jax.ShapeDtypeStruct((b, 2, nc, RWKV_HEAD_DIM, c), BF16),
                   jax.ShapeDtypeStruct((b, 2, nc, RWKV_HEAD_DIM, c), F32),
                   jax.ShapeDtypeStruct((b, l, c), BF16), jax.ShapeDtypeStruct((b, l, c), BF16)],
        compiler_params=_cparams("parallel", "parallel"),
        name="rwkv_chunk_ops",
    )(p, p, p, mu, w0, w2, a0, a2, g2, k_k, k_a, r_k)


def _rwkv_sweep_kernel(qf_ref, y0f_ref, mf_ref, nf_ref, qb_ref, y0b_ref, mb_ref, nb_ref, s0_ref,
                       yf_ref, yb_ref, sfin_ref, s_ref):
    j = pl.program_id(1)
    hd = RWKV_HEAD_DIM

    @pl.when(j == 0)
    def _():
        s_ref[...] = s0_ref[0]

    dirs = ((qf_ref, y0f_ref, mf_ref, nf_ref, yf_ref), (qb_ref, y0b_ref, mb_ref, nb_ref, yb_ref))
    psl = [slice(p * LANES, (p + 1) * LANES) for p in range(RWKV_DIM // LANES)]
    mask_b = _group_ones(LANES, hd)
    pair_diag = lambda x: _pair_diag(x, mask_b)
    per_step = mf_ref.shape[2]
    state = [s_ref[0], s_ref[1]]
    for u in range(per_step):
        for d, (q_ref, y0_ref, m_ref, n_ref, y_ref) in enumerate(dirs):
            cu = u if d == 0 else per_step - 1 - u
            rows = slice(cu * CHUNK, (cu + 1) * CHUNK)
            s_b = state[d].astype(BF16)
            q_b = q_ref[0, 0, rows, :]
            m_b = m_ref[0, 0, cu]
            ys = [_dot(q_b[:, s], pair_diag(s_b[:, s]), _NT) for s in psl]
            sm = [_dot(s_b[:, s], pair_diag(m_b[:, s])) for s in psl]
            y_ref[0, rows, :] = (y0_ref[0, 0, rows, :].astype(F32) + jnp.concatenate(ys, axis=1)).astype(y_ref.dtype)
            state[d] = n_ref[0, 0, cu] + jnp.concatenate(sm, axis=1)
    s_ref[0] = state[0]
    s_ref[1] = state[1]

    @pl.when(j == pl.num_programs(1) - 1)
    def _():
        sfin_ref[0] = s_ref[...]


def _rwkv_sweep(q, y0, m, n, s0):
    b, _, l, c = q.shape
    nc = l // CHUNK
    hd = RWKV_HEAD_DIM
    g = _pick(nc, (8, 4, 2, 1))
    ns = nc // g
    tokf = lambda: pl.BlockSpec((1, 1, g * CHUNK, c), lambda bi, j: (bi, 0, j, 0))
    tokb = lambda: pl.BlockSpec((1, 1, g * CHUNK, c), lambda bi, j: (bi, 1, ns - 1 - j, 0))
    opf = lambda: pl.BlockSpec((1, 1, g, hd, c), lambda bi, j: (bi, 0, j, 0, 0))
    opb = lambda: pl.BlockSpec((1, 1, g, hd, c), lambda bi, j: (bi, 1, ns - 1 - j, 0, 0))
    return pl.pallas_call(
        _rwkv_sweep_kernel,
        grid=(b, ns),
        in_specs=[tokf(), tokf(), opf(), opf(), tokb(), tokb(), opb(), opb(),
                  pl.BlockSpec((1, 2, hd, c), lambda bi, j: (bi, 0, 0, 0))],
        out_specs=[pl.BlockSpec((1, g * CHUNK, c), lambda bi, j: (bi, j, 0)),
                   pl.BlockSpec((1, g * CHUNK, c), lambda bi, j: (bi, ns - 1 - j, 0)),
                   pl.BlockSpec((1, 2, hd, c), lambda bi, j: (bi, 0, 0, 0))],
        out_shape=[jax.ShapeDtypeStruct((b, l, c), BF16), jax.ShapeDtypeStruct((b, l, c), BF16),
                   jax.ShapeDtypeStruct((b, 2, hd, c), F32)],
        scratch_shapes=[pltpu.VMEM((2, hd, c), F32)],
        compiler_params=_cparams("parallel", "arbitrary"),
        name="rwkv_sweep",
    )(q, y0, m, n, q, y0, m, n, s0)


def _rwkv_scans(px_r, pc_r, rparams):
    b = px_r.shape[0]
    qc, y0c, mc, nc_, _, _ = _rwkv_chunk_ops(pc_r, rparams)
    s_zero = jnp.zeros((b, 2, RWKV_HEAD_DIM, RWKV_DIM), F32)
    _, _, s_ctx = _rwkv_sweep(qc, y0c, mc, nc_, s_zero)
    qx, y0x, mx, nx, bonus, gate = _rwkv_chunk_ops(px_r, rparams)
    yf, yb, _ = _rwkv_sweep(qx, y0x, mx, nx, s_ctx)
    return (yf, yb), bonus, gate


def _diff_prep_kernel(p_ref, cos_ref, sin_ref, qg_ref, kg_ref, q_ref, k_ref, v_ref, *, rope):
    lane = lax.broadcasted_iota(jnp.int32, (1, LANES), 1)
    first = (lane % (2 * ROPE_SUB)) < ROPE_SUB
    for hd in range(DIFF_HEADS):
        for off, g_ref, o_ref, scale in ((0, qg_ref, q_ref, DIFF_SCALE * math.log2(math.e)),
                                         (DIFF_QK_COLS, kg_ref, k_ref, 1.0)):
            cs = slice(hd * LANES, (hd + 1) * LANES)
            xb = p_ref[0, :, off + hd * LANES:off + (hd + 1) * LANES].astype(F32)
            ms = _group_sum(xb * xb, DIFF_QK_DIM, pieces=1) * (1.0 / DIFF_QK_DIM)
            y = xb * lax.rsqrt(ms + NORM_EPS) * g_ref[...]
            if rope:
                swapped = jnp.where(first, pltpu.roll(y, LANES - ROPE_SUB, axis=1), pltpu.roll(y, ROPE_SUB, axis=1))
                y = y * cos_ref[...] + swapped * sin_ref[...]
            o_ref[0, :, cs] = (y * scale).astype(BF16)
    ones = jnp.ones((p_ref.shape[1], V_EXT - DIFF_V_DIM), BF16)
    for hd in range(DIFF_HEADS):
        vb = p_ref[0, :, 2 * DIFF_QK_COLS + hd * DIFF_V_DIM:2 * DIFF_QK_COLS + (hd + 1) * DIFF_V_DIM]
        v_ref[0, :, hd * V_EXT:(hd + 1) * V_EXT] = jnp.concatenate([vb.astype(BF16), ones], axis=1)


def _diff_prep(p, cos_t, sin_t, qg, kg, rope):
    b, l, pc = p.shape
    t = _pick(l, (512, 256, 128))
    tok = lambda w=DIFF_DIM: pl.BlockSpec((1, t, w), lambda bi, i: (bi, i, 0))
    shp = jax.ShapeDtypeStruct((b, l, DIFF_DIM), BF16)
    shp_v = jax.ShapeDtypeStruct((b, l, DIFF_HEADS * V_EXT), BF16)
    return pl.pallas_call(
        functools.partial(_diff_prep_kernel, rope=rope),
        grid=(b, l // t),
        in_specs=[pl.BlockSpec((1, t, pc), lambda bi, i: (bi, i, 0)),
                  pl.BlockSpec((t, LANES), lambda bi, i: (i, 0)),
                  pl.BlockSpec((t, LANES), lambda bi, i: (i, 0)),
                  pl.BlockSpec((1, LANES), lambda bi, i: (0, 0)),
                  pl.BlockSpec((1, LANES), lambda bi, i: (0, 0))],
        out_specs=[tok(), tok(), tok(DIFF_HEADS * V_EXT)],
        out_shape=[shp, shp, shp_v],
        compiler_params=_cparams("parallel", "parallel"),
        name="diff_prep_rope" if rope else "diff_prep",
    )(p, cos_t, sin_t, qg, kg)


def _diff_finish(acc1, acc2, lam_ref, sg_ref, o_ref):
    dv = DIFF_V_DIM
    o = acc1[:, :dv] / acc1[:, dv:] - lam_ref[...] * (acc2[:, :dv] / acc2[:, dv:])
    o = o * lax.rsqrt(jnp.mean(o * o, axis=-1, keepdims=True) + SUBLN_EPS)
    o_ref[0] = (o * sg_ref[...]).astype(o_ref.dtype)


def _flash_online_kernel(lam_ref, sg_ref, q_ref, kc_ref, vc_ref, k_ref, v_ref, o_ref, *, tk):
    qd = DIFF_QK_DIM
    tq = q_ref.shape[1]
    q = q_ref[0]
    qs = (q[:, :qd], q[:, qd:])

    def absorb(state, k, v):
        reps = k.shape[0] // LANES
        out = []
        for mp in range(2):
            m_prev, acc = state[mp]
            s = lax.dot_general(qs[mp], k[:, mp * qd:(mp + 1) * qd], _NT, preferred_element_type=F32)
            m_new = jnp.maximum(m_prev, jnp.max(s, axis=-1, keepdims=True))
            alpha = jnp.exp2(m_prev - m_new)
            pr = jnp.exp2(s - jnp.concatenate([m_new] * reps, axis=1))
            acc_new = jnp.concatenate([alpha, alpha], axis=1) * acc + jnp.dot(pr.astype(BF16), v, preferred_element_type=F32)
            out.append((m_new, acc_new))
        return tuple(out)

    init = (jnp.full((tq, LANES), -jnp.inf, F32), jnp.zeros((tq, V_EXT), F32))
    state = absorb((init, init), kc_ref[0], vc_ref[0])

    def body(j, state):
        rows = pl.ds(pl.multiple_of(j * tk, tk), tk)
        return absorb(state, k_ref[0, rows, :], v_ref[0, rows, :])

    (_, acc1), (_, acc2) = lax.fori_loop(0, k_ref.shape[1] // tk, body, state)
    _diff_finish(acc1, acc2, lam_ref, sg_ref, o_ref)


def _flash_bounded_kernel(lam_ref, sg_ref, q_ref, kc_ref, vc_ref, k_ref, v_ref, o_ref, acc_ref, p_ref, *, tk):
    qd = DIFF_QK_DIM
    q = q_ref[0]
    qs = (q[:, :qd], q[:, qd:])
    n_kv = k_ref.shape[1] // tk

    def weights(mp, k):
        s = lax.dot_general(qs[mp], k[:, mp * qd:(mp + 1) * qd], _NT, preferred_element_type=F32)
        return jnp.exp2(s).astype(BF16)

    def chunk(c):
        return pl.ds(pl.multiple_of(jnp.minimum(c, n_kv - 1) * tk, tk), tk)

    for mp in range(2):
        acc_ref[mp] = jnp.dot(weights(mp, kc_ref[0]), vc_ref[0], preferred_element_type=F32)
        p_ref[0, mp] = weights(mp, k_ref[0, pl.ds(0, tk), :])

    steps = _pick(n_kv, (16, 8, 4, 2))

    def body(j, carry):
        for u in range(steps):
            c = steps * j + u
            v = v_ref[0, chunk(c), :]
            k_next = k_ref[0, chunk(c + 1), :]
            for mp in range(2):
                acc_ref[mp] += jnp.dot(p_ref[u % 2, mp], v, preferred_element_type=F32)
                p_ref[1 - u % 2, mp] = weights(mp, k_next)
        return carry

    lax.fori_loop(0, n_kv // steps, body, 0)
    _diff_finish(acc_ref[0], acc_ref[1], lam_ref, sg_ref, o_ref)


def _diff_attention(q, kc, vc, k, v, lam_vec, sg_vec, bounded):
    b, l, _ = q.shape
    lc = kc.shape[1]
    tq = _pick(l, (256, 128))
    tk = _pick(l, (512, 256, 128))
    if bounded:
        tq = _pick(l, (1024, 512, 256, 128))
        tk = _pick(l // 2, (512, 256, 128))
        assert l % (2 * tk) == 0
        body = functools.partial(_flash_bounded_kernel, tk=tk)
        scratch = [pltpu.VMEM((2, tq, V_EXT), F32), pltpu.VMEM((2, 2, tq, tk), BF16)]
    else:
        body = functools.partial(_flash_online_kernel, tk=tk)
        scratch = []
    return pl.pallas_call(
        body,
        grid=(b, DIFF_HEADS, l // tq),
        in_specs=[pl.BlockSpec((1, LANES), lambda bi, h, i: (0, 0)),
                  pl.BlockSpec((1, LANES), lambda bi, h, i: (0, 0)),
                  pl.BlockSpec((1, tq, LANES), lambda bi, h, i: (bi, i, h)),
                  pl.BlockSpec((1, lc, LANES), lambda bi, h, i: (bi, 0, h)),
                  pl.BlockSpec((1, lc, V_EXT), lambda bi, h, i: (bi, 0, h)),
                  pl.BlockSpec((1, l, LANES), lambda bi, h, i: (bi, 0, h)),
                  pl.BlockSpec((1, l, V_EXT), lambda bi, h, i: (bi, 0, h))],
        out_specs=pl.BlockSpec((1, tq, LANES), lambda bi, h, i: (bi, i, h)),
        out_shape=jax.ShapeDtypeStruct((b, l, DIFF_DIM), BF16),
        scratch_shapes=scratch,
        compiler_params=_cparams("parallel", "parallel", "arbitrary"),
        name="diff_flash_bounded" if bounded else "diff_flash_online",
    )(lam_vec, sg_vec, q, kc, vc, k, v)


def _merge_kernel(yf_ref, yb_ref, bonus_ref, gate_ref, yd_ref, pg_ref, lg_ref, lb_ref, wpa_ref, wpb_ref, o_ref):
    hd = RWKV_HEAD_DIM
    y = yf_ref[0].astype(F32) + yb_ref[0].astype(F32)
    dev = y - _group_sum(y, hd) * (1.0 / hd)
    var = _group_sum(dev * dev, hd, pieces=1) * (1.0 / hd)
    yn = dev * lax.rsqrt(var + LNX_EPS) * lg_ref[...] + lb_ref[...]
    y_rwkv = (yn + bonus_ref[0].astype(F32)) * gate_ref[0].astype(F32)
    a = _dot(y_rwkv, wpa_ref[...])
    bb = jnp.dot(yd_ref[0], wpb_ref[...], preferred_element_type=F32)
    ga = _sigmoid(pg_ref[0, :, :D_MODEL].astype(F32))
    gb = _sigmoid(pg_ref[0, :, D_MODEL:].astype(F32))
    o_ref[0] = (ga * a + gb * bb).astype(BF16)


def _merge(y_sweep, bonus, gate, y_diff, p_gate, lnx_g, lnx_b, w_pa, w_pb):
    b, l, c = bonus.shape
    d = D_MODEL
    tm = _pick(l, (256, 128))
    const = lambda shape: pl.BlockSpec(shape, lambda bi, i: (0,) * len(shape))
    tok = lambda w: pl.BlockSpec((1, tm, w), lambda bi, i: (bi, i, 0))
    return pl.pallas_call(
        _merge_kernel,
        grid=(b, l // tm),
        in_specs=[tok(c), tok(c), tok(c), tok(c), tok(DIFF_DIM), tok(GATE_COLS),
                  const((1, c)), const((1, c)), const((c, d)), const((DIFF_DIM, d))],
        out_specs=tok(d),
        out_shape=jax.ShapeDtypeStruct((b, l, d), BF16),
        compiler_params=_cparams("parallel", "parallel"),
        name="merge",
    )(y_sweep[0], y_sweep[1], bonus, gate, y_diff, p_gate, lnx_g, lnx_b, w_pa, w_pb)


def _outproj_kernel(mx_ref, x_ref, gt_ref, g_ref, sc_ref, sh_ref, wo_ref, wr_ref, br_ref, xn_ref, h_ref, rt_ref):
    mix = jnp.dot(mx_ref[0], wo_ref[...], preferred_element_type=F32)
    xn = x_ref[0] + gt_ref[0] * mix
    xn_ref[0] = xn
    y = xn * lax.rsqrt(jnp.mean(xn * xn, axis=-1, keepdims=True) + NORM_EPS) * g_ref[...]
    h = y * (1.0 + sc_ref[0]) + sh_ref[0]
    h_hi, h_lo = _bf16_parts(h, 2)
    h_ref[0] = h_hi
    dot = lambda a, b: jnp.dot(a, b, preferred_element_type=F32)
    hh = dot(h_hi, wr_ref[...])
    lg = hh[:, :ROUTER_PAD] + (dot(h_lo, wr_ref[:, :ROUTER_PAD]) + hh[:, ROUTER_PAD:]) + br_ref[...]
    rt_ref[0] = _route_rows(lg)


def _route_rows(lg):
    lane = lax.broadcasted_iota(jnp.int32, lg.shape, 1).astype(F32)
    neg = jnp.float32(-3.0e38)
    far = jnp.float32(LANES)
    gl = jnp.where(lane < N_GROUPS, lg, neg)
    g_max = jnp.max(gl, axis=-1, keepdims=True)
    g_top = jnp.min(jnp.where(gl == g_max, lane, far), axis=-1, keepdims=True)
    p_g = 1.0 / jnp.sum(jnp.where(lane < N_GROUPS, jnp.exp(gl - g_max), 0.0), axis=-1, keepdims=True)
    e_lo = N_GROUPS + EXPERTS_PER_GROUP * g_top
    el = jnp.where((lane >= e_lo) & (lane < e_lo + EXPERTS_PER_GROUP), lg, neg)
    m1 = jnp.max(el, axis=-1, keepdims=True)
    i1 = jnp.min(jnp.where(el == m1, lane, far), axis=-1, keepdims=True)
    el2 = jnp.where(lane == i1, neg, el)
    m2 = jnp.max(el2, axis=-1, keepdims=True)
    i2 = jnp.min(jnp.where(el2 == m2, lane, far), axis=-1, keepdims=True)
    e2 = jnp.exp(m2 - m1)
    gate1 = p_g / (1.0 + e2)
    gate2 = p_g * e2 / (1.0 + e2)
    return jnp.where(lane == 0, i1 - N_GROUPS,
                     jnp.where(lane == 1, i2 - N_GROUPS, jnp.where(lane == 2, gate1, jnp.where(lane == 3, gate2, 0.0))))


def _outproj(mixed, x, gt1, g2, sc2, sh2, w_out, w_router, b_router):
    b, l, d = x.shape
    tm = _pick(l, (512, 256, 128))
    const = lambda shape: pl.BlockSpec(shape, lambda bi, i: (0,) * len(shape))
    tok = lambda w: pl.BlockSpec((1, tm, w), lambda bi, i: (bi, i, 0))
    per_b = lambda: pl.BlockSpec((1, 1, d), lambda bi, i: (bi, 0, 0))
    return pl.pallas_call(
        _outproj_kernel,
        grid=(b, l // tm),
        in_specs=[tok(d), tok(d), per_b(), const((1, d)), per_b(), per_b(),
                  const((d, d)), const((d, 2 * ROUTER_PAD)), const((1, ROUTER_PAD))],
        out_specs=[tok(d), tok(d), tok(ROUTER_PAD)],
        out_shape=[jax.ShapeDtypeStruct((b, l, d), F32), jax.ShapeDtypeStruct((b, l, d), BF16),
                   jax.ShapeDtypeStruct((b, l, ROUTER_PAD), F32)],
        compiler_params=_cparams("parallel", "parallel"),
        name="outproj_router",
    )(mixed, x, gt1, g2, sc2, sh2, w_out, w_router, b_router)


def _moe_kernel(wb_ref, we_ref, lo_ref, hi_ref, x_ref, sw_ref, w1_ref, w3_ref, w2_ref, o_ref,
                w1b_ref, w3b_ref, w2b_ref, cached_ref):
    i = pl.program_id(0)
    lo, hi = lo_ref[i], hi_ref[i]
    live = hi > lo

    @pl.when(i == 0)
    def _():
        cached_ref[0] = -1

    @pl.when((i == 0) | (wb_ref[i] != wb_ref[jnp.maximum(i - 1, 0)]))
    def _():
        o_ref[...] = jnp.zeros(o_ref.shape, o_ref.dtype)

    @pl.when(live & (cached_ref[0] != we_ref[i]))
    def _():
        w1b_ref[...] = w1_ref[0].astype(BF16)
        w3b_ref[...] = w3_ref[0].astype(BF16)
        w2b_ref[...] = w2_ref[0].astype(BF16)
        cached_ref[0] = we_ref[i]

    @pl.when(live)
    def _():
        xb = x_ref[...]
        u = jnp.dot(xb, w1b_ref[...], preferred_element_type=F32)
        g = jnp.dot(xb, w3b_ref[...], preferred_element_type=F32)
        hmid = (u * _sigmoid(u) * g).astype(BF16)
        res = (jnp.dot(hmid, w2b_ref[...], preferred_element_type=F32) * sw_ref[...]).astype(o_ref.dtype)
        row = lax.broadcasted_iota(jnp.int32, (o_ref.shape[0], 1), 0)
        o_ref[...] = jnp.where((row >= lo) & (row < hi), res, o_ref[...])


def _moe_ffn(xs, sw, items, w1, w3, w2):
    n_rows, d = xs.shape
    wb, we, lo, hi = items
    grid_spec = pltpu.PrefetchScalarGridSpec(
        num_scalar_prefetch=4,
        grid=(wb.shape[0],),
        in_specs=[pl.BlockSpec((MOE_TILE, d), lambda i, wb, we, lo, hi: (wb[i], 0)),
                  pl.BlockSpec((MOE_TILE, 1), lambda i, wb, we, lo, hi: (wb[i], 0)),
                  pl.BlockSpec((1, d, D_EXPERT), lambda i, wb, we, lo, hi: (we[i], 0, 0)),
                  pl.BlockSpec((1, d, D_EXPERT), lambda i, wb, we, lo, hi: (we[i], 0, 0)),
                  pl.BlockSpec((1, D_EXPERT, d), lambda i, wb, we, lo, hi: (we[i], 0, 0))],
        out_specs=pl.BlockSpec((MOE_TILE, d), lambda i, wb, we, lo, hi: (wb[i], 0)),
        scratch_shapes=[pltpu.VMEM((d, D_EXPERT), BF16), pltpu.VMEM((d, D_EXPERT), BF16),
                        pltpu.VMEM((D_EXPERT, d), BF16), pltpu.SMEM((1,), jnp.int32)],
    )
    return pl.pallas_call(
        _moe_kernel,
        grid_spec=grid_spec,
        out_shape=jax.ShapeDtypeStruct((n_rows, d), BF16),
        compiler_params=_cparams("arbitrary"),
        name="moe_ffn",
    )(wb, we, lo, hi, xs, sw, w1, w3, w2)


def _final_kernel(x_ref, gt_ref, y0_ref, y1_ref, o_ref):
    o_ref[0] = x_ref[0] + gt_ref[0] * (y0_ref[0].astype(F32) + y1_ref[0].astype(F32))


def _final(x_new, gt2, y0, y1):
    b, l, d = x_new.shape
    tm = _pick(l, (512, 256, 128))
    tok = lambda: pl.BlockSpec((1, tm, d), lambda bi, i: (bi, i, 0))
    return pl.pallas_call(
        _final_kernel,
        grid=(b, l // tm),
        in_specs=[tok(), pl.BlockSpec((1, 1, d), lambda bi, i: (bi, 0, 0)), tok(), tok()],
        out_specs=tok(),
        out_shape=jax.ShapeDtypeStruct((b, l, d), F32),
        compiler_params=_cparams("parallel", "parallel"),
        name="moe_residual",
    )(x_new, gt2, y0, y1)


def _pad_lora_cols(w, widths):
    parts, o = [], 0
    for wd in widths:
        blk = w[..., o:o + wd]
        parts.append(jnp.pad(blk, [(0, 0)] * (w.ndim - 1) + [(0, LORA_PAD - wd)]))
        o += wd
    return jnp.concatenate(parts, axis=-1)


def _rope_tables(l):
    half = DIFF_QK_DIM // 2
    inv_freq = ROPE_THETA ** (-jnp.arange(0, half, 2, dtype=F32) / half)
    t = jnp.arange(l, dtype=jnp.int32)
    rows = (t // GRID_W).astype(F32)[:, None] * inv_freq
    cols = (t % GRID_W).astype(F32)[:, None] * inv_freq
    cos64 = jnp.concatenate([jnp.cos(rows), jnp.cos(rows), jnp.cos(cols), jnp.cos(cols)], axis=1)
    sin64 = jnp.concatenate([-jnp.sin(rows), jnp.sin(rows), -jnp.sin(cols), jnp.sin(cols)], axis=1)
    return jnp.tile(cos64, (1, 2)), jnp.tile(sin64, (1, 2))


def _route(routed, n_tok):
    expert = routed[:, :TOP_K].astype(jnp.int32)
    gate = routed[:, TOP_K:2 * TOP_K]

    n_assign = n_tok * TOP_K
    assert n_assign % MOE_TILE == 0
    flat_e = expert.reshape(-1).astype(jnp.int32)
    ids = jnp.arange(n_assign, dtype=jnp.int32)
    sorted_e, order, sorted_w = lax.sort((flat_e, ids, gate.reshape(-1)), num_keys=1, is_stable=True)
    _, rank = lax.sort((order, ids), num_keys=1)
    ends = jnp.searchsorted(sorted_e, jnp.arange(N_EXPERTS, dtype=jnp.int32), side='right').astype(jnp.int32)
    starts = jnp.concatenate([jnp.zeros((1,), jnp.int32), ends[:-1]])

    nb = n_assign // MOE_TILE
    blk_lo = jnp.arange(nb, dtype=jnp.int32) * MOE_TILE
    e_first = jnp.searchsorted(ends, blk_lo, side='right').astype(jnp.int32)
    e_last = jnp.searchsorted(ends, blk_lo + MOE_TILE - 1, side='right').astype(jnp.int32)
    per_blk = e_last - e_first + 1
    cum = jnp.cumsum(per_blk)
    it = jnp.arange(nb + N_EXPERTS - 1, dtype=jnp.int32)
    wb = jnp.minimum(jnp.searchsorted(cum, it, side='right'), nb - 1).astype(jnp.int32)
    we = jnp.clip(e_first[wb] + it - (cum[wb] - per_blk[wb]), 0, N_EXPERTS - 1)
    lo = jnp.clip(starts[we], blk_lo[wb], blk_lo[wb] + MOE_TILE) - blk_lo[wb]
    hi = jnp.clip(ends[we], blk_lo[wb], blk_lo[wb] + MOE_TILE) - blk_lo[wb]
    hi = jnp.where(it < cum[-1], hi, lo)
    return order // TOP_K, sorted_w, rank.reshape(n_tok, TOP_K), (wb, we, lo, hi)


def kernel(x, c, ctx, c_ctx, ada_w, ada_b, norm1_g, norm2_g, w_in, shift_mu, rwkv_w0, rwkv_w2, rwkv_a0, rwkv_a2,
           rwkv_g2, rwkv_k_k, rwkv_k_a, rwkv_r_k, rwkv_lnx_g, rwkv_lnx_b, qn_g, kn_g, diff_lambda, subln_g,
           w_pa, w_pb, w_out, router_g_w, router_g_b, router_e_w, router_e_b, exp_w1, exp_w3, exp_w2):
    assert ada_w.shape[0] == 1, "single-layer block"
    b, l, d = x.shape
    lc = ctx.shape[1]
    lam_init = 0.8 - 0.6 * math.exp(-0.3 * 0)
    lv = diff_lambda[0].astype(F32)
    lam = jnp.exp(jnp.sum(lv[0] * lv[1])) - jnp.exp(jnp.sum(lv[2] * lv[3])) + lam_init

    rows = (b + 1 + SUBLANES - 1) // SUBLANES * SUBLANES
    cm = jnp.zeros((rows, d), F32).at[:b].set(c).at[b].set(c_ctx)
    mod = _modulation(cm, ada_w[0], ada_b[0])
    sh1, sc1, gt1, sh2, sc2, gt2 = [mod[:b, None, k * d:(k + 1) * d] for k in range(6)]
    csh1, csc1 = [jnp.broadcast_to(mod[b, k * d:(k + 1) * d], (b, 1, d)) for k in range(2)]

    w = w_in[0]
    lora_widths = (DECAY_LORA, DECAY_LORA, AAA_LORA, AAA_LORA)
    o_lora = 3 * RWKV_DIM
    o_glora = o_lora + sum(lora_widths)
    pad_cols = lambda m: jnp.concatenate(
        [m[..., :o_lora], _pad_lora_cols(m[..., o_lora:o_glora], lora_widths), m[..., o_glora:RWKV_COLS]], axis=-1)
    w_rwkv = pad_cols(w).astype(BF16)
    w_diff = w[:, RWKV_COLS:RWKV_COLS + DIFF_COLS].astype(BF16)
    w_gate = w[:, RWKV_COLS + DIFF_COLS:].astype(BF16)
    g1 = norm1_g[0]
    hx = _norm_mod(x, g1, sc1, sh1)
    hc = _norm_mod(ctx, g1, csc1, csh1)
    px_r, px_d, px_g = _proj(hx, w_rwkv, F32), _proj(hx, w_diff, BF16), _proj(hx, w_gate, BF16)
    pc_r, pc_d = _proj(hc, w_rwkv, F32), _proj(hc, w_diff, BF16)

    pad_rows = lambda m: jnp.pad(m, ((0, 0), (0, LORA_PAD - m.shape[1]), (0, 0)))
    rparams = (pad_cols(shift_mu[0])[None], rwkv_w0[0], pad_rows(rwkv_w2[0]), rwkv_a0[0], pad_rows(rwkv_a2[0]),
               rwkv_g2[0], rwkv_k_k[0][None], rwkv_k_a[0][None], rwkv_r_k[0].reshape(1, RWKV_DIM))
    y_sweep, bonus, gate = _rwkv_scans(px_r, pc_r, rparams)

    cos_t, sin_t = _rope_tables(l)
    qg = jnp.tile(qn_g[0], 2)[None]
    kg = jnp.tile(kn_g[0], 2)[None]
    q_x, k_x, v_x = _diff_prep(px_d, cos_t, sin_t, qg, kg, True)
    _, k_c, v_c = _diff_prep(pc_d, cos_t[:lc], sin_t[:lc], qg, kg, False)
    lam_vec = jnp.full((1, LANES), lam, F32)
    sg_vec = (subln_g[0] * (1.0 - lam_init))[None]
    score_bound = (1.05 * DIFF_QK_DIM * DIFF_SCALE * math.log2(math.e)
                   * jnp.max(jnp.abs(qn_g[0])) * jnp.max(jnp.abs(kn_g[0])))
    attn_args = (q_x, k_c, v_c, k_x, v_x, lam_vec, sg_vec)
    y_diff = lax.cond(score_bound <= SCORE_LOG2_LIMIT,
                      lambda a: _diff_attention(*a, bounded=True),
                      lambda a: _diff_attention(*a, bounded=False), attn_args)

    mixed = _merge(y_sweep, bonus, gate, y_diff, px_g, rwkv_lnx_g[0][None], rwkv_lnx_b[0][None],
                   w_pa[0].astype(BF16), w_pb[0].astype(BF16))
    n_r = N_GROUPS + N_EXPERTS
    w_router = jnp.zeros((d, ROUTER_PAD), F32).at[:, :N_GROUPS].set(router_g_w[0]).at[:, N_GROUPS:n_r].set(router_e_w[0])
    b_router = jnp.zeros((1, ROUTER_PAD), F32).at[0, :N_GROUPS].set(router_g_b[0]).at[0, N_GROUPS:n_r].set(router_e_b[0])
    w_router_hi = w_router.astype(BF16)
    w_router_lo = (w_router - w_router_hi.astype(F32)).astype(BF16)
    x_new, h2, routed = _outproj(mixed, x, gt1, norm2_g[0][None], sc2, sh2, w_out[0].astype(BF16),
                                 jnp.concatenate([w_router_hi, w_router_lo], axis=1), b_router)

    n_tok = b * l
    row_tok, row_w, row_of, items = _route(routed.reshape(n_tok, ROUTER_PAD), n_tok)
    xs = h2.reshape(n_tok, d)[row_tok]
    out = _moe_ffn(xs, row_w[:, None], items, exp_w1[0], exp_w3[0], exp_w2[0])
    y0 = out[row_of[:, 0]].reshape(b, l, d)
    y1 = out[row_of[:, 1]].reshape(b, l, d)
    return _final(x_new, gt2, y0, y1)
```

```python
import functools
import math

import jax
import jax.numpy as jnp
from jax import lax
from jax.experimental import pallas as pl
from jax.experimental.pallas import tpu as pltpu

F32 = jnp.float32
BF16 = jnp.bfloat16
HIGHEST = lax.Precision.HIGHEST

D_MODEL = 2048
GRID_W = 64
RWKV_HEADS = 16
RWKV_HEAD_DIM = 64
RWKV_DIM = RWKV_HEADS * RWKV_HEAD_DIM
DECAY_LORA = 96
AAA_LORA = 96
GATE_LORA = 256
RWKV_COLS = 3 * RWKV_DIM + 2 * DECAY_LORA + 2 * AAA_LORA + GATE_LORA
DIFF_HEADS = 8
DIFF_QK_DIM = 64
DIFF_V_DIM = 2 * DIFF_QK_DIM
DIFF_DIM = DIFF_HEADS * DIFF_V_DIM
DIFF_QK_COLS = DIFF_HEADS * 2 * DIFF_QK_DIM
DIFF_COLS = 2 * DIFF_QK_COLS + DIFF_DIM
DIFF_SCALE = DIFF_QK_DIM ** -0.5
ROPE_THETA = 10000.0
ROPE_SUB = DIFF_QK_DIM // 4
GATE_COLS = 2 * D_MODEL
N_GROUPS = 4
EXPERTS_PER_GROUP = 8
N_EXPERTS = N_GROUPS * EXPERTS_PER_GROUP
TOP_K = 2
D_EXPERT = 512
NORM_EPS = 1e-6
SUBLN_EPS = 1e-5
LNX_EPS = 64e-5

LANES = 128
SUBLANES = 8
VMEM_LIMIT_BYTES = 56 * 1024 * 1024

LORA_PAD = LANES
RWKV_PCOLS = 3 * RWKV_DIM + 4 * LORA_PAD + GATE_LORA
CHUNK = 64
V_EXT = 2 * DIFF_V_DIM
SCORE_LOG2_LIMIT = 60.0
ROUTER_PAD = LANES
MOE_TILE = 512


def _cparams(*sem):
    return pltpu.CompilerParams(dimension_semantics=sem, vmem_limit_bytes=VMEM_LIMIT_BYTES)


def _sigmoid(x):
    return 1.0 / (1.0 + jnp.exp(-x))


def _dot(a, b, dims=(((1,), (0,)), ((), ()))):
    return lax.dot_general(a.astype(BF16), b.astype(BF16), dims, preferred_element_type=F32)


def _dot_f32(a, b, dims=(((1,), (0,)), ((), ()))):
    return lax.dot_general(a, b, dims, precision=HIGHEST, preferred_element_type=F32)


_NT = (((1,), (1,)), ((), ()))
_TN = (((0,), (0,)), ((), ()))


def _bf16_parts(x, n):
    parts = []
    for _ in range(n):
        p = x.astype(BF16)
        parts.append(p)
        x = x - p.astype(F32)
    return parts


def _dot_split(x, w_b, n, lhs=True):
    parts = _bf16_parts(x, n)
    if n == 1:
        return jnp.dot(parts[0], w_b, preferred_element_type=F32) if lhs else jnp.dot(w_b, parts[0], preferred_element_type=F32)
    if lhs:
        return jnp.dot(jnp.concatenate(parts, axis=1), jnp.concatenate([w_b] * n, axis=0), preferred_element_type=F32)
    return jnp.dot(jnp.concatenate([w_b] * n, axis=1), jnp.concatenate(parts, axis=0), preferred_element_type=F32)


def _group_ones(width, group):
    r = lax.broadcasted_iota(jnp.int32, (width, width), 0) // group
    c = lax.broadcasted_iota(jnp.int32, (width, width), 1) // group
    return (r == c).astype(BF16)


def _group_sum(x, group, pieces=2):
    ones = _group_ones(LANES, group)
    parts = [_dot_split(x[:, j * LANES:(j + 1) * LANES], ones, pieces) for j in range(x.shape[1] // LANES)]
    return parts[0] if len(parts) == 1 else jnp.concatenate(parts, axis=1)


def _mod_kernel(c_ref, w_ref, b_ref, o_ref):
    c = c_ref[...]
    o_ref[...] = _dot_f32(c * _sigmoid(c), w_ref[...]) + b_ref[...]


def _modulation(cm, ada_w, ada_b):
    rows, d = cm.shape
    n = ada_w.shape[1]
    tn = _pick(n, (1536, 1024, 512, 256, 128))
    return pl.pallas_call(
        _mod_kernel,
        grid=(n // tn,),
        in_specs=[pl.BlockSpec((rows, d), lambda j: (0, 0)),
                  pl.BlockSpec((d, tn), lambda j: (0, j)),
                  pl.BlockSpec((1, tn), lambda j: (0, j))],
        out_specs=pl.BlockSpec((rows, tn), lambda j: (0, j)),
        out_shape=jax.ShapeDtypeStruct((rows, n), F32),
        compiler_params=_cparams("parallel"),
        name="modulation",
    )(cm, ada_w, ada_b.reshape(1, n))


def _norm_mod_kernel(x_ref, g_ref, sc_ref, sh_ref, h_ref):
    x = x_ref[0]
    y = x * lax.rsqrt(jnp.mean(x * x, axis=-1, keepdims=True) + NORM_EPS) * g_ref[...]
    h_ref[0] = (y * (1.0 + sc_ref[0]) + sh_ref[0]).astype(BF16)


def _pick(n, prefs):
    for t in prefs:
        if n % t == 0:
            return t
    return n


def _norm_mod(x, g, sc, sh):
    b, l, d = x.shape
    tm = _pick(l, (512, 256, 128))
    return pl.pallas_call(
        _norm_mod_kernel,
        grid=(b, l // tm),
        in_specs=[pl.BlockSpec((1, tm, d), lambda bi, i: (bi, i, 0)),
                  pl.BlockSpec((1, d), lambda bi, i: (0, 0)),
                  pl.BlockSpec((1, 1, d), lambda bi, i: (bi, 0, 0)),
                  pl.BlockSpec((1, 1, d), lambda bi, i: (bi, 0, 0))],
        out_specs=pl.BlockSpec((1, tm, d), lambda bi, i: (bi, i, 0)),
        out_shape=jax.ShapeDtypeStruct((b, l, d), BF16),
        compiler_params=_cparams("parallel", "parallel"),
        name="norm_mod",
    )(x, g.reshape(1, d), sc, sh)


def _proj_kernel(h_ref, w_ref, o_ref):
    o_ref[0] = jnp.dot(h_ref[0], w_ref[...], preferred_element_type=F32).astype(o_ref.dtype)


def _proj(h, w, out_dtype):
    b, l, d = h.shape
    n = w.shape[1]
    tm = _pick(l, (2048, 1024, 512, 256, 128))
    tn = _pick(n, (1024, 768, 512, 256, 128))
    return pl.pallas_call(
        _proj_kernel,
        grid=(b, l // tm, n // tn),
        in_specs=[pl.BlockSpec((1, tm, d), lambda bi, i, j: (bi, i, 0)),
                  pl.BlockSpec((d, tn), lambda bi, i, j: (0, j))],
        out_specs=pl.BlockSpec((1, tm, tn), lambda bi, i, j: (bi, i, j)),
        out_shape=jax.ShapeDtypeStruct((b, l, n), out_dtype),
        compiler_params=_cparams("parallel", "parallel", "parallel"),
        name="proj",
    )(h, w)


def _pair_diag(x_b, mask_b):
    return jnp.concatenate([x_b, x_b], axis=0) * mask_b


def _tri_inverse_pairs(a_list, eye_f, mask_b):
    n = CHUNK
    mm = lambda l, r: jnp.dot(l.astype(BF16), _pair_diag(r.astype(BF16), mask_b), preferred_element_type=F32)
    xs = [eye_f + a for a in a_list]
    ps = [mm(a, a) for a in a_list]
    steps = int(math.log2(n)) - 1
    for s in range(steps):
        if s < steps - 1:
            xps = [mm(jnp.concatenate([x, p], axis=0), p) for x, p in zip(xs, ps)]
            xs = [x + xp[:n] for x, xp in zip(xs, xps)]
            ps = [xp[n:] for xp in xps]
        else:
            xs = [x + mm(x, p) for x, p in zip(xs, ps)]
    return xs


def _rwkv_chunk_kernel(p_ref, pp_ref, pn_ref, mu_ref, w0_ref, w2_ref, a0_ref, a2_ref, g2_ref, kk_ref, ka_ref,
                       rk_ref, q_ref, y0_ref, m_ref, n_ref, bonus_ref, gate_ref):
    i = pl.program_id(1)
    last = pl.num_programs(1) - 1
    c = RWKV_DIM
    hd = RWKV_HEAD_DIM
    p = p_ref[0]
    t_rows = p.shape[0]
    n_sub = t_rows // CHUNK
    row = lax.broadcasted_iota(jnp.int32, (t_rows, 1), 0)
    prev_row = jnp.where(i == 0, 0.0, pp_ref[0, SUBLANES - 1:SUBLANES, :])
    next_row = jnp.where(i == last, 0.0, pn_ref[0, 0:1, :])
    prev = jnp.where(row == 0, prev_row, pltpu.roll(p, 1, axis=0))
    nxt = jnp.where(row == t_rows - 1, next_row, pltpu.roll(p, t_rows - 1, axis=0))
    ps = p + mu_ref[...] * (0.5 * (prev + nxt) - p)

    r, k, v = ps[:, :c], ps[:, c:2 * c], ps[:, 2 * c:3 * c]
    o = 3 * c
    xw = (ps[:, o:o + LORA_PAD], ps[:, o + LORA_PAD:o + 2 * LORA_PAD])
    xa = (ps[:, o + 2 * LORA_PAD:o + 3 * LORA_PAD], ps[:, o + 3 * LORA_PAD:o + 4 * LORA_PAD])
    xg = ps[:, o + 4 * LORA_PAD:]

    gate_ref[0] = _dot(_sigmoid(xg), g2_ref[...]).astype(gate_ref.dtype)
    kk = k * kk_ref[...]
    kk = kk * lax.rsqrt(_group_sum(kk * kk, hd, pieces=1) + 1e-12)

    tr = lax.broadcasted_iota(jnp.int32, (t_rows, t_rows), 0)
    tc = lax.broadcasted_iota(jnp.int32, (t_rows, t_rows), 1)
    same_chunk = (tr // CHUNK) == (tc // CHUNK)
    tr2 = lax.broadcasted_iota(jnp.int32, (CHUNK, LANES), 0)
    lane2 = lax.broadcasted_iota(jnp.int32, (CHUNK, LANES), 1)
    tc2 = lane2 % CHUNK
    lane_head = lane2 // hd
    eye2 = tr2 == tc2
    mask_b = _group_ones(LANES, hd)
    kd_sum = jnp.zeros_like(k)
    v_b = v.astype(BF16)
    psl = [slice(p * LANES, (p + 1) * LANES) for p in range(c // LANES)]
    before2, upto2, prep = [], [], []
    for d in range(2):
        before2.append((tc2 < tr2) if d == 0 else (tc2 > tr2))
        upto = same_chunk & ((tc <= tr) if d == 0 else (tc >= tr))
        upto2.append((tc2 <= tr2) if d == 0 else (tc2 >= tr2))
        z = w0_ref[d:d + 1, :] + _dot(jnp.tanh(xw[d]), w2_ref[d])
        w_log = -(jnp.maximum(-z, 0.0) + jnp.log(1.0 + jnp.exp(-jnp.abs(z)))) - 0.5
        logw = -jnp.exp(w_log)
        a = _sigmoid(a0_ref[d:d + 1, :] + _dot(xa[d], a2_ref[d]))
        kd = k * (1.0 + (a - 1.0) * ka_ref[...])
        kd_sum = kd_sum + kd
        cum = _dot_split(logw, upto.astype(BF16), 3, lhs=False)
        last_row = [u * CHUNK + (CHUNK - 1 if d == 0 else 0) for u in range(n_sub)]
        totals = [cum[t:t + 1, :] for t in last_row]
        total = totals[0]
        for u in range(1, n_sub):
            total = jnp.where(row >= u * CHUNK, totals[u], total)
        e_neg = jnp.exp(-cum)
        e_rest = jnp.exp(total - cum)
        p_total = [jnp.exp(t) for t in totals]
        beta = kk * a
        al = -kk * jnp.exp(cum - logw)
        rt = r * jnp.exp(cum)
        al_b = al.astype(BF16)
        rt_b = rt.astype(BF16)
        bt_b = (beta * e_neg).astype(BF16)
        kt_b = (kd * e_neg).astype(BF16)
        bh_b = (beta * e_rest).astype(BF16)
        kh_b = (kd * e_rest).astype(BF16)
        prep.append((al_b, rt_b, bt_b, kt_b, bh_b, kh_b, rt, p_total))

    rsl = [slice(u * CHUNK, (u + 1) * CHUNK) for u in range(n_sub)]
    dp = [(d, rs, s) for rs in rsl for d in range(2) for s in psl]
    fdot = lambda a, b: jnp.dot(a, b, preferred_element_type=F32)
    bd = lambda x: _pair_diag(x, mask_b)
    v_bd = [bd(v_b[rs, s]) for d, rs, s in dp]
    zeros_sq = jnp.zeros((LANES, LANES), BF16)
    zeros_tl = jnp.zeros((CHUNK, LANES), BF16)
    gm = [lax.dot_general(jnp.concatenate([prep[d][0][rs, s], prep[d][1][rs, s]], axis=0),
                          jnp.concatenate([bd(prep[d][2][rs, s]), bd(prep[d][3][rs, s])], axis=0),
                          _NT, preferred_element_type=F32) for d, rs, s in dp]
    a_ab = [jnp.where(before2[d], x[:CHUNK, :LANES], 0.0) for x, (d, rs, s) in zip(gm, dp)]
    a_ak = [jnp.where(before2[d], x[:CHUNK, LANES:], 0.0).astype(BF16) for x, (d, rs, s) in zip(gm, dp)]
    a_rb = [jnp.where(upto2[d], x[CHUNK:, :LANES], 0.0).astype(BF16) for x, (d, rs, s) in zip(gm, dp)]
    a_rk = [jnp.where(upto2[d], x[CHUNK:, LANES:], 0.0).astype(BF16) for x, (d, rs, s) in zip(gm, dp)]
    akv = [fdot(ak, vd) for ak, vd in zip(a_ak, v_bd)]
    t_inv = _tri_inverse_pairs(a_ab, eye2.astype(F32), mask_b)
    wu = [fdot(t.astype(BF16), jnp.concatenate([bd(prep[d][0][rs, s]), bd(u.astype(BF16))], axis=1))
          for t, u, (d, rs, s) in zip(t_inv, akv, dp)]
    w_b = [x[:, :LANES].astype(BF16) for x in wu]
    u_b = [x[:, LANES:].astype(BF16) for x in wu]
    qy = [fdot(jnp.concatenate([rb, rk], axis=1),
               jnp.concatenate([jnp.concatenate([bd(w), bd(u)], axis=1),
                                jnp.concatenate([zeros_sq, vd], axis=1)], axis=0))
          for rb, rk, w, u, vd in zip(a_rb, a_rk, w_b, u_b, v_bd)]
    full = [lax.dot_general(jnp.concatenate([jnp.concatenate([w, u], axis=1),
                                             jnp.concatenate([zeros_tl, v_b[rs, s]], axis=1)], axis=0),
                            jnp.concatenate([prep[d][4][rs, s], prep[d][5][rs, s]], axis=0),
                            _TN, preferred_element_type=F32)
            for w, u, (d, rs, s) in zip(w_b, u_b, dp)]

    def diag_blocks(x):
        return jnp.where(lane_head == 0, x[:hd], 0.0) + jnp.where(lane_head == 1, x[hd:2 * hd], 0.0)

    npair = len(psl)
    for u, rs in enumerate(rsl):
        for d in range(2):
            sl = slice((2 * u + d) * npair, (2 * u + d + 1) * npair)
            rt, p_total = prep[d][6][rs], prep[d][7][u]
            q_ref[0, d, rs, :] = (rt + jnp.concatenate([x[:, :LANES] for x in qy[sl]], axis=1)).astype(q_ref.dtype)
            y0_ref[0, d, rs, :] = jnp.concatenate([x[:, LANES:] for x in qy[sl]], axis=1).astype(y0_ref.dtype)
            m_ref[0, d, u] = jnp.concatenate([jnp.where(eye2, p_total[:, s], 0.0) + diag_blocks(x[:LANES])
                                              for x, s in zip(full[sl], psl)], axis=1).astype(m_ref.dtype)
            n_ref[0, d, u] = jnp.concatenate([diag_blocks(x[LANES:]) for x in full[sl]], axis=1)
    bonus_ref[0] = (_group_sum(r * kd_sum * rk_ref[...], hd) * v).astype(bonus_ref.dtype)


def _rwkv_chunk_ops(p, params):
    b, l, pc = p.shape
    nc = l // CHUNK
    c = RWKV_DIM
    g = _pick(nc, (2, 1))
    t = g * CHUNK
    hb = t // SUBLANES
    nb8 = l // SUBLANES
    mu, w0, w2, a0, a2, g2, k_k, k_a, r_k = params
    const = lambda shape: pl.BlockSpec(shape, lambda bi, i: (0,) * len(shape))
    tok = lambda: pl.BlockSpec((1, t, c), lambda bi, i: (bi, i, 0))
    return pl.pallas_call(
        _rwkv_chunk_kernel,
        grid=(b, nc // g),
        in_specs=[pl.BlockSpec((1, t, pc), lambda bi, i: (bi, i, 0)),
                  pl.BlockSpec((1, SUBLANES, pc), lambda bi, i: (bi, jnp.maximum(i * hb - 1, 0), 0)),
                  pl.BlockSpec((1, SUBLANES, pc), lambda bi, i: (bi, jnp.minimum((i + 1) * hb, nb8 - 1), 0)),
                  const((1, pc)), const((2, c)), const((2, LORA_PAD, c)), const((2, c)), const((2, LORA_PAD, c)),
                  const((GATE_LORA, c)), const((1, c)), const((1, c)), const((1, c))],
        out_specs=[pl.BlockSpec((1, 2, t, c), lambda bi, i: (bi, 0, i, 0)),
                   pl.BlockSpec((1, 2, t, c), lambda bi, i: (bi, 0, i, 0)),
                   pl.BlockSpec((1, 2, g, RWKV_HEAD_DIM, c), lambda bi, i: (bi, 0, i, 0, 0)),
                   pl.BlockSpec((1, 2, g, RWKV_HEAD_DIM, c), lambda bi, i: (bi, 0, i, 0, 0)),
                   tok(), tok()],
        out_shape=[jax.ShapeDtypeStruct((b, 2, l, c), BF16), jax.ShapeDtypeStruct((b, 2, l, c), BF16),
                   jax.ShapeDtypeStruct((b, 2, nc, RWKV_HEAD_DIM, c), BF16),
                   jax.ShapeDtypeStruct((b, 2, nc, RWKV_HEAD_DIM, c), F32),
                   jax.ShapeDtypeStruct((b, l, c), BF16), jax.ShapeDtypeStruct((b, l, c), BF16)],
        compiler_params=_cparams("parallel", "parallel"),
        name="rwkv_chunk_ops",
    )(p, p, p, mu, w0, w2, a0, a2, g2, k_k, k_a, r_k)


def _rwkv_sweep_kernel(qf_ref, y0f_ref, mf_ref, nf_ref, qb_ref, y0b_ref, mb_ref, nb_ref, s0_ref,
                       yf_ref, yb_ref, sfin_ref, s_ref):
    j = pl.program_id(1)
    hd = RWKV_HEAD_DIM

    @pl.when(j == 0)
    def _():
        s_ref[...] = s0_ref[0]

    dirs = ((qf_ref, y0f_ref, mf_ref, nf_ref, yf_ref), (qb_ref, y0b_ref, mb_ref, nb_ref, yb_ref))
    psl = [slice(p * LANES, (p + 1) * LANES) for p in range(RWKV_DIM // LANES)]
    mask_b = _group_ones(LANES, hd)
    pair_diag = lambda x: _pair_diag(x, mask_b)
    per_step = mf_ref.shape[2]
    state = [s_ref[0], s_ref[1]]
    for u in range(per_step):
        for d, (q_ref, y0_ref, m_ref, n_ref, y_ref) in enumerate(dirs):
            cu = u if d == 0 else per_step - 1 - u
            rows = slice(cu * CHUNK, (cu + 1) * CHUNK)
            s_b = state[d].astype(BF16)
            q_b = q_ref[0, 0, rows, :]
            m_b = m_ref[0, 0, cu]
            ys = [_dot(q_b[:, s], pair_diag(s_b[:, s]), _NT) for s in psl]
            sm = [_dot(s_b[:, s], pair_diag(m_b[:, s])) for s in psl]
            y_ref[0, rows, :] = (y0_ref[0, 0, rows, :].astype(F32) + jnp.concatenate(ys, axis=1)).astype(y_ref.dtype)
            state[d] = n_ref[0, 0, cu] + jnp.concatenate(sm, axis=1)
    s_ref[0] = state[0]
    s_ref[1] = state[1]

    @pl.when(j == pl.num_programs(1) - 1)
    def _():
        sfin_ref[0] = s_ref[...]


def _rwkv_sweep(q, y0, m, n, s0):
    b, _, l, c = q.shape
    nc = l // CHUNK
    hd = RWKV_HEAD_DIM
    g = _pick(nc, (8, 4, 2, 1))
    ns = nc // g
    tokf = lambda: pl.BlockSpec((1, 1, g * CHUNK, c), lambda bi, j: (bi, 0, j, 0))
    tokb = lambda: pl.BlockSpec((1, 1, g * CHUNK, c), lambda bi, j: (bi, 1, ns - 1 - j, 0))
    opf = lambda: pl.BlockSpec((1, 1, g, hd, c), lambda bi, j: (bi, 0, j, 0, 0))
    opb = lambda: pl.BlockSpec((1, 1, g, hd, c), lambda bi, j: (bi, 1, ns - 1 - j, 0, 0))
    return pl.pallas_call(
        _rwkv_sweep_kernel,
        grid=(b, ns),
        in_specs=[tokf(), tokf(), opf(), opf(), tokb(), tokb(), opb(), opb(),
                  pl.BlockSpec((1, 2, hd, c), lambda bi, j: (bi, 0, 0, 0))],
        out_specs=[pl.BlockSpec((1, g * CHUNK, c), lambda bi, j: (bi, j, 0)),
                   pl.BlockSpec((1, g * CHUNK, c), lambda bi, j: (bi, ns - 1 - j, 0)),
                   pl.BlockSpec((1, 2, hd, c), lambda bi, j: (bi, 0, 0, 0))],
        out_shape=[jax.ShapeDtypeStruct((b, l, c), BF16), jax.ShapeDtypeStruct((b, l, c), BF16),
                   jax.ShapeDtypeStruct((b, 2, hd, c), F32)],
        scratch_shapes=[pltpu.VMEM((2, hd, c), F32)],
        compiler_params=_cparams("parallel", "arbitrary"),
        name="rwkv_sweep",
    )(q, y0, m, n, q, y0, m, n, s0)


def _rwkv_scans(px_r, pc_r, rparams):
    b = px_r.shape[0]
    qc, y0c, mc, nc_, _, _ = _rwkv_chunk_ops(pc_r, rparams)
    s_zero = jnp.zeros((b, 2, RWKV_HEAD_DIM, RWKV_DIM), F32)
    _, _, s_ctx = _rwkv_sweep(qc, y0c, mc, nc_, s_zero)
    qx, y0x, mx, nx, bonus, gate = _rwkv_chunk_ops(px_r, rparams)
    yf, yb, _ = _rwkv_sweep(qx, y0x, mx, nx, s_ctx)
    return (yf, yb), bonus, gate


def _diff_prep_kernel(p_ref, cos_ref, sin_ref, qg_ref, kg_ref, q_ref, k_ref, v_ref, *, rope):
    lane = lax.broadcasted_iota(jnp.int32, (1, LANES), 1)
    first = (lane % (2 * ROPE_SUB)) < ROPE_SUB
    for hd in range(DIFF_HEADS):
        for off, g_ref, o_ref, scale in ((0, qg_ref, q_ref, DIFF_SCALE * math.log2(math.e)),
                                         (DIFF_QK_COLS, kg_ref, k_ref, 1.0)):
            cs = slice(hd * LANES, (hd + 1) * LANES)
            xb = p_ref[0, :, off + hd * LANES:off + (hd + 1) * LANES].astype(F32)
            ms = _group_sum(xb * xb, DIFF_QK_DIM, pieces=1) * (1.0 / DIFF_QK_DIM)
            y = xb * lax.rsqrt(ms + NORM_EPS) * g_ref[...]
            if rope:
                swapped = jnp.where(first, pltpu.roll(y, LANES - ROPE_SUB, axis=1), pltpu.roll(y, ROPE_SUB, axis=1))
                y = y * cos_ref[...] + swapped * sin_ref[...]
            o_ref[0, :, cs] = (y * scale).astype(BF16)
    ones = jnp.ones((p_ref.shape[1], V_EXT - DIFF_V_DIM), BF16)
    for hd in range(DIFF_HEADS):
        vb = p_ref[0, :, 2 * DIFF_QK_COLS + hd * DIFF_V_DIM:2 * DIFF_QK_COLS + (hd + 1) * DIFF_V_DIM]
        v_ref[0, :, hd * V_EXT:(hd + 1) * V_EXT] = jnp.concatenate([vb.astype(BF16), ones], axis=1)


def _diff_prep(p, cos_t, sin_t, qg, kg, rope):
    b, l, pc = p.shape
    t = _pick(l, (512, 256, 128))
    tok = lambda w=DIFF_DIM: pl.BlockSpec((1, t, w), lambda bi, i: (bi, i, 0))
    shp = jax.ShapeDtypeStruct((b, l, DIFF_DIM), BF16)
    shp_v = jax.ShapeDtypeStruct((b, l, DIFF_HEADS * V_EXT), BF16)
    return pl.pallas_call(
        functools.partial(_diff_prep_kernel, rope=rope),
        grid=(b, l // t),
        in_specs=[pl.BlockSpec((1, t, pc), lambda bi, i: (bi, i, 0)),
                  pl.BlockSpec((t, LANES), lambda bi, i: (i, 0)),
                  pl.BlockSpec((t, LANES), lambda bi, i: (i, 0)),
                  pl.BlockSpec((1, LANES), lambda bi, i: (0, 0)),
                  pl.BlockSpec((1, LANES), lambda bi, i: (0, 0))],
        out_specs=[tok(), tok(), tok(DIFF_HEADS * V_EXT)],
        out_shape=[shp, shp, shp_v],
        compiler_params=_cparams("parallel", "parallel"),
        name="diff_prep_rope" if rope else "diff_prep",
    )(p, cos_t, sin_t, qg, kg)


def _diff_finish(acc1, acc2, lam_ref, sg_ref, o_ref):
    dv = DIFF_V_DIM
    o = acc1[:, :dv] / acc1[:, dv:] - lam_ref[...] * (acc2[:, :dv] / acc2[:, dv:])
    o = o * lax.rsqrt(jnp.mean(o * o, axis=-1, keepdims=True) + SUBLN_EPS)
    o_ref[0] = (o * sg_ref[...]).astype(o_ref.dtype)


def _flash_online_kernel(lam_ref, sg_ref, q_ref, kc_ref, vc_ref, k_ref, v_ref, o_ref, *, tk):
    qd = DIFF_QK_DIM
    tq = q_ref.shape[1]
    q = q_ref[0]
    qs = (q[:, :qd], q[:, qd:])

    def absorb(state, k, v):
        reps = k.shape[0] // LANES
        out = []
        for mp in range(2):
            m_prev, acc = state[mp]
            s = lax.dot_general(qs[mp], k[:, mp * qd:(mp + 1) * qd], _NT, preferred_element_type=F32)
            m_new = jnp.maximum(m_prev, jnp.max(s, axis=-1, keepdims=True))
            alpha = jnp.exp2(m_prev - m_new)
            pr = jnp.exp2(s - jnp.concatenate([m_new] * reps, axis=1))
            acc_new = jnp.concatenate([alpha, alpha], axis=1) * acc + jnp.dot(pr.astype(BF16), v, preferred_element_type=F32)
            out.append((m_new, acc_new))
        return tuple(out)

    init = (jnp.full((tq, LANES), -jnp.inf, F32), jnp.zeros((tq, V_EXT), F32))
    state = absorb((init, init), kc_ref[0], vc_ref[0])

    def body(j, state):
        rows = pl.ds(pl.multiple_of(j * tk, tk), tk)
        return absorb(state, k_ref[0, rows, :], v_ref[0, rows, :])

    (_, acc1), (_, acc2) = lax.fori_loop(0, k_ref.shape[1] // tk, body, state)
    _diff_finish(acc1, acc2, lam_ref, sg_ref, o_ref)


def _flash_bounded_kernel(lam_ref, sg_ref, q_ref, kc_ref, vc_ref, k_ref, v_ref, o_ref, acc_ref, p_ref, *, tk):
    qd = DIFF_QK_DIM
    q = q_ref[0]
    qs = (q[:, :qd], q[:, qd:])
    n_kv = k_ref.shape[1] // tk

    def weights(mp, k):
        s = lax.dot_general(qs[mp], k[:, mp * qd:(mp + 1) * qd], _NT, preferred_element_type=F32)
        return jnp.exp2(s).astype(BF16)

    def chunk(c):
        return pl.ds(pl.multiple_of(jnp.minimum(c, n_kv - 1) * tk, tk), tk)

    for mp in range(2):
        acc_ref[mp] = jnp.dot(weights(mp, kc_ref[0]), vc_ref[0], preferred_element_type=F32)
        p_ref[0, mp] = weights(mp, k_ref[0, pl.ds(0, tk), :])

    steps = _pick(n_kv, (16, 8, 4, 2))

    def body(j, carry):
        for u in range(steps):
            c = steps * j + u
            v = v_ref[0, chunk(c), :]
            k_next = k_ref[0, chunk(c + 1), :]
            for mp in range(2):
                acc_ref[mp] += jnp.dot(p_ref[u % 2, mp], v, preferred_element_type=F32)
                p_ref[1 - u % 2, mp] = weights(mp, k_next)
        return carry

    lax.fori_loop(0, n_kv // steps, body, 0)
    _diff_finish(acc_ref[0], acc_ref[1], lam_ref, sg_ref, o_ref)


def _diff_attention(q, kc, vc, k, v, lam_vec, sg_vec, bounded):
    b, l, _ = q.shape
    lc = kc.shape[1]
    tq = _pick(l, (256, 128))
    tk = _pick(l, (512, 256, 128))
    if bounded:
        tq = _pick(l, (1024, 512, 256, 128))
        tk = _pick(l // 2, (512, 256, 128))
        assert l % (2 * tk) == 0
        body = functools.partial(_flash_bounded_kernel, tk=tk)
        scratch = [pltpu.VMEM((2, tq, V_EXT), F32), pltpu.VMEM((2, 2, tq, tk), BF16)]
    else:
        body = functools.partial(_flash_online_kernel, tk=tk)
        scratch = []
    return pl.pallas_call(
        body,
        grid=(b, DIFF_HEADS, l // tq),
        in_specs=[pl.BlockSpec((1, LANES), lambda bi, h, i: (0, 0)),
                  pl.BlockSpec((1, LANES), lambda bi, h, i: (0, 0)),
                  pl.BlockSpec((1, tq, LANES), lambda bi, h, i: (bi, i, h)),
                  pl.BlockSpec((1, lc, LANES), lambda bi, h, i: (bi, 0, h)),
                  pl.BlockSpec((1, lc, V_EXT), lambda bi, h, i: (bi, 0, h)),
                  pl.BlockSpec((1, l, LANES), lambda bi, h, i: (bi, 0, h)),
                  pl.BlockSpec((1, l, V_EXT), lambda bi, h, i: (bi, 0, h))],
        out_specs=pl.BlockSpec((1, tq, LANES), lambda bi, h, i: (bi, i, h)),
        out_shape=jax.ShapeDtypeStruct((b, l, DIFF_DIM), BF16),
        scratch_shapes=scratch,
        compiler_params=_cparams("parallel", "parallel", "arbitrary"),
        name="diff_flash_bounded" if bounded else "diff_flash_online",
    )(lam_vec, sg_vec, q, kc, vc, k, v)


def _merge_kernel(yf_ref, yb_ref, bonus_ref, gate_ref, yd_ref, pg_ref, lg_ref, lb_ref, wpa_ref, wpb_ref, o_ref):
    hd = RWKV_HEAD_DIM
    y = yf_ref[0].astype(F32) + yb_ref[0].astype(F32)
    dev = y - _group_sum(y, hd) * (1.0 / hd)
    var = _group_sum(dev * dev, hd, pieces=1) * (1.0 / hd)
    yn = dev * lax.rsqrt(var + LNX_EPS) * lg_ref[...] + lb_ref[...]
    y_rwkv = (yn + bonus_ref[0].astype(F32)) * gate_ref[0].astype(F32)
    a = _dot(y_rwkv, wpa_ref[...])
    bb = jnp.dot(yd_ref[0], wpb_ref[...], preferred_element_type=F32)
    ga = _sigmoid(pg_ref[0, :, :D_MODEL].astype(F32))
    gb = _sigmoid(pg_ref[0, :, D_MODEL:].astype(F32))
    o_ref[0] = (ga * a + gb * bb).astype(BF16)


def _merge(y_sweep, bonus, gate, y_diff, p_gate, lnx_g, lnx_b, w_pa, w_pb):
    b, l, c = bonus.shape
    d = D_MODEL
    tm = _pick(l, (256, 128))
    const = lambda shape: pl.BlockSpec(shape, lambda bi, i: (0,) * len(shape))
    tok = lambda w: pl.BlockSpec((1, tm, w), lambda bi, i: (bi, i, 0))
    return pl.pallas_call(
        _merge_kernel,
        grid=(b, l // tm),
        in_specs=[tok(c), tok(c), tok(c), tok(c), tok(DIFF_DIM), tok(GATE_COLS),
                  const((1, c)), const((1, c)), const((c, d)), const((DIFF_DIM, d))],
        out_specs=tok(d),
        out_shape=jax.ShapeDtypeStruct((b, l, d), BF16),
        compiler_params=_cparams("parallel", "parallel"),
        name="merge",
    )(y_sweep[0], y_sweep[1], bonus, gate, y_diff, p_gate, lnx_g, lnx_b, w_pa, w_pb)


def _outproj_kernel(mx_ref, x_ref, gt_ref, g_ref, sc_ref, sh_ref, wo_ref, wr_ref, br_ref, xn_ref, h_ref, rt_ref):
    mix = jnp.dot(mx_ref[0], wo_ref[...], preferred_element_type=F32)
    xn = x_ref[0] + gt_ref[0] * mix
    xn_ref[0] = xn
    y = xn * lax.rsqrt(jnp.mean(xn * xn, axis=-1, keepdims=True) + NORM_EPS) * g_ref[...]
    h = y * (1.0 + sc_ref[0]) + sh_ref[0]
    h_hi, h_lo = _bf16_parts(h, 2)
    h_ref[0] = h_hi
    dot = lambda a, b: jnp.dot(a, b, preferred_element_type=F32)
    hh = dot(h_hi, wr_ref[...])
    lg = hh[:, :ROUTER_PAD] + (dot(h_lo, wr_ref[:, :ROUTER_PAD]) + hh[:, ROUTER_PAD:]) + br_ref[...]
    rt_ref[0] = _route_rows(lg)


def _route_rows(lg):
    lane = lax.broadcasted_iota(jnp.int32, lg.shape, 1).astype(F32)
    neg = jnp.float32(-3.0e38)
    far = jnp.float32(LANES)
    gl = jnp.where(lane < N_GROUPS, lg, neg)
    g_max = jnp.max(gl, axis=-1, keepdims=True)
    g_top = jnp.min(jnp.where(gl == g_max, lane, far), axis=-1, keepdims=True)
    p_g = 1.0 / jnp.sum(jnp.where(lane < N_GROUPS, jnp.exp(gl - g_max), 0.0), axis=-1, keepdims=True)
    e_lo = N_GROUPS + EXPERTS_PER_GROUP * g_top
    el = jnp.where((lane >= e_lo) & (lane < e_lo + EXPERTS_PER_GROUP), lg, neg)
    m1 = jnp.max(el, axis=-1, keepdims=True)
    i1 = jnp.min(jnp.where(el == m1, lane, far), axis=-1, keepdims=True)
    el2 = jnp.where(lane == i1, neg, el)
    m2 = jnp.max(el2, axis=-1, keepdims=True)
    i2 = jnp.min(jnp.where(el2 == m2, lane, far), axis=-1, keepdims=True)
    e2 = jnp.exp(m2 - m1)
    gate1 = p_g / (1.0 + e2)
    gate2 = p_g * e2 / (1.0 + e2)
    return jnp.where(lane == 0, i1 - N_GROUPS,
                     jnp.where(lane == 1, i2 - N_GROUPS, jnp.where(lane == 2, gate1, jnp.where(lane == 3, gate2, 0.0))))


def _outproj(mixed, x, gt1, g2, sc2, sh2, w_out, w_router, b_router):
    b, l, d = x.shape
    tm = _pick(l, (512, 256, 128))
    const = lambda shape: pl.BlockSpec(shape, lambda bi, i: (0,) * len(shape))
    tok = lambda w: pl.BlockSpec((1, tm, w), lambda bi, i: (bi, i, 0))
    per_b = lambda: pl.BlockSpec((1, 1, d), lambda bi, i: (bi, 0, 0))
    return pl.pallas_call(
        _outproj_kernel,
        grid=(b, l // tm),
        in_specs=[tok(d), tok(d), per_b(), const((1, d)), per_b(), per_b(),
                  const((d, d)), const((d, 2 * ROUTER_PAD)), const((1, ROUTER_PAD))],
        out_specs=[tok(d), tok(d), tok(ROUTER_PAD)],
        out_shape=[jax.ShapeDtypeStruct((b, l, d), F32), jax.ShapeDtypeStruct((b, l, d), BF16),
                   jax.ShapeDtypeStruct((b, l, ROUTER_PAD), F32)],
        compiler_params=_cparams("parallel", "parallel"),
        name="outproj_router",
    )(mixed, x, gt1, g2, sc2, sh2, w_out, w_router, b_router)


def _moe_kernel(wb_ref, we_ref, lo_ref, hi_ref, x_ref, sw_ref, w1_ref, w3_ref, w2_ref, o_ref,
                w1b_ref, w3b_ref, w2b_ref, cached_ref):
    i = pl.program_id(0)
    lo, hi = lo_ref[i], hi_ref[i]
    live = hi > lo

    @pl.when(i == 0)
    def _():
        cached_ref[0] = -1

    @pl.when((i == 0) | (wb_ref[i] != wb_ref[jnp.maximum(i - 1, 0)]))
    def _():
        o_ref[...] = jnp.zeros(o_ref.shape, o_ref.dtype)

    @pl.when(live & (cached_ref[0] != we_ref[i]))
    def _():
        w1b_ref[...] = w1_ref[0].astype(BF16)
        w3b_ref[...] = w3_ref[0].astype(BF16)
        w2b_ref[...] = w2_ref[0].astype(BF16)
        cached_ref[0] = we_ref[i]

    @pl.when(live)
    def _():
        xb = x_ref[...]
        u = jnp.dot(xb, w1b_ref[...], preferred_element_type=F32)
        g = jnp.dot(xb, w3b_ref[...], preferred_element_type=F32)
        hmid = (u * _sigmoid(u) * g).astype(BF16)
        res = (jnp.dot(hmid, w2b_ref[...], preferred_element_type=F32) * sw_ref[...]).astype(o_ref.dtype)
        row = lax.broadcasted_iota(jnp.int32, (o_ref.shape[0], 1), 0)
        o_ref[...] = jnp.where((row >= lo) & (row < hi), res, o_ref[...])


def _moe_ffn(xs, sw, items, w1, w3, w2):
    n_rows, d = xs.shape
    wb, we, lo, hi = items
    grid_spec = pltpu.PrefetchScalarGridSpec(
        num_scalar_prefetch=4,
        grid=(wb.shape[0],),
        in_specs=[pl.BlockSpec((MOE_TILE, d), lambda i, wb, we, lo, hi: (wb[i], 0)),
                  pl.BlockSpec((MOE_TILE, 1), lambda i, wb, we, lo, hi: (wb[i], 0)),
                  pl.BlockSpec((1, d, D_EXPERT), lambda i, wb, we, lo, hi: (we[i], 0, 0)),
                  pl.BlockSpec((1, d, D_EXPERT), lambda i, wb, we, lo, hi: (we[i], 0, 0)),
                  pl.BlockSpec((1, D_EXPERT, d), lambda i, wb, we, lo, hi: (we[i], 0, 0))],
        out_specs=pl.BlockSpec((MOE_TILE, d), lambda i, wb, we, lo, hi: (wb[i], 0)),
        scratch_shapes=[pltpu.VMEM((d, D_EXPERT), BF16), pltpu.VMEM((d, D_EXPERT), BF16),
                        pltpu.VMEM((D_EXPERT, d), BF16), pltpu.SMEM((1,), jnp.int32)],
    )
    return pl.pallas_call(
        _moe_kernel,
        grid_spec=grid_spec,
        out_shape=jax.ShapeDtypeStruct((n_rows, d), BF16),
        compiler_params=_cparams("arbitrary"),
        name="moe_ffn",
    )(wb, we, lo, hi, xs, sw, w1, w3, w2)


def _final_kernel(x_ref, gt_ref, y0_ref, y1_ref, o_ref):
    o_ref[0] = x_ref[0] + gt_ref[0] * (y0_ref[0].astype(F32) + y1_ref[0].astype(F32))


def _final(x_new, gt2, y0, y1):
    b, l, d = x_new.shape
    tm = _pick(l, (512, 256, 128))
    tok = lambda: pl.BlockSpec((1, tm, d), lambda bi, i: (bi, i, 0))
    return pl.pallas_call(
        _final_kernel,
        grid=(b, l // tm),
        in_specs=[tok(), pl.BlockSpec((1, 1, d), lambda bi, i: (bi, 0, 0)), tok(), tok()],
        out_specs=tok(),
        out_shape=jax.ShapeDtypeStruct((b, l, d), F32),
        compiler_params=_cparams("parallel", "parallel"),
        name="moe_residual",
    )(x_new, gt2, y0, y1)


def _pad_lora_cols(w, widths):
    parts, o = [], 0
    for wd in widths:
        blk = w[..., o:o + wd]
        parts.append(jnp.pad(blk, [(0, 0)] * (w.ndim - 1) + [(0, LORA_PAD - wd)]))
        o += wd
    return jnp.concatenate(parts, axis=-1)


def _rope_tables(l):
    half = DIFF_QK_DIM // 2
    inv_freq = ROPE_THETA ** (-jnp.arange(0, half, 2, dtype=F32) / half)
    t = jnp.arange(l, dtype=jnp.int32)
    rows = (t // GRID_W).astype(F32)[:, None] * inv_freq
    cols = (t % GRID_W).astype(F32)[:, None] * inv_freq
    cos64 = jnp.concatenate([jnp.cos(rows), jnp.cos(rows), jnp.cos(cols), jnp.cos(cols)], axis=1)
    sin64 = jnp.concatenate([-jnp.sin(rows), jnp.sin(rows), -jnp.sin(cols), jnp.sin(cols)], axis=1)
    return jnp.tile(cos64, (1, 2)), jnp.tile(sin64, (1, 2))


def _route(routed, n_tok):
    expert = routed[:, :TOP_K].astype(jnp.int32)
    gate = routed[:, TOP_K:2 * TOP_K]

    n_assign = n_tok * TOP_K
    assert n_assign % MOE_TILE == 0
    flat_e = expert.reshape(-1).astype(jnp.int32)
    ids = jnp.arange(n_assign, dtype=jnp.int32)
    sorted_e, order, sorted_w = lax.sort((flat_e, ids, gate.reshape(-1)), num_keys=1, is_stable=True)
    _, rank = lax.sort((order, ids), num_keys=1)
    ends = jnp.searchsorted(sorted_e, jnp.arange(N_EXPERTS, dtype=jnp.int32), side='right').astype(jnp.int32)
    starts = jnp.concatenate([jnp.zeros((1,), jnp.int32), ends[:-1]])

    nb = n_assign // MOE_TILE
    blk_lo = jnp.arange(nb, dtype=jnp.int32) * MOE_TILE
    e_first = jnp.searchsorted(ends, blk_lo, side='right').astype(jnp.int32)
    e_last = jnp.searchsorted(ends, blk_lo + MOE_TILE - 1, side='right').astype(jnp.int32)
    per_blk = e_last - e_first + 1
    cum = jnp.cumsum(per_blk)
    it = jnp.arange(nb + N_EXPERTS - 1, dtype=jnp.int32)
    wb = jnp.minimum(jnp.searchsorted(cum, it, side='right'), nb - 1).astype(jnp.int32)
    we = jnp.clip(e_first[wb] + it - (cum[wb] - per_blk[wb]), 0, N_EXPERTS - 1)
    lo = jnp.clip(starts[we], blk_lo[wb], blk_lo[wb] + MOE_TILE) - blk_lo[wb]
    hi = jnp.clip(ends[we], blk_lo[wb], blk_lo[wb] + MOE_TILE) - blk_lo[wb]
    hi = jnp.where(it < cum[-1], hi, lo)
    return order // TOP_K, sorted_w, rank.reshape(n_tok, TOP_K), (wb, we, lo, hi)


def kernel(x, c, ctx, c_ctx, ada_w, ada_b, norm1_g, norm2_g, w_in, shift_mu, rwkv_w0, rwkv_w2, rwkv_a0, rwkv_a2,
           rwkv_g2, rwkv_k_k, rwkv_k_a, rwkv_r_k, rwkv_lnx_g, rwkv_lnx_b, qn_g, kn_g, diff_lambda, subln_g,
           w_pa, w_pb, w_out, router_g_w, router_g_b, router_e_w, router_e_b, exp_w1, exp_w3, exp_w2):
    assert ada_w.shape[0] == 1, "single-layer block"
    b, l, d = x.shape
    lc = ctx.shape[1]
    lam_init = 0.8 - 0.6 * math.exp(-0.3 * 0)
    lv = diff_lambda[0].astype(F32)
    lam = jnp.exp(jnp.sum(lv[0] * lv[1])) - jnp.exp(jnp.sum(lv[2] * lv[3])) + lam_init

    rows = (b + 1 + SUBLANES - 1) // SUBLANES * SUBLANES
    cm = jnp.zeros((rows, d), F32).at[:b].set(c).at[b].set(c_ctx)
    mod = _modulation(cm, ada_w[0], ada_b[0])
    sh1, sc1, gt1, sh2, sc2, gt2 = [mod[:b, None, k * d:(k + 1) * d] for k in range(6)]
    csh1, csc1 = [jnp.broadcast_to(mod[b, k * d:(k + 1) * d], (b, 1, d)) for k in range(2)]

    w = w_in[0]
    lora_widths = (DECAY_LORA, DECAY_LORA, AAA_LORA, AAA_LORA)
    o_lora = 3 * RWKV_DIM
    o_glora = o_lora + sum(lora_widths)
    pad_cols = lambda m: jnp.concatenate(
        [m[..., :o_lora], _pad_lora_cols(m[..., o_lora:o_glora], lora_widths), m[..., o_glora:RWKV_COLS]], axis=-1)
    w_rwkv = pad_cols(w).astype(BF16)
    w_diff = w[:, RWKV_COLS:RWKV_COLS + DIFF_COLS].astype(BF16)
    w_gate = w[:, RWKV_COLS + DIFF_COLS:].astype(BF16)
    g1 = norm1_g[0]
    hx = _norm_mod(x, g1, sc1, sh1)
    hc = _norm_mod(ctx, g1, csc1, csh1)
    px_r, px_d, px_g = _proj(hx, w_rwkv, F32), _proj(hx, w_diff, BF16), _proj(hx, w_gate, BF16)
    pc_r, pc_d = _proj(hc, w_rwkv, F32), _proj(hc, w_diff, BF16)

    pad_rows = lambda m: jnp.pad(m, ((0, 0), (0, LORA_PAD - m.shape[1]), (0, 0)))
    rparams = (pad_cols(shift_mu[0])[None], rwkv_w0[0], pad_rows(rwkv_w2[0]), rwkv_a0[0], pad_rows(rwkv_a2[0]),
               rwkv_g2[0], rwkv_k_k[0][None], rwkv_k_a[0][None], rwkv_r_k[0].reshape(1, RWKV_DIM))
    y_sweep, bonus, gate = _rwkv_scans(px_r, pc_r, rparams)

    cos_t, sin_t = _rope_tables(l)
    qg = jnp.tile(qn_g[0], 2)[None]
    kg = jnp.tile(kn_g[0], 2)[None]
    q_x, k_x, v_x = _diff_prep(px_d, cos_t, sin_t, qg, kg, True)
    _, k_c, v_c = _diff_prep(pc_d, cos_t[:lc], sin_t[:lc], qg, kg, False)
    lam_vec = jnp.full((1, LANES), lam, F32)
    sg_vec = (subln_g[0] * (1.0 - lam_init))[None]
    score_bound = (1.05 * DIFF_QK_DIM * DIFF_SCALE * math.log2(math.e)
                   * jnp.max(jnp.abs(qn_g[0])) * jnp.max(jnp.abs(kn_g[0])))
    attn_args = (q_x, k_c, v_c, k_x, v_x, lam_vec, sg_vec)
    y_diff = lax.cond(score_bound <= SCORE_LOG2_LIMIT,
                      lambda a: _diff_attention(*a, bounded=True),
                      lambda a: _diff_attention(*a, bounded=False), attn_args)

    mixed = _merge(y_sweep, bonus, gate, y_diff, px_g, rwkv_lnx_g[0][None], rwkv_lnx_b[0][None],
                   w_pa[0].astype(BF16), w_pb[0].astype(BF16))
    n_r = N_GROUPS + N_EXPERTS
    w_router = jnp.zeros((d, ROUTER_PAD), F32).at[:, :N_GROUPS].set(router_g_w[0]).at[:, N_GROUPS:n_r].set(router_e_w[0])
    b_router = jnp.zeros((1, ROUTER_PAD), F32).at[0, :N_GROUPS].set(router_g_b[0]).at[0, N_GROUPS:n_r].set(router_e_b[0])
    w_router_hi = w_router.astype(BF16)
    w_router_lo = (w_router - w_router_hi.astype(F32)).astype(BF16)
    x_new, h2, routed = _outproj(mixed, x, gt1, norm2_g[0][None], sc2, sh2, w_out[0].astype(BF16),
                                 jnp.concatenate([w_router_hi, w_router_lo], axis=1), b_router)

    n_tok = b * l
    row_tok, row_w, row_of, items = _route(routed.reshape(n_tok, ROUTER_PAD), n_tok)
    xs = h2.reshape(n_tok, d)[row_tok]
    out = _moe_ffn(xs, row_w[:, None], items, exp_w1[0], exp_w3[0], exp_w2[0])
    y0 = out[row_of[:, 0]].reshape(b, l, d)
    y1 = out[row_of[:, 1]].reshape(b, l, d)
    return _final(x_new, gt2, y0, y1)
```
